```python
import math
import jax, jax.numpy as jnp
from jax import lax
import numpy as np

D_MODEL = 2048
BATCH = 4
SEQ = 2048
DEPTH = 2

GRID_W = 64
CTX_LEN = 256
HEAD_DIM = 128
NA_HEADS = 6
MLA_HEADS = 5
RET_HEADS = 5
NA_WIDTH = NA_HEADS * HEAD_DIM
MLA_WIDTH = MLA_HEADS * HEAD_DIM
RET_WIDTH = RET_HEADS * HEAD_DIM
MIX_WIDTH = NA_WIDTH + MLA_WIDTH + RET_WIDTH
NA_WIN_R = 8
NA_WIN_C = 16
NA_QB = 16
NA_BAND = 2 * NA_WIN_C
MLA_Q_RANK = 512
MLA_KV_RANK = 512
MLA_NOPE = 128
MLA_ROPE = 64
MLA_V = 128
RET_CHUNK = 128
RET_DK = HEAD_DIM
D_FF = 5632
CONV_W = 3
ROPE_BASE = 10000.0
Q_BLOCK = 128
LN_EPS = 1e-5
RMS_EPS = 1e-6
NEG_INF = -1e30
DEEPNORM_ALPHA = (2 * DEPTH) ** 0.25
DEEPNORM_BETA = (8 * DEPTH) ** -0.25
IN_SPLITS = (NA_WIDTH, NA_WIDTH, NA_WIDTH, MLA_Q_RANK, MLA_KV_RANK, MLA_ROPE, RET_WIDTH, RET_WIDTH, RET_WIDTH, RET_WIDTH)
IN_WIDTH = sum(IN_SPLITS)

kernel_name = "hybrid_na_mla_retention_dit_trunk"


def layer_norm(x, g, b):
    xf = x.astype(jnp.float32)
    mu = xf.mean(-1, keepdims=True)
    var = jnp.square(xf - mu).mean(-1, keepdims=True)
    return ((xf - mu) * lax.rsqrt(var + LN_EPS) * g + b).astype(x.dtype)


def rms_norm(x, g):
    xf = x.astype(jnp.float32)
    return (xf * lax.rsqrt(jnp.square(xf).mean(-1, keepdims=True) + RMS_EPS) * g).astype(x.dtype)


def head_norm(x):
    xf = x.astype(jnp.float32)
    mu = xf.mean(-1, keepdims=True)
    var = jnp.square(xf - mu).mean(-1, keepdims=True)
    return ((xf - mu) * lax.rsqrt(var + LN_EPS)).astype(x.dtype)


def modulate(x, shift, scale):
    return x * (1.0 + scale) + shift


def split_heads(a, h):
    return a.reshape(a.shape[:-1] + (h, a.shape[-1] // h))


def split_cols(p):
    return jnp.split(p, [int(i) for i in np.cumsum(IN_SPLITS)[:-1]], axis=-1)


def axial_rope_tables(T, rot_dim):
    t = jnp.arange(T)
    row = (t // GRID_W).astype(jnp.float32)
    col = (t % GRID_W).astype(jnp.float32)
    nf = rot_dim // 4
    inv = ROPE_BASE ** (-jnp.arange(nf, dtype=jnp.float32) / nf)
    ar = row[:, None] * inv[None]
    ac = col[:, None] * inv[None]
    ang = jnp.concatenate([ar, ar, ac, ac], -1)
    return jnp.cos(ang), jnp.sin(ang)


def apply_rope(x, cos, sin):
    shape = (x.shape[1],) + (1,) * (x.ndim - 3) + (x.shape[-1],)
    cos = cos.reshape(shape)
    sin = sin.reshape(shape)
    half = x.shape[-1] // 2
    nf = half // 2

    def rot_half(y):
        return jnp.concatenate([-y[..., nf:], y[..., :nf]], -1)

    x_rot = jnp.concatenate([rot_half(x[..., :half]), rot_half(x[..., half:])], -1)
    return (x * cos + x_rot * sin).astype(x.dtype)


def dense_attention(q, k, v):
    B, Tq, H, dk = q.shape
    nb = Tq // Q_BLOCK
    scale = dk ** -0.5
    qb = q.reshape(B, nb, Q_BLOCK, H, dk).transpose(1, 0, 2, 3, 4)

    def block(q_blk):
        s = jnp.einsum('bqhd,bkhd->bhqk', q_blk, k).astype(jnp.float32) * scale
        p = jax.nn.softmax(s, axis=-1).astype(v.dtype)
        return jnp.einsum('bhqk,bkhd->bqhd', p, v)

    out = lax.map(block, qb)
    return out.transpose(1, 0, 2, 3, 4).reshape(B, Tq, H, v.shape[-1])


def neighborhood_attention(q, k, v, kc, vc, rpb):
    B, T, H, d = q.shape
    rows = T // GRID_W
    kr = min(NA_WIN_R, rows)
    ncb = GRID_W // NA_QB
    scale = d ** -0.5
    q_col = np.arange(GRID_W).reshape(ncb, NA_QB)
    col_start = np.clip(q_col - NA_WIN_C // 2, 0, GRID_W - NA_WIN_C)
    band_start = np.minimum(col_start[:, 0], GRID_W - NA_BAND)
    key_col = band_start[:, None] + np.arange(NA_BAND)[None]
    in_win = (key_col[:, None, :] >= col_start[..., None]) & (key_col[:, None, :] < col_start[..., None] + NA_WIN_C)
    dc_idx = np.clip(key_col[:, None, :] - q_col[..., None] + NA_WIN_C - 1, 0, 2 * NA_WIN_C - 2)
    mask = in_win[:, :, None, :]
    kg = k.reshape(B, rows, GRID_W, H, d)
    vg = v.reshape(B, rows, GRID_W, H, d)
    qg = q.reshape(B, rows, ncb, NA_QB, H, d).transpose(1, 0, 2, 3, 4, 5)
    n_loc = kr * NA_BAND

    def row_block(args):
        r, q_r = args
        r0 = jnp.clip(r - kr // 2, 0, rows - kr)
        k_r = lax.dynamic_slice_in_dim(kg, r0, kr, axis=1)[:, :, key_col]
        v_r = lax.dynamic_slice_in_dim(vg, r0, kr, axis=1)[:, :, key_col]
        dr_idx = r0 + jnp.arange(kr) - r + NA_WIN_R - 1
        bias = rpb[:, dr_idx[None, None, :, None], dc_idx[:, :, None, :]]
        s_loc = jnp.einsum('bjqhd,bkjchd->bhjqkc', q_r, k_r).astype(jnp.float32) * scale + bias.astype(jnp.float32)
        s_loc = jnp.where(mask, s_loc, NEG_INF)
        s_ctx = jnp.einsum('bjqhd,bnhd->bhjqn', q_r, kc).astype(jnp.float32) * scale
        s = jnp.concatenate([s_loc.reshape(B, H, ncb, NA_QB, n_loc), s_ctx], -1)
        p = jax.nn.softmax(s, axis=-1).astype(v.dtype)
        p_loc = p[..., :n_loc].reshape(B, H, ncb, NA_QB, kr, NA_BAND)
        out = jnp.einsum('bhjqkc,bkjchd->bjqhd', p_loc, v_r) + jnp.einsum('bhjqn,bnhd->bjqhd', p[..., n_loc:], vc)
        return out.reshape(B, GRID_W, H, d)

    out = lax.map(row_block, (jnp.arange(rows), qg))
    return out.transpose(1, 0, 2, 3, 4).reshape(B, T, H, d)


def mla_q(cq, g_q, w_uq, rope):
    q = split_heads(rms_norm(cq, g_q) @ w_uq, MLA_HEADS)
    q_nope, q_pe = q[..., :MLA_NOPE], q[..., MLA_NOPE:]
    if rope is not None:
        q_pe = apply_rope(q_pe, *rope)
    return jnp.concatenate([q_nope, q_pe], -1)


def mla_kv(ckv, kpe, g_kv, w_ukv, rope):
    kv = split_heads(rms_norm(ckv, g_kv) @ w_ukv, MLA_HEADS)
    k_nope, v = kv[..., :MLA_NOPE], kv[..., MLA_NOPE:]
    if rope is not None:
        kpe = apply_rope(kpe, *rope)
    kpe = jnp.broadcast_to(kpe[:, :, None, :], k_nope.shape[:-1] + (MLA_ROPE,))
    return jnp.concatenate([k_nope, kpe], -1), v


def retention_chunked(q, k, v, log_gamma, s0):
    B, T, H, _ = q.shape
    dv = v.shape[-1]
    L = RET_CHUNK
    n = T // L

    def chunks(a):
        return a.astype(jnp.float32).reshape(B, n, L, H, a.shape[-1]).transpose(1, 0, 3, 2, 4)

    pos = jnp.arange(L, dtype=jnp.float32)
    diff = pos[:, None] - pos[None, :]
    decay = jnp.where(diff >= 0, jnp.exp(jnp.maximum(diff, 0.0)[None] * log_gamma[:, None, None]), 0.0)
    q_dec = jnp.exp((pos + 1.0)[None] * log_gamma[:, None])[:, :, None]
    k_dec = jnp.exp((L - 1.0 - pos)[None] * log_gamma[:, None])[:, :, None]
    chunk_dec = jnp.exp(L * log_gamma)[:, None, None]

    def step(state, qkv):
        qi, ki, vi = qkv
        inner = jnp.einsum('bhlm,bhme->bhle', jnp.einsum('bhld,bhmd->bhlm', qi, ki) * decay, vi)
        cross = jnp.einsum('bhld,bhde->bhle', qi * q_dec, state)
        state = chunk_dec * state + jnp.einsum('bhld,bhle->bhde', ki * k_dec, vi)
        return state, inner + cross

    _, out = lax.scan(step, s0.astype(jnp.float32), (chunks(q), chunks(k), chunks(v)))
    return out.transpose(1, 0, 3, 2, 4).reshape(B, T, H, dv).astype(v.dtype)


def retention_final_state(k, v, log_gamma, reverse):
    C = k.shape[1]
    pos = jnp.arange(C, dtype=jnp.float32)
    steps = pos if reverse else (C - 1.0 - pos)
    w = jnp.exp(steps[None, :] * log_gamma[:, None])
    return jnp.einsum('bmhd,bmhe,hm->bhde', k.astype(jnp.float32), v.astype(jnp.float32), w)


def bidir_retention(q, k, v, lg_f, lg_b, s_f, s_b):
    flip = lambda a: jnp.flip(a, axis=1)
    o_f = retention_chunked(q, k, v, lg_f, s_f)
    o_b = flip(retention_chunked(flip(q), flip(k), flip(v), lg_b, s_b))
    return o_f + o_b


def conv_glu(h, w_up, conv_w, conv_b, w_down):
    T = h.shape[1]
    a, u = jnp.split(h @ w_up, 2, axis=-1)
    pad = CONV_W // 2
    ap = jnp.pad(a, ((0, 0), (pad, pad), (0, 0)))
    acc = conv_b
    for i in range(CONV_W):
        acc = acc + ap[:, i:i + T] * conv_w[i]
    return (jax.nn.silu(acc) * u) @ w_down


def setup_inputs(seed: int = 0) -> dict:
    key = jax.random.key(seed)
    ks = jax.random.split(key, 24)
    f32 = jnp.float32
    D = D_MODEL

    def normal(k, shape, scale):
        return jax.random.normal(k, shape, f32) * scale

    return {
        "x": normal(ks[0], (BATCH, SEQ, D), 1.0),
        "c": normal(ks[1], (BATCH, D), 1.0),
        "ctx": normal(ks[2], (BATCH, CTX_LEN, D), 1.0),
        "c_ctx": normal(ks[3], (D,), 1.0),
        "w_ada": normal(ks[4], (DEPTH, D, 6 * D), 0.5 * D ** -0.5),
        "b_ada": normal(ks[5], (DEPTH, 6 * D), 0.02),
        "w_in": normal(ks[6], (DEPTH, D, IN_WIDTH), D ** -0.5),
        "mla_q_norm": 1.0 + normal(ks[7], (DEPTH, MLA_Q_RANK), 0.02),
        "mla_kv_norm": 1.0 + normal(ks[8], (DEPTH, MLA_KV_RANK), 0.02),
        "w_uq": normal(ks[9], (DEPTH, MLA_Q_RANK, MLA_HEADS * (MLA_NOPE + MLA_ROPE)), MLA_Q_RANK ** -0.5),
        "w_ukv": normal(ks[10], (DEPTH, MLA_KV_RANK, MLA_HEADS * (MLA_NOPE + MLA_V)), MLA_KV_RANK ** -0.5),
        "na_rpb": normal(ks[11], (DEPTH, NA_HEADS, 2 * NA_WIN_R - 1, 2 * NA_WIN_C - 1), 0.02),
        "ret_decay": -5.0 - jnp.arange(RET_HEADS, dtype=f32) + normal(ks[12], (DEPTH, 2, RET_HEADS), 0.1),
        "w_o": normal(ks[13], (DEPTH, MIX_WIDTH, D), DEEPNORM_BETA * MIX_WIDTH ** -0.5),
        "ln1_g": 1.0 + normal(ks[14], (DEPTH, D), 0.02),
        "ln1_b": normal(ks[15], (DEPTH, D), 0.02),
        "w_up": normal(ks[16], (DEPTH, D, 2 * D_FF), D ** -0.5),
        "conv_w": normal(ks[17], (DEPTH, CONV_W, D_FF), CONV_W ** -0.5),
        "conv_b": normal(ks[18], (DEPTH, D_FF), 0.02),
        "w_down": normal(ks[19], (DEPTH, D_FF, D), DEEPNORM_BETA * D_FF ** -0.5),
        "ln2_g": 1.0 + normal(ks[20], (DEPTH, D), 0.02),
        "ln2_b": normal(ks[21], (DEPTH, D), 0.02),
    }


def reference(x, c, ctx, c_ctx, w_ada, b_ada, w_in, mla_q_norm, mla_kv_norm, w_uq, w_ukv, na_rpb, ret_decay,
              w_o, ln1_g, ln1_b, w_up, conv_w, conv_b, w_down, ln2_g, ln2_b):
    B, T, _ = x.shape
    rope_pe = axial_rope_tables(T, MLA_ROPE)
    rope_ret = axial_rope_tables(T, RET_DK)
    silu_c = jax.nn.silu(c)
    silu_cc = jax.nn.silu(c_ctx)
    ret_scale = RET_DK ** -0.5
    for l in range(DEPTH):
        last = l == DEPTH - 1
        mod = (silu_c @ w_ada[l] + b_ada[l])[:, None, :]
        mod_c = silu_cc @ w_ada[l] + b_ada[l]
        sh1, sc1, g1, sh2, sc2, g2 = jnp.split(mod, 6, axis=-1)
        csh1, csc1, cg1, csh2, csc2, cg2 = jnp.split(mod_c, 6, axis=-1)

        na_q, na_k, na_v, cq, ckv, kpe, r_q, r_k, r_v, r_g = split_cols(modulate(x, sh1, sc1) @ w_in[l])
        na_qc, na_kc, na_vc, cq_c, ckv_c, kpe_c, r_qc, r_kc, r_vc, r_gc = split_cols(modulate(ctx, csh1, csc1) @ w_in[l])

        kc_na = split_heads(na_kc, NA_HEADS)
        vc_na = split_heads(na_vc, NA_HEADS)
        y_na = neighborhood_attention(split_heads(na_q, NA_HEADS), split_heads(na_k, NA_HEADS),
                                      split_heads(na_v, NA_HEADS), kc_na, vc_na, na_rpb[l])

        k_mc, v_mc = mla_kv(ckv_c, kpe_c, mla_kv_norm[l], w_ukv[l], None)
        k_ml, v_ml = mla_kv(ckv, kpe, mla_kv_norm[l], w_ukv[l], rope_pe)
        q_ml = mla_q(cq, mla_q_norm[l], w_uq[l], rope_pe)
        y_mla = dense_attention(q_ml, jnp.concatenate([k_mc, k_ml], 1), jnp.concatenate([v_mc, v_ml], 1))

        lg = jnp.log1p(-jnp.exp2(ret_decay[l].astype(jnp.float32)))
        lg_f, lg_b = lg[0], lg[1]
        kc_r = split_heads(r_kc, RET_HEADS) * ret_scale
        vc_r = split_heads(r_vc, RET_HEADS)
        s_f = retention_final_state(kc_r, vc_r, lg_f, reverse=False)
        s_b = retention_final_state(kc_r, vc_r, lg_b, reverse=True)
        q_r = apply_rope(split_heads(r_q, RET_HEADS), *rope_ret)
        k_r = apply_rope(split_heads(r_k, RET_HEADS), *rope_ret) * ret_scale
        o_r = bidir_retention(q_r, k_r, split_heads(r_v, RET_HEADS), lg_f, lg_b, s_f, s_b)
        y_ret = head_norm(o_r) * jax.nn.silu(split_heads(r_g, RET_HEADS))

        y = jnp.concatenate([y_na.reshape(B, T, NA_WIDTH), y_mla.reshape(B, T, MLA_WIDTH),
                             y_ret.reshape(B, T, RET_WIDTH)], -1) @ w_o[l]
        x_new = layer_norm(DEEPNORM_ALPHA * x + g1 * y, ln1_g[l], ln1_b[l])

        if not last:
            Bc, C, _ = ctx.shape
            yc_na = dense_attention(split_heads(na_qc, NA_HEADS), kc_na, vc_na)
            yc_mla = dense_attention(mla_q(cq_c, mla_q_norm[l], w_uq[l], None), k_mc, v_mc)
            zero_state = jnp.zeros((Bc, RET_HEADS, RET_DK, HEAD_DIM), jnp.float32)
            oc_r = bidir_retention(split_heads(r_qc, RET_HEADS), kc_r, vc_r, lg_f, lg_b, zero_state, zero_state)
            yc_ret = head_norm(oc_r) * jax.nn.silu(split_heads(r_gc, RET_HEADS))
            yc = jnp.concatenate([yc_na.reshape(Bc, C, NA_WIDTH), yc_mla.reshape(Bc, C, MLA_WIDTH),
                                  yc_ret.reshape(Bc, C, RET_WIDTH)], -1) @ w_o[l]
            ctx = layer_norm(DEEPNORM_ALPHA * ctx + cg1 * yc, ln1_g[l], ln1_b[l])
        x = x_new

        yf = conv_glu(modulate(x, sh2, sc2), w_up[l], conv_w[l], conv_b[l], w_down[l])
        x = layer_norm(DEEPNORM_ALPHA * x + g2 * yf, ln2_g[l], ln2_b[l])
        if not last:
            ycf = conv_glu(modulate(ctx, csh2, csc2), w_up[l], conv_w[l], conv_b[l], w_down[l])
            ctx = layer_norm(DEEPNORM_ALPHA * ctx + cg2 * ycf, ln2_g[l], ln2_b[l])
    return x
```

```python
import functools

import numpy as np
import jax
import jax.numpy as jnp
from jax import lax
from jax.experimental import pallas as pl
from jax.experimental.pallas import tpu as pltpu

GRID_W = 64
HEAD_DIM = 128
NA_HEADS = 6
MLA_HEADS = 5
RET_HEADS = 5
NA_WIDTH = NA_HEADS * HEAD_DIM
MLA_WIDTH = MLA_HEADS * HEAD_DIM
RET_WIDTH = RET_HEADS * HEAD_DIM
NA_WIN_R = 8
NA_WIN_C = 16
MLA_Q_RANK = 512
MLA_KV_RANK = 512
MLA_NOPE = 128
MLA_ROPE = 64
MLA_V = 128
MLA_QK_PAD = 256
RET_CHUNK = 128
CONV_W = 3
ROPE_BASE = 10000.0
LN_EPS = 1e-5
RMS_EPS = 1e-6
NEG_INF = -1e30
IN_SPLITS = (NA_WIDTH, NA_WIDTH, NA_WIDTH, MLA_Q_RANK, MLA_KV_RANK, MLA_ROPE,
             RET_WIDTH, RET_WIDTH, RET_WIDTH, RET_WIDTH)

LANE = 128
CB_CQ, CB_CKV, CB_NAQ, CB_NAK, CB_NAV = 0, 4, 8, 14, 20
CB_RQ, CB_RK, CB_RV, CB_RG, CB_KPE = 26, 31, 36, 41, 46
PACKED_IN_WIDTH = 48 * LANE
MOD_ROWS = 8
VMEM_LIMIT_MB = 56

_BF = jnp.bfloat16
_F32 = jnp.float32


def _cparams(sem, vmem_mb=VMEM_LIMIT_MB):
    return pltpu.CompilerParams(dimension_semantics=sem, vmem_limit_bytes=vmem_mb << 20)


def _tile(n, pref):
    t = min(n, pref)
    while n % t:
        t //= 2
    return t


def _dot(a, b):
    return jnp.dot(a, b, preferred_element_type=_F32)


def _dot_nt(a, b):
    return lax.dot_general(a, b, (((1,), (1,)), ((), ())), preferred_element_type=_F32)


def _silu(x):
    return x / (1.0 + jnp.exp(-x))


def _resident_spec(shape):
    return pl.BlockSpec(shape, lambda *_: (0,) * len(shape), pipeline_mode=pl.Buffered(1))


def _ada_body(c_ref, w_ref, b_ref, o_ref):
    s = _silu(c_ref[...]).astype(_BF)
    o_ref[0] = _dot(s, w_ref[0].astype(_BF)) + b_ref[0]


def _ada(cvec, w_ada, b_ada):
    depth, d, n = w_ada.shape
    tn = _tile(n, 1024)
    out = pl.pallas_call(
        _ada_body,
        grid=(depth, n // tn),
        in_specs=[pl.BlockSpec((MOD_ROWS, d), lambda l, j: (0, 0)),
                  pl.BlockSpec((1, d, tn), lambda l, j: (l, 0, j)),
                  pl.BlockSpec((1, 1, tn), lambda l, j: (l, 0, j))],
        out_specs=pl.BlockSpec((1, MOD_ROWS, tn), lambda l, j: (l, 0, j)),
        out_shape=jax.ShapeDtypeStruct((depth, MOD_ROWS, n), _F32),
        compiler_params=_cparams(("arbitrary", "arbitrary")),
    )(cvec, w_ada, b_ada.reshape(depth, 1, n))
    return out.reshape(depth * MOD_ROWS, 1, n)


def _mod_spec(d, layer, chunk, row_fn):
    return pl.BlockSpec((1, 1, d), lambda i, *_: (layer * MOD_ROWS + row_fn(i), 0, chunk))


def _modulate_body(x_ref, sh_ref, sc_ref, o_ref):
    o_ref[...] = (x_ref[...] * (1.0 + sc_ref[0]) + sh_ref[0]).astype(o_ref.dtype)


def _modulate(x2, mod3, layer, row_fn, tm):
    m, d = x2.shape
    return pl.pallas_call(
        _modulate_body,
        grid=(m // tm,),
        in_specs=[pl.BlockSpec((tm, d), lambda i: (i, 0)),
                  _mod_spec(d, layer, 0, row_fn), _mod_spec(d, layer, 1, row_fn)],
        out_specs=pl.BlockSpec((tm, d), lambda i: (i, 0)),
        out_shape=jax.ShapeDtypeStruct((m, d), _BF),
        compiler_params=_cparams(("arbitrary",)),
    )(x2, mod3, mod3)


def _mm_body(a_ref, w_ref, o_ref):
    o_ref[...] = _dot(a_ref[...], w_ref[...]).astype(o_ref.dtype)


def _mm(a, w, tm, tn):
    m, k = a.shape
    n = w.shape[1]
    return pl.pallas_call(
        _mm_body,
        grid=(m // tm, n // tn),
        in_specs=[pl.BlockSpec((tm, k), lambda i, j: (i, 0)),
                  pl.BlockSpec((k, tn), lambda i, j: (0, j))],
        out_specs=pl.BlockSpec((tm, tn), lambda i, j: (i, j)),
        out_shape=jax.ShapeDtypeStruct((m, n), _BF),
        compiler_params=_cparams(("arbitrary", "arbitrary")),
    )(a, w)


def _rope_tables(t, rot_dim, identity=False):
    pos = jnp.arange(t)
    row = (pos // GRID_W).astype(_F32)
    col = (pos % GRID_W).astype(_F32)
    nf = rot_dim // 4
    inv = ROPE_BASE ** (-jnp.arange(nf, dtype=_F32) / nf)
    ar = row[:, None] * inv[None]
    ac = col[:, None] * inv[None]
    ang = jnp.concatenate([ar, ar, ac, ac], -1)
    cos, sin = jnp.cos(ang), jnp.sin(ang)
    if identity:
        cos, sin = jnp.ones_like(cos), jnp.zeros_like(sin)
    first = (np.arange(rot_dim) % (rot_dim // 2)) < nf
    sin_a = jnp.where(first, -sin, 0.0)
    sin_b = jnp.where(first, 0.0, sin)
    pad = LANE - rot_dim
    if pad:
        cos = jnp.pad(cos, ((0, 0), (0, pad)), constant_values=1.0)
        sin_a = jnp.pad(sin_a, ((0, 0), (0, pad)))
        sin_b = jnp.pad(sin_b, ((0, 0), (0, pad)))
    return cos, sin_a, sin_b


def _rope(x, cos, sin_a, sin_b, nf):
    return x * cos + pltpu.roll(x, LANE - nf, 1) * sin_a + pltpu.roll(x, nf, 1) * sin_b


def _rms(x, g):
    return x * lax.rsqrt(jnp.mean(x * x, -1, keepdims=True) + RMS_EPS) * g


def _mla_q_body(p_ref, g_ref, w_ref, cos_ref, sa_ref, sb_ref, o_ref):
    xb = _rms(p_ref[...].astype(_F32), g_ref[...]).astype(_BF)
    cos, sa, sb = cos_ref[...], sa_ref[...], sb_ref[...]
    for h in range(MLA_HEADS):
        acc = _dot(xb, w_ref[:, h * MLA_QK_PAD:(h + 1) * MLA_QK_PAD])
        o_ref[0, h, :, 0:LANE] = acc[:, :LANE].astype(_BF)
        o_ref[0, h, :, LANE:] = _rope(acc[:, LANE:], cos, sa, sb, MLA_ROPE // 4).astype(_BF)


def _mla_q(p, g, w, tables, b, t, tm):
    nb = t // tm
    tab_spec = pl.BlockSpec((tm, LANE), lambda i: (i % nb, 0))
    return pl.pallas_call(
        _mla_q_body,
        grid=(b * nb,),
        in_specs=[pl.BlockSpec((tm, MLA_Q_RANK), lambda i: (i, CB_CQ // 4)),
                  pl.BlockSpec((1, MLA_Q_RANK), lambda i: (0, 0)),
                  pl.BlockSpec(w.shape, lambda i: (0, 0)),
                  tab_spec, tab_spec, tab_spec],
        out_specs=pl.BlockSpec((1, MLA_HEADS, tm, MLA_QK_PAD), lambda i: (i // nb, 0, i % nb, 0)),
        out_shape=jax.ShapeDtypeStruct((b, MLA_HEADS, t, MLA_QK_PAD), _BF),
        compiler_params=_cparams(("arbitrary",)),
    )(p, g, w, *tables)


def _mla_kv_body(ckv_ref, kpe_ref, g_ref, w_ref, cos_ref, sa_ref, sb_ref, k_ref, v_ref):
    xb = _rms(ckv_ref[...].astype(_F32), g_ref[...]).astype(_BF)
    pe = _rope(kpe_ref[...].astype(_F32), cos_ref[...], sa_ref[...], sb_ref[...], MLA_ROPE // 4).astype(_BF)
    width = MLA_NOPE + MLA_V
    for h in range(MLA_HEADS):
        acc = _dot(xb, w_ref[:, h * width:(h + 1) * width])
        k_ref[0, h, :, 0:LANE] = acc[:, :MLA_NOPE].astype(_BF)
        k_ref[0, h, :, LANE:] = pe
        v_ref[0, h] = acc[:, MLA_NOPE:].astype(_BF)


def _mla_kv(p, g, w, tables, b, t, tm):
    nb = t // tm
    tab_spec = pl.BlockSpec((tm, LANE), lambda i: (i % nb, 0))
    return pl.pallas_call(
        _mla_kv_body,
        grid=(b * nb,),
        in_specs=[pl.BlockSpec((tm, MLA_KV_RANK), lambda i: (i, CB_CKV // 4)),
                  pl.BlockSpec((tm, LANE), lambda i: (i, CB_KPE)),
                  pl.BlockSpec((1, MLA_KV_RANK), lambda i: (0, 0)),
                  pl.BlockSpec(w.shape, lambda i: (0, 0)),
                  tab_spec, tab_spec, tab_spec],
        out_specs=[pl.BlockSpec((1, MLA_HEADS, tm, MLA_QK_PAD), lambda i: (i // nb, 0, i % nb, 0)),
                   pl.BlockSpec((1, MLA_HEADS, tm, MLA_V), lambda i: (i // nb, 0, i % nb, 0))],
        out_shape=[jax.ShapeDtypeStruct((b, MLA_HEADS, t, MLA_QK_PAD), _BF),
                   jax.ShapeDtypeStruct((b, MLA_HEADS, t, MLA_V), _BF)],
        compiler_params=_cparams(("arbitrary",)),
    )(p, p, g, w, *tables)


def _attn_body(*refs, nseg):
    q = refs[0][...]
    o_ref = refs[-1]
    s = [_dot_nt(q, refs[1 + 2 * i][...]) for i in range(nseg)]
    m = functools.reduce(jnp.maximum, [jnp.max(si, -1, keepdims=True) for si in s])
    p = [jnp.exp(si - m) for si in s]
    l = functools.reduce(jnp.add, [jnp.sum(pi, -1, keepdims=True) for pi in p])
    acc = functools.reduce(jnp.add, [_dot(p[i].astype(_BF), refs[2 + 2 * i][...]) for i in range(nseg)])
    o_ref[...] = (acc / l).astype(o_ref.dtype)


def _dense_attn(q, q_spec, kv, b, heads, tq_blocks, tq, dv):
    args, specs = [q], [q_spec]
    for k, ks, v, vs in kv:
        args += [k, v]
        specs += [ks, vs]
    return pl.pallas_call(
        functools.partial(_attn_body, nseg=len(kv)),
        grid=(b, heads, tq_blocks),
        in_specs=specs,
        out_specs=pl.BlockSpec((tq, dv), lambda bi, h, i: (bi * tq_blocks + i, h)),
        out_shape=jax.ShapeDtypeStruct((b * tq_blocks * tq, heads * dv), _BF),
        compiler_params=_cparams(("arbitrary", "arbitrary", "arbitrary")),
    )(*args)


def _head4_spec(n, d):
    return pl.BlockSpec((None, None, n, d), lambda bi, h, i: (bi, h, 0, 0))


def _na_bias_table(rpb, rows):
    c = np.arange(GRID_W)
    col_start = np.clip(c - NA_WIN_C // 2, 0, GRID_W - NA_WIN_C)
    kc = np.arange(GRID_W)
    in_win = (kc[None, :] >= col_start[:, None]) & (kc[None, :] < col_start[:, None] + NA_WIN_C)
    dc = np.clip(kc[None, :] - c[:, None] + NA_WIN_C - 1, 0, 2 * NA_WIN_C - 2)
    d = np.arange(NA_WIN_R)
    kr = np.arange(NA_WIN_R)
    dr = np.clip(kr[None, :] - d[:, None] + NA_WIN_R - 1, 0, 2 * NA_WIN_R - 2)
    tab = rpb[:, dr[:, None, :, None], dc[None, :, None, :]]
    tab = jnp.where(in_win[None, None, :, None, :], tab.astype(_F32), NEG_INF)
    h = rpb.shape[0]
    return tab.transpose(1, 0, 2, 3, 4).reshape(NA_WIN_R, h, GRID_W, NA_WIN_R * GRID_W)


def _na_body(q_ref, k_ref, v_ref, kc_ref, vc_ref, bias_ref, o_ref, *, rows):
    kc = kc_ref[...]
    vc = vc_ref[...]
    nwin = NA_WIN_R * GRID_W

    def row(r, carry):
        r0 = jnp.clip(r - NA_WIN_R // 2, 0, rows - NA_WIN_R)
        q = q_ref[pl.ds(pl.multiple_of(r * GRID_W, GRID_W), GRID_W), :]
        start = pl.multiple_of(r0 * GRID_W, GRID_W)
        kw = k_ref[pl.ds(start, nwin), :]
        vw = v_ref[pl.ds(start, nwin), :]
        s_loc = _dot_nt(q, kw) + bias_ref[r - r0]
        s_ctx = _dot_nt(q, kc)
        m = jnp.maximum(jnp.max(s_loc, -1, keepdims=True), jnp.max(s_ctx, -1, keepdims=True))
        p_loc = jnp.exp(s_loc - m)
        p_ctx = jnp.exp(s_ctx - m)
        l = jnp.sum(p_loc, -1, keepdims=True) + jnp.sum(p_ctx, -1, keepdims=True)
        acc = _dot(p_loc.astype(_BF), vw) + _dot(p_ctx.astype(_BF), vc)
        o_ref[pl.ds(pl.multiple_of(r * GRID_W, GRID_W), GRID_W), :] = (acc / l).astype(o_ref.dtype)
        return carry

    lax.fori_loop(0, rows, row, 0)


def _na_attn(p, pc, bias, b, t, c):
    rows = t // GRID_W
    assert rows >= NA_WIN_R and t % GRID_W == 0
    return pl.pallas_call(
        functools.partial(_na_body, rows=rows),
        grid=(NA_HEADS, b),
        in_specs=[pl.BlockSpec((t, LANE), lambda h, bi: (bi, CB_NAQ + h)),
                  pl.BlockSpec((t, LANE), lambda h, bi: (bi, CB_NAK + h)),
                  pl.BlockSpec((t, LANE), lambda h, bi: (bi, CB_NAV + h)),
                  pl.BlockSpec((c, LANE), lambda h, bi: (bi, CB_NAK + h)),
                  pl.BlockSpec((c, LANE), lambda h, bi: (bi, CB_NAV + h)),
                  pl.BlockSpec((NA_WIN_R, None, GRID_W, NA_WIN_R * GRID_W), lambda h, bi: (0, h, 0, 0))],
        out_specs=pl.BlockSpec((t, LANE), lambda h, bi: (bi, h)),
        out_shape=jax.ShapeDtypeStruct((b * t, NA_WIDTH), _BF),
        compiler_params=_cparams(("arbitrary", "arbitrary")),
    )(p, p, p, pc, pc, bias)


def _ret_body(lg_ref, q_ref, k_ref, v_ref, g_ref, s0f_ref, s0b_ref, cos_ref, sa_ref, sb_ref,
              o_ref, sf_ref, sbo_ref, sb_hist, *, n):
    h = pl.program_id(0)
    lgf = lg_ref[0, h]
    lgb = lg_ref[1, h]
    L = RET_CHUNK
    pos_c = lax.broadcasted_iota(jnp.int32, (L, 1), 0).astype(_F32)
    pos_r = lax.broadcasted_iota(jnp.int32, (1, L), 1).astype(_F32)
    diff = pos_c - pos_r
    decay = jnp.where(diff > 0, jnp.exp(jnp.maximum(diff, 0.0) * lgf),
                      jnp.where(diff < 0, jnp.exp(jnp.maximum(-diff, 0.0) * lgb), 2.0))
    qdec_f = jnp.exp((pos_c + 1.0) * lgf)
    kdec_f = jnp.exp((L - 1.0 - pos_c) * lgf)
    qdec_b = jnp.exp((L - pos_c) * lgb)
    kdec_b = jnp.exp(pos_c * lgb)
    cd_f = jnp.exp(L * lgf)
    cd_b = jnp.exp(L * lgb)
    nf = HEAD_DIM // 4

    def load(ref, i, rope):
        rows = pl.ds(pl.multiple_of(i * L, L), L)
        x = ref[rows, :].astype(_F32)
        if rope:
            x = _rope(x, cos_ref[rows, :], sa_ref[rows, :], sb_ref[rows, :], nf)
        return x

    def bwd(j, state):
        i = n - 1 - j
        sb_hist[i] = state
        k = load(k_ref, i, True)
        v = v_ref[pl.ds(pl.multiple_of(i * L, L), L), :]
        return cd_b * state + _dot((k * kdec_b).T.astype(_BF), v)

    sbo_ref[...] = lax.fori_loop(0, n, bwd, s0b_ref[...])

    def fwd(i, state):
        rows = pl.ds(pl.multiple_of(i * L, L), L)
        q = load(q_ref, i, True)
        k = load(k_ref, i, True)
        v = v_ref[rows, :]
        a = _dot_nt(q.astype(_BF), k.astype(_BF)) * decay
        o = _dot(a.astype(_BF), v)
        o = o + _dot((q * qdec_f).astype(_BF), state.astype(_BF))
        o = o + _dot((q * qdec_b).astype(_BF), sb_hist[i].astype(_BF))
        mu = jnp.mean(o, -1, keepdims=True)
        oc = o - mu
        var = jnp.mean(oc * oc, -1, keepdims=True)
        y = oc * lax.rsqrt(var + LN_EPS) * _silu(g_ref[rows, :].astype(_F32))
        o_ref[rows, :] = y.astype(o_ref.dtype)
        return cd_f * state + _dot((k * kdec_f).T.astype(_BF), v)

    sf_ref[...] = lax.fori_loop(0, n, fwd, s0f_ref[...])


def _retention(p, lg, s0f, s0b, tables, b, t):
    n = t // RET_CHUNK
    col = lambda cb: pl.BlockSpec((t, LANE), lambda h, bi: (bi, cb + h))
    st_spec = pl.BlockSpec((None, None, HEAD_DIM, HEAD_DIM), lambda h, bi: (bi, h, 0, 0))
    tab_spec = pl.BlockSpec((t, LANE), lambda h, bi: (0, 0))
    st_shape = jax.ShapeDtypeStruct((b, RET_HEADS, HEAD_DIM, HEAD_DIM), _F32)
    return pl.pallas_call(
        functools.partial(_ret_body, n=n),
        grid=(RET_HEADS, b),
        in_specs=[pl.BlockSpec(memory_space=pltpu.SMEM),
                  col(CB_RQ), col(CB_RK), col(CB_RV), col(CB_RG), st_spec, st_spec,
                  tab_spec, tab_spec, tab_spec],
        out_specs=[pl.BlockSpec((t, LANE), lambda h, bi: (bi, h)), st_spec, st_spec],
        out_shape=[jax.ShapeDtypeStruct((b * t, RET_WIDTH), _BF), st_shape, st_shape],
        scratch_shapes=[pltpu.VMEM((n, HEAD_DIM, HEAD_DIM), _F32)],
        compiler_params=_cparams(("arbitrary", "arbitrary")),
    )(lg, p, p, p, p, s0f, s0b, *tables)


def _ln_epilogue(y, x_ref, gate_ref, lng_ref, lnb_ref, alpha, xo_ref, ho_ref, sh_ref, sc_ref):
    z = alpha * x_ref[...] + gate_ref[0] * y
    mu = jnp.mean(z, -1, keepdims=True)
    zc = z - mu
    var = jnp.mean(zc * zc, -1, keepdims=True)
    xn = zc * lax.rsqrt(var + LN_EPS) * lng_ref[...] + lnb_ref[...]
    xo_ref[...] = xn
    if ho_ref is not None:
        ho_ref[...] = (xn * (1.0 + sc_ref[0]) + sh_ref[0]).astype(ho_ref.dtype)


def _wo_ln_body(a1_ref, a2_ref, a3_ref, w_ref, x_ref, gate_ref, sh_ref, sc_ref, lng_ref, lnb_ref,
                xo_ref, ho_ref, *, alpha):
    o1, o2 = NA_WIDTH, NA_WIDTH + MLA_WIDTH
    y = _dot(a1_ref[...], w_ref[0:o1, :]) + _dot(a2_ref[...], w_ref[o1:o2, :]) + _dot(a3_ref[...], w_ref[o2:, :])
    _ln_epilogue(y, x_ref, gate_ref, lng_ref, lnb_ref, alpha, xo_ref, ho_ref, sh_ref, sc_ref)


def _wo_ln(a1, a2, a3, w, x2, mod3, layer, row_fn, lng, lnb, alpha, tm):
    m, d = x2.shape
    row = lambda width: pl.BlockSpec((tm, width), lambda i: (i, 0))
    vec = pl.BlockSpec((1, d), lambda i: (0, 0))
    return pl.pallas_call(
        functools.partial(_wo_ln_body, alpha=alpha),
        grid=(m // tm,),
        in_specs=[row(NA_WIDTH), row(MLA_WIDTH), row(RET_WIDTH),
                  _resident_spec(w.shape),
                  row(d),
                  _mod_spec(d, layer, 2, row_fn), _mod_spec(d, layer, 3, row_fn), _mod_spec(d, layer, 4, row_fn),
                  vec, vec],
        out_specs=[row(d), row(d)],
        out_shape=[jax.ShapeDtypeStruct((m, d), _F32), jax.ShapeDtypeStruct((m, d), _BF)],
        compiler_params=_cparams(("arbitrary",)),
    )(a1, a2, a3, w, x2, mod3, mod3, mod3, lng, lnb)


def _down_ln_body(*refs, alpha, with_next):
    if with_next:
        g_ref, w_ref, x_ref, gate_ref, sh_ref, sc_ref, lng_ref, lnb_ref, xo_ref, ho_ref = refs
    else:
        g_ref, w_ref, x_ref, gate_ref, lng_ref, lnb_ref, xo_ref = refs
        sh_ref = sc_ref = ho_ref = None
    y = _dot(g_ref[...], w_ref[...])
    _ln_epilogue(y, x_ref, gate_ref, lng_ref, lnb_ref, alpha, xo_ref, ho_ref, sh_ref, sc_ref)


def _down_ln(g, w, x2, mod3, layer, row_fn, lng, lnb, alpha, tm, with_next):
    m, d = x2.shape
    dff = g.shape[1]
    row = lambda width: pl.BlockSpec((tm, width), lambda i: (i, 0))
    vec = pl.BlockSpec((1, d), lambda i: (0, 0))
    in_specs = [row(dff), _resident_spec(w.shape), row(d), _mod_spec(d, layer, 5, row_fn)]
    args = [g, w, x2, mod3]
    if with_next:
        in_specs += [_mod_spec(d, layer + 1, 0, row_fn), _mod_spec(d, layer + 1, 1, row_fn)]
        args += [mod3, mod3]
    in_specs += [vec, vec]
    args += [lng, lnb]
    out_specs = [row(d)] + ([row(d)] if with_next else [])
    out_shape = [jax.ShapeDtypeStruct((m, d), _F32)] + ([jax.ShapeDtypeStruct((m, d), _BF)] if with_next else [])
    out = pl.pallas_call(
        functools.partial(_down_ln_body, alpha=alpha, with_next=with_next),
        grid=(m // tm,),
        in_specs=in_specs, out_specs=out_specs, out_shape=out_shape,
        compiler_params=_cparams(("arbitrary",)),
    )(*args)
    return (out[0], out[1]) if with_next else (out[0], None)


def _glu_body(h_ref, wa_ref, wu_ref, cw_ref, cb_ref, o_ref, a_scr, *, t, rc):
    nchunk = t // rc
    halo = 8
    tn = o_ref.shape[1]
    zeros = jnp.zeros((halo, tn), _F32)
    a_scr[0:halo, :] = zeros
    a_scr[halo + t:, :] = zeros
    w0, w1, w2 = cw_ref[0:1, :], cw_ref[1:2, :], cw_ref[2:3, :]
    bias = cb_ref[...]
    u_prev = None
    for c in range(nchunk + 1):
        u_cur = None
        if c < nchunk:
            hc = h_ref[c * rc:(c + 1) * rc, :]
            a_scr[halo + c * rc:halo + (c + 1) * rc, :] = _dot(hc, wa_ref[...])
            u_cur = _dot(hc, wu_ref[...])
        if c > 0:
            base = halo + (c - 1) * rc
            lo = a_scr[base - 1:base - 1 + rc, :]
            mid = a_scr[base:base + rc, :]
            hi = a_scr[base + 1:base + 1 + rc, :]
            acc = bias + lo * w0 + mid * w1 + hi * w2
            o_ref[(c - 1) * rc:c * rc, :] = (_silu(acc) * u_prev).astype(o_ref.dtype)
        u_prev = u_cur


def _glu(h, w_up, conv_w, conv_b, b, t, tn):
    d = h.shape[1]
    dff = conv_w.shape[1]
    nj = dff // tn
    rc = _tile(t, 256)
    return pl.pallas_call(
        functools.partial(_glu_body, t=t, rc=rc),
        grid=(nj, b),
        in_specs=[pl.BlockSpec((t, d), lambda j, bi: (bi, 0)),
                  pl.BlockSpec((d, tn), lambda j, bi: (0, j)),
                  pl.BlockSpec((d, tn), lambda j, bi: (0, nj + j)),
                  pl.BlockSpec((CONV_W, tn), lambda j, bi: (0, j)),
                  pl.BlockSpec((1, tn), lambda j, bi: (0, j))],
        out_specs=pl.BlockSpec((t, tn), lambda j, bi: (bi, j)),
        out_shape=jax.ShapeDtypeStruct((b * t, dff), _BF),
        scratch_shapes=[pltpu.VMEM((t + 16, tn), _F32)],
        compiler_params=_cparams(("arbitrary", "arbitrary")),
    )(h, w_up, w_up, conv_w, conv_b)


def _pack_w_in(w):
    d = w.shape[0]
    na_q, na_k, na_v, cq, ckv, kpe, r_q, r_k, r_v, r_g = jnp.split(
        w, [int(i) for i in np.cumsum(IN_SPLITS)[:-1]], axis=-1)
    scale = HEAD_DIM ** -0.5
    pad = jnp.zeros((d, PACKED_IN_WIDTH - (CB_KPE * LANE + MLA_ROPE)), w.dtype)
    return jnp.concatenate([cq, ckv, na_q * scale, na_k, na_v, r_q, r_k * scale, r_v, r_g, kpe, pad], -1).astype(_BF)


def _pack_w_uq(w):
    r = w.shape[0]
    w = w.reshape(r, MLA_HEADS, MLA_NOPE + MLA_ROPE) * ((MLA_NOPE + MLA_ROPE) ** -0.5)
    w = jnp.pad(w, ((0, 0), (0, 0), (0, MLA_QK_PAD - MLA_NOPE - MLA_ROPE)))
    return w.reshape(r, MLA_HEADS * MLA_QK_PAD).astype(_BF)


def kernel(x, c, ctx, c_ctx, w_ada, b_ada, w_in, mla_q_norm, mla_kv_norm, w_uq, w_ukv, na_rpb, ret_decay,
           w_o, ln1_g, ln1_b, w_up, conv_w, conv_b, w_down, ln2_g, ln2_b):
    b, t, d = x.shape
    cl = ctx.shape[1]
    depth = w_ada.shape[0]
    dff = conv_w.shape[-1]
    assert b + 1 <= MOD_ROWS
    alpha = (2 * depth) ** 0.25
    ctx_row = b

    cvec = jnp.concatenate([c, c_ctx[None], jnp.zeros((MOD_ROWS - b - 1, d), c.dtype)], 0)
    mod3 = _ada(cvec, w_ada, b_ada)

    tm_l = _tile(t, 1024)
    tm_c = _tile(cl, 1024)
    lat_row = lambda tm: (lambda i: i // (t // tm))
    ctx_rowf = lambda i: ctx_row

    rope_pe = _rope_tables(t, MLA_ROPE)
    rope_ret = _rope_tables(t, HEAD_DIM)
    id_pe = _rope_tables(cl, MLA_ROPE, identity=True)
    id_ret = _rope_tables(cl, HEAD_DIM, identity=True)

    xf = x.reshape(b * t, d)
    cf = ctx.reshape(b * cl, d)
    h1 = _modulate(xf, mod3, 0, lat_row(tm_l), tm_l)
    hc1 = _modulate(cf, mod3, 0, ctx_rowf, tm_c)

    tq = _tile(t, 256)
    tqc = _tile(cl, 256)
    tn_ff = _tile(dff, 512)
    zero_state = jnp.zeros((b, RET_HEADS, HEAD_DIM, HEAD_DIM), _F32)

    for l in range(depth):
        last = l == depth - 1
        w_in_p = _pack_w_in(w_in[l])
        w_uq_p = _pack_w_uq(w_uq[l])
        w_ukv_p = w_ukv[l].astype(_BF)
        gq = mla_q_norm[l][None]
        gkv = mla_kv_norm[l][None]
        lg = jnp.log1p(-jnp.exp2(ret_decay[l].astype(_F32)))
        bias = _na_bias_table(na_rpb[l], t // GRID_W)

        p = _mm(h1, w_in_p, tm_l, 1024)
        pc = _mm(hc1, w_in_p, tm_c, 1024)

        kc_m, vc_m = _mla_kv(pc, gkv, w_ukv_p, id_pe, b, cl, _tile(cl, 512))
        kl_m, vl_m = _mla_kv(p, gkv, w_ukv_p, rope_pe, b, t, _tile(t, 512))
        q_m = _mla_q(p, gq, w_uq_p, rope_pe, b, t, _tile(t, 512))
        y_mla = _dense_attn(
            q_m, pl.BlockSpec((None, None, tq, MLA_QK_PAD), lambda bi, h, i: (bi, h, i, 0)),
            [(kc_m, _head4_spec(cl, MLA_QK_PAD), vc_m, _head4_spec(cl, MLA_V)),
             (kl_m, _head4_spec(t, MLA_QK_PAD), vl_m, _head4_spec(t, MLA_V))],
            b, MLA_HEADS, t // tq, tq, MLA_V)

        y_na = _na_attn(p, pc, bias, b, t, cl)

        yc_ret, s_f, s_b = _retention(pc, lg, zero_state, zero_state, id_ret, b, cl)
        y_ret, _, _ = _retention(p, lg, s_f, s_b, rope_ret, b, t)

        tm_o = _tile(t, 256)
        x_new, h2 = _wo_ln(y_na, y_mla, y_ret, w_o[l].astype(_BF), xf, mod3, l, lat_row(tm_o),
                           ln1_g[l][None], ln1_b[l][None], alpha, tm_o)

        w_up_b = w_up[l].astype(_BF)
        w_down_b = w_down[l].astype(_BF)
        tm_d = _tile(t, 256)

        if not last:
            qc_m = _mla_q(pc, gq, w_uq_p, id_pe, b, cl, _tile(cl, 512))
            yc_mla = _dense_attn(
                qc_m, pl.BlockSpec((None, None, tqc, MLA_QK_PAD), lambda bi, h, i: (bi, h, i, 0)),
                [(kc_m, _head4_spec(cl, MLA_QK_PAD), vc_m, _head4_spec(cl, MLA_V))],
                b, MLA_HEADS, cl // tqc, tqc, MLA_V)
            nbc = cl // tqc
            yc_na = _dense_attn(
                pc, pl.BlockSpec((tqc, LANE), lambda bi, h, i: (bi * nbc + i, CB_NAQ + h)),
                [(pc, pl.BlockSpec((cl, LANE), lambda bi, h, i: (bi, CB_NAK + h)),
                  pc, pl.BlockSpec((cl, LANE), lambda bi, h, i: (bi, CB_NAV + h)))],
                b, NA_HEADS, nbc, tqc, HEAD_DIM)
            tm_oc = _tile(cl, 256)
            c_new, hc2 = _wo_ln(yc_na, yc_mla, yc_ret, w_o[l].astype(_BF), cf, mod3, l, ctx_rowf,
                                ln1_g[l][None], ln1_b[l][None], alpha, tm_oc)

        g_l = _glu(h2, w_up_b, conv_w[l], conv_b[l][None], b, t, tn_ff)
        xf, h1 = _down_ln(g_l, w_down_b, x_new, mod3, l, lat_row(tm_d), ln2_g[l][None], ln2_b[l][None],
                          alpha, tm_d, with_next=not last)
        if not last:
            g_c = _glu(hc2, w_up_b, conv_w[l], conv_b[l][None], b, cl, tn_ff)
            tm_dc = _tile(cl, 256)
            cf, hc1 = _down_ln(g_c, w_down_b, c_new, mod3, l, ctx_rowf, ln2_g[l][None], ln2_b[l][None],
                               alpha, tm_dc, with_next=True)

    return xf.reshape(b, t, d)
```

```python
import functools

import numpy as np
import jax
import jax.numpy as jnp
from jax import lax
from jax.experimental import pallas as pl
from jax.experimental.pallas import tpu as pltpu

GRID_W = 64
HEAD_DIM = 128
NA_HEADS = 6
MLA_HEADS = 5
RET_HEADS = 5
NA_WIDTH = NA_HEADS * HEAD_DIM
MLA_WIDTH = MLA_HEADS * HEAD_DIM
RET_WIDTH = RET_HEADS * HEAD_DIM
NA_WIN_R = 8
NA_WIN_C = 16
MLA_Q_RANK = 512
MLA_KV_RANK = 512
MLA_NOPE = 128
MLA_ROPE = 64
MLA_V = 128
MLA_QK_PAD = 256
RET_CHUNK = 128
CONV_W = 3
ROPE_BASE = 10000.0
LN_EPS = 1e-5
RMS_EPS = 1e-6
NEG_INF = -1e30
IN_SPLITS = (NA_WIDTH, NA_WIDTH, NA_WIDTH, MLA_Q_RANK, MLA_KV_RANK, MLA_ROPE,
             RET_WIDTH, RET_WIDTH, RET_WIDTH, RET_WIDTH)

LANE = 128
CB_CQ, CB_CKV, CB_NAQ, CB_NAK, CB_NAV = 0, 4, 8, 14, 20
CB_RQ, CB_RK, CB_RV, CB_RG, CB_KPE = 26, 31, 36, 41, 46
PACKED_IN_WIDTH = 48 * LANE
MOD_ROWS = 8
VMEM_LIMIT_MB = 56

_BF = jnp.bfloat16
_F32 = jnp.float32


def _cparams(sem, vmem_mb=VMEM_LIMIT_MB):
    return pltpu.CompilerParams(dimension_semantics=sem, vmem_limit_bytes=vmem_mb << 20)


def _tile(n, pref):
    t = min(n, pref)
    while n % t:
        t //= 2
    return t


def _dot(a, b):
    return jnp.dot(a, b, preferred_element_type=_F32)


def _dot_nt(a, b):
    return lax.dot_general(a, b, (((1,), (1,)), ((), ())), preferred_element_type=_F32)


def _silu(x):
    return x / (1.0 + jnp.exp(-x))


def _resident_spec(shape):
    return pl.BlockSpec(shape, lambda *_: (0,) * len(shape), pipeline_mode=pl.Buffered(1))


def _ada_body(c_ref, w_ref, b_ref, o_ref):
    s = _silu(c_ref[...]).astype(_BF)
    o_ref[0] = _dot(s, w_ref[0].astype(_BF)) + b_ref[0]


def _ada(cvec, w_ada, b_ada):
    depth, d, n = w_ada.shape
    tn = _tile(n, 1024)
    out = pl.pallas_call(
        _ada_body,
        grid=(depth, n // tn),
        in_specs=[pl.BlockSpec((MOD_ROWS, d), lambda l, j: (0, 0)),
                  pl.BlockSpec((1, d, tn), lambda l, j: (l, 0, j)),
                  pl.BlockSpec((1, 1, tn), lambda l, j: (l, 0, j))],
        out_specs=pl.BlockSpec((1, MOD_ROWS, tn), lambda l, j: (l, 0, j)),
        out_shape=jax.ShapeDtypeStruct((depth, MOD_ROWS, n), _F32),
        compiler_params=_cparams(("arbitrary", "arbitrary")),
    )(cvec, w_ada, b_ada.reshape(depth, 1, n))
    return out.reshape(depth * MOD_ROWS, 1, n)


def _mod_spec(d, layer, chunk, row_fn):
    return pl.BlockSpec((1, 1, d), lambda i, *_: (layer * MOD_ROWS + row_fn(i), 0, chunk))


def _modulate_body(x_ref, sh_ref, sc_ref, o_ref):
    o_ref[...] = (x_ref[...] * (1.0 + sc_ref[0]) + sh_ref[0]).astype(o_ref.dtype)


def _modulate(x2, mod3, layer, row_fn, tm):
    m, d = x2.shape
    return pl.pallas_call(
        _modulate_body,
        grid=(m // tm,),
        in_specs=[pl.BlockSpec((tm, d), lambda i: (i, 0)),
                  _mod_spec(d, layer, 0, row_fn), _mod_spec(d, layer, 1, row_fn)],
        out_specs=pl.BlockSpec((tm, d), lambda i: (i, 0)),
        out_shape=jax.ShapeDtypeStruct((m, d), _BF),
        compiler_params=_cparams(("arbitrary",)),
    )(x2, mod3, mod3)


def _mm_body(a_ref, w_ref, o_ref):
    o_ref[...] = _dot(a_ref[...], w_ref[...]).astype(o_ref.dtype)


def _mm(a, w, tm, tn):
    m, k = a.shape
    n = w.shape[1]
    return pl.pallas_call(
        _mm_body,
        grid=(m // tm, n // tn),
        in_specs=[pl.BlockSpec((tm, k), lambda i, j: (i, 0)),
                  pl.BlockSpec((k, tn), lambda i, j: (0, j))],
        out_specs=pl.BlockSpec((tm, tn), lambda i, j: (i, j)),
        out_shape=jax.ShapeDtypeStruct((m, n), _BF),
        compiler_params=_cparams(("arbitrary", "arbitrary")),
    )(a, w)


def _rope_tables(t, rot_dim, identity=False):
    pos = jnp.arange(t)
    row = (pos // GRID_W).astype(_F32)
    col = (pos % GRID_W).astype(_F32)
    nf = rot_dim // 4
    inv = ROPE_BASE ** (-jnp.arange(nf, dtype=_F32) / nf)
    ar = row[:, None] * inv[None]
    ac = col[:, None] * inv[None]
    ang = jnp.concatenate([ar, ar, ac, ac], -1)
    cos, sin = jnp.cos(ang), jnp.sin(ang)
    if identity:
        cos, sin = jnp.ones_like(cos), jnp.zeros_like(sin)
    first = (np.arange(rot_dim) % (rot_dim // 2)) < nf
    sin_a = jnp.where(first, -sin, 0.0)
    sin_b = jnp.where(first, 0.0, sin)
    pad = LANE - rot_dim
    if pad:
        cos = jnp.pad(cos, ((0, 0), (0, pad)), constant_values=1.0)
        sin_a = jnp.pad(sin_a, ((0, 0), (0, pad)))
        sin_b = jnp.pad(sin_b, ((0, 0), (0, pad)))
    return cos, sin_a, sin_b


def _rope(x, cos, sin_a, sin_b, nf):
    return x * cos + pltpu.roll(x, LANE - nf, 1) * sin_a + pltpu.roll(x, nf, 1) * sin_b


def _rms(x, g):
    return x * lax.rsqrt(jnp.mean(x * x, -1, keepdims=True) + RMS_EPS) * g


def _mla_q_body(p_ref, g_ref, w_ref, cos_ref, sa_ref, sb_ref, o_ref):
    xb = _rms(p_ref[...].astype(_F32), g_ref[...]).astype(_BF)
    cos, sa, sb = cos_ref[...], sa_ref[...], sb_ref[...]
    for h in range(MLA_HEADS):
        acc = _dot(xb, w_ref[:, h * MLA_QK_PAD:(h + 1) * MLA_QK_PAD])
        o_ref[0, h, :, 0:LANE] = acc[:, :LANE].astype(_BF)
        o_ref[0, h, :, LANE:] = _rope(acc[:, LANE:], cos, sa, sb, MLA_ROPE // 4).astype(_BF)


def _mla_q(p, g, w, tables, b, t, tm):
    nb = t // tm
    tab_spec = pl.BlockSpec((tm, LANE), lambda i: (i % nb, 0))
    return pl.pallas_call(
        _mla_q_body,
        grid=(b * nb,),
        in_specs=[pl.BlockSpec((tm, MLA_Q_RANK), lambda i: (i, CB_CQ // 4)),
                  pl.BlockSpec((1, MLA_Q_RANK), lambda i: (0, 0)),
                  pl.BlockSpec(w.shape, lambda i: (0, 0)),
                  tab_spec, tab_spec, tab_spec],
        out_specs=pl.BlockSpec((1, MLA_HEADS, tm, MLA_QK_PAD), lambda i: (i // nb, 0, i % nb, 0)),
        out_shape=jax.ShapeDtypeStruct((b, MLA_HEADS, t, MLA_QK_PAD), _BF),
        compiler_params=_cparams(("arbitrary",)),
    )(p, g, w, *tables)


def _mla_kv_body(ckv_ref, kpe_ref, g_ref, w_ref, cos_ref, sa_ref, sb_ref, k_ref, v_ref):
    xb = _rms(ckv_ref[...].astype(_F32), g_ref[...]).astype(_BF)
    pe = _rope(kpe_ref[...].astype(_F32), cos_ref[...], sa_ref[...], sb_ref[...], MLA_ROPE // 4).astype(_BF)
    width = MLA_NOPE + MLA_V
    for h in range(MLA_HEADS):
        acc = _dot(xb, w_ref[:, h * width:(h + 1) * width])
        k_ref[0, h, :, 0:LANE] = acc[:, :MLA_NOPE].astype(_BF)
        k_ref[0, h, :, LANE:] = pe
        v_ref[0, h] = acc[:, MLA_NOPE:].astype(_BF)


def _mla_kv(p, g, w, tables, b, t, tm):
    nb = t // tm
    tab_spec = pl.BlockSpec((tm, LANE), lambda i: (i % nb, 0))
    return pl.pallas_call(
        _mla_kv_body,
        grid=(b * nb,),
        in_specs=[pl.BlockSpec((tm, MLA_KV_RANK), lambda i: (i, CB_CKV // 4)),
                  pl.BlockSpec((tm, LANE), lambda i: (i, CB_KPE)),
                  pl.BlockSpec((1, MLA_KV_RANK), lambda i: (0, 0)),
                  pl.BlockSpec(w.shape, lambda i: (0, 0)),
                  tab_spec, tab_spec, tab_spec],
        out_specs=[pl.BlockSpec((1, MLA_HEADS, tm, MLA_QK_PAD), lambda i: (i // nb, 0, i % nb, 0)),
                   pl.BlockSpec((1, MLA_HEADS, tm, MLA_V), lambda i: (i // nb, 0, i % nb, 0))],
        out_shape=[jax.ShapeDtypeStruct((b, MLA_HEADS, t, MLA_QK_PAD), _BF),
                   jax.ShapeDtypeStruct((b, MLA_HEADS, t, MLA_V), _BF)],
        compiler_params=_cparams(("arbitrary",)),
    )(p, p, g, w, *tables)


def _attn_body(*refs, nseg):
    q = refs[0][...]
    o_ref = refs[-1]
    s = [_dot_nt(q, refs[1 + 2 * i][...]) for i in range(nseg)]
    m = functools.reduce(jnp.maximum, [jnp.max(si, -1, keepdims=True) for si in s])
    p = [jnp.exp(si - m) for si in s]
    l = functools.reduce(jnp.add, [jnp.sum(pi, -1, keepdims=True) for pi in p])
    acc = functools.reduce(jnp.add, [_dot(p[i].astype(_BF), refs[2 + 2 * i][...]) for i in range(nseg)])
    o_ref[...] = (acc / l).astype(o_ref.dtype)


def _dense_attn(q, q_spec, kv, b, heads, tq_blocks, tq, dv):
    args, specs = [q], [q_spec]
    for k, ks, v, vs in kv:
        args += [k, v]
        specs += [ks, vs]
    return pl.pallas_call(
        functools.partial(_attn_body, nseg=len(kv)),
        grid=(b, heads, tq_blocks),
        in_specs=specs,
        out_specs=pl.BlockSpec((tq, dv), lambda bi, h, i: (bi * tq_blocks + i, h)),
        out_shape=jax.ShapeDtypeStruct((b * tq_blocks * tq, heads * dv), _BF),
        compiler_params=_cparams(("arbitrary", "arbitrary", "arbitrary")),
    )(*args)


def _head4_spec(n, d):
    return pl.BlockSpec((None, None, n, d), lambda bi, h, i: (bi, h, 0, 0))


def _na_bias_table(rpb, rows):
    c = np.arange(GRID_W)
    col_start = np.clip(c - NA_WIN_C // 2, 0, GRID_W - NA_WIN_C)
    kc = np.arange(GRID_W)
    in_win = (kc[None, :] >= col_start[:, None]) & (kc[None, :] < col_start[:, None] + NA_WIN_C)
    dc = np.clip(kc[None, :] - c[:, None] + NA_WIN_C - 1, 0, 2 * NA_WIN_C - 2)
    onehot = (dc[:, :, None] == np.arange(2 * NA_WIN_C - 1)[None, None, :]).astype(np.float32)
    toep = jnp.einsum('hrj,ckj->hrck', rpb.astype(_F32), onehot, precision=lax.Precision.HIGHEST)
    toep = jnp.where(in_win[None, None], toep, NEG_INF)
    tab = jnp.stack([toep[:, NA_WIN_R - 1 - d:2 * NA_WIN_R - 1 - d] for d in range(NA_WIN_R)], 0)
    h = rpb.shape[0]
    return tab.transpose(0, 1, 3, 2, 4).reshape(NA_WIN_R, h, GRID_W, NA_WIN_R * GRID_W)


def _na_body(q_ref, k_ref, v_ref, kc_ref, vc_ref, bias_ref, o_ref, *, rows):
    kc = kc_ref[...]
    vc = vc_ref[...]
    nwin = NA_WIN_R * GRID_W

    def row(r, carry):
        r0 = jnp.clip(r - NA_WIN_R // 2, 0, rows - NA_WIN_R)
        q = q_ref[pl.ds(pl.multiple_of(r * GRID_W, GRID_W), GRID_W), :]
        start = pl.multiple_of(r0 * GRID_W, GRID_W)
        kw = k_ref[pl.ds(start, nwin), :]
        vw = v_ref[pl.ds(start, nwin), :]
        s_loc = _dot_nt(q, kw) + bias_ref[r - r0]
        s_ctx = _dot_nt(q, kc)
        m = jnp.maximum(jnp.max(s_loc, -1, keepdims=True), jnp.max(s_ctx, -1, keepdims=True))
        p_loc = jnp.exp(s_loc - m)
        p_ctx = jnp.exp(s_ctx - m)
        l = jnp.sum(p_loc, -1, keepdims=True) + jnp.sum(p_ctx, -1, keepdims=True)
        acc = _dot(p_loc.astype(_BF), vw) + _dot(p_ctx.astype(_BF), vc)
        o_ref[pl.ds(pl.multiple_of(r * GRID_W, GRID_W), GRID_W), :] = (acc / l).astype(o_ref.dtype)
        return carry

    lax.fori_loop(0, rows, row, 0, unroll=4)


def _na_attn(p, pc, bias, b, t, c):
    rows = t // GRID_W
    assert rows >= NA_WIN_R and t % GRID_W == 0
    return pl.pallas_call(
        functools.partial(_na_body, rows=rows),
        grid=(NA_HEADS, b),
        in_specs=[pl.BlockSpec((t, LANE), lambda h, bi: (bi, CB_NAQ + h)),
                  pl.BlockSpec((t, LANE), lambda h, bi: (bi, CB_NAK + h)),
                  pl.BlockSpec((t, LANE), lambda h, bi: (bi, CB_NAV + h)),
                  pl.BlockSpec((c, LANE), lambda h, bi: (bi, CB_NAK + h)),
                  pl.BlockSpec((c, LANE), lambda h, bi: (bi, CB_NAV + h)),
                  pl.BlockSpec((NA_WIN_R, None, GRID_W, NA_WIN_R * GRID_W), lambda h, bi: (0, h, 0, 0))],
        out_specs=pl.BlockSpec((t, LANE), lambda h, bi: (bi, h)),
        out_shape=jax.ShapeDtypeStruct((b * t, NA_WIDTH), _BF),
        compiler_params=_cparams(("arbitrary", "arbitrary")),
    )(p, p, p, pc, pc, bias)


def _ret_body(lg_ref, q_ref, k_ref, v_ref, g_ref, s0f_ref, s0b_ref, cos_ref, sa_ref, sb_ref,
              o_ref, sf_ref, sbo_ref, sb_hist, *, n):
    h = pl.program_id(0)
    lgf = lg_ref[0, h]
    lgb = lg_ref[1, h]
    L = RET_CHUNK
    pos_c = lax.broadcasted_iota(jnp.int32, (L, 1), 0).astype(_F32)
    pos_r = lax.broadcasted_iota(jnp.int32, (1, L), 1).astype(_F32)
    diff = pos_c - pos_r
    decay = jnp.where(diff > 0, jnp.exp(jnp.maximum(diff, 0.0) * lgf),
                      jnp.where(diff < 0, jnp.exp(jnp.maximum(-diff, 0.0) * lgb), 2.0))
    qdec_f = jnp.exp((pos_c + 1.0) * lgf)
    kdec_f = jnp.exp((L - 1.0 - pos_c) * lgf)
    qdec_b = jnp.exp((L - pos_c) * lgb)
    kdec_b = jnp.exp(pos_c * lgb)
    cd_f = jnp.exp(L * lgf)
    cd_b = jnp.exp(L * lgb)
    nf = HEAD_DIM // 4

    def load(ref, i, rope):
        rows = pl.ds(pl.multiple_of(i * L, L), L)
        x = ref[rows, :].astype(_F32)
        if rope:
            x = _rope(x, cos_ref[rows, :], sa_ref[rows, :], sb_ref[rows, :], nf)
        return x

    def bwd(j, state):
        i = n - 1 - j
        sb_hist[i] = state
        k = load(k_ref, i, True)
        v = v_ref[pl.ds(pl.multiple_of(i * L, L), L), :]
        return cd_b * state + _dot((k * kdec_b).T.astype(_BF), v)

    sbo_ref[...] = lax.fori_loop(0, n, bwd, s0b_ref[...], unroll=min(n, 4))

    def fwd(i, state):
        rows = pl.ds(pl.multiple_of(i * L, L), L)
        q = load(q_ref, i, True)
        k = load(k_ref, i, True)
        v = v_ref[rows, :]
        a = _dot_nt(q.astype(_BF), k.astype(_BF)) * decay
        o = _dot(a.astype(_BF), v)
        o = o + _dot((q * qdec_f).astype(_BF), state.astype(_BF))
        o = o + _dot((q * qdec_b).astype(_BF), sb_hist[i].astype(_BF))
        mu = jnp.mean(o, -1, keepdims=True)
        oc = o - mu
        var = jnp.mean(oc * oc, -1, keepdims=True)
        y = oc * lax.rsqrt(var + LN_EPS) * _silu(g_ref[rows, :].astype(_F32))
        o_ref[rows, :] = y.astype(o_ref.dtype)
        return cd_f * state + _dot((k * kdec_f).T.astype(_BF), v)

    sf_ref[...] = lax.fori_loop(0, n, fwd, s0f_ref[...], unroll=min(n, 4))


def _retention(p, lg, s0f, s0b, tables, b, t):
    n = t // RET_CHUNK
    col = lambda cb: pl.BlockSpec((t, LANE), lambda h, bi: (bi, cb + h))
    st_spec = pl.BlockSpec((None, None, HEAD_DIM, HEAD_DIM), lambda h, bi: (bi, h, 0, 0))
    tab_spec = pl.BlockSpec((t, LANE), lambda h, bi: (0, 0))
    st_shape = jax.ShapeDtypeStruct((b, RET_HEADS, HEAD_DIM, HEAD_DIM), _F32)
    return pl.pallas_call(
        functools.partial(_ret_body, n=n),
        grid=(RET_HEADS, b),
        in_specs=[pl.BlockSpec(memory_space=pltpu.SMEM),
                  col(CB_RQ), col(CB_RK), col(CB_RV), col(CB_RG), st_spec, st_spec,
                  tab_spec, tab_spec, tab_spec],
        out_specs=[pl.BlockSpec((t, LANE), lambda h, bi: (bi, h)), st_spec, st_spec],
        out_shape=[jax.ShapeDtypeStruct((b * t, RET_WIDTH), _BF), st_shape, st_shape],
        scratch_shapes=[pltpu.VMEM((n, HEAD_DIM, HEAD_DIM), _F32)],
        compiler_params=_cparams(("arbitrary", "arbitrary")),
    )(lg, p, p, p, p, s0f, s0b, *tables)


def _ln_epilogue(y, x_ref, gate_ref, lng_ref, lnb_ref, alpha, xo_ref, ho_ref, sh_ref, sc_ref):
    z = alpha * x_ref[...] + gate_ref[0] * y
    mu = jnp.mean(z, -1, keepdims=True)
    zc = z - mu
    var = jnp.mean(zc * zc, -1, keepdims=True)
    xn = zc * lax.rsqrt(var + LN_EPS) * lng_ref[...] + lnb_ref[...]
    xo_ref[...] = xn
    if ho_ref is not None:
        ho_ref[...] = (xn * (1.0 + sc_ref[0]) + sh_ref[0]).astype(ho_ref.dtype)


def _wo_ln_body(a1_ref, a2_ref, a3_ref, w_ref, x_ref, gate_ref, sh_ref, sc_ref, lng_ref, lnb_ref,
                xo_ref, ho_ref, *, alpha):
    o1, o2 = NA_WIDTH, NA_WIDTH + MLA_WIDTH
    y = _dot(a1_ref[...], w_ref[0:o1, :]) + _dot(a2_ref[...], w_ref[o1:o2, :]) + _dot(a3_ref[...], w_ref[o2:, :])
    _ln_epilogue(y, x_ref, gate_ref, lng_ref, lnb_ref, alpha, xo_ref, ho_ref, sh_ref, sc_ref)


def _wo_ln(a1, a2, a3, w, x2, mod3, layer, row_fn, lng, lnb, alpha, tm):
    m, d = x2.shape
    row = lambda width: pl.BlockSpec((tm, width), lambda i: (i, 0))
    vec = pl.BlockSpec((1, d), lambda i: (0, 0))
    return pl.pallas_call(
        functools.partial(_wo_ln_body, alpha=alpha),
        grid=(m // tm,),
        in_specs=[row(NA_WIDTH), row(MLA_WIDTH), row(RET_WIDTH),
                  _resident_spec(w.shape),
                  row(d),
                  _mod_spec(d, layer, 2, row_fn), _mod_spec(d, layer, 3, row_fn), _mod_spec(d, layer, 4, row_fn),
                  vec, vec],
        out_specs=[row(d), row(d)],
        out_shape=[jax.ShapeDtypeStruct((m, d), _F32), jax.ShapeDtypeStruct((m, d), _BF)],
        compiler_params=_cparams(("arbitrary",)),
    )(a1, a2, a3, w, x2, mod3, mod3, mod3, lng, lnb)


def _down_ln_body(*refs, alpha, with_next):
    if with_next:
        g_ref, w_ref, x_ref, gate_ref, sh_ref, sc_ref, lng_ref, lnb_ref, xo_ref, ho_ref = refs
    else:
        g_ref, w_ref, x_ref, gate_ref, lng_ref, lnb_ref, xo_ref = refs
        sh_ref = sc_ref = ho_ref = None
    y = _dot(g_ref[...], w_ref[...])
    _ln_epilogue(y, x_ref, gate_ref, lng_ref, lnb_ref, alpha, xo_ref, ho_ref, sh_ref, sc_ref)


def _down_ln(g, w, x2, mod3, layer, row_fn, lng, lnb, alpha, tm, with_next):
    m, d = x2.shape
    dff = g.shape[1]
    row = lambda width: pl.BlockSpec((tm, width), lambda i: (i, 0))
    vec = pl.BlockSpec((1, d), lambda i: (0, 0))
    in_specs = [row(dff), _resident_spec(w.shape), row(d), _mod_spec(d, layer, 5, row_fn)]
    args = [g, w, x2, mod3]
    if with_next:
        in_specs += [_mod_spec(d, layer + 1, 0, row_fn), _mod_spec(d, layer + 1, 1, row_fn)]
        args += [mod3, mod3]
    in_specs += [vec, vec]
    args += [lng, lnb]
    out_specs = [row(d)] + ([row(d)] if with_next else [])
    out_shape = [jax.ShapeDtypeStruct((m, d), _F32)] + ([jax.ShapeDtypeStruct((m, d), _BF)] if with_next else [])
    out = pl.pallas_call(
        functools.partial(_down_ln_body, alpha=alpha, with_next=with_next),
        grid=(m // tm,),
        in_specs=in_specs, out_specs=out_specs, out_shape=out_shape,
        compiler_params=_cparams(("arbitrary",)),
    )(*args)
    return (out[0], out[1]) if with_next else (out[0], None)


def _glu_body(h_ref, wa_ref, wu_ref, cw_ref, cb_ref, o_ref, a_scr, *, t, rc):
    nchunk = t // rc
    halo = 8
    tn = o_ref.shape[1]
    zeros = jnp.zeros((halo, tn), _F32)
    a_scr[0:halo, :] = zeros
    a_scr[halo + t:, :] = zeros
    w0, w1, w2 = cw_ref[0:1, :], cw_ref[1:2, :], cw_ref[2:3, :]
    bias = cb_ref[...]
    u_prev = None
    for c in range(nchunk + 1):
        u_cur = None
        if c < nchunk:
            hc = h_ref[c * rc:(c + 1) * rc, :]
            a_scr[halo + c * rc:halo + (c + 1) * rc, :] = _dot(hc, wa_ref[...])
            u_cur = _dot(hc, wu_ref[...])
        if c > 0:
            base = halo + (c - 1) * rc
            lo = a_scr[base - 1:base - 1 + rc, :]
            mid = a_scr[base:base + rc, :]
            hi = a_scr[base + 1:base + 1 + rc, :]
            acc = bias + lo * w0 + mid * w1 + hi * w2
            o_ref[(c - 1) * rc:c * rc, :] = (_silu(acc) * u_prev).astype(o_ref.dtype)
        u_prev = u_cur


def _glu(h, w_up, conv_w, conv_b, b, t, tn):
    d = h.shape[1]
    dff = conv_w.shape[1]
    nj = dff // tn
    rc = _tile(t, 256)
    return pl.pallas_call(
        functools.partial(_glu_body, t=t, rc=rc),
        grid=(nj, b),
        in_specs=[pl.BlockSpec((t, d), lambda j, bi: (bi, 0)),
                  pl.BlockSpec((d, tn), lambda j, bi: (0, j)),
                  pl.BlockSpec((d, tn), lambda j, bi: (0, nj + j)),
                  pl.BlockSpec((CONV_W, tn), lambda j, bi: (0, j)),
                  pl.BlockSpec((1, tn), lambda j, bi: (0, j))],
        out_specs=pl.BlockSpec((t, tn), lambda j, bi: (bi, j)),
        out_shape=jax.ShapeDtypeStruct((b * t, dff), _BF),
        scratch_shapes=[pltpu.VMEM((t + 16, tn), _F32)],
        compiler_params=_cparams(("arbitrary", "arbitrary")),
    )(h, w_up, w_up, conv_w, conv_b)


def _pack_w_in(w):
    d = w.shape[0]
    na_q, na_k, na_v, cq, ckv, kpe, r_q, r_k, r_v, r_g = jnp.split(
        w, [int(i) for i in np.cumsum(IN_SPLITS)[:-1]], axis=-1)
    scale = HEAD_DIM ** -0.5
    pad = jnp.zeros((d, PACKED_IN_WIDTH - (CB_KPE * LANE + MLA_ROPE)), w.dtype)
    return jnp.concatenate([cq, ckv, na_q * scale, na_k, na_v, r_q, r_k * scale, r_v, r_g, kpe, pad], -1).astype(_BF)


def _pack_w_uq(w):
    r = w.shape[0]
    w = w.reshape(r, MLA_HEADS, MLA_NOPE + MLA_ROPE) * ((MLA_NOPE + MLA_ROPE) ** -0.5)
    w = jnp.pad(w, ((0, 0), (0, 0), (0, MLA_QK_PAD - MLA_NOPE - MLA_ROPE)))
    return w.reshape(r, MLA_HEADS * MLA_QK_PAD).astype(_BF)


def kernel(x, c, ctx, c_ctx, w_ada, b_ada, w_in, mla_q_norm, mla_kv_norm, w_uq, w_ukv, na_rpb, ret_decay,
           w_o, ln1_g, ln1_b, w_up, conv_w, conv_b, w_down, ln2_g, ln2_b):
    b, t, d = x.shape
    cl = ctx.shape[1]
    depth = w_ada.shape[0]
    dff = conv_w.shape[-1]
    assert b + 1 <= MOD_ROWS
    alpha = (2 * depth) ** 0.25
    ctx_row = b

    cvec = jnp.concatenate([c, c_ctx[None], jnp.zeros((MOD_ROWS - b - 1, d), c.dtype)], 0)
    mod3 = _ada(cvec, w_ada, b_ada)

    tm_l = _tile(t, 1024)
    tm_c = _tile(cl, 1024)
    lat_row = lambda tm: (lambda i: i // (t // tm))
    ctx_rowf = lambda i: ctx_row

    rope_pe = _rope_tables(t, MLA_ROPE)
    rope_ret = _rope_tables(t, HEAD_DIM)
    id_pe = _rope_tables(cl, MLA_ROPE, identity=True)
    id_ret = _rope_tables(cl, HEAD_DIM, identity=True)

    xf = x.reshape(b * t, d)
    cf = ctx.reshape(b * cl, d)
    h1 = _modulate(xf, mod3, 0, lat_row(tm_l), tm_l)
    hc1 = _modulate(cf, mod3, 0, ctx_rowf, tm_c)

    tq = _tile(t, 256)
    tqc = _tile(cl, 256)
    tn_ff = _tile(dff, 512)
    zero_state = jnp.zeros((b, RET_HEADS, HEAD_DIM, HEAD_DIM), _F32)

    for l in range(depth):
        last = l == depth - 1
        w_in_p = _pack_w_in(w_in[l])
        w_uq_p = _pack_w_uq(w_uq[l])
        w_ukv_p = w_ukv[l].astype(_BF)
        gq = mla_q_norm[l][None]
        gkv = mla_kv_norm[l][None]
        lg = jnp.log1p(-jnp.exp2(ret_decay[l].astype(_F32)))
        bias = _na_bias_table(na_rpb[l], t // GRID_W)

        p = _mm(h1, w_in_p, tm_l, 1024)
        pc = _mm(hc1, w_in_p, tm_c, 1024)

        kc_m, vc_m = _mla_kv(pc, gkv, w_ukv_p, id_pe, b, cl, _tile(cl, 512))
        kl_m, vl_m = _mla_kv(p, gkv, w_ukv_p, rope_pe, b, t, _tile(t, 512))
        q_m = _mla_q(p, gq, w_uq_p, rope_pe, b, t, _tile(t, 512))
        y_mla = _dense_attn(
            q_m, pl.BlockSpec((None, None, tq, MLA_QK_PAD), lambda bi, h, i: (bi, h, i, 0)),
            [(kc_m, _head4_spec(cl, MLA_QK_PAD), vc_m, _head4_spec(cl, MLA_V)),
             (kl_m, _head4_spec(t, MLA_QK_PAD), vl_m, _head4_spec(t, MLA_V))],
            b, MLA_HEADS, t // tq, tq, MLA_V)

        y_na = _na_attn(p, pc, bias, b, t, cl)

        yc_ret, s_f, s_b = _retention(pc, lg, zero_state, zero_state, id_ret, b, cl)
        y_ret, _, _ = _retention(p, lg, s_f, s_b, rope_ret, b, t)

        tm_o = _tile(t, 256)
        x_new, h2 = _wo_ln(y_na, y_mla, y_ret, w_o[l].astype(_BF), xf, mod3, l, lat_row(tm_o),
                           ln1_g[l][None], ln1_b[l][None], alpha, tm_o)

        w_up_b = w_up[l].astype(_BF)
        w_down_b = w_down[l].astype(_BF)
        tm_d = _tile(t, 256)

        if not last:
            qc_m = _mla_q(pc, gq, w_uq_p, id_pe, b, cl, _tile(cl, 512))
            yc_mla = _dense_attn(
                qc_m, pl.BlockSpec((None, None, tqc, MLA_QK_PAD), lambda bi, h, i: (bi, h, i, 0)),
                [(kc_m, _head4_spec(cl, MLA_QK_PAD), vc_m, _head4_spec(cl, MLA_V))],
                b, MLA_HEADS, cl // tqc, tqc, MLA_V)
            nbc = cl // tqc
            yc_na = _dense_attn(
                pc, pl.BlockSpec((tqc, LANE), lambda bi, h, i: (bi * nbc + i, CB_NAQ + h)),
                [(pc, pl.BlockSpec((cl, LANE), lambda bi, h, i: (bi, CB_NAK + h)),
                  pc, pl.BlockSpec((cl, LANE), lambda bi, h, i: (bi, CB_NAV + h)))],
                b, NA_HEADS, nbc, tqc, HEAD_DIM)
            tm_oc = _tile(cl, 256)
            c_new, hc2 = _wo_ln(yc_na, yc_mla, yc_ret, w_o[l].astype(_BF), cf, mod3, l, ctx_rowf,
                                ln1_g[l][None], ln1_b[l][None], alpha, tm_oc)

        g_l = _glu(h2, w_up_b, conv_w[l], conv_b[l][None], b, t, tn_ff)
        xf, h1 = _down_ln(g_l, w_down_b, x_new, mod3, l, lat_row(tm_d), ln2_g[l][None], ln2_b[l][None],
                          alpha, tm_d, with_next=not last)
        if not last:
            g_c = _glu(hc2, w_up_b, conv_w[l], conv_b[l][None], b, cl, tn_ff)
            tm_dc = _tile(cl, 256)
            cf, hc1 = _down_ln(g_c, w_down_b, c_new, mod3, l, ctx_rowf, ln2_g[l][None], ln2_b[l][None],
                               alpha, tm_dc, with_next=True)

    return xf.reshape(b, t, d)
```

```python
import functools

import numpy as np
import jax
import jax.numpy as jnp
from jax import lax
from jax.experimental import pallas as pl
from jax.experimental.pallas import tpu as pltpu

GRID_W = 64
HEAD_DIM = 128
NA_HEADS = 6
MLA_HEADS = 5
RET_HEADS = 5
NA_WIDTH = NA_HEADS * HEAD_DIM
MLA_WIDTH = MLA_HEADS * HEAD_DIM
RET_WIDTH = RET_HEADS * HEAD_DIM
NA_WIN_R = 8
NA_WIN_C = 16
MLA_Q_RANK = 512
MLA_KV_RANK = 512
MLA_NOPE = 128
MLA_ROPE = 64
MLA_V = 128
MLA_QK_PAD = 256
RET_CHUNK = 128
CONV_W = 3
ROPE_BASE = 10000.0
LN_EPS = 1e-5
RMS_EPS = 1e-6
NEG_INF = -1e30
IN_SPLITS = (NA_WIDTH, NA_WIDTH, NA_WIDTH, MLA_Q_RANK, MLA_KV_RANK, MLA_ROPE,
             RET_WIDTH, RET_WIDTH, RET_WIDTH, RET_WIDTH)

LANE = 128
CB_CQ, CB_CKV, CB_NAQ, CB_NAK, CB_NAV = 0, 4, 8, 14, 20
CB_RQ, CB_RK, CB_RV, CB_RG, CB_KPE = 26, 31, 36, 41, 46
PACKED_IN_WIDTH = 48 * LANE
MOD_ROWS = 8
VMEM_LIMIT_MB = 56

_BF = jnp.bfloat16
_F32 = jnp.float32


def _cparams(sem, vmem_mb=VMEM_LIMIT_MB):
    return pltpu.CompilerParams(dimension_semantics=sem, vmem_limit_bytes=vmem_mb << 20)


def _tile(n, pref):
    t = min(n, pref)
    while n % t:
        t //= 2
    return t


def _dot(a, b):
    return jnp.dot(a, b, preferred_element_type=_F32)


def _dot_nt(a, b):
    return lax.dot_general(a, b, (((1,), (1,)), ((), ())), preferred_element_type=_F32)


def _silu(x):
    return x / (1.0 + jnp.exp(-x))


def _resident_spec(shape):
    return pl.BlockSpec(shape, lambda *_: (0,) * len(shape), pipeline_mode=pl.Buffered(1))


def _ada_body(c_ref, w_ref, b_ref, o_ref):
    s = _silu(c_ref[...]).astype(_BF)
    o_ref[0] = _dot(s, w_ref[0].astype(_BF)) + b_ref[0]


def _ada(cvec, w_ada, b_ada):
    depth, d, n = w_ada.shape
    tn = _tile(n, 1024)
    out = pl.pallas_call(
        _ada_body,
        grid=(depth, n // tn),
        in_specs=[pl.BlockSpec((MOD_ROWS, d), lambda l, j: (0, 0)),
                  pl.BlockSpec((1, d, tn), lambda l, j: (l, 0, j)),
                  pl.BlockSpec((1, 1, tn), lambda l, j: (l, 0, j))],
        out_specs=pl.BlockSpec((1, MOD_ROWS, tn), lambda l, j: (l, 0, j)),
        out_shape=jax.ShapeDtypeStruct((depth, MOD_ROWS, n), _F32),
        compiler_params=_cparams(("arbitrary", "arbitrary")),
    )(cvec, w_ada, b_ada.reshape(depth, 1, n))
    return out.reshape(depth * MOD_ROWS, 1, n)


def _mod_spec(d, layer, chunk, row_fn):
    return pl.BlockSpec((1, 1, d), lambda i, *_: (layer * MOD_ROWS + row_fn(i), 0, chunk))


def _modulate_body(x_ref, sh_ref, sc_ref, o_ref):
    o_ref[...] = (x_ref[...] * (1.0 + sc_ref[0]) + sh_ref[0]).astype(o_ref.dtype)


def _modulate(x2, mod3, layer, row_fn, tm):
    m, d = x2.shape
    return pl.pallas_call(
        _modulate_body,
        grid=(m // tm,),
        in_specs=[pl.BlockSpec((tm, d), lambda i: (i, 0)),
                  _mod_spec(d, layer, 0, row_fn), _mod_spec(d, layer, 1, row_fn)],
        out_specs=pl.BlockSpec((tm, d), lambda i: (i, 0)),
        out_shape=jax.ShapeDtypeStruct((m, d), _BF),
        compiler_params=_cparams(("arbitrary",)),
    )(x2, mod3, mod3)


def _mm_body(a_ref, w_ref, o_ref):
    o_ref[...] = _dot(a_ref[...], w_ref[...]).astype(o_ref.dtype)


def _mm(a, w, tm, tn):
    m, k = a.shape
    n = w.shape[1]
    return pl.pallas_call(
        _mm_body,
        grid=(m // tm, n // tn),
        in_specs=[pl.BlockSpec((tm, k), lambda i, j: (i, 0)),
                  pl.BlockSpec((k, tn), lambda i, j: (0, j))],
        out_specs=pl.BlockSpec((tm, tn), lambda i, j: (i, j)),
        out_shape=jax.ShapeDtypeStruct((m, n), _BF),
        compiler_params=_cparams(("arbitrary", "arbitrary")),
    )(a, w)


def _rope_tables(t, rot_dim, identity=False):
    pos = jnp.arange(t)
    row = (pos // GRID_W).astype(_F32)
    col = (pos % GRID_W).astype(_F32)
    nf = rot_dim // 4
    inv = ROPE_BASE ** (-jnp.arange(nf, dtype=_F32) / nf)
    ar = row[:, None] * inv[None]
    ac = col[:, None] * inv[None]
    ang = jnp.concatenate([ar, ar, ac, ac], -1)
    cos, sin = jnp.cos(ang), jnp.sin(ang)
    if identity:
        cos, sin = jnp.ones_like(cos), jnp.zeros_like(sin)
    first = (np.arange(rot_dim) % (rot_dim // 2)) < nf
    sin_a = jnp.where(first, -sin, 0.0)
    sin_b = jnp.where(first, 0.0, sin)
    pad = LANE - rot_dim
    if pad:
        cos = jnp.pad(cos, ((0, 0), (0, pad)), constant_values=1.0)
        sin_a = jnp.pad(sin_a, ((0, 0), (0, pad)))
        sin_b = jnp.pad(sin_b, ((0, 0), (0, pad)))
    return cos, sin_a, sin_b


def _rope(x, cos, sin_a, sin_b, nf):
    return x * cos + pltpu.roll(x, LANE - nf, 1) * sin_a + pltpu.roll(x, nf, 1) * sin_b


def _rms(x, g):
    return x * lax.rsqrt(jnp.mean(x * x, -1, keepdims=True) + RMS_EPS) * g


def _mla_q_body(p_ref, g_ref, w_ref, cos_ref, sa_ref, sb_ref, o_ref):
    xb = _rms(p_ref[...].astype(_F32), g_ref[...]).astype(_BF)
    cos, sa, sb = cos_ref[...], sa_ref[...], sb_ref[...]
    for h in range(MLA_HEADS):
        acc = _dot(xb, w_ref[:, h * MLA_QK_PAD:(h + 1) * MLA_QK_PAD])
        o_ref[0, h, :, 0:LANE] = acc[:, :LANE].astype(_BF)
        o_ref[0, h, :, LANE:] = _rope(acc[:, LANE:], cos, sa, sb, MLA_ROPE // 4).astype(_BF)


def _mla_q(p, g, w, tables, b, t, tm):
    nb = t // tm
    tab_spec = pl.BlockSpec((tm, LANE), lambda i: (i % nb, 0))
    return pl.pallas_call(
        _mla_q_body,
        grid=(b * nb,),
        in_specs=[pl.BlockSpec((tm, MLA_Q_RANK), lambda i: (i, CB_CQ // 4)),
                  pl.BlockSpec((1, MLA_Q_RANK), lambda i: (0, 0)),
                  pl.BlockSpec(w.shape, lambda i: (0, 0)),
                  tab_spec, tab_spec, tab_spec],
        out_specs=pl.BlockSpec((1, MLA_HEADS, tm, MLA_QK_PAD), lambda i: (i // nb, 0, i % nb, 0)),
        out_shape=jax.ShapeDtypeStruct((b, MLA_HEADS, t, MLA_QK_PAD), _BF),
        compiler_params=_cparams(("arbitrary",)),
    )(p, g, w, *tables)


def _mla_kv_body(ckv_ref, kpe_ref, g_ref, w_ref, cos_ref, sa_ref, sb_ref, k_ref, v_ref):
    xb = _rms(ckv_ref[...].astype(_F32), g_ref[...]).astype(_BF)
    pe = _rope(kpe_ref[...].astype(_F32), cos_ref[...], sa_ref[...], sb_ref[...], MLA_ROPE // 4).astype(_BF)
    width = MLA_NOPE + MLA_V
    for h in range(MLA_HEADS):
        acc = _dot(xb, w_ref[:, h * width:(h + 1) * width])
        k_ref[0, h, :, 0:LANE] = acc[:, :MLA_NOPE].astype(_BF)
        k_ref[0, h, :, LANE:] = pe
        v_ref[0, h] = acc[:, MLA_NOPE:].astype(_BF)


def _mla_kv(p, g, w, tables, b, t, tm):
    nb = t // tm
    tab_spec = pl.BlockSpec((tm, LANE), lambda i: (i % nb, 0))
    return pl.pallas_call(
        _mla_kv_body,
        grid=(b * nb,),
        in_specs=[pl.BlockSpec((tm, MLA_KV_RANK), lambda i: (i, CB_CKV // 4)),
                  pl.BlockSpec((tm, LANE), lambda i: (i, CB_KPE)),
                  pl.BlockSpec((1, MLA_KV_RANK), lambda i: (0, 0)),
                  pl.BlockSpec(w.shape, lambda i: (0, 0)),
                  tab_spec, tab_spec, tab_spec],
        out_specs=[pl.BlockSpec((1, MLA_HEADS, tm, MLA_QK_PAD), lambda i: (i // nb, 0, i % nb, 0)),
                   pl.BlockSpec((1, MLA_HEADS, tm, MLA_V), lambda i: (i // nb, 0, i % nb, 0))],
        out_shape=[jax.ShapeDtypeStruct((b, MLA_HEADS, t, MLA_QK_PAD), _BF),
                   jax.ShapeDtypeStruct((b, MLA_HEADS, t, MLA_V), _BF)],
        compiler_params=_cparams(("arbitrary",)),
    )(p, p, g, w, *tables)


def _softmax_pv(scores, values):
    m = functools.reduce(jnp.maximum, [jnp.max(s, -1, keepdims=True) for s in scores])
    p = [jnp.exp(s - m) for s in scores]
    l = functools.reduce(jnp.add, [jnp.sum(pi, -1, keepdims=True) for pi in p])
    acc = functools.reduce(jnp.add, [_dot(pi.astype(_BF), v) for pi, v in zip(p, values)])
    return acc / l


def _attn_body(*refs, nseg, nsub):
    q_ref, o_ref = refs[0], refs[-1]
    ks = [refs[1 + 2 * i][...] for i in range(nseg)]
    vs = [refs[2 + 2 * i][...] for i in range(nseg)]
    ts = q_ref.shape[0] // nsub
    for j in range(nsub):
        q = q_ref[j * ts:(j + 1) * ts, :]
        out = _softmax_pv([_dot_nt(q, k) for k in ks], vs)
        o_ref[j * ts:(j + 1) * ts, :] = out.astype(o_ref.dtype)


def _dense_attn(q, q_spec, kv, b, heads, tq_blocks, tq, dv, nsub):
    args, specs = [q], [q_spec]
    for k, ks, v, vs in kv:
        args += [k, v]
        specs += [ks, vs]
    return pl.pallas_call(
        functools.partial(_attn_body, nseg=len(kv), nsub=nsub),
        grid=(b, heads, tq_blocks),
        in_specs=specs,
        out_specs=pl.BlockSpec((tq, dv), lambda bi, h, i: (bi * tq_blocks + i, h)),
        out_shape=jax.ShapeDtypeStruct((b * tq_blocks * tq, heads * dv), _BF),
        compiler_params=_cparams(("arbitrary", "arbitrary", "arbitrary")),
    )(*args)


def _head4_spec(n, d):
    return pl.BlockSpec((None, None, n, d), lambda bi, h, i: (bi, h, 0, 0))


NA_GROUP = 4
NA_KEY_ROWS = 12


def _na_group_window(gi, rows):
    ngroups = rows // NA_GROUP
    if gi == 0:
        return 0, 0
    if gi == ngroups - 1:
        return 2, rows - NA_KEY_ROWS
    return 1, gi * NA_GROUP - NA_WIN_R // 2


def _na_bias_table(rpb):
    c = np.arange(GRID_W)
    col_start = np.clip(c - NA_WIN_C // 2, 0, GRID_W - NA_WIN_C)
    kc = np.arange(GRID_W)
    in_win = (kc[None, :] >= col_start[:, None]) & (kc[None, :] < col_start[:, None] + NA_WIN_C)
    dc = np.clip(kc[None, :] - c[:, None] + NA_WIN_C - 1, 0, 2 * NA_WIN_C - 2)
    col_sel = ((dc[:, :, None] == np.arange(2 * NA_WIN_C - 1)) & in_win[:, :, None]).astype(np.float32)
    g = np.arange(NA_GROUP)[:, None]
    kr = np.arange(NA_KEY_ROWS)[None, :]
    half = NA_WIN_R // 2
    dr = np.stack([kr - g + NA_WIN_R - 1, kr - g + NA_WIN_R - 1 - half, kr - g - 1])
    w0 = np.stack([0 * g, g, half + 0 * g])
    row_ok = (kr[None] >= w0) & (kr[None] < w0 + NA_WIN_R)
    row_sel = ((dr[..., None] == np.arange(2 * NA_WIN_R - 1)) & row_ok[..., None]).astype(np.float32)
    tab = jnp.einsum('tgkr,hrj,cqj->thgckq', row_sel, rpb.astype(_F32), col_sel, precision=lax.Precision.HIGHEST)
    valid = row_ok[:, None, :, None, :, None] & in_win[None, None, None, :, None, :]
    tab = jnp.where(valid, tab, NEG_INF)
    h = rpb.shape[0]
    return tab.reshape(3, h, NA_GROUP * GRID_W, NA_KEY_ROWS * GRID_W)


def _na_body(q_ref, k_ref, v_ref, kc_ref, vc_ref, bias_ref, o_ref, *, rows):
    kc = kc_ref[...]
    vc = vc_ref[...]
    gq = NA_GROUP * GRID_W
    for gi in range(rows // NA_GROUP):
        typ, ws = _na_group_window(gi, rows)
        q = q_ref[gi * gq:(gi + 1) * gq, :]
        kw = k_ref[ws * GRID_W:(ws + NA_KEY_ROWS) * GRID_W, :]
        vw = v_ref[ws * GRID_W:(ws + NA_KEY_ROWS) * GRID_W, :]
        out = _softmax_pv([_dot_nt(q, kw) + bias_ref[typ], _dot_nt(q, kc)], [vw, vc])
        o_ref[gi * gq:(gi + 1) * gq, :] = out.astype(o_ref.dtype)


def _na_attn(p, pc, bias, b, t, c):
    rows = t // GRID_W
    assert t % GRID_W == 0 and rows % NA_GROUP == 0 and rows >= NA_KEY_ROWS + NA_GROUP
    return pl.pallas_call(
        functools.partial(_na_body, rows=rows),
        grid=(NA_HEADS, b),
        in_specs=[pl.BlockSpec((t, LANE), lambda h, bi: (bi, CB_NAQ + h)),
                  pl.BlockSpec((t, LANE), lambda h, bi: (bi, CB_NAK + h)),
                  pl.BlockSpec((t, LANE), lambda h, bi: (bi, CB_NAV + h)),
                  pl.BlockSpec((c, LANE), lambda h, bi: (bi, CB_NAK + h)),
                  pl.BlockSpec((c, LANE), lambda h, bi: (bi, CB_NAV + h)),
                  pl.BlockSpec((3, None, NA_GROUP * GRID_W, NA_KEY_ROWS * GRID_W), lambda h, bi: (0, h, 0, 0))],
        out_specs=pl.BlockSpec((t, LANE), lambda h, bi: (bi, h)),
        out_shape=jax.ShapeDtypeStruct((b * t, NA_WIDTH), _BF),
        compiler_params=_cparams(("arbitrary", "arbitrary")),
    )(p, p, p, pc, pc, bias)


def _ret_body(lg_ref, q_ref, k_ref, v_ref, g_ref, s0f_ref, s0b_ref, cos_ref, sa_ref, sb_ref,
              o_ref, sf_ref, sbo_ref, kr_scr, u_hist, st_hist, *, n):
    h = pl.program_id(0)
    lgf = lg_ref[0, h]
    lgb = lg_ref[1, h]
    L = RET_CHUNK
    pos_c = lax.broadcasted_iota(jnp.int32, (L, 1), 0).astype(_F32)
    pos_r = lax.broadcasted_iota(jnp.int32, (1, L), 1).astype(_F32)
    diff = pos_c - pos_r
    decay = jnp.where(diff > 0, jnp.exp(jnp.maximum(diff, 0.0) * lgf),
                      jnp.where(diff < 0, jnp.exp(jnp.maximum(-diff, 0.0) * lgb), 2.0))
    qdec_f = jnp.exp((pos_c + 1.0) * lgf)
    kdec_f = jnp.exp((L - 1.0 - pos_c) * lgf)
    qdec_b = jnp.exp((L - pos_c) * lgb)
    kdec_b = jnp.exp(pos_c * lgb)
    cd_f = jnp.exp(L * lgf)
    cd_b = jnp.exp(L * lgb)
    nf = HEAD_DIM // 4

    def rows_of(i):
        return slice(i * L, (i + 1) * L)

    def roped(ref, i):
        r = rows_of(i)
        return _rope(ref[r, :].astype(_F32), cos_ref[r, :], sa_ref[r, :], sb_ref[r, :], nf)

    for i in range(n):
        k = roped(k_ref, i)
        kr_scr[rows_of(i), :] = k.astype(_BF)
        kd = jnp.concatenate([k * kdec_f, k * kdec_b], 1).astype(_BF)
        u_hist[i] = lax.dot_general(kd, v_ref[rows_of(i), :], (((0,), (0,)), ((), ())),
                                    preferred_element_type=_F32)

    state = s0f_ref[...]
    for i in range(n):
        st_hist[i, 0:HEAD_DIM, :] = state.astype(_BF)
        state = cd_f * state + u_hist[i, 0:HEAD_DIM, :]
    sf_ref[...] = state
    state = s0b_ref[...]
    for i in reversed(range(n)):
        st_hist[i, HEAD_DIM:, :] = state.astype(_BF)
        state = cd_b * state + u_hist[i, HEAD_DIM:, :]
    sbo_ref[...] = state

    for i in range(n):
        r = rows_of(i)
        q = roped(q_ref, i)
        a = _dot_nt(q.astype(_BF), kr_scr[r, :]) * decay
        qd = jnp.concatenate([q * qdec_f, q * qdec_b], 1).astype(_BF)
        o = _dot(a.astype(_BF), v_ref[r, :]) + _dot(qd, st_hist[i])
        mu = jnp.mean(o, -1, keepdims=True)
        oc = o - mu
        var = jnp.mean(oc * oc, -1, keepdims=True)
        y = oc * lax.rsqrt(var + LN_EPS) * _silu(g_ref[r, :].astype(_F32))
        o_ref[r, :] = y.astype(o_ref.dtype)


def _retention(p, lg, s0f, s0b, tables, b, t):
    n = t // RET_CHUNK
    col = lambda cb: pl.BlockSpec((t, LANE), lambda h, bi: (bi, cb + h))
    st_spec = pl.BlockSpec((None, None, HEAD_DIM, HEAD_DIM), lambda h, bi: (bi, h, 0, 0))
    tab_spec = pl.BlockSpec((t, LANE), lambda h, bi: (0, 0))
    st_shape = jax.ShapeDtypeStruct((b, RET_HEADS, HEAD_DIM, HEAD_DIM), _F32)
    return pl.pallas_call(
        functools.partial(_ret_body, n=n),
        grid=(RET_HEADS, b),
        in_specs=[pl.BlockSpec(memory_space=pltpu.SMEM),
                  col(CB_RQ), col(CB_RK), col(CB_RV), col(CB_RG), st_spec, st_spec,
                  tab_spec, tab_spec, tab_spec],
        out_specs=[pl.BlockSpec((t, LANE), lambda h, bi: (bi, h)), st_spec, st_spec],
        out_shape=[jax.ShapeDtypeStruct((b * t, RET_WIDTH), _BF), st_shape, st_shape],
        scratch_shapes=[pltpu.VMEM((t, HEAD_DIM), _BF),
                        pltpu.VMEM((n, 2 * HEAD_DIM, HEAD_DIM), _F32),
                        pltpu.VMEM((n, 2 * HEAD_DIM, HEAD_DIM), _BF)],
        compiler_params=_cparams(("arbitrary", "arbitrary")),
    )(lg, p, p, p, p, s0f, s0b, *tables)


def _ln_epilogue(y, x_ref, gate_ref, lng_ref, lnb_ref, alpha, xo_ref, ho_ref, sh_ref, sc_ref):
    z = alpha * x_ref[...] + gate_ref[0] * y
    mu = jnp.mean(z, -1, keepdims=True)
    zc = z - mu
    var = jnp.mean(zc * zc, -1, keepdims=True)
    xn = zc * lax.rsqrt(var + LN_EPS) * lng_ref[...] + lnb_ref[...]
    xo_ref[...] = xn
    if ho_ref is not None:
        ho_ref[...] = (xn * (1.0 + sc_ref[0]) + sh_ref[0]).astype(ho_ref.dtype)


def _wo_ln_body(a1_ref, a2_ref, a3_ref, w_ref, x_ref, gate_ref, sh_ref, sc_ref, lng_ref, lnb_ref,
                xo_ref, ho_ref, *, alpha):
    o1, o2 = NA_WIDTH, NA_WIDTH + MLA_WIDTH
    y = _dot(a1_ref[...], w_ref[0:o1, :]) + _dot(a2_ref[...], w_ref[o1:o2, :]) + _dot(a3_ref[...], w_ref[o2:, :])
    _ln_epilogue(y, x_ref, gate_ref, lng_ref, lnb_ref, alpha, xo_ref, ho_ref, sh_ref, sc_ref)


def _wo_ln(a1, a2, a3, w, x2, mod3, layer, row_fn, lng, lnb, alpha, tm):
    m, d = x2.shape
    row = lambda width: pl.BlockSpec((tm, width), lambda i: (i, 0))
    vec = pl.BlockSpec((1, d), lambda i: (0, 0))
    return pl.pallas_call(
        functools.partial(_wo_ln_body, alpha=alpha),
        grid=(m // tm,),
        in_specs=[row(NA_WIDTH), row(MLA_WIDTH), row(RET_WIDTH),
                  _resident_spec(w.shape),
                  row(d),
                  _mod_spec(d, layer, 2, row_fn), _mod_spec(d, layer, 3, row_fn), _mod_spec(d, layer, 4, row_fn),
                  vec, vec],
        out_specs=[row(d), row(d)],
        out_shape=[jax.ShapeDtypeStruct((m, d), _F32), jax.ShapeDtypeStruct((m, d), _BF)],
        compiler_params=_cparams(("arbitrary",)),
    )(a1, a2, a3, w, x2, mod3, mod3, mod3, lng, lnb)


def _down_ln_body(*refs, alpha, with_next):
    if with_next:
        g_ref, w_ref, x_ref, gate_ref, sh_ref, sc_ref, lng_ref, lnb_ref, xo_ref, ho_ref = refs
    else:
        g_ref, w_ref, x_ref, gate_ref, lng_ref, lnb_ref, xo_ref = refs
        sh_ref = sc_ref = ho_ref = None
    y = _dot(g_ref[...], w_ref[...])
    _ln_epilogue(y, x_ref, gate_ref, lng_ref, lnb_ref, alpha, xo_ref, ho_ref, sh_ref, sc_ref)


def _down_ln(g, w, x2, mod3, layer, row_fn, lng, lnb, alpha, tm, with_next):
    m, d = x2.shape
    dff = g.shape[1]
    row = lambda width: pl.BlockSpec((tm, width), lambda i: (i, 0))
    vec = pl.BlockSpec((1, d), lambda i: (0, 0))
    in_specs = [row(dff), _resident_spec(w.shape), row(d), _mod_spec(d, layer, 5, row_fn)]
    args = [g, w, x2, mod3]
    if with_next:
        in_specs += [_mod_spec(d, layer + 1, 0, row_fn), _mod_spec(d, layer + 1, 1, row_fn)]
        args += [mod3, mod3]
    in_specs += [vec, vec]
    args += [lng, lnb]
    out_specs = [row(d)] + ([row(d)] if with_next else [])
    out_shape = [jax.ShapeDtypeStruct((m, d), _F32)] + ([jax.ShapeDtypeStruct((m, d), _BF)] if with_next else [])
    out = pl.pallas_call(
        functools.partial(_down_ln_body, alpha=alpha, with_next=with_next),
        grid=(m // tm,),
        in_specs=in_specs, out_specs=out_specs, out_shape=out_shape,
        compiler_params=_cparams(("arbitrary",)),
    )(*args)
    return (out[0], out[1]) if with_next else (out[0], None)


def _glu_body(h_ref, wa_ref, wu_ref, cw_ref, cb_ref, o_ref, a_scr, *, t, rc):
    nchunk = t // rc
    halo = 8
    tn = o_ref.shape[1]
    zeros = jnp.zeros((halo, tn), _F32)
    a_scr[0:halo, :] = zeros
    a_scr[halo + t:, :] = zeros
    w0, w1, w2 = cw_ref[0:1, :], cw_ref[1:2, :], cw_ref[2:3, :]
    bias = cb_ref[...]
    u_prev = None
    for c in range(nchunk + 1):
        u_cur = None
        if c < nchunk:
            hc = h_ref[c * rc:(c + 1) * rc, :]
            a_scr[halo + c * rc:halo + (c + 1) * rc, :] = _dot(hc, wa_ref[...])
            u_cur = _dot(hc, wu_ref[...])
        if c > 0:
            base = halo + (c - 1) * rc
            lo = a_scr[base - 1:base - 1 + rc, :]
            mid = a_scr[base:base + rc, :]
            hi = a_scr[base + 1:base + 1 + rc, :]
            acc = bias + lo * w0 + mid * w1 + hi * w2
            o_ref[(c - 1) * rc:c * rc, :] = (_silu(acc) * u_prev).astype(o_ref.dtype)
        u_prev = u_cur


def _glu(h, w_up, conv_w, conv_b, b, t, tn):
    d = h.shape[1]
    dff = conv_w.shape[1]
    nj = dff // tn
    rc = _tile(t, 256)
    return pl.pallas_call(
        functools.partial(_glu_body, t=t, rc=rc),
        grid=(nj, b),
        in_specs=[pl.BlockSpec((t, d), lambda j, bi: (bi, 0)),
                  pl.BlockSpec((d, tn), lambda j, bi: (0, j)),
                  pl.BlockSpec((d, tn), lambda j, bi: (0, nj + j)),
                  pl.BlockSpec((CONV_W, tn), lambda j, bi: (0, j)),
                  pl.BlockSpec((1, tn), lambda j, bi: (0, j))],
        out_specs=pl.BlockSpec((t, tn), lambda j, bi: (bi, j)),
        out_shape=jax.ShapeDtypeStruct((b * t, dff), _BF),
        scratch_shapes=[pltpu.VMEM((t + 16, tn), _F32)],
        compiler_params=_cparams(("arbitrary", "arbitrary")),
    )(h, w_up, w_up, conv_w, conv_b)


def _pack_w_in(w):
    d = w.shape[0]
    na_q, na_k, na_v, cq, ckv, kpe, r_q, r_k, r_v, r_g = jnp.split(
        w, [int(i) for i in np.cumsum(IN_SPLITS)[:-1]], axis=-1)
    scale = HEAD_DIM ** -0.5
    pad = jnp.zeros((d, PACKED_IN_WIDTH - (CB_KPE * LANE + MLA_ROPE)), w.dtype)
    return jnp.concatenate([cq, ckv, na_q * scale, na_k, na_v, r_q, r_k * scale, r_v, r_g, kpe, pad], -1).astype(_BF)


def _pack_w_uq(w):
    r = w.shape[0]
    w = w.reshape(r, MLA_HEADS, MLA_NOPE + MLA_ROPE) * ((MLA_NOPE + MLA_ROPE) ** -0.5)
    w = jnp.pad(w, ((0, 0), (0, 0), (0, MLA_QK_PAD - MLA_NOPE - MLA_ROPE)))
    return w.reshape(r, MLA_HEADS * MLA_QK_PAD).astype(_BF)


def kernel(x, c, ctx, c_ctx, w_ada, b_ada, w_in, mla_q_norm, mla_kv_norm, w_uq, w_ukv, na_rpb, ret_decay,
           w_o, ln1_g, ln1_b, w_up, conv_w, conv_b, w_down, ln2_g, ln2_b):
    b, t, d = x.shape
    cl = ctx.shape[1]
    depth = w_ada.shape[0]
    dff = conv_w.shape[-1]
    assert b + 1 <= MOD_ROWS
    alpha = (2 * depth) ** 0.25
    ctx_row = b

    cvec = jnp.concatenate([c, c_ctx[None], jnp.zeros((MOD_ROWS - b - 1, d), c.dtype)], 0)
    mod3 = _ada(cvec, w_ada, b_ada)

    tm_l = _tile(t, 1024)
    tm_c = _tile(cl, 1024)
    lat_row = lambda tm: (lambda i: i // (t // tm))
    ctx_rowf = lambda i: ctx_row

    rope_pe = _rope_tables(t, MLA_ROPE)
    rope_ret = _rope_tables(t, HEAD_DIM)
    id_pe = _rope_tables(cl, MLA_ROPE, identity=True)
    id_ret = _rope_tables(cl, HEAD_DIM, identity=True)

    xf = x.reshape(b * t, d)
    cf = ctx.reshape(b * cl, d)
    h1 = _modulate(xf, mod3, 0, lat_row(tm_l), tm_l)
    hc1 = _modulate(cf, mod3, 0, ctx_rowf, tm_c)

    tq = _tile(t, 512)
    tqc = _tile(cl, 256)
    tn_ff = _tile(dff, 512)
    zero_state = jnp.zeros((b, RET_HEADS, HEAD_DIM, HEAD_DIM), _F32)

    for l in range(depth):
        last = l == depth - 1
        w_in_p = _pack_w_in(w_in[l])
        w_uq_p = _pack_w_uq(w_uq[l])
        w_ukv_p = w_ukv[l].astype(_BF)
        gq = mla_q_norm[l][None]
        gkv = mla_kv_norm[l][None]
        lg = jnp.log1p(-jnp.exp2(ret_decay[l].astype(_F32)))
        bias = _na_bias_table(na_rpb[l])

        p = _mm(h1, w_in_p, tm_l, 1024)
        pc = _mm(hc1, w_in_p, tm_c, 1024)

        kc_m, vc_m = _mla_kv(pc, gkv, w_ukv_p, id_pe, b, cl, _tile(cl, 512))
        kl_m, vl_m = _mla_kv(p, gkv, w_ukv_p, rope_pe, b, t, _tile(t, 512))
        q_m = _mla_q(p, gq, w_uq_p, rope_pe, b, t, _tile(t, 512))
        y_mla = _dense_attn(
            q_m, pl.BlockSpec((None, None, tq, MLA_QK_PAD), lambda bi, h, i: (bi, h, i, 0)),
            [(kc_m, _head4_spec(cl, MLA_QK_PAD), vc_m, _head4_spec(cl, MLA_V)),
             (kl_m, _head4_spec(t, MLA_QK_PAD), vl_m, _head4_spec(t, MLA_V))],
            b, MLA_HEADS, t // tq, tq, MLA_V, nsub=tq // _tile(tq, 256))

        y_na = _na_attn(p, pc, bias, b, t, cl)

        yc_ret, s_f, s_b = _retention(pc, lg, zero_state, zero_state, id_ret, b, cl)
        y_ret, _, _ = _retention(p, lg, s_f, s_b, rope_ret, b, t)

        tm_o = _tile(t, 256)
        x_new, h2 = _wo_ln(y_na, y_mla, y_ret, w_o[l].astype(_BF), xf, mod3, l, lat_row(tm_o),
                           ln1_g[l][None], ln1_b[l][None], alpha, tm_o)

        w_up_b = w_up[l].astype(_BF)
        w_down_b = w_down[l].astype(_BF)
        tm_d = _tile(t, 256)

        if not last:
            qc_m = _mla_q(pc, gq, w_uq_p, id_pe, b, cl, _tile(cl, 512))
            yc_mla = _dense_attn(
                qc_m, pl.BlockSpec((None, None, tqc, MLA_QK_PAD), lambda bi, h, i: (bi, h, i, 0)),
                [(kc_m, _head4_spec(cl, MLA_QK_PAD), vc_m, _head4_spec(cl, MLA_V))],
                b, MLA_HEADS, cl // tqc, tqc, MLA_V, nsub=1)
            nbc = cl // tqc
            yc_na = _dense_attn(
                pc, pl.BlockSpec((tqc, LANE), lambda bi, h, i: (bi * nbc + i, CB_NAQ + h)),
                [(pc, pl.BlockSpec((cl, LANE), lambda bi, h, i: (bi, CB_NAK + h)),
                  pc, pl.BlockSpec((cl, LANE), lambda bi, h, i: (bi, CB_NAV + h)))],
                b, NA_HEADS, nbc, tqc, HEAD_DIM, nsub=1)
            tm_oc = _tile(cl, 256)
            c_new, hc2 = _wo_ln(yc_na, yc_mla, yc_ret, w_o[l].astype(_BF), cf, mod3, l, ctx_rowf,
                                ln1_g[l][None], ln1_b[l][None], alpha, tm_oc)

        g_l = _glu(h2, w_up_b, conv_w[l], conv_b[l][None], b, t, tn_ff)
        xf, h1 = _down_ln(g_l, w_down_b, x_new, mod3, l, lat_row(tm_d), ln2_g[l][None], ln2_b[l][None],
                          alpha, tm_d, with_next=not last)
        if not last:
            g_c = _glu(hc2, w_up_b, conv_w[l], conv_b[l][None], b, cl, tn_ff)
            tm_dc = _tile(cl, 256)
            cf, hc1 = _down_ln(g_c, w_down_b, c_new, mod3, l, ctx_rowf, ln2_g[l][None], ln2_b[l][None],
                               alpha, tm_dc, with_next=True)

    return xf.reshape(b, t, d)
```

```python
import functools

import numpy as np
import jax
import jax.numpy as jnp
from jax import lax
from jax.experimental import pallas as pl
from jax.experimental.pallas import tpu as pltpu

GRID_W = 64
HEAD_DIM = 128
NA_HEADS = 6
MLA_HEADS = 5
RET_HEADS = 5
NA_WIDTH = NA_HEADS * HEAD_DIM
MLA_WIDTH = MLA_HEADS * HEAD_DIM
RET_WIDTH = RET_HEADS * HEAD_DIM
NA_WIN_R = 8
NA_WIN_C = 16
MLA_Q_RANK = 512
MLA_KV_RANK = 512
MLA_NOPE = 128
MLA_ROPE = 64
MLA_V = 128
MLA_QK_PAD = 256
RET_CHUNK = 128
CONV_W = 3
ROPE_BASE = 10000.0
LN_EPS = 1e-5
RMS_EPS = 1e-6
NEG_INF = -1e30
IN_SPLITS = (NA_WIDTH, NA_WIDTH, NA_WIDTH, MLA_Q_RANK, MLA_KV_RANK, MLA_ROPE,
             RET_WIDTH, RET_WIDTH, RET_WIDTH, RET_WIDTH)

LANE = 128
CB_CQ, CB_CKV, CB_NAQ, CB_NAK, CB_NAV = 0, 4, 8, 14, 20
CB_RQ, CB_RK, CB_RV, CB_RG, CB_KPE = 26, 31, 36, 41, 46
PACKED_IN_WIDTH = 48 * LANE
MOD_ROWS = 8
VMEM_LIMIT_MB = 56

_BF = jnp.bfloat16
_F32 = jnp.float32


def _cparams(sem, vmem_mb=VMEM_LIMIT_MB):
    return pltpu.CompilerParams(dimension_semantics=sem, vmem_limit_bytes=vmem_mb << 20)


def _tile(n, pref):
    t = min(n, pref)
    while n % t:
        t //= 2
    return t


def _dot(a, b):
    return jnp.dot(a, b, preferred_element_type=_F32)


def _dot_nt(a, b):
    return lax.dot_general(a, b, (((1,), (1,)), ((), ())), preferred_element_type=_F32)


def _silu(x):
    return x / (1.0 + jnp.exp(-x))


def _resident_spec(shape):
    return pl.BlockSpec(shape, lambda *_: (0,) * len(shape), pipeline_mode=pl.Buffered(1))


def _ada_body(c_ref, w_ref, b_ref, o_ref):
    s = _silu(c_ref[...]).astype(_BF)
    o_ref[0] = _dot(s, w_ref[0].astype(_BF)) + b_ref[0]


def _ada(cvec, w_ada, b_ada):
    depth, d, n = w_ada.shape
    tn = _tile(n, 1024)
    out = pl.pallas_call(
        _ada_body,
        grid=(depth, n // tn),
        in_specs=[pl.BlockSpec((MOD_ROWS, d), lambda l, j: (0, 0)),
                  pl.BlockSpec((1, d, tn), lambda l, j: (l, 0, j)),
                  pl.BlockSpec((1, 1, tn), lambda l, j: (l, 0, j))],
        out_specs=pl.BlockSpec((1, MOD_ROWS, tn), lambda l, j: (l, 0, j)),
        out_shape=jax.ShapeDtypeStruct((depth, MOD_ROWS, n), _F32),
        compiler_params=_cparams(("arbitrary", "arbitrary")),
    )(cvec, w_ada, b_ada.reshape(depth, 1, n))
    return out.reshape(depth * MOD_ROWS, 1, n)


def _mod_spec(d, layer, chunk, row_fn):
    return pl.BlockSpec((1, 1, d), lambda i, *_: (layer * MOD_ROWS + row_fn(i), 0, chunk))


def _modulate_body(x_ref, sh_ref, sc_ref, o_ref):
    o_ref[...] = (x_ref[...] * (1.0 + sc_ref[0]) + sh_ref[0]).astype(o_ref.dtype)


def _modulate(x2, mod3, layer, row_fn, tm):
    m, d = x2.shape
    return pl.pallas_call(
        _modulate_body,
        grid=(m // tm,),
        in_specs=[pl.BlockSpec((tm, d), lambda i: (i, 0)),
                  _mod_spec(d, layer, 0, row_fn), _mod_spec(d, layer, 1, row_fn)],
        out_specs=pl.BlockSpec((tm, d), lambda i: (i, 0)),
        out_shape=jax.ShapeDtypeStruct((m, d), _BF),
        compiler_params=_cparams(("arbitrary",)),
    )(x2, mod3, mod3)


def _mm_body(a_ref, w_ref, o_ref):
    o_ref[...] = _dot(a_ref[...], w_ref[...]).astype(o_ref.dtype)


def _mm(a, w, tm, tn):
    m, k = a.shape
    n = w.shape[1]
    return pl.pallas_call(
        _mm_body,
        grid=(m // tm, n // tn),
        in_specs=[pl.BlockSpec((tm, k), lambda i, j: (i, 0)),
                  pl.BlockSpec((k, tn), lambda i, j: (0, j))],
        out_specs=pl.BlockSpec((tm, tn), lambda i, j: (i, j)),
        out_shape=jax.ShapeDtypeStruct((m, n), _BF),
        compiler_params=_cparams(("arbitrary", "arbitrary")),
    )(a, w)


def _rope_tables(t, rot_dim, identity=False):
    pos = jnp.arange(t)
    row = (pos // GRID_W).astype(_F32)
    col = (pos % GRID_W).astype(_F32)
    nf = rot_dim // 4
    inv = ROPE_BASE ** (-jnp.arange(nf, dtype=_F32) / nf)
    ar = row[:, None] * inv[None]
    ac = col[:, None] * inv[None]
    ang = jnp.concatenate([ar, ar, ac, ac], -1)
    cos, sin = jnp.cos(ang), jnp.sin(ang)
    if identity:
        cos, sin = jnp.ones_like(cos), jnp.zeros_like(sin)
    first = (np.arange(rot_dim) % (rot_dim // 2)) < nf
    sin_a = jnp.where(first, -sin, 0.0)
    sin_b = jnp.where(first, 0.0, sin)
    pad = LANE - rot_dim
    if pad:
        cos = jnp.pad(cos, ((0, 0), (0, pad)), constant_values=1.0)
        sin_a = jnp.pad(sin_a, ((0, 0), (0, pad)))
        sin_b = jnp.pad(sin_b, ((0, 0), (0, pad)))
    return cos, sin_a, sin_b


def _rope(x, cos, sin_a, sin_b, nf):
    return x * cos + pltpu.roll(x, LANE - nf, 1) * sin_a + pltpu.roll(x, nf, 1) * sin_b


def _rms(x, g):
    return x * lax.rsqrt(jnp.mean(x * x, -1, keepdims=True) + RMS_EPS) * g


def _mla_q_body(p_ref, g_ref, w_ref, cos_ref, sa_ref, sb_ref, o_ref):
    xb = _rms(p_ref[...].astype(_F32), g_ref[...]).astype(_BF)
    cos, sa, sb = cos_ref[...], sa_ref[...], sb_ref[...]
    for h in range(MLA_HEADS):
        acc = _dot(xb, w_ref[:, h * MLA_QK_PAD:(h + 1) * MLA_QK_PAD])
        o_ref[0, h, :, 0:LANE] = acc[:, :LANE].astype(_BF)
        o_ref[0, h, :, LANE:] = _rope(acc[:, LANE:], cos, sa, sb, MLA_ROPE // 4).astype(_BF)


def _mla_q(p, g, w, tables, b, t, tm):
    nb = t // tm
    tab_spec = pl.BlockSpec((tm, LANE), lambda i: (i % nb, 0))
    return pl.pallas_call(
        _mla_q_body,
        grid=(b * nb,),
        in_specs=[pl.BlockSpec((tm, MLA_Q_RANK), lambda i: (i, CB_CQ // 4)),
                  pl.BlockSpec((1, MLA_Q_RANK), lambda i: (0, 0)),
                  pl.BlockSpec(w.shape, lambda i: (0, 0)),
                  tab_spec, tab_spec, tab_spec],
        out_specs=pl.BlockSpec((1, MLA_HEADS, tm, MLA_QK_PAD), lambda i: (i // nb, 0, i % nb, 0)),
        out_shape=jax.ShapeDtypeStruct((b, MLA_HEADS, t, MLA_QK_PAD), _BF),
        compiler_params=_cparams(("arbitrary",)),
    )(p, g, w, *tables)


def _mla_kv_body(ckv_ref, kpe_ref, g_ref, w_ref, cos_ref, sa_ref, sb_ref, k_ref, v_ref):
    xb = _rms(ckv_ref[...].astype(_F32), g_ref[...]).astype(_BF)
    pe = _rope(kpe_ref[...].astype(_F32), cos_ref[...], sa_ref[...], sb_ref[...], MLA_ROPE // 4).astype(_BF)
    width = MLA_NOPE + MLA_V
    for h in range(MLA_HEADS):
        acc = _dot(xb, w_ref[:, h * width:(h + 1) * width])
        k_ref[0, h, :, 0:LANE] = acc[:, :MLA_NOPE].astype(_BF)
        k_ref[0, h, :, LANE:] = pe
        v_ref[0, h] = acc[:, MLA_NOPE:].astype(_BF)


def _mla_kv(p, g, w, tables, b, t, tm):
    nb = t // tm
    tab_spec = pl.BlockSpec((tm, LANE), lambda i: (i % nb, 0))
    return pl.pallas_call(
        _mla_kv_body,
        grid=(b * nb,),
        in_specs=[pl.BlockSpec((tm, MLA_KV_RANK), lambda i: (i, CB_CKV // 4)),
                  pl.BlockSpec((tm, LANE), lambda i: (i, CB_KPE)),
                  pl.BlockSpec((1, MLA_KV_RANK), lambda i: (0, 0)),
                  pl.BlockSpec(w.shape, lambda i: (0, 0)),
                  tab_spec, tab_spec, tab_spec],
        out_specs=[pl.BlockSpec((1, MLA_HEADS, tm, MLA_QK_PAD), lambda i: (i // nb, 0, i % nb, 0)),
                   pl.BlockSpec((1, MLA_HEADS, tm, MLA_V), lambda i: (i // nb, 0, i % nb, 0))],
        out_shape=[jax.ShapeDtypeStruct((b, MLA_HEADS, t, MLA_QK_PAD), _BF),
                   jax.ShapeDtypeStruct((b, MLA_HEADS, t, MLA_V), _BF)],
        compiler_params=_cparams(("arbitrary",)),
    )(p, p, g, w, *tables)


def _softmax_pv(scores, values):
    m = functools.reduce(jnp.maximum, [jnp.max(s, -1, keepdims=True) for s in scores])
    p = [jnp.exp(s - m) for s in scores]
    l = functools.reduce(jnp.add, [jnp.sum(pi, -1, keepdims=True) for pi in p])
    acc = functools.reduce(jnp.add, [_dot(pi.astype(_BF), v) for pi, v in zip(p, values)])
    return acc / l


def _attn_body(*refs, nseg, nsub):
    q_ref, o_ref = refs[0], refs[-1]
    ks = [refs[1 + 2 * i][...] for i in range(nseg)]
    vs = [refs[2 + 2 * i][...] for i in range(nseg)]
    ts = q_ref.shape[0] // nsub
    for j in range(nsub):
        q = q_ref[j * ts:(j + 1) * ts, :]
        out = _softmax_pv([_dot_nt(q, k) for k in ks], vs)
        o_ref[j * ts:(j + 1) * ts, :] = out.astype(o_ref.dtype)


def _dense_attn(q, q_spec, kv, b, heads, tq_blocks, tq, dv, nsub):
    args, specs = [q], [q_spec]
    for k, ks, v, vs in kv:
        args += [k, v]
        specs += [ks, vs]
    return pl.pallas_call(
        functools.partial(_attn_body, nseg=len(kv), nsub=nsub),
        grid=(b, heads, tq_blocks),
        in_specs=specs,
        out_specs=pl.BlockSpec((tq, dv), lambda bi, h, i: (bi * tq_blocks + i, h)),
        out_shape=jax.ShapeDtypeStruct((b * tq_blocks * tq, heads * dv), _BF),
        compiler_params=_cparams(("arbitrary", "arbitrary", "arbitrary")),
    )(*args)


def _head4_spec(n, d):
    return pl.BlockSpec((None, None, n, d), lambda bi, h, i: (bi, h, 0, 0))


NA_GROUP = 4
NA_KEY_ROWS = 12


def _na_group_window(gi, rows):
    ngroups = rows // NA_GROUP
    if gi == 0:
        return 0, 0
    if gi == ngroups - 1:
        return 2, rows - NA_KEY_ROWS
    return 1, gi * NA_GROUP - NA_WIN_R // 2


NA_DR = 2 * NA_WIN_R - 1


def _na_col_bias(rpb):
    c = np.arange(GRID_W)
    col_start = np.clip(c - NA_WIN_C // 2, 0, GRID_W - NA_WIN_C)
    kc = np.arange(GRID_W)
    in_win = (kc[None, :] >= col_start[:, None]) & (kc[None, :] < col_start[:, None] + NA_WIN_C)
    dc = np.clip(kc[None, :] - c[:, None] + NA_WIN_C - 1, 0, 2 * NA_WIN_C - 2)
    col_sel = (dc[:, :, None] == np.arange(2 * NA_WIN_C - 1)).astype(np.float32)
    toep = jnp.einsum('hrj,cqj->hrcq', rpb.astype(_F32), col_sel, precision=lax.Precision.HIGHEST)
    toep = jnp.where(in_win[None, None], toep, NEG_INF)
    masked = jnp.full((rpb.shape[0], 1, GRID_W, GRID_W), NEG_INF, _F32)
    return jnp.concatenate([toep, masked], 1)


def _na_block_index():
    g = np.arange(NA_GROUP)[:, None]
    kr = np.arange(NA_KEY_ROWS)[None, :]
    half = NA_WIN_R // 2
    dr = np.stack([kr - g + NA_WIN_R - 1, kr - g + NA_WIN_R - 1 - half, kr - g - 1])
    w0 = np.stack([0 * g, g, half + 0 * g])
    row_ok = (kr[None] >= w0) & (kr[None] < w0 + NA_WIN_R)
    assert np.all((dr[row_ok] >= 0) & (dr[row_ok] < NA_DR))
    return np.where(row_ok, dr, NA_DR)


def _na_body(q_ref, k_ref, v_ref, kc_ref, vc_ref, cb_ref, o_ref, bias_ref, *, rows):
    @pl.when(pl.program_id(1) == 0)
    def _():
        idx = _na_block_index()
        for typ in range(3):
            for g in range(NA_GROUP):
                for kp in range(NA_KEY_ROWS // 2):
                    pair = jnp.concatenate([cb_ref[int(idx[typ, g, 2 * kp])], cb_ref[int(idx[typ, g, 2 * kp + 1])]], 1)
                    bias_ref[typ, g * GRID_W:(g + 1) * GRID_W, kp * 2 * GRID_W:(kp + 1) * 2 * GRID_W] = pair

    kc = kc_ref[...]
    vc = vc_ref[...]
    gq = NA_GROUP * GRID_W
    for gi in range(rows // NA_GROUP):
        typ, ws = _na_group_window(gi, rows)
        q = q_ref[gi * gq:(gi + 1) * gq, :]
        kw = k_ref[ws * GRID_W:(ws + NA_KEY_ROWS) * GRID_W, :]
        vw = v_ref[ws * GRID_W:(ws + NA_KEY_ROWS) * GRID_W, :]
        out = _softmax_pv([_dot_nt(q, kw) + bias_ref[typ], _dot_nt(q, kc)], [vw, vc])
        o_ref[gi * gq:(gi + 1) * gq, :] = out.astype(o_ref.dtype)


def _na_attn(p, pc, bias, b, t, c):
    rows = t // GRID_W
    assert t % GRID_W == 0 and rows % NA_GROUP == 0 and rows >= NA_KEY_ROWS + NA_GROUP and 2 * GRID_W == LANE
    return pl.pallas_call(
        functools.partial(_na_body, rows=rows),
        grid=(NA_HEADS, b),
        in_specs=[pl.BlockSpec((t, LANE), lambda h, bi: (bi, CB_NAQ + h)),
                  pl.BlockSpec((t, LANE), lambda h, bi: (bi, CB_NAK + h)),
                  pl.BlockSpec((t, LANE), lambda h, bi: (bi, CB_NAV + h)),
                  pl.BlockSpec((c, LANE), lambda h, bi: (bi, CB_NAK + h)),
                  pl.BlockSpec((c, LANE), lambda h, bi: (bi, CB_NAV + h)),
                  pl.BlockSpec((None, NA_DR + 1, GRID_W, GRID_W), lambda h, bi: (h, 0, 0, 0))],
        out_specs=pl.BlockSpec((t, LANE), lambda h, bi: (bi, h)),
        out_shape=jax.ShapeDtypeStruct((b * t, NA_WIDTH), _BF),
        scratch_shapes=[pltpu.VMEM((3, NA_GROUP * GRID_W, NA_KEY_ROWS * GRID_W), _F32)],
        compiler_params=_cparams(("arbitrary", "arbitrary")),
    )(p, p, p, pc, pc, bias)


def _ret_body(lg_ref, q_ref, k_ref, v_ref, g_ref, s0f_ref, s0b_ref, cos_ref, sa_ref, sb_ref,
              o_ref, sf_ref, sbo_ref, kr_scr, u_hist, st_hist, *, n):
    h = pl.program_id(0)
    lgf = lg_ref[0, h]
    lgb = lg_ref[1, h]
    L = RET_CHUNK
    pos_c = lax.broadcasted_iota(jnp.int32, (L, 1), 0).astype(_F32)
    pos_r = lax.broadcasted_iota(jnp.int32, (1, L), 1).astype(_F32)
    diff = pos_c - pos_r
    decay = jnp.where(diff > 0, jnp.exp(jnp.maximum(diff, 0.0) * lgf),
                      jnp.where(diff < 0, jnp.exp(jnp.maximum(-diff, 0.0) * lgb), 2.0))
    qdec_f = jnp.exp((pos_c + 1.0) * lgf)
    kdec_f = jnp.exp((L - 1.0 - pos_c) * lgf)
    qdec_b = jnp.exp((L - pos_c) * lgb)
    kdec_b = jnp.exp(pos_c * lgb)
    cd_f = jnp.exp(L * lgf)
    cd_b = jnp.exp(L * lgb)
    nf = HEAD_DIM // 4

    def rows_of(i):
        return slice(i * L, (i + 1) * L)

    def roped(ref, i):
        r = rows_of(i)
        return _rope(ref[r, :].astype(_F32), cos_ref[r, :], sa_ref[r, :], sb_ref[r, :], nf)

    for i in range(n):
        k = roped(k_ref, i)
        kr_scr[rows_of(i), :] = k.astype(_BF)
        kd = jnp.concatenate([k * kdec_f, k * kdec_b], 1).astype(_BF)
        u_hist[i] = lax.dot_general(kd, v_ref[rows_of(i), :], (((0,), (0,)), ((), ())),
                                    preferred_element_type=_F32)

    state = s0f_ref[...]
    for i in range(n):
        st_hist[i, 0:HEAD_DIM, :] = state.astype(_BF)
        state = cd_f * state + u_hist[i, 0:HEAD_DIM, :]
    sf_ref[...] = state
    state = s0b_ref[...]
    for i in reversed(range(n)):
        st_hist[i, HEAD_DIM:, :] = state.astype(_BF)
        state = cd_b * state + u_hist[i, HEAD_DIM:, :]
    sbo_ref[...] = state

    for i in range(n):
        r = rows_of(i)
        q = roped(q_ref, i)
        a = _dot_nt(q.astype(_BF), kr_scr[r, :]) * decay
        qd = jnp.concatenate([q * qdec_f, q * qdec_b], 1).astype(_BF)
        o = _dot(a.astype(_BF), v_ref[r, :]) + _dot(qd, st_hist[i])
        mu = jnp.mean(o, -1, keepdims=True)
        oc = o - mu
        var = jnp.mean(oc * oc, -1, keepdims=True)
        y = oc * lax.rsqrt(var + LN_EPS) * _silu(g_ref[r, :].astype(_F32))
        o_ref[r, :] = y.astype(o_ref.dtype)


def _retention(p, lg, s0f, s0b, tables, b, t):
    n = t // RET_CHUNK
    col = lambda cb: pl.BlockSpec((t, LANE), lambda h, bi: (bi, cb + h))
    st_spec = pl.BlockSpec((None, None, HEAD_DIM, HEAD_DIM), lambda h, bi: (bi, h, 0, 0))
    tab_spec = pl.BlockSpec((t, LANE), lambda h, bi: (0, 0))
    st_shape = jax.ShapeDtypeStruct((b, RET_HEADS, HEAD_DIM, HEAD_DIM), _F32)
    return pl.pallas_call(
        functools.partial(_ret_body, n=n),
        grid=(RET_HEADS, b),
        in_specs=[pl.BlockSpec(memory_space=pltpu.SMEM),
                  col(CB_RQ), col(CB_RK), col(CB_RV), col(CB_RG), st_spec, st_spec,
                  tab_spec, tab_spec, tab_spec],
        out_specs=[pl.BlockSpec((t, LANE), lambda h, bi: (bi, h)), st_spec, st_spec],
        out_shape=[jax.ShapeDtypeStruct((b * t, RET_WIDTH), _BF), st_shape, st_shape],
        scratch_shapes=[pltpu.VMEM((t, HEAD_DIM), _BF),
                        pltpu.VMEM((n, 2 * HEAD_DIM, HEAD_DIM), _F32),
                        pltpu.VMEM((n, 2 * HEAD_DIM, HEAD_DIM), _BF)],
        compiler_params=_cparams(("arbitrary", "arbitrary")),
    )(lg, p, p, p, p, s0f, s0b, *tables)


def _ln_epilogue(y, x_ref, gate_ref, lng_ref, lnb_ref, alpha, xo_ref, ho_ref, sh_ref, sc_ref):
    z = alpha * x_ref[...] + gate_ref[0] * y
    mu = jnp.mean(z, -1, keepdims=True)
    zc = z - mu
    var = jnp.mean(zc * zc, -1, keepdims=True)
    xn = zc * lax.rsqrt(var + LN_EPS) * lng_ref[...] + lnb_ref[...]
    xo_ref[...] = xn
    if ho_ref is not None:
        ho_ref[...] = (xn * (1.0 + sc_ref[0]) + sh_ref[0]).astype(ho_ref.dtype)


def _wo_ln_body(a1_ref, a2_ref, a3_ref, w_ref, x_ref, gate_ref, sh_ref, sc_ref, lng_ref, lnb_ref,
                xo_ref, ho_ref, *, alpha):
    o1, o2 = NA_WIDTH, NA_WIDTH + MLA_WIDTH
    y = _dot(a1_ref[...], w_ref[0:o1, :]) + _dot(a2_ref[...], w_ref[o1:o2, :]) + _dot(a3_ref[...], w_ref[o2:, :])
    _ln_epilogue(y, x_ref, gate_ref, lng_ref, lnb_ref, alpha, xo_ref, ho_ref, sh_ref, sc_ref)


def _wo_ln(a1, a2, a3, w, x2, mod3, layer, row_fn, lng, lnb, alpha, tm):
    m, d = x2.shape
    row = lambda width: pl.BlockSpec((tm, width), lambda i: (i, 0))
    vec = pl.BlockSpec((1, d), lambda i: (0, 0))
    return pl.pallas_call(
        functools.partial(_wo_ln_body, alpha=alpha),
        grid=(m // tm,),
        in_specs=[row(NA_WIDTH), row(MLA_WIDTH), row(RET_WIDTH),
                  _resident_spec(w.shape),
                  row(d),
                  _mod_spec(d, layer, 2, row_fn), _mod_spec(d, layer, 3, row_fn), _mod_spec(d, layer, 4, row_fn),
                  vec, vec],
        out_specs=[row(d), row(d)],
        out_shape=[jax.ShapeDtypeStruct((m, d), _F32), jax.ShapeDtypeStruct((m, d), _BF)],
        compiler_params=_cparams(("arbitrary",)),
    )(a1, a2, a3, w, x2, mod3, mod3, mod3, lng, lnb)


def _down_ln_body(*refs, alpha, with_next):
    if with_next:
        g_ref, w_ref, x_ref, gate_ref, sh_ref, sc_ref, lng_ref, lnb_ref, xo_ref, ho_ref = refs
    else:
        g_ref, w_ref, x_ref, gate_ref, lng_ref, lnb_ref, xo_ref = refs
        sh_ref = sc_ref = ho_ref = None
    y = _dot(g_ref[...], w_ref[...])
    _ln_epilogue(y, x_ref, gate_ref, lng_ref, lnb_ref, alpha, xo_ref, ho_ref, sh_ref, sc_ref)


def _down_ln(g, w, x2, mod3, layer, row_fn, lng, lnb, alpha, tm, with_next):
    m, d = x2.shape
    dff = g.shape[1]
    row = lambda width: pl.BlockSpec((tm, width), lambda i: (i, 0))
    vec = pl.BlockSpec((1, d), lambda i: (0, 0))
    in_specs = [row(dff), _resident_spec(w.shape), row(d), _mod_spec(d, layer, 5, row_fn)]
    args = [g, w, x2, mod3]
    if with_next:
        in_specs += [_mod_spec(d, layer + 1, 0, row_fn), _mod_spec(d, layer + 1, 1, row_fn)]
        args += [mod3, mod3]
    in_specs += [vec, vec]
    args += [lng, lnb]
    out_specs = [row(d)] + ([row(d)] if with_next else [])
    out_shape = [jax.ShapeDtypeStruct((m, d), _F32)] + ([jax.ShapeDtypeStruct((m, d), _BF)] if with_next else [])
    out = pl.pallas_call(
        functools.partial(_down_ln_body, alpha=alpha, with_next=with_next),
        grid=(m // tm,),
        in_specs=in_specs, out_specs=out_specs, out_shape=out_shape,
        compiler_params=_cparams(("arbitrary",)),
    )(*args)
    return (out[0], out[1]) if with_next else (out[0], None)


def _glu_body(h_ref, wa_ref, wu_ref, cw_ref, cb_ref, o_ref, a_scr, wab, wub, *, t, rc):
    @pl.when(pl.program_id(1) == 0)
    def _():
        wab[...] = wa_ref[...].astype(_BF)
        wub[...] = wu_ref[...].astype(_BF)

    nchunk = t // rc
    halo = 8
    tn = o_ref.shape[1]
    zeros = jnp.zeros((halo, tn), _F32)
    a_scr[0:halo, :] = zeros
    a_scr[halo + t:, :] = zeros
    w0, w1, w2 = cw_ref[0:1, :], cw_ref[1:2, :], cw_ref[2:3, :]
    bias = cb_ref[...]
    u_prev = None
    for c in range(nchunk + 1):
        u_cur = None
        if c < nchunk:
            hc = h_ref[c * rc:(c + 1) * rc, :]
            a_scr[halo + c * rc:halo + (c + 1) * rc, :] = _dot(hc, wab[...])
            u_cur = _dot(hc, wub[...])
        if c > 0:
            base = halo + (c - 1) * rc
            lo = a_scr[base - 1:base - 1 + rc, :]
            mid = a_scr[base:base + rc, :]
            hi = a_scr[base + 1:base + 1 + rc, :]
            acc = bias + lo * w0 + mid * w1 + hi * w2
            o_ref[(c - 1) * rc:c * rc, :] = (_silu(acc) * u_prev).astype(o_ref.dtype)
        u_prev = u_cur


def _glu(h, w_up, layer, conv_w, conv_b, b, t, tn):
    d = h.shape[1]
    dff = conv_w.shape[1]
    nj = dff // tn
    rc = _tile(t, 512)
    return pl.pallas_call(
        functools.partial(_glu_body, t=t, rc=rc),
        grid=(nj, b),
        in_specs=[pl.BlockSpec((t, d), lambda j, bi: (bi, 0)),
                  pl.BlockSpec((None, d, tn), lambda j, bi: (layer, 0, j)),
                  pl.BlockSpec((None, d, tn), lambda j, bi: (layer, 0, nj + j)),
                  pl.BlockSpec((CONV_W, tn), lambda j, bi: (0, j)),
                  pl.BlockSpec((1, tn), lambda j, bi: (0, j))],
        out_specs=pl.BlockSpec((t, tn), lambda j, bi: (bi, j)),
        out_shape=jax.ShapeDtypeStruct((b * t, dff), _BF),
        scratch_shapes=[pltpu.VMEM((t + 16, tn), _F32), pltpu.VMEM((d, tn), _BF), pltpu.VMEM((d, tn), _BF)],
        compiler_params=_cparams(("arbitrary", "arbitrary")),
    )(h, w_up, w_up, conv_w, conv_b)


def _pack_w_in(w):
    d = w.shape[0]
    na_q, na_k, na_v, cq, ckv, kpe, r_q, r_k, r_v, r_g = jnp.split(
        w, [int(i) for i in np.cumsum(IN_SPLITS)[:-1]], axis=-1)
    scale = HEAD_DIM ** -0.5
    pad = jnp.zeros((d, PACKED_IN_WIDTH - (CB_KPE * LANE + MLA_ROPE)), w.dtype)
    return jnp.concatenate([cq, ckv, na_q * scale, na_k, na_v, r_q, r_k * scale, r_v, r_g, kpe, pad], -1).astype(_BF)


def _pack_w_uq(w):
    r = w.shape[0]
    w = w.reshape(r, MLA_HEADS, MLA_NOPE + MLA_ROPE) * ((MLA_NOPE + MLA_ROPE) ** -0.5)
    w = jnp.pad(w, ((0, 0), (0, 0), (0, MLA_QK_PAD - MLA_NOPE - MLA_ROPE)))
    return w.reshape(r, MLA_HEADS * MLA_QK_PAD).astype(_BF)


def kernel(x, c, ctx, c_ctx, w_ada, b_ada, w_in, mla_q_norm, mla_kv_norm, w_uq, w_ukv, na_rpb, ret_decay,
           w_o, ln1_g, ln1_b, w_up, conv_w, conv_b, w_down, ln2_g, ln2_b):
    b, t, d = x.shape
    cl = ctx.shape[1]
    depth = w_ada.shape[0]
    dff = conv_w.shape[-1]
    assert b + 1 <= MOD_ROWS
    alpha = (2 * depth) ** 0.25
    ctx_row = b

    cvec = jnp.concatenate([c, c_ctx[None], jnp.zeros((MOD_ROWS - b - 1, d), c.dtype)], 0)
    mod3 = _ada(cvec, w_ada, b_ada)

    tm_l = _tile(t, 1024)
    tm_c = _tile(cl, 1024)
    lat_row = lambda tm: (lambda i: i // (t // tm))
    ctx_rowf = lambda i: ctx_row

    rope_pe = _rope_tables(t, MLA_ROPE)
    rope_ret = _rope_tables(t, HEAD_DIM)
    id_pe = _rope_tables(cl, MLA_ROPE, identity=True)
    id_ret = _rope_tables(cl, HEAD_DIM, identity=True)

    xf = x.reshape(b * t, d)
    cf = ctx.reshape(b * cl, d)
    h1 = _modulate(xf, mod3, 0, lat_row(tm_l), tm_l)
    hc1 = _modulate(cf, mod3, 0, ctx_rowf, tm_c)

    tq = _tile(t, 1024)
    tqc = _tile(cl, 256)
    tn_ff = _tile(dff, 512)
    zero_state = jnp.zeros((b, RET_HEADS, HEAD_DIM, HEAD_DIM), _F32)

    for l in range(depth):
        last = l == depth - 1
        w_in_p = _pack_w_in(w_in[l])
        w_uq_p = _pack_w_uq(w_uq[l])
        w_ukv_p = w_ukv[l].astype(_BF)
        gq = mla_q_norm[l][None]
        gkv = mla_kv_norm[l][None]
        lg = jnp.log1p(-jnp.exp2(ret_decay[l].astype(_F32)))
        bias = _na_col_bias(na_rpb[l])

        p = _mm(h1, w_in_p, tm_l, 1024)
        pc = _mm(hc1, w_in_p, tm_c, 1024)

        kc_m, vc_m = _mla_kv(pc, gkv, w_ukv_p, id_pe, b, cl, _tile(cl, 512))
        kl_m, vl_m = _mla_kv(p, gkv, w_ukv_p, rope_pe, b, t, _tile(t, 512))
        q_m = _mla_q(p, gq, w_uq_p, rope_pe, b, t, _tile(t, 512))
        y_mla = _dense_attn(
            q_m, pl.BlockSpec((None, None, tq, MLA_QK_PAD), lambda bi, h, i: (bi, h, i, 0)),
            [(kc_m, _head4_spec(cl, MLA_QK_PAD), vc_m, _head4_spec(cl, MLA_V)),
             (kl_m, _head4_spec(t, MLA_QK_PAD), vl_m, _head4_spec(t, MLA_V))],
            b, MLA_HEADS, t // tq, tq, MLA_V, nsub=tq // _tile(tq, 256))

        y_na = _na_attn(p, pc, bias, b, t, cl)

        yc_ret, s_f, s_b = _retention(pc, lg, zero_state, zero_state, id_ret, b, cl)
        y_ret, _, _ = _retention(p, lg, s_f, s_b, rope_ret, b, t)

        tm_o = _tile(t, 256)
        x_new, h2 = _wo_ln(y_na, y_mla, y_ret, w_o[l].astype(_BF), xf, mod3, l, lat_row(tm_o),
                           ln1_g[l][None], ln1_b[l][None], alpha, tm_o)

        w_down_b = w_down[l].astype(_BF)
        tm_d = _tile(t, 256)

        if not last:
            qc_m = _mla_q(pc, gq, w_uq_p, id_pe, b, cl, _tile(cl, 512))
            yc_mla = _dense_attn(
                qc_m, pl.BlockSpec((None, None, tqc, MLA_QK_PAD), lambda bi, h, i: (bi, h, i, 0)),
                [(kc_m, _head4_spec(cl, MLA_QK_PAD), vc_m, _head4_spec(cl, MLA_V))],
                b, MLA_HEADS, cl // tqc, tqc, MLA_V, nsub=1)
            nbc = cl // tqc
            yc_na = _dense_attn(
                pc, pl.BlockSpec((tqc, LANE), lambda bi, h, i: (bi * nbc + i, CB_NAQ + h)),
                [(pc, pl.BlockSpec((cl, LANE), lambda bi, h, i: (bi, CB_NAK + h)),
                  pc, pl.BlockSpec((cl, LANE), lambda bi, h, i: (bi, CB_NAV + h)))],
                b, NA_HEADS, nbc, tqc, HEAD_DIM, nsub=1)
            tm_oc = _tile(cl, 256)
            c_new, hc2 = _wo_ln(yc_na, yc_mla, yc_ret, w_o[l].astype(_BF), cf, mod3, l, ctx_rowf,
                                ln1_g[l][None], ln1_b[l][None], alpha, tm_oc)

        g_l = _glu(h2, w_up, l, conv_w[l], conv_b[l][None], b, t, tn_ff)
        xf, h1 = _down_ln(g_l, w_down_b, x_new, mod3, l, lat_row(tm_d), ln2_g[l][None], ln2_b[l][None],
                          alpha, tm_d, with_next=not last)
        if not last:
            g_c = _glu(hc2, w_up, l, conv_w[l], conv_b[l][None], b, cl, tn_ff)
            tm_dc = _tile(cl, 256)
            cf, hc1 = _down_ln(g_c, w_down_b, c_new, mod3, l, ctx_rowf, ln2_g[l][None], ln2_b[l][None],
                               alpha, tm_dc, with_next=True)

    return xf.reshape(b, t, d)
```

```python
import functools

import numpy as np
import jax
import jax.numpy as jnp
from jax import lax
from jax.experimental import pallas as pl
from jax.experimental.pallas import tpu as pltpu

GRID_W = 64
HEAD_DIM = 128
NA_HEADS = 6
MLA_HEADS = 5
RET_HEADS = 5
NA_WIDTH = NA_HEADS * HEAD_DIM
MLA_WIDTH = MLA_HEADS * HEAD_DIM
RET_WIDTH = RET_HEADS * HEAD_DIM
NA_WIN_R = 8
NA_WIN_C = 16
MLA_Q_RANK = 512
MLA_KV_RANK = 512
MLA_NOPE = 128
MLA_ROPE = 64
MLA_V = 128
MLA_QK_PAD = 256
RET_CHUNK = 128
CONV_W = 3
ROPE_BASE = 10000.0
LN_EPS = 1e-5
RMS_EPS = 1e-6
NEG_INF = -1e30
IN_SPLITS = (NA_WIDTH, NA_WIDTH, NA_WIDTH, MLA_Q_RANK, MLA_KV_RANK, MLA_ROPE,
             RET_WIDTH, RET_WIDTH, RET_WIDTH, RET_WIDTH)

LANE = 128
CB_CQ, CB_CKV, CB_NAQ, CB_NAK, CB_NAV = 0, 4, 8, 14, 20
CB_RQ, CB_RK, CB_RV, CB_RG, CB_KPE = 26, 31, 36, 41, 46
PACKED_IN_WIDTH = 48 * LANE
MOD_ROWS = 8
VMEM_LIMIT_MB = 56

_BF = jnp.bfloat16
_F32 = jnp.float32


def _cparams(sem, vmem_mb=VMEM_LIMIT_MB):
    return pltpu.CompilerParams(dimension_semantics=sem, vmem_limit_bytes=vmem_mb << 20)


def _tile(n, pref):
    t = min(n, pref)
    while n % t:
        t //= 2
    return t


def _dot(a, b):
    return jnp.dot(a, b, preferred_element_type=_F32)


def _dot_nt(a, b):
    return lax.dot_general(a, b, (((1,), (1,)), ((), ())), preferred_element_type=_F32)


def _silu(x):
    return x / (1.0 + jnp.exp(-x))


def _resident_spec(shape):
    return pl.BlockSpec(shape, lambda *_: (0,) * len(shape), pipeline_mode=pl.Buffered(1))


def _ada_body(c_ref, w_ref, b_ref, o_ref):
    part = _dot(_silu(c_ref[...]).astype(_BF), w_ref[0].astype(_BF))

    @pl.when(pl.program_id(1) == 0)
    def _():
        o_ref[0] = part + b_ref[0]

    @pl.when(pl.program_id(1) > 0)
    def _():
        o_ref[0] += part


def _ada(cvec, w_ada, b_ada):
    depth, d, n = w_ada.shape
    tk = _tile(d, 256)
    out = pl.pallas_call(
        _ada_body,
        grid=(depth, d // tk),
        in_specs=[pl.BlockSpec((MOD_ROWS, tk), lambda l, k: (0, k)),
                  pl.BlockSpec((1, tk, n), lambda l, k: (l, k, 0)),
                  pl.BlockSpec((1, 1, n), lambda l, k: (l, 0, 0))],
        out_specs=pl.BlockSpec((1, MOD_ROWS, n), lambda l, k: (l, 0, 0)),
        out_shape=jax.ShapeDtypeStruct((depth, MOD_ROWS, n), _F32),
        compiler_params=_cparams(("arbitrary", "arbitrary")),
    )(cvec, w_ada, b_ada.reshape(depth, 1, n))
    return out.reshape(depth * MOD_ROWS, 1, n)


def _mod_spec(d, layer, chunk, row_fn):
    return pl.BlockSpec((1, 1, d), lambda i, *_: (layer * MOD_ROWS + row_fn(i), 0, chunk))


def _modulate_body(x_ref, sh_ref, sc_ref, o_ref):
    o_ref[...] = (x_ref[...] * (1.0 + sc_ref[0]) + sh_ref[0]).astype(o_ref.dtype)


def _modulate(x2, mod3, layer, row_fn, tm):
    m, d = x2.shape
    return pl.pallas_call(
        _modulate_body,
        grid=(m // tm,),
        in_specs=[pl.BlockSpec((tm, d), lambda i: (i, 0)),
                  _mod_spec(d, layer, 0, row_fn), _mod_spec(d, layer, 1, row_fn)],
        out_specs=pl.BlockSpec((tm, d), lambda i: (i, 0)),
        out_shape=jax.ShapeDtypeStruct((m, d), _BF),
        compiler_params=_cparams(("arbitrary",)),
    )(x2, mod3, mod3)


def _mm_body(a_ref, w_ref, o_ref):
    o_ref[...] = _dot(a_ref[...], w_ref[...]).astype(o_ref.dtype)


def _mm(a, w, tm, tn):
    m, k = a.shape
    n = w.shape[1]
    return pl.pallas_call(
        _mm_body,
        grid=(m // tm, n // tn),
        in_specs=[pl.BlockSpec((tm, k), lambda i, j: (i, 0)),
                  pl.BlockSpec((k, tn), lambda i, j: (0, j))],
        out_specs=pl.BlockSpec((tm, tn), lambda i, j: (i, j)),
        out_shape=jax.ShapeDtypeStruct((m, n), _BF),
        compiler_params=_cparams(("arbitrary", "arbitrary")),
    )(a, w)


def _rope_tables(t, rot_dim, identity=False):
    pos = jnp.arange(t)
    row = (pos // GRID_W).astype(_F32)
    col = (pos % GRID_W).astype(_F32)
    nf = rot_dim // 4
    inv = ROPE_BASE ** (-jnp.arange(nf, dtype=_F32) / nf)
    ar = row[:, None] * inv[None]
    ac = col[:, None] * inv[None]
    ang = jnp.concatenate([ar, ar, ac, ac], -1)
    cos, sin = jnp.cos(ang), jnp.sin(ang)
    if identity:
        cos, sin = jnp.ones_like(cos), jnp.zeros_like(sin)
    first = (np.arange(rot_dim) % (rot_dim // 2)) < nf
    sin_a = jnp.where(first, -sin, 0.0)
    sin_b = jnp.where(first, 0.0, sin)
    pad = LANE - rot_dim
    if pad:
        cos = jnp.pad(cos, ((0, 0), (0, pad)), constant_values=1.0)
        sin_a = jnp.pad(sin_a, ((0, 0), (0, pad)))
        sin_b = jnp.pad(sin_b, ((0, 0), (0, pad)))
    return cos, sin_a, sin_b


def _rope(x, cos, sin_a, sin_b, nf):
    return x * cos + pltpu.roll(x, LANE - nf, 1) * sin_a + pltpu.roll(x, nf, 1) * sin_b


def _rms(x, g):
    return x * lax.rsqrt(jnp.mean(x * x, -1, keepdims=True) + RMS_EPS) * g


def _mla_q_body(p_ref, g_ref, w_ref, cos_ref, sa_ref, sb_ref, o_ref):
    xb = _rms(p_ref[...].astype(_F32), g_ref[...]).astype(_BF)
    cos, sa, sb = cos_ref[...], sa_ref[...], sb_ref[...]
    for h in range(MLA_HEADS):
        acc = _dot(xb, w_ref[:, h * MLA_QK_PAD:(h + 1) * MLA_QK_PAD])
        o_ref[0, h, :, 0:LANE] = acc[:, :LANE].astype(_BF)
        o_ref[0, h, :, LANE:] = _rope(acc[:, LANE:], cos, sa, sb, MLA_ROPE // 4).astype(_BF)


def _mla_q(p, g, w, tables, b, t, tm):
    nb = t // tm
    tab_spec = pl.BlockSpec((tm, LANE), lambda i: (i % nb, 0))
    return pl.pallas_call(
        _mla_q_body,
        grid=(b * nb,),
        in_specs=[pl.BlockSpec((tm, MLA_Q_RANK), lambda i: (i, CB_CQ // 4)),
                  pl.BlockSpec((1, MLA_Q_RANK), lambda i: (0, 0)),
                  pl.BlockSpec(w.shape, lambda i: (0, 0)),
                  tab_spec, tab_spec, tab_spec],
        out_specs=pl.BlockSpec((1, MLA_HEADS, tm, MLA_QK_PAD), lambda i: (i // nb, 0, i % nb, 0)),
        out_shape=jax.ShapeDtypeStruct((b, MLA_HEADS, t, MLA_QK_PAD), _BF),
        compiler_params=_cparams(("arbitrary",)),
    )(p, g, w, *tables)


def _mla_kv_body(ckv_ref, kpe_ref, g_ref, w_ref, cos_ref, sa_ref, sb_ref, k_ref, v_ref):
    xb = _rms(ckv_ref[...].astype(_F32), g_ref[...]).astype(_BF)
    pe = _rope(kpe_ref[...].astype(_F32), cos_ref[...], sa_ref[...], sb_ref[...], MLA_ROPE // 4).astype(_BF)
    width = MLA_NOPE + MLA_V
    for h in range(MLA_HEADS):
        acc = _dot(xb, w_ref[:, h * width:(h + 1) * width])
        k_ref[0, h, :, 0:LANE] = acc[:, :MLA_NOPE].astype(_BF)
        k_ref[0, h, :, LANE:] = pe
        v_ref[0, h] = acc[:, MLA_NOPE:].astype(_BF)


def _mla_kv(p, g, w, tables, b, t, tm):
    nb = t // tm
    tab_spec = pl.BlockSpec((tm, LANE), lambda i: (i % nb, 0))
    return pl.pallas_call(
        _mla_kv_body,
        grid=(b * nb,),
        in_specs=[pl.BlockSpec((tm, MLA_KV_RANK), lambda i: (i, CB_CKV // 4)),
                  pl.BlockSpec((tm, LANE), lambda i: (i, CB_KPE)),
                  pl.BlockSpec((1, MLA_KV_RANK), lambda i: (0, 0)),
                  pl.BlockSpec(w.shape, lambda i: (0, 0)),
                  tab_spec, tab_spec, tab_spec],
        out_specs=[pl.BlockSpec((1, MLA_HEADS, tm, MLA_QK_PAD), lambda i: (i // nb, 0, i % nb, 0)),
                   pl.BlockSpec((1, MLA_HEADS, tm, MLA_V), lambda i: (i // nb, 0, i % nb, 0))],
        out_shape=[jax.ShapeDtypeStruct((b, MLA_HEADS, t, MLA_QK_PAD), _BF),
                   jax.ShapeDtypeStruct((b, MLA_HEADS, t, MLA_V), _BF)],
        compiler_params=_cparams(("arbitrary",)),
    )(p, p, g, w, *tables)


def _softmax_pv(scores, values):
    m = functools.reduce(jnp.maximum, [jnp.max(s, -1, keepdims=True) for s in scores])
    p = [jnp.exp(s - m) for s in scores]
    l = functools.reduce(jnp.add, [jnp.sum(pi, -1, keepdims=True) for pi in p])
    acc = functools.reduce(jnp.add, [_dot(pi.astype(_BF), v) for pi, v in zip(p, values)])
    return acc / l


def _attn_body(*refs, nseg, nsub):
    q_ref, o_ref = refs[0], refs[-1]
    ks = [refs[1 + 2 * i][...] for i in range(nseg)]
    vs = [refs[2 + 2 * i][...] for i in range(nseg)]
    ts = q_ref.shape[0] // nsub
    for j in range(nsub):
        q = q_ref[j * ts:(j + 1) * ts, :]
        out = _softmax_pv([_dot_nt(q, k) for k in ks], vs)
        o_ref[j * ts:(j + 1) * ts, :] = out.astype(o_ref.dtype)


def _dense_attn(q, q_spec, kv, b, heads, tq_blocks, tq, dv, nsub):
    args, specs = [q], [q_spec]
    for k, ks, v, vs in kv:
        args += [k, v]
        specs += [ks, vs]
    return pl.pallas_call(
        functools.partial(_attn_body, nseg=len(kv), nsub=nsub),
        grid=(b, heads, tq_blocks),
        in_specs=specs,
        out_specs=pl.BlockSpec((tq, dv), lambda bi, h, i: (bi * tq_blocks + i, h)),
        out_shape=jax.ShapeDtypeStruct((b * tq_blocks * tq, heads * dv), _BF),
        compiler_params=_cparams(("arbitrary", "arbitrary", "arbitrary")),
    )(*args)


def _head4_spec(n, d):
    return pl.BlockSpec((None, None, n, d), lambda bi, h, i: (bi, h, 0, 0))


NA_GROUP = 4
NA_KEY_ROWS = 12


def _na_group_window(gi, rows):
    ngroups = rows // NA_GROUP
    if gi == 0:
        return 0, 0
    if gi == ngroups - 1:
        return 2, rows - NA_KEY_ROWS
    return 1, gi * NA_GROUP - NA_WIN_R // 2


NA_DR = 2 * NA_WIN_R - 1


def _na_col_bias(rpb):
    c = np.arange(GRID_W)
    col_start = np.clip(c - NA_WIN_C // 2, 0, GRID_W - NA_WIN_C)
    kc = np.arange(GRID_W)
    in_win = (kc[None, :] >= col_start[:, None]) & (kc[None, :] < col_start[:, None] + NA_WIN_C)
    dc = np.clip(kc[None, :] - c[:, None] + NA_WIN_C - 1, 0, 2 * NA_WIN_C - 2)
    col_sel = (dc[:, :, None] == np.arange(2 * NA_WIN_C - 1)).astype(np.float32)
    toep = jnp.einsum('hrj,cqj->hrcq', rpb.astype(_F32), col_sel, precision=lax.Precision.HIGHEST)
    toep = jnp.where(in_win[None, None], toep, NEG_INF)
    masked = jnp.full((rpb.shape[0], 1, GRID_W, GRID_W), NEG_INF, _F32)
    return jnp.concatenate([toep, masked], 1)


def _na_block_index():
    g = np.arange(NA_GROUP)[:, None]
    kr = np.arange(NA_KEY_ROWS)[None, :]
    half = NA_WIN_R // 2
    dr = np.stack([kr - g + NA_WIN_R - 1, kr - g + NA_WIN_R - 1 - half, kr - g - 1])
    w0 = np.stack([0 * g, g, half + 0 * g])
    row_ok = (kr[None] >= w0) & (kr[None] < w0 + NA_WIN_R)
    assert np.all((dr[row_ok] >= 0) & (dr[row_ok] < NA_DR))
    return np.where(row_ok, dr, NA_DR)


def _na_body(q_ref, k_ref, v_ref, kc_ref, vc_ref, cb_ref, o_ref, bias_ref, *, rows):
    @pl.when(pl.program_id(1) == 0)
    def _():
        idx = _na_block_index()
        for typ in range(3):
            for g in range(NA_GROUP):
                for kp in range(NA_KEY_ROWS // 2):
                    pair = jnp.concatenate([cb_ref[int(idx[typ, g, 2 * kp])], cb_ref[int(idx[typ, g, 2 * kp + 1])]], 1)
                    bias_ref[typ, g * GRID_W:(g + 1) * GRID_W, kp * 2 * GRID_W:(kp + 1) * 2 * GRID_W] = pair

    kc = kc_ref[...]
    vc = vc_ref[...]
    gq = NA_GROUP * GRID_W
    for gi in range(rows // NA_GROUP):
        typ, ws = _na_group_window(gi, rows)
        q = q_ref[gi * gq:(gi + 1) * gq, :]
        kw = k_ref[ws * GRID_W:(ws + NA_KEY_ROWS) * GRID_W, :]
        vw = v_ref[ws * GRID_W:(ws + NA_KEY_ROWS) * GRID_W, :]
        out = _softmax_pv([_dot_nt(q, kw) + bias_ref[typ], _dot_nt(q, kc)], [vw, vc])
        o_ref[gi * gq:(gi + 1) * gq, :] = out.astype(o_ref.dtype)


def _na_attn(p, pc, bias, b, t, c):
    rows = t // GRID_W
    assert t % GRID_W == 0 and rows % NA_GROUP == 0 and rows >= NA_KEY_ROWS + NA_GROUP and 2 * GRID_W == LANE
    return pl.pallas_call(
        functools.partial(_na_body, rows=rows),
        grid=(NA_HEADS, b),
        in_specs=[pl.BlockSpec((t, LANE), lambda h, bi: (bi, CB_NAQ + h)),
                  pl.BlockSpec((t, LANE), lambda h, bi: (bi, CB_NAK + h)),
                  pl.BlockSpec((t, LANE), lambda h, bi: (bi, CB_NAV + h)),
                  pl.BlockSpec((c, LANE), lambda h, bi: (bi, CB_NAK + h)),
                  pl.BlockSpec((c, LANE), lambda h, bi: (bi, CB_NAV + h)),
                  pl.BlockSpec((None, NA_DR + 1, GRID_W, GRID_W), lambda h, bi: (h, 0, 0, 0))],
        out_specs=pl.BlockSpec((t, LANE), lambda h, bi: (bi, h)),
        out_shape=jax.ShapeDtypeStruct((b * t, NA_WIDTH), _BF),
        scratch_shapes=[pltpu.VMEM((3, NA_GROUP * GRID_W, NA_KEY_ROWS * GRID_W), _F32)],
        compiler_params=_cparams(("arbitrary", "arbitrary")),
    )(p, p, p, pc, pc, bias)


def _ret_body(lg_ref, q_ref, k_ref, v_ref, g_ref, s0f_ref, s0b_ref, cos_ref, sa_ref, sb_ref,
              o_ref, sf_ref, sbo_ref, kr_scr, u_hist, st_hist, *, n):
    h = pl.program_id(0)
    lgf = lg_ref[0, h]
    lgb = lg_ref[1, h]
    L = RET_CHUNK
    pos_c = lax.broadcasted_iota(jnp.int32, (L, 1), 0).astype(_F32)
    pos_r = lax.broadcasted_iota(jnp.int32, (1, L), 1).astype(_F32)
    diff = pos_c - pos_r
    decay = jnp.where(diff > 0, jnp.exp(jnp.maximum(diff, 0.0) * lgf),
                      jnp.where(diff < 0, jnp.exp(jnp.maximum(-diff, 0.0) * lgb), 2.0))
    qdec_f = jnp.exp((pos_c + 1.0) * lgf)
    kdec_f = jnp.exp((L - 1.0 - pos_c) * lgf)
    qdec_b = jnp.exp((L - pos_c) * lgb)
    kdec_b = jnp.exp(pos_c * lgb)
    cd_f = jnp.exp(L * lgf)
    cd_b = jnp.exp(L * lgb)
    nf = HEAD_DIM // 4

    def rows_of(i):
        return slice(i * L, (i + 1) * L)

    def roped(ref, i):
        r = rows_of(i)
        return _rope(ref[r, :].astype(_F32), cos_ref[r, :], sa_ref[r, :], sb_ref[r, :], nf)

    for i in range(n):
        k = roped(k_ref, i)
        kr_scr[rows_of(i), :] = k.astype(_BF)
        kd = jnp.concatenate([k * kdec_f, k * kdec_b], 1).astype(_BF)
        u_hist[i] = lax.dot_general(kd, v_ref[rows_of(i), :], (((0,), (0,)), ((), ())),
                                    preferred_element_type=_F32)

    state = s0f_ref[...]
    for i in range(n):
        st_hist[i, 0:HEAD_DIM, :] = state.astype(_BF)
        state = cd_f * state + u_hist[i, 0:HEAD_DIM, :]
    sf_ref[...] = state
    state = s0b_ref[...]
    for i in reversed(range(n)):
        st_hist[i, HEAD_DIM:, :] = state.astype(_BF)
        state = cd_b * state + u_hist[i, HEAD_DIM:, :]
    sbo_ref[...] = state

    for i in range(n):
        r = rows_of(i)
        q = roped(q_ref, i)
        a = _dot_nt(q.astype(_BF), kr_scr[r, :]) * decay
        qd = jnp.concatenate([q * qdec_f, q * qdec_b], 1).astype(_BF)
        o = _dot(a.astype(_BF), v_ref[r, :]) + _dot(qd, st_hist[i])
        mu = jnp.mean(o, -1, keepdims=True)
        oc = o - mu
        var = jnp.mean(oc * oc, -1, keepdims=True)
        y = oc * lax.rsqrt(var + LN_EPS) * _silu(g_ref[r, :].astype(_F32))
        o_ref[r, :] = y.astype(o_ref.dtype)


def _retention(p, lg, s0f, s0b, tables, b, t):
    n = t // RET_CHUNK
    col = lambda cb: pl.BlockSpec((t, LANE), lambda h, bi: (bi, cb + h))
    st_spec = pl.BlockSpec((None, None, HEAD_DIM, HEAD_DIM), lambda h, bi: (bi, h, 0, 0))
    tab_spec = pl.BlockSpec((t, LANE), lambda h, bi: (0, 0))
    st_shape = jax.ShapeDtypeStruct((b, RET_HEADS, HEAD_DIM, HEAD_DIM), _F32)
    return pl.pallas_call(
        functools.partial(_ret_body, n=n),
        grid=(RET_HEADS, b),
        in_specs=[pl.BlockSpec(memory_space=pltpu.SMEM),
                  col(CB_RQ), col(CB_RK), col(CB_RV), col(CB_RG), st_spec, st_spec,
                  tab_spec, tab_spec, tab_spec],
        out_specs=[pl.BlockSpec((t, LANE), lambda h, bi: (bi, h)), st_spec, st_spec],
        out_shape=[jax.ShapeDtypeStruct((b * t, RET_WIDTH), _BF), st_shape, st_shape],
        scratch_shapes=[pltpu.VMEM((t, HEAD_DIM), _BF),
                        pltpu.VMEM((n, 2 * HEAD_DIM, HEAD_DIM), _F32),
                        pltpu.VMEM((n, 2 * HEAD_DIM, HEAD_DIM), _BF)],
        compiler_params=_cparams(("arbitrary", "arbitrary")),
    )(lg, p, p, p, p, s0f, s0b, *tables)


def _ln_epilogue(y, x_ref, gate_ref, lng_ref, lnb_ref, alpha, xo_ref, ho_ref, sh_ref, sc_ref):
    z = alpha * x_ref[...] + gate_ref[0] * y
    mu = jnp.mean(z, -1, keepdims=True)
    zc = z - mu
    var = jnp.mean(zc * zc, -1, keepdims=True)
    xn = zc * lax.rsqrt(var + LN_EPS) * lng_ref[...] + lnb_ref[...]
    xo_ref[...] = xn
    if ho_ref is not None:
        ho_ref[...] = (xn * (1.0 + sc_ref[0]) + sh_ref[0]).astype(ho_ref.dtype)


def _wo_ln_body(a1_ref, a2_ref, a3_ref, w_ref, x_ref, gate_ref, sh_ref, sc_ref, lng_ref, lnb_ref,
                xo_ref, ho_ref, *, alpha):
    o1, o2 = NA_WIDTH, NA_WIDTH + MLA_WIDTH
    y = _dot(a1_ref[...], w_ref[0:o1, :]) + _dot(a2_ref[...], w_ref[o1:o2, :]) + _dot(a3_ref[...], w_ref[o2:, :])
    _ln_epilogue(y, x_ref, gate_ref, lng_ref, lnb_ref, alpha, xo_ref, ho_ref, sh_ref, sc_ref)


def _wo_ln(a1, a2, a3, w, x2, mod3, layer, row_fn, lng, lnb, alpha, tm):
    m, d = x2.shape
    row = lambda width: pl.BlockSpec((tm, width), lambda i: (i, 0))
    vec = pl.BlockSpec((1, d), lambda i: (0, 0))
    return pl.pallas_call(
        functools.partial(_wo_ln_body, alpha=alpha),
        grid=(m // tm,),
        in_specs=[row(NA_WIDTH), row(MLA_WIDTH), row(RET_WIDTH),
                  _resident_spec(w.shape),
                  row(d),
                  _mod_spec(d, layer, 2, row_fn), _mod_spec(d, layer, 3, row_fn), _mod_spec(d, layer, 4, row_fn),
                  vec, vec],
        out_specs=[row(d), row(d)],
        out_shape=[jax.ShapeDtypeStruct((m, d), _F32), jax.ShapeDtypeStruct((m, d), _BF)],
        compiler_params=_cparams(("arbitrary",)),
    )(a1, a2, a3, w, x2, mod3, mod3, mod3, lng, lnb)


def _down_ln_body(*refs, alpha, with_next):
    if with_next:
        g_ref, w_ref, x_ref, gate_ref, sh_ref, sc_ref, lng_ref, lnb_ref, xo_ref, ho_ref = refs
    else:
        g_ref, w_ref, x_ref, gate_ref, lng_ref, lnb_ref, xo_ref = refs
        sh_ref = sc_ref = ho_ref = None
    y = _dot(g_ref[...], w_ref[...])
    _ln_epilogue(y, x_ref, gate_ref, lng_ref, lnb_ref, alpha, xo_ref, ho_ref, sh_ref, sc_ref)


def _down_ln(g, w, x2, mod3, layer, row_fn, lng, lnb, alpha, tm, with_next):
    m, d = x2.shape
    dff = g.shape[1]
    row = lambda width: pl.BlockSpec((tm, width), lambda i: (i, 0))
    vec = pl.BlockSpec((1, d), lambda i: (0, 0))
    in_specs = [row(dff), _resident_spec(w.shape), row(d), _mod_spec(d, layer, 5, row_fn)]
    args = [g, w, x2, mod3]
    if with_next:
        in_specs += [_mod_spec(d, layer + 1, 0, row_fn), _mod_spec(d, layer + 1, 1, row_fn)]
        args += [mod3, mod3]
    in_specs += [vec, vec]
    args += [lng, lnb]
    out_specs = [row(d)] + ([row(d)] if with_next else [])
    out_shape = [jax.ShapeDtypeStruct((m, d), _F32)] + ([jax.ShapeDtypeStruct((m, d), _BF)] if with_next else [])
    out = pl.pallas_call(
        functools.partial(_down_ln_body, alpha=alpha, with_next=with_next),
        grid=(m // tm,),
        in_specs=in_specs, out_specs=out_specs, out_shape=out_shape,
        compiler_params=_cparams(("arbitrary",)),
    )(*args)
    return (out[0], out[1]) if with_next else (out[0], None)


def _glu_body(h_ref, wa_ref, wu_ref, cw_ref, cb_ref, o_ref, a_scr, wab, wub, *, t, rc):
    @pl.when(pl.program_id(1) == 0)
    def _():
        wab[...] = wa_ref[...].astype(_BF)
        wub[...] = wu_ref[...].astype(_BF)

    nchunk = t // rc
    halo = 8
    tn = o_ref.shape[1]
    zeros = jnp.zeros((halo, tn), _F32)
    a_scr[0:halo, :] = zeros
    a_scr[halo + t:, :] = zeros
    w0, w1, w2 = cw_ref[0:1, :], cw_ref[1:2, :], cw_ref[2:3, :]
    bias = cb_ref[...]
    u_prev = None
    for c in range(nchunk + 1):
        u_cur = None
        if c < nchunk:
            hc = h_ref[c * rc:(c + 1) * rc, :]
            a_scr[halo + c * rc:halo + (c + 1) * rc, :] = _dot(hc, wab[...])
            u_cur = _dot(hc, wub[...])
        if c > 0:
            base = halo + (c - 1) * rc
            lo = a_scr[base - 1:base - 1 + rc, :]
            mid = a_scr[base:base + rc, :]
            hi = a_scr[base + 1:base + 1 + rc, :]
            acc = bias + lo * w0 + mid * w1 + hi * w2
            o_ref[(c - 1) * rc:c * rc, :] = (_silu(acc) * u_prev).astype(o_ref.dtype)
        u_prev = u_cur


def _glu(h, w_up, layer, conv_w, conv_b, b, t, tn):
    d = h.shape[1]
    dff = conv_w.shape[1]
    nj = dff // tn
    rc = _tile(t, 512)
    return pl.pallas_call(
        functools.partial(_glu_body, t=t, rc=rc),
        grid=(nj, b),
        in_specs=[pl.BlockSpec((t, d), lambda j, bi: (bi, 0)),
                  pl.BlockSpec((None, d, tn), lambda j, bi: (layer, 0, j)),
                  pl.BlockSpec((None, d, tn), lambda j, bi: (layer, 0, nj + j)),
                  pl.BlockSpec((CONV_W, tn), lambda j, bi: (0, j)),
                  pl.BlockSpec((1, tn), lambda j, bi: (0, j))],
        out_specs=pl.BlockSpec((t, tn), lambda j, bi: (bi, j)),
        out_shape=jax.ShapeDtypeStruct((b * t, dff), _BF),
        scratch_shapes=[pltpu.VMEM((t + 16, tn), _F32), pltpu.VMEM((d, tn), _BF), pltpu.VMEM((d, tn), _BF)],
        compiler_params=_cparams(("arbitrary", "arbitrary")),
    )(h, w_up, w_up, conv_w, conv_b)


def _pack_w_in_body(w_ref, o_ref):
    src = dict(zip(("na_q", "na_k", "na_v", "cq", "ckv", "kpe", "r_q", "r_k", "r_v", "r_g"),
                   [int(i) for i in np.cumsum((0,) + IN_SPLITS[:-1])]))
    scale = HEAD_DIM ** -0.5

    def put(dst_block, name, width, mul=None):
        x = w_ref[:, src[name]:src[name] + width]
        if mul is not None:
            x = x * mul
        o_ref[:, dst_block * LANE:dst_block * LANE + width] = x.astype(o_ref.dtype)

    put(CB_CQ, "cq", MLA_Q_RANK + MLA_KV_RANK)
    put(CB_NAQ, "na_q", NA_WIDTH, scale)
    put(CB_NAK, "na_k", 2 * NA_WIDTH)
    put(CB_RQ, "r_q", RET_WIDTH)
    put(CB_RK, "r_k", RET_WIDTH, scale)
    put(CB_RV, "r_v", 2 * RET_WIDTH)
    rows = w_ref.shape[0]
    tail = jnp.concatenate([w_ref[:, src["kpe"]:src["kpe"] + MLA_ROPE],
                            jnp.zeros((rows, PACKED_IN_WIDTH - CB_KPE * LANE - MLA_ROPE), _F32)], 1)
    o_ref[:, CB_KPE * LANE:] = tail.astype(o_ref.dtype)


def _pack_w_in(w_in, layer):
    _, d, n = w_in.shape
    tk = _tile(d, 256)
    return pl.pallas_call(
        _pack_w_in_body,
        grid=(d // tk,),
        in_specs=[pl.BlockSpec((None, tk, n), lambda i: (layer, i, 0))],
        out_specs=pl.BlockSpec((tk, PACKED_IN_WIDTH), lambda i: (i, 0)),
        out_shape=jax.ShapeDtypeStruct((d, PACKED_IN_WIDTH), _BF),
        compiler_params=_cparams(("arbitrary",)),
    )(w_in)


def _pack_w_uq(w):
    r = w.shape[0]
    w = w.reshape(r, MLA_HEADS, MLA_NOPE + MLA_ROPE) * ((MLA_NOPE + MLA_ROPE) ** -0.5)
    w = jnp.pad(w, ((0, 0), (0, 0), (0, MLA_QK_PAD - MLA_NOPE - MLA_ROPE)))
    return w.reshape(r, MLA_HEADS * MLA_QK_PAD).astype(_BF)


def kernel(x, c, ctx, c_ctx, w_ada, b_ada, w_in, mla_q_norm, mla_kv_norm, w_uq, w_ukv, na_rpb, ret_decay,
           w_o, ln1_g, ln1_b, w_up, conv_w, conv_b, w_down, ln2_g, ln2_b):
    b, t, d = x.shape
    cl = ctx.shape[1]
    depth = w_ada.shape[0]
    dff = conv_w.shape[-1]
    assert b + 1 <= MOD_ROWS
    alpha = (2 * depth) ** 0.25
    ctx_row = b

    cvec = jnp.concatenate([c, c_ctx[None], jnp.zeros((MOD_ROWS - b - 1, d), c.dtype)], 0)
    mod3 = _ada(cvec, w_ada, b_ada)

    tm_l = _tile(t, 1024)
    tm_c = _tile(cl, 1024)
    lat_row = lambda tm: (lambda i: i // (t // tm))
    ctx_rowf = lambda i: ctx_row

    rope_pe = _rope_tables(t, MLA_ROPE)
    rope_ret = _rope_tables(t, HEAD_DIM)
    id_pe = _rope_tables(cl, MLA_ROPE, identity=True)
    id_ret = _rope_tables(cl, HEAD_DIM, identity=True)

    xf = x.reshape(b * t, d)
    cf = ctx.reshape(b * cl, d)
    h1 = _modulate(xf, mod3, 0, lat_row(tm_l), tm_l)
    hc1 = _modulate(cf, mod3, 0, ctx_rowf, tm_c)

    tq = _tile(t, 1024)
    tqc = _tile(cl, 256)
    tn_ff = _tile(dff, 512)
    zero_state = jnp.zeros((b, RET_HEADS, HEAD_DIM, HEAD_DIM), _F32)

    for l in range(depth):
        last = l == depth - 1
        w_in_p = _pack_w_in(w_in, l)
        w_uq_p = _pack_w_uq(w_uq[l])
        w_ukv_p = w_ukv[l].astype(_BF)
        gq = mla_q_norm[l][None]
        gkv = mla_kv_norm[l][None]
        lg = jnp.log1p(-jnp.exp2(ret_decay[l].astype(_F32)))
        bias = _na_col_bias(na_rpb[l])

        p = _mm(h1, w_in_p, tm_l, 1024)
        pc = _mm(hc1, w_in_p, tm_c, 1024)

        kc_m, vc_m = _mla_kv(pc, gkv, w_ukv_p, id_pe, b, cl, _tile(cl, 512))
        kl_m, vl_m = _mla_kv(p, gkv, w_ukv_p, rope_pe, b, t, _tile(t, 512))
        q_m = _mla_q(p, gq, w_uq_p, rope_pe, b, t, _tile(t, 512))
        y_mla = _dense_attn(
            q_m, pl.BlockSpec((None, None, tq, MLA_QK_PAD), lambda bi, h, i: (bi, h, i, 0)),
            [(kc_m, _head4_spec(cl, MLA_QK_PAD), vc_m, _head4_spec(cl, MLA_V)),
             (kl_m, _head4_spec(t, MLA_QK_PAD), vl_m, _head4_spec(t, MLA_V))],
            b, MLA_HEADS, t // tq, tq, MLA_V, nsub=tq // _tile(tq, 256))

        y_na = _na_attn(p, pc, bias, b, t, cl)

        yc_ret, s_f, s_b = _retention(pc, lg, zero_state, zero_state, id_ret, b, cl)
        y_ret, _, _ = _retention(p, lg, s_f, s_b, rope_ret, b, t)

        tm_o = _tile(t, 256)
        x_new, h2 = _wo_ln(y_na, y_mla, y_ret, w_o[l].astype(_BF), xf, mod3, l, lat_row(tm_o),
                           ln1_g[l][None], ln1_b[l][None], alpha, tm_o)

        w_down_b = w_down[l].astype(_BF)
        tm_d = _tile(t, 256)

        if not last:
            qc_m = _mla_q(pc, gq, w_uq_p, id_pe, b, cl, _tile(cl, 512))
            yc_mla = _dense_attn(
                qc_m, pl.BlockSpec((None, None, tqc, MLA_QK_PAD), lambda bi, h, i: (bi, h, i, 0)),
                [(kc_m, _head4_spec(cl, MLA_QK_PAD), vc_m, _head4_spec(cl, MLA_V))],
                b, MLA_HEADS, cl // tqc, tqc, MLA_V, nsub=1)
            nbc = cl // tqc
            yc_na = _dense_attn(
                pc, pl.BlockSpec((tqc, LANE), lambda bi, h, i: (bi * nbc + i, CB_NAQ + h)),
                [(pc, pl.BlockSpec((cl, LANE), lambda bi, h, i: (bi, CB_NAK + h)),
                  pc, pl.BlockSpec((cl, LANE), lambda bi, h, i: (bi, CB_NAV + h)))],
                b, NA_HEADS, nbc, tqc, HEAD_DIM, nsub=1)
            tm_oc = _tile(cl, 256)
            c_new, hc2 = _wo_ln(yc_na, yc_mla, yc_ret, w_o[l].astype(_BF), cf, mod3, l, ctx_rowf,
                                ln1_g[l][None], ln1_b[l][None], alpha, tm_oc)

        g_l = _glu(h2, w_up, l, conv_w[l], conv_b[l][None], b, t, tn_ff)
        xf, h1 = _down_ln(g_l, w_down_b, x_new, mod3, l, lat_row(tm_d), ln2_g[l][None], ln2_b[l][None],
                          alpha, tm_d, with_next=not last)
        if not last:
            g_c = _glu(hc2, w_up, l, conv_w[l], conv_b[l][None], b, cl, tn_ff)
            tm_dc = _tile(cl, 256)
            cf, hc1 = _down_ln(g_c, w_down_b, c_new, mod3, l, ctx_rowf, ln2_g[l][None], ln2_b[l][None],
                               alpha, tm_dc, with_next=True)

    return xf.reshape(b, t, d)
```

```python
import functools

import numpy as np
import jax
import jax.numpy as jnp
from jax import lax
from jax.experimental import pallas as pl
from jax.experimental.pallas import tpu as pltpu

GRID_W = 64
HEAD_DIM = 128
NA_HEADS = 6
MLA_HEADS = 5
RET_HEADS = 5
NA_WIDTH = NA_HEADS * HEAD_DIM
MLA_WIDTH = MLA_HEADS * HEAD_DIM
RET_WIDTH = RET_HEADS * HEAD_DIM
NA_WIN_R = 8
NA_WIN_C = 16
MLA_Q_RANK = 512
MLA_KV_RANK = 512
MLA_NOPE = 128
MLA_ROPE = 64
MLA_V = 128
MLA_QK_PAD = 256
RET_CHUNK = 128
CONV_W = 3
ROPE_BASE = 10000.0
LN_EPS = 1e-5
RMS_EPS = 1e-6
NEG_INF = -1e30
LOG2E = 1.4426950408889634
IN_SPLITS = (NA_WIDTH, NA_WIDTH, NA_WIDTH, MLA_Q_RANK, MLA_KV_RANK, MLA_ROPE,
             RET_WIDTH, RET_WIDTH, RET_WIDTH, RET_WIDTH)

LANE = 128
CB_CQ, CB_CKV, CB_NAQ, CB_NAK, CB_NAV = 0, 4, 8, 14, 20
CB_RQ, CB_RK, CB_RV, CB_RG, CB_KPE = 26, 31, 36, 41, 46
PACKED_IN_WIDTH = 48 * LANE
MOD_ROWS = 8
VMEM_LIMIT_MB = 56

_BF = jnp.bfloat16
_F32 = jnp.float32


def _cparams(sem, vmem_mb=VMEM_LIMIT_MB):
    return pltpu.CompilerParams(dimension_semantics=sem, vmem_limit_bytes=vmem_mb << 20)


def _tile(n, pref):
    t = min(n, pref)
    while n % t:
        t //= 2
    return t


def _dot(a, b):
    return jnp.dot(a, b, preferred_element_type=_F32)


def _dot_nt(a, b):
    return lax.dot_general(a, b, (((1,), (1,)), ((), ())), preferred_element_type=_F32)


def _silu(x):
    return x / (1.0 + jnp.exp(-x))


def _resident_spec(shape):
    return pl.BlockSpec(shape, lambda *_: (0,) * len(shape), pipeline_mode=pl.Buffered(1))


def _ada_body(c_ref, w_ref, b_ref, o_ref):
    part = _dot(_silu(c_ref[...]).astype(_BF), w_ref[0].astype(_BF))

    @pl.when(pl.program_id(1) == 0)
    def _():
        o_ref[0] = part + b_ref[0]

    @pl.when(pl.program_id(1) > 0)
    def _():
        o_ref[0] += part


def _ada(cvec, w_ada, b_ada):
    depth, d, n = w_ada.shape
    tk = _tile(d, 256)
    out = pl.pallas_call(
        _ada_body,
        grid=(depth, d // tk),
        in_specs=[pl.BlockSpec((MOD_ROWS, tk), lambda l, k: (0, k)),
                  pl.BlockSpec((1, tk, n), lambda l, k: (l, k, 0)),
                  pl.BlockSpec((1, 1, n), lambda l, k: (l, 0, 0))],
        out_specs=pl.BlockSpec((1, MOD_ROWS, n), lambda l, k: (l, 0, 0)),
        out_shape=jax.ShapeDtypeStruct((depth, MOD_ROWS, n), _F32),
        compiler_params=_cparams(("arbitrary", "arbitrary")),
    )(cvec, w_ada, b_ada.reshape(depth, 1, n))
    return out.reshape(depth * MOD_ROWS, 1, n)


def _mod_spec(d, layer, chunk, row_fn):
    return pl.BlockSpec((1, 1, d), lambda i, *_: (layer * MOD_ROWS + row_fn(i), 0, chunk))


def _modulate_body(x_ref, sh_ref, sc_ref, o_ref):
    o_ref[...] = (x_ref[...] * (1.0 + sc_ref[0]) + sh_ref[0]).astype(o_ref.dtype)


def _modulate(x2, mod3, layer, row_fn, tm):
    m, d = x2.shape
    return pl.pallas_call(
        _modulate_body,
        grid=(m // tm,),
        in_specs=[pl.BlockSpec((tm, d), lambda i: (i, 0)),
                  _mod_spec(d, layer, 0, row_fn), _mod_spec(d, layer, 1, row_fn)],
        out_specs=pl.BlockSpec((tm, d), lambda i: (i, 0)),
        out_shape=jax.ShapeDtypeStruct((m, d), _BF),
        compiler_params=_cparams(("arbitrary",)),
    )(x2, mod3, mod3)


def _mm_body(a_ref, w_ref, o_ref):
    o_ref[...] = _dot(a_ref[...], w_ref[...]).astype(o_ref.dtype)


def _mm(a, w, tm, tn):
    m, k = a.shape
    n = w.shape[1]
    return pl.pallas_call(
        _mm_body,
        grid=(m // tm, n // tn),
        in_specs=[pl.BlockSpec((tm, k), lambda i, j: (i, 0)),
                  pl.BlockSpec((k, tn), lambda i, j: (0, j))],
        out_specs=pl.BlockSpec((tm, tn), lambda i, j: (i, j)),
        out_shape=jax.ShapeDtypeStruct((m, n), _BF),
        compiler_params=_cparams(("arbitrary", "arbitrary")),
    )(a, w)


def _rope_tables(t, rot_dim, identity=False):
    pos = jnp.arange(t)
    row = (pos // GRID_W).astype(_F32)
    col = (pos % GRID_W).astype(_F32)
    nf = rot_dim // 4
    inv = ROPE_BASE ** (-jnp.arange(nf, dtype=_F32) / nf)
    ar = row[:, None] * inv[None]
    ac = col[:, None] * inv[None]
    ang = jnp.concatenate([ar, ar, ac, ac], -1)
    cos, sin = jnp.cos(ang), jnp.sin(ang)
    if identity:
        cos, sin = jnp.ones_like(cos), jnp.zeros_like(sin)
    first = (np.arange(rot_dim) % (rot_dim // 2)) < nf
    sin_a = jnp.where(first, -sin, 0.0)
    sin_b = jnp.where(first, 0.0, sin)
    pad = LANE - rot_dim
    if pad:
        cos = jnp.pad(cos, ((0, 0), (0, pad)), constant_values=1.0)
        sin_a = jnp.pad(sin_a, ((0, 0), (0, pad)))
        sin_b = jnp.pad(sin_b, ((0, 0), (0, pad)))
    return cos, sin_a, sin_b


def _rope(x, cos, sin_a, sin_b, nf):
    return x * cos + pltpu.roll(x, LANE - nf, 1) * sin_a + pltpu.roll(x, nf, 1) * sin_b


def _rms(x, g):
    return x * lax.rsqrt(jnp.mean(x * x, -1, keepdims=True) + RMS_EPS) * g


def _mla_q_body(p_ref, g_ref, w_ref, cos_ref, sa_ref, sb_ref, o_ref):
    xb = _rms(p_ref[...].astype(_F32), g_ref[...]).astype(_BF)
    cos, sa, sb = cos_ref[...], sa_ref[...], sb_ref[...]
    for h in range(MLA_HEADS):
        acc = _dot(xb, w_ref[:, h * MLA_QK_PAD:(h + 1) * MLA_QK_PAD])
        o_ref[0, h, :, 0:LANE] = acc[:, :LANE].astype(_BF)
        o_ref[0, h, :, LANE:] = _rope(acc[:, LANE:], cos, sa, sb, MLA_ROPE // 4).astype(_BF)


def _mla_q(p, g, w, tables, b, t, tm):
    nb = t // tm
    tab_spec = pl.BlockSpec((tm, LANE), lambda i: (i % nb, 0))
    return pl.pallas_call(
        _mla_q_body,
        grid=(b * nb,),
        in_specs=[pl.BlockSpec((tm, MLA_Q_RANK), lambda i: (i, CB_CQ // 4)),
                  pl.BlockSpec((1, MLA_Q_RANK), lambda i: (0, 0)),
                  pl.BlockSpec(w.shape, lambda i: (0, 0)),
                  tab_spec, tab_spec, tab_spec],
        out_specs=pl.BlockSpec((1, MLA_HEADS, tm, MLA_QK_PAD), lambda i: (i // nb, 0, i % nb, 0)),
        out_shape=jax.ShapeDtypeStruct((b, MLA_HEADS, t, MLA_QK_PAD), _BF),
        compiler_params=_cparams(("arbitrary",)),
    )(p, g, w, *tables)


VT_ROWS = HEAD_DIM + 16


def _values_t(v):
    n = v.shape[0]
    ones = (lax.broadcasted_iota(jnp.int32, (VT_ROWS - HEAD_DIM, n), 0) == 0).astype(_BF)
    return jnp.concatenate([v.astype(_F32).T.astype(_BF), ones], 0)


def _mla_kv_body(ckv_ref, kpe_ref, g_ref, w_ref, cos_ref, sa_ref, sb_ref, k_ref, vt_ref):
    xb = _rms(ckv_ref[...].astype(_F32), g_ref[...]).astype(_BF)
    pe = _rope(kpe_ref[...].astype(_F32), cos_ref[...], sa_ref[...], sb_ref[...], MLA_ROPE // 4).astype(_BF)
    width = MLA_NOPE + MLA_V
    for h in range(MLA_HEADS):
        acc = _dot(xb, w_ref[:, h * width:(h + 1) * width])
        k_ref[0, h, :, 0:LANE] = acc[:, :MLA_NOPE].astype(_BF)
        k_ref[0, h, :, LANE:] = pe
        vt_ref[0, h] = _values_t(acc[:, MLA_NOPE:])


def _mla_kv(p, g, w, tables, b, t, tm):
    nb = t // tm
    tab_spec = pl.BlockSpec((tm, LANE), lambda i: (i % nb, 0))
    return pl.pallas_call(
        _mla_kv_body,
        grid=(b * nb,),
        in_specs=[pl.BlockSpec((tm, MLA_KV_RANK), lambda i: (i, CB_CKV // 4)),
                  pl.BlockSpec((tm, LANE), lambda i: (i, CB_KPE)),
                  pl.BlockSpec((1, MLA_KV_RANK), lambda i: (0, 0)),
                  pl.BlockSpec(w.shape, lambda i: (0, 0)),
                  tab_spec, tab_spec, tab_spec],
        out_specs=[pl.BlockSpec((1, MLA_HEADS, tm, MLA_QK_PAD), lambda i: (i // nb, 0, i % nb, 0)),
                   pl.BlockSpec((1, MLA_HEADS, VT_ROWS, tm), lambda i: (i // nb, 0, 0, i % nb))],
        out_shape=[jax.ShapeDtypeStruct((b, MLA_HEADS, t, MLA_QK_PAD), _BF),
                   jax.ShapeDtypeStruct((b, MLA_HEADS, VT_ROWS, t), _BF)],
        compiler_params=_cparams(("arbitrary",)),
    )(p, p, g, w, *tables)


def _skewed(n, stages):
    carry = [None] * n
    for step in range(n + len(stages) - 1):
        for s, stage in enumerate(stages):
            j = step - s
            if 0 <= j < n:
                carry[j] = stage(j, carry[j])


def _attn_stages(scores_fn, values_fn, store_fn):
    def s_scores(j, _):
        return scores_fn(j)

    def s_max(j, sc):
        return sc, functools.reduce(jnp.maximum, [jnp.max(s, 0, keepdims=True) for s in sc])

    def s_exp(j, c):
        sc, m = c
        return [jnp.exp2(s - m).astype(_BF) for s in sc]

    def s_pv(j, p):
        acc = functools.reduce(jnp.add, [_dot(vt, pi) for vt, pi in zip(values_fn(j), p)])
        store_fn(j, (acc[0:HEAD_DIM, :] / acc[HEAD_DIM:HEAD_DIM + 1, :]).T)

    return [s_scores, s_max, s_exp, s_pv]


def _attn_body(*refs, nseg, nsub, v_row_major):
    q_ref, o_ref = refs[0], refs[-1]
    ks = [refs[1 + 2 * i][...] for i in range(nseg)]
    vts = [refs[2 + 2 * i][...] for i in range(nseg)]
    if v_row_major:
        vts = [_values_t(v) for v in vts]
    ts = q_ref.shape[0] // nsub
    rows = lambda j: slice(j * ts, (j + 1) * ts)

    def store(j, out):
        o_ref[rows(j), :] = out.astype(o_ref.dtype)

    _skewed(nsub, _attn_stages(lambda j: [_dot_nt(k, q_ref[rows(j), :]) for k in ks], lambda j: vts, store))


def _dense_attn(q, q_spec, kv, b, heads, tq_blocks, tq, dv, nsub, v_row_major=False):
    args, specs = [q], [q_spec]
    for k, ks, v, vs in kv:
        args += [k, v]
        specs += [ks, vs]
    return pl.pallas_call(
        functools.partial(_attn_body, nseg=len(kv), nsub=nsub, v_row_major=v_row_major),
        grid=(b, heads, tq_blocks),
        in_specs=specs,
        out_specs=pl.BlockSpec((tq, dv), lambda bi, h, i: (bi * tq_blocks + i, h)),
        out_shape=jax.ShapeDtypeStruct((b * tq_blocks * tq, heads * dv), _BF),
        compiler_params=_cparams(("arbitrary", "arbitrary", "arbitrary")),
    )(*args)


def _head4_spec(rows, cols):
    return pl.BlockSpec((None, None, rows, cols), lambda bi, h, i: (bi, h, 0, 0))


NA_GROUP = 4
NA_KEY_ROWS = 12


def _na_group_window(gi, rows):
    ngroups = rows // NA_GROUP
    if gi == 0:
        return 0, 0
    if gi == ngroups - 1:
        return 2, rows - NA_KEY_ROWS
    return 1, gi * NA_GROUP - NA_WIN_R // 2


NA_DR = 2 * NA_WIN_R - 1


def _na_col_bias(rpb):
    c = np.arange(GRID_W)
    col_start = np.clip(c - NA_WIN_C // 2, 0, GRID_W - NA_WIN_C)
    kc = np.arange(GRID_W)
    in_win = (kc[None, :] >= col_start[:, None]) & (kc[None, :] < col_start[:, None] + NA_WIN_C)
    dc = np.clip(kc[None, :] - c[:, None] + NA_WIN_C - 1, 0, 2 * NA_WIN_C - 2)
    col_sel = (dc[:, :, None] == np.arange(2 * NA_WIN_C - 1)).astype(np.float32)
    toep = jnp.einsum('hrj,cqj->hrqc', rpb.astype(_F32), col_sel, precision=lax.Precision.HIGHEST)
    toep = jnp.where(in_win.T[None, None], toep * LOG2E, NEG_INF)
    masked = jnp.full((rpb.shape[0], 1, GRID_W, GRID_W), NEG_INF, _F32)
    return jnp.concatenate([toep, masked], 1)


def _na_block_index():
    g = np.arange(NA_GROUP)[:, None]
    kr = np.arange(NA_KEY_ROWS)[None, :]
    half = NA_WIN_R // 2
    dr = np.stack([kr - g + NA_WIN_R - 1, kr - g + NA_WIN_R - 1 - half, kr - g - 1])
    w0 = np.stack([0 * g, g, half + 0 * g])
    row_ok = (kr[None] >= w0) & (kr[None] < w0 + NA_WIN_R)
    assert np.all((dr[row_ok] >= 0) & (dr[row_ok] < NA_DR))
    return np.where(row_ok, dr, NA_DR)


def _na_body(q_ref, k_ref, v_ref, kc_ref, vc_ref, cb_ref, o_ref, bias_ref, vt_ref, *, rows):
    @pl.when(pl.program_id(1) == 0)
    def _():
        idx = _na_block_index()
        for typ in range(3):
            for kr in range(NA_KEY_ROWS):
                for gp in range(NA_GROUP // 2):
                    pair = jnp.concatenate([cb_ref[int(idx[typ, 2 * gp, kr])], cb_ref[int(idx[typ, 2 * gp + 1, kr])]], 1)
                    bias_ref[typ, kr * GRID_W:(kr + 1) * GRID_W, gp * 2 * GRID_W:(gp + 1) * 2 * GRID_W] = pair

    vt_ref[...] = _values_t(v_ref[...])
    kc = kc_ref[...]
    vct = _values_t(vc_ref[...])
    gq = NA_GROUP * GRID_W

    def band(gi):
        ws = _na_group_window(gi, rows)[1]
        return slice(ws * GRID_W, (ws + NA_KEY_ROWS) * GRID_W)

    def scores(gi):
        q = q_ref[gi * gq:(gi + 1) * gq, :]
        return [_dot_nt(k_ref[band(gi), :], q) + bias_ref[_na_group_window(gi, rows)[0]], _dot_nt(kc, q)]

    def store(gi, out):
        o_ref[gi * gq:(gi + 1) * gq, :] = out.astype(o_ref.dtype)

    _skewed(rows // NA_GROUP, _attn_stages(scores, lambda gi: [vt_ref[:, band(gi)], vct], store))


def _na_attn(p, pc, bias, b, t, c):
    rows = t // GRID_W
    assert t % GRID_W == 0 and rows % NA_GROUP == 0 and rows >= NA_KEY_ROWS + NA_GROUP and 2 * GRID_W == LANE
    return pl.pallas_call(
        functools.partial(_na_body, rows=rows),
        grid=(NA_HEADS, b),
        in_specs=[pl.BlockSpec((t, LANE), lambda h, bi: (bi, CB_NAQ + h)),
                  pl.BlockSpec((t, LANE), lambda h, bi: (bi, CB_NAK + h)),
                  pl.BlockSpec((t, LANE), lambda h, bi: (bi, CB_NAV + h)),
                  pl.BlockSpec((c, LANE), lambda h, bi: (bi, CB_NAK + h)),
                  pl.BlockSpec((c, LANE), lambda h, bi: (bi, CB_NAV + h)),
                  pl.BlockSpec((None, NA_DR + 1, GRID_W, GRID_W), lambda h, bi: (h, 0, 0, 0))],
        out_specs=pl.BlockSpec((t, LANE), lambda h, bi: (bi, h)),
        out_shape=jax.ShapeDtypeStruct((b * t, NA_WIDTH), _BF),
        scratch_shapes=[pltpu.VMEM((3, NA_KEY_ROWS * GRID_W, NA_GROUP * GRID_W), _F32),
                        pltpu.VMEM((VT_ROWS, t), _BF)],
        compiler_params=_cparams(("arbitrary", "arbitrary")),
    )(p, p, p, pc, pc, bias)


def _ret_body(lg_ref, q_ref, k_ref, v_ref, g_ref, s0f_ref, s0b_ref, cos_ref, sa_ref, sb_ref,
              o_ref, sf_ref, sbo_ref, kr_scr, u_hist, st_hist, *, n):
    h = pl.program_id(0)
    lgf = lg_ref[0, h]
    lgb = lg_ref[1, h]
    L = RET_CHUNK
    pos_c = lax.broadcasted_iota(jnp.int32, (L, 1), 0).astype(_F32)
    pos_r = lax.broadcasted_iota(jnp.int32, (1, L), 1).astype(_F32)
    diff = pos_c - pos_r
    decay = jnp.where(diff > 0, jnp.exp(jnp.maximum(diff, 0.0) * lgf),
                      jnp.where(diff < 0, jnp.exp(jnp.maximum(-diff, 0.0) * lgb), 2.0))
    qdec_f = jnp.exp((pos_c + 1.0) * lgf)
    kdec_f = jnp.exp((L - 1.0 - pos_c) * lgf)
    qdec_b = jnp.exp((L - pos_c) * lgb)
    kdec_b = jnp.exp(pos_c * lgb)
    cd_f = jnp.exp(L * lgf)
    cd_b = jnp.exp(L * lgb)
    nf = HEAD_DIM // 4

    def rows_of(i):
        return slice(i * L, (i + 1) * L)

    def roped(ref, i):
        r = rows_of(i)
        return _rope(ref[r, :].astype(_F32), cos_ref[r, :], sa_ref[r, :], sb_ref[r, :], nf)

    for i in range(n):
        k = roped(k_ref, i)
        kr_scr[rows_of(i), :] = k.astype(_BF)
        kd = jnp.concatenate([k * kdec_f, k * kdec_b], 1).astype(_BF)
        u_hist[i] = lax.dot_general(kd, v_ref[rows_of(i), :], (((0,), (0,)), ((), ())),
                                    preferred_element_type=_F32)

    state = s0f_ref[...]
    for i in range(n):
        st_hist[i, 0:HEAD_DIM, :] = state.astype(_BF)
        state = cd_f * state + u_hist[i, 0:HEAD_DIM, :]
    sf_ref[...] = state
    state = s0b_ref[...]
    for i in reversed(range(n)):
        st_hist[i, HEAD_DIM:, :] = state.astype(_BF)
        state = cd_b * state + u_hist[i, HEAD_DIM:, :]
    sbo_ref[...] = state

    for i in range(n):
        r = rows_of(i)
        q = roped(q_ref, i)
        a = _dot_nt(q.astype(_BF), kr_scr[r, :]) * decay
        qd = jnp.concatenate([q * qdec_f, q * qdec_b], 1).astype(_BF)
        o = _dot(a.astype(_BF), v_ref[r, :]) + _dot(qd, st_hist[i])
        mu = jnp.mean(o, -1, keepdims=True)
        oc = o - mu
        var = jnp.mean(oc * oc, -1, keepdims=True)
        y = oc * lax.rsqrt(var + LN_EPS) * _silu(g_ref[r, :].astype(_F32))
        o_ref[r, :] = y.astype(o_ref.dtype)


def _retention(p, lg, s0f, s0b, tables, b, t):
    n = t // RET_CHUNK
    col = lambda cb: pl.BlockSpec((t, LANE), lambda h, bi: (bi, cb + h))
    st_spec = pl.BlockSpec((None, None, HEAD_DIM, HEAD_DIM), lambda h, bi: (bi, h, 0, 0))
    tab_spec = pl.BlockSpec((t, LANE), lambda h, bi: (0, 0))
    st_shape = jax.ShapeDtypeStruct((b, RET_HEADS, HEAD_DIM, HEAD_DIM), _F32)
    return pl.pallas_call(
        functools.partial(_ret_body, n=n),
        grid=(RET_HEADS, b),
        in_specs=[pl.BlockSpec(memory_space=pltpu.SMEM),
                  col(CB_RQ), col(CB_RK), col(CB_RV), col(CB_RG), st_spec, st_spec,
                  tab_spec, tab_spec, tab_spec],
        out_specs=[pl.BlockSpec((t, LANE), lambda h, bi: (bi, h)), st_spec, st_spec],
        out_shape=[jax.ShapeDtypeStruct((b * t, RET_WIDTH), _BF), st_shape, st_shape],
        scratch_shapes=[pltpu.VMEM((t, HEAD_DIM), _BF),
                        pltpu.VMEM((n, 2 * HEAD_DIM, HEAD_DIM), _F32),
                        pltpu.VMEM((n, 2 * HEAD_DIM, HEAD_DIM), _BF)],
        compiler_params=_cparams(("arbitrary", "arbitrary")),
    )(lg, p, p, p, p, s0f, s0b, *tables)


def _ln_epilogue(y, x_ref, gate_ref, lng_ref, lnb_ref, alpha, xo_ref, ho_ref, sh_ref, sc_ref):
    z = alpha * x_ref[...] + gate_ref[0] * y
    mu = jnp.mean(z, -1, keepdims=True)
    zc = z - mu
    var = jnp.mean(zc * zc, -1, keepdims=True)
    xn = zc * lax.rsqrt(var + LN_EPS) * lng_ref[...] + lnb_ref[...]
    xo_ref[...] = xn
    if ho_ref is not None:
        ho_ref[...] = (xn * (1.0 + sc_ref[0]) + sh_ref[0]).astype(ho_ref.dtype)


def _proj_ln_body(*refs, n_in, alpha, with_next):
    a_refs, refs = refs[:n_in], refs[n_in:]
    if with_next:
        w_ref, x_ref, gate_ref, sh_ref, sc_ref, lng_ref, lnb_ref, xo_ref, ho_ref, y_scr = refs
    else:
        w_ref, x_ref, gate_ref, lng_ref, lnb_ref, xo_ref, y_scr = refs
        sh_ref = sc_ref = ho_ref = None
    i = pl.program_id(0)

    @pl.when(i == 0)
    def _():
        y_scr[1] = jnp.zeros(y_scr.shape[1:], _F32)

    def step(slot):
        y_prev = y_scr[1 - slot]
        a = a_refs[0][...] if n_in == 1 else jnp.concatenate([r[...] for r in a_refs], 1)
        y_scr[slot] = _dot(a, w_ref[...])
        _ln_epilogue(y_prev, x_ref, gate_ref, lng_ref, lnb_ref, alpha, xo_ref, ho_ref, sh_ref, sc_ref)

    for parity in range(2):
        pl.when(i % 2 == parity)(functools.partial(step, parity))


def _proj_ln(a_list, w, x2, mod3, layer, gate_chunk, row_fn, lng, lnb, alpha, tm, with_next):
    m, d = x2.shape
    n = m // tm
    cur = lambda i: jnp.minimum(i, n - 1)
    fin = lambda i: jnp.maximum(i - 1, 0)
    fin_row = lambda width: pl.BlockSpec((tm, width), lambda i: (fin(i), 0))
    fin_mod = lambda lyr, chunk: _mod_spec(d, lyr, chunk, lambda i: row_fn(fin(i)))
    vec = pl.BlockSpec((1, d), lambda i: (0, 0))
    in_specs = [pl.BlockSpec((tm, a.shape[1]), lambda i: (cur(i), 0)) for a in a_list]
    in_specs += [_resident_spec(w.shape), fin_row(d), fin_mod(layer, gate_chunk)]
    args = list(a_list) + [w, x2, mod3]
    if with_next:
        nxt = (layer, gate_chunk + 1) if gate_chunk < 5 else (layer + 1, 0)
        in_specs += [fin_mod(nxt[0], nxt[1]), fin_mod(nxt[0], nxt[1] + 1)]
        args += [mod3, mod3]
    in_specs += [vec, vec]
    args += [lng, lnb]
    out_specs = [fin_row(d)] + ([fin_row(d)] if with_next else [])
    out_shape = [jax.ShapeDtypeStruct((m, d), _F32)] + ([jax.ShapeDtypeStruct((m, d), _BF)] if with_next else [])
    out = pl.pallas_call(
        functools.partial(_proj_ln_body, n_in=len(a_list), alpha=alpha, with_next=with_next),
        grid=(n + 1,),
        in_specs=in_specs, out_specs=out_specs, out_shape=out_shape,
        scratch_shapes=[pltpu.VMEM((2, tm, d), _F32)],
        compiler_params=_cparams(("arbitrary",)),
    )(*args)
    return (out[0], out[1]) if with_next else (out[0], None)


def _glu_body(h_ref, wa_ref, wu_ref, cw_ref, cb_ref, o_ref, a_scr, wab, wub, *, t, rc):
    @pl.when(pl.program_id(1) == 0)
    def _():
        wab[...] = wa_ref[...].astype(_BF)
        wub[...] = wu_ref[...].astype(_BF)

    nchunk = t // rc
    halo = 8
    tn = o_ref.shape[1]
    zeros = jnp.zeros((halo, tn), _F32)
    a_scr[0:halo, :] = zeros
    a_scr[halo + t:, :] = zeros
    w0, w1, w2 = cw_ref[0:1, :], cw_ref[1:2, :], cw_ref[2:3, :]
    bias = cb_ref[...]
    u_prev = None
    for c in range(nchunk + 1):
        u_cur = None
        if c < nchunk:
            hc = h_ref[c * rc:(c + 1) * rc, :]
            a_scr[halo + c * rc:halo + (c + 1) * rc, :] = _dot(hc, wab[...])
            u_cur = _dot(hc, wub[...])
        if c > 0:
            base = halo + (c - 1) * rc
            lo = a_scr[base - 1:base - 1 + rc, :]
            mid = a_scr[base:base + rc, :]
            hi = a_scr[base + 1:base + 1 + rc, :]
            acc = bias + lo * w0 + mid * w1 + hi * w2
            o_ref[(c - 1) * rc:c * rc, :] = (_silu(acc) * u_prev).astype(o_ref.dtype)
        u_prev = u_cur


def _glu(h, w_up, layer, conv_w, conv_b, b, t, tn):
    d = h.shape[1]
    dff = conv_w.shape[1]
    nj = dff // tn
    rc = _tile(t, 512)
    return pl.pallas_call(
        functools.partial(_glu_body, t=t, rc=rc),
        grid=(nj, b),
        in_specs=[pl.BlockSpec((t, d), lambda j, bi: (bi, 0)),
                  pl.BlockSpec((None, d, tn), lambda j, bi: (layer, 0, j)),
                  pl.BlockSpec((None, d, tn), lambda j, bi: (layer, 0, nj + j)),
                  pl.BlockSpec((CONV_W, tn), lambda j, bi: (0, j)),
                  pl.BlockSpec((1, tn), lambda j, bi: (0, j))],
        out_specs=pl.BlockSpec((t, tn), lambda j, bi: (bi, j)),
        out_shape=jax.ShapeDtypeStruct((b * t, dff), _BF),
        scratch_shapes=[pltpu.VMEM((t + 16, tn), _F32), pltpu.VMEM((d, tn), _BF), pltpu.VMEM((d, tn), _BF)],
        compiler_params=_cparams(("arbitrary", "arbitrary")),
    )(h, w_up, w_up, conv_w, conv_b)


def _pack_w_in_body(w_ref, o_ref):
    src = dict(zip(("na_q", "na_k", "na_v", "cq", "ckv", "kpe", "r_q", "r_k", "r_v", "r_g"),
                   [int(i) for i in np.cumsum((0,) + IN_SPLITS[:-1])]))
    q_scale = HEAD_DIM ** -0.5 * LOG2E
    k_scale = HEAD_DIM ** -0.5

    def put(dst_block, name, width, mul=None):
        x = w_ref[:, src[name]:src[name] + width]
        if mul is not None:
            x = x * mul
        o_ref[:, dst_block * LANE:dst_block * LANE + width] = x.astype(o_ref.dtype)

    put(CB_CQ, "cq", MLA_Q_RANK + MLA_KV_RANK)
    put(CB_NAQ, "na_q", NA_WIDTH, q_scale)
    put(CB_NAK, "na_k", 2 * NA_WIDTH)
    put(CB_RQ, "r_q", RET_WIDTH)
    put(CB_RK, "r_k", RET_WIDTH, k_scale)
    put(CB_RV, "r_v", 2 * RET_WIDTH)
    rows = w_ref.shape[0]
    tail = jnp.concatenate([w_ref[:, src["kpe"]:src["kpe"] + MLA_ROPE],
                            jnp.zeros((rows, PACKED_IN_WIDTH - CB_KPE * LANE - MLA_ROPE), _F32)], 1)
    o_ref[:, CB_KPE * LANE:] = tail.astype(o_ref.dtype)


def _pack_w_in(w_in, layer):
    _, d, n = w_in.shape
    tk = _tile(d, 256)
    return pl.pallas_call(
        _pack_w_in_body,
        grid=(d // tk,),
        in_specs=[pl.BlockSpec((None, tk, n), lambda i: (layer, i, 0))],
        out_specs=pl.BlockSpec((tk, PACKED_IN_WIDTH), lambda i: (i, 0)),
        out_shape=jax.ShapeDtypeStruct((d, PACKED_IN_WIDTH), _BF),
        compiler_params=_cparams(("arbitrary",)),
    )(w_in)


def _pack_w_uq(w):
    r = w.shape[0]
    w = w.reshape(r, MLA_HEADS, MLA_NOPE + MLA_ROPE) * ((MLA_NOPE + MLA_ROPE) ** -0.5 * LOG2E)
    w = jnp.pad(w, ((0, 0), (0, 0), (0, MLA_QK_PAD - MLA_NOPE - MLA_ROPE)))
    return w.reshape(r, MLA_HEADS * MLA_QK_PAD).astype(_BF)


def kernel(x, c, ctx, c_ctx, w_ada, b_ada, w_in, mla_q_norm, mla_kv_norm, w_uq, w_ukv, na_rpb, ret_decay,
           w_o, ln1_g, ln1_b, w_up, conv_w, conv_b, w_down, ln2_g, ln2_b):
    b, t, d = x.shape
    cl = ctx.shape[1]
    depth = w_ada.shape[0]
    dff = conv_w.shape[-1]
    assert b + 1 <= MOD_ROWS
    alpha = (2 * depth) ** 0.25
    ctx_row = b

    cvec = jnp.concatenate([c, c_ctx[None], jnp.zeros((MOD_ROWS - b - 1, d), c.dtype)], 0)
    mod3 = _ada(cvec, w_ada, b_ada)

    tm_l = _tile(t, 1024)
    tm_c = _tile(cl, 1024)
    lat_row = lambda tm: (lambda i: i // (t // tm))
    ctx_rowf = lambda i: ctx_row

    rope_pe = _rope_tables(t, MLA_ROPE)
    rope_ret = _rope_tables(t, HEAD_DIM)
    id_pe = _rope_tables(cl, MLA_ROPE, identity=True)
    id_ret = _rope_tables(cl, HEAD_DIM, identity=True)

    xf = x.reshape(b * t, d)
    cf = ctx.reshape(b * cl, d)
    h1 = _modulate(xf, mod3, 0, lat_row(tm_l), tm_l)
    hc1 = _modulate(cf, mod3, 0, ctx_rowf, tm_c)

    tq = _tile(t, 2048)
    tqc = _tile(cl, 256)
    tn_ff = _tile(dff, 512)
    zero_state = jnp.zeros((b, RET_HEADS, HEAD_DIM, HEAD_DIM), _F32)

    for l in range(depth):
        last = l == depth - 1
        w_in_p = _pack_w_in(w_in, l)
        w_uq_p = _pack_w_uq(w_uq[l])
        w_ukv_p = w_ukv[l].astype(_BF)
        gq = mla_q_norm[l][None]
        gkv = mla_kv_norm[l][None]
        lg = jnp.log1p(-jnp.exp2(ret_decay[l].astype(_F32)))
        bias = _na_col_bias(na_rpb[l])

        p = _mm(h1, w_in_p, tm_l, 1024)
        pc = _mm(hc1, w_in_p, tm_c, 1024)

        kc_m, vc_m = _mla_kv(pc, gkv, w_ukv_p, id_pe, b, cl, _tile(cl, 512))
        kl_m, vl_m = _mla_kv(p, gkv, w_ukv_p, rope_pe, b, t, _tile(t, 512))
        q_m = _mla_q(p, gq, w_uq_p, rope_pe, b, t, _tile(t, 512))
        y_mla = _dense_attn(
            q_m, pl.BlockSpec((None, None, tq, MLA_QK_PAD), lambda bi, h, i: (bi, h, i, 0)),
            [(kc_m, _head4_spec(cl, MLA_QK_PAD), vc_m, _head4_spec(VT_ROWS, cl)),
             (kl_m, _head4_spec(t, MLA_QK_PAD), vl_m, _head4_spec(VT_ROWS, t))],
            b, MLA_HEADS, t // tq, tq, MLA_V, nsub=tq // _tile(tq, 256))

        y_na = _na_attn(p, pc, bias, b, t, cl)

        yc_ret, s_f, s_b = _retention(pc, lg, zero_state, zero_state, id_ret, b, cl)
        y_ret, _, _ = _retention(p, lg, s_f, s_b, rope_ret, b, t)

        tm_o = _tile(t, 256)
        x_new, h2 = _proj_ln([y_na, y_mla, y_ret], w_o[l].astype(_BF), xf, mod3, l, 2, lat_row(tm_o),
                             ln1_g[l][None], ln1_b[l][None], alpha, tm_o, with_next=True)

        w_down_b = w_down[l].astype(_BF)
        tm_d = _tile(t, 256)

        if not last:
            qc_m = _mla_q(pc, gq, w_uq_p, id_pe, b, cl, _tile(cl, 512))
            yc_mla = _dense_attn(
                qc_m, pl.BlockSpec((None, None, tqc, MLA_QK_PAD), lambda bi, h, i: (bi, h, i, 0)),
                [(kc_m, _head4_spec(cl, MLA_QK_PAD), vc_m, _head4_spec(VT_ROWS, cl))],
                b, MLA_HEADS, cl // tqc, tqc, MLA_V, nsub=1)
            nbc = cl // tqc
            yc_na = _dense_attn(
                pc, pl.BlockSpec((tqc, LANE), lambda bi, h, i: (bi * nbc + i, CB_NAQ + h)),
                [(pc, pl.BlockSpec((cl, LANE), lambda bi, h, i: (bi, CB_NAK + h)),
                  pc, pl.BlockSpec((cl, LANE), lambda bi, h, i: (bi, CB_NAV + h)))],
                b, NA_HEADS, nbc, tqc, HEAD_DIM, nsub=1, v_row_major=True)
            tm_oc = _tile(cl, 256)
            c_new, hc2 = _proj_ln([yc_na, yc_mla, yc_ret], w_o[l].astype(_BF), cf, mod3, l, 2, ctx_rowf,
                                  ln1_g[l][None], ln1_b[l][None], alpha, tm_oc, with_next=True)

        g_l = _glu(h2, w_up, l, conv_w[l], conv_b[l][None], b, t, tn_ff)
        xf, h1 = _proj_ln([g_l], w_down_b, x_new, mod3, l, 5, lat_row(tm_d), ln2_g[l][None], ln2_b[l][None],
                          alpha, tm_d, with_next=not last)
        if not last:
            g_c = _glu(hc2, w_up, l, conv_w[l], conv_b[l][None], b, cl, tn_ff)
            tm_dc = _tile(cl, 256)
            cf, hc1 = _proj_ln([g_c], w_down_b, c_new, mod3, l, 5, ctx_rowf, ln2_g[l][None], ln2_b[l][None],
                               alpha, tm_dc, with_next=True)

    return xf.reshape(b, t, d)
```

```python
import functools

import numpy as np
import jax
import jax.numpy as jnp
from jax import lax
from jax.experimental import pallas as pl
from jax.experimental.pallas import tpu as pltpu

GRID_W = 64
HEAD_DIM = 128
NA_HEADS = 6
MLA_HEADS = 5
RET_HEADS = 5
NA_WIDTH = NA_HEADS * HEAD_DIM
MLA_WIDTH = MLA_HEADS * HEAD_DIM
RET_WIDTH = RET_HEADS * HEAD_DIM
NA_WIN_R = 8
NA_WIN_C = 16
MLA_Q_RANK = 512
MLA_KV_RANK = 512
MLA_NOPE = 128
MLA_ROPE = 64
MLA_V = 128
MLA_QK_PAD = 256
RET_CHUNK = 128
CONV_W = 3
ROPE_BASE = 10000.0
LN_EPS = 1e-5
RMS_EPS = 1e-6
NEG_INF = -1e30
LOG2E = 1.4426950408889634
IN_SPLITS = (NA_WIDTH, NA_WIDTH, NA_WIDTH, MLA_Q_RANK, MLA_KV_RANK, MLA_ROPE,
             RET_WIDTH, RET_WIDTH, RET_WIDTH, RET_WIDTH)

LANE = 128
CB_CQ, CB_CKV, CB_NAQ, CB_NAK, CB_NAV = 0, 4, 8, 14, 20
CB_RQ, CB_RK, CB_RV, CB_RG, CB_KPE = 26, 31, 36, 41, 46
PACKED_IN_WIDTH = 48 * LANE
MOD_ROWS = 8
VMEM_LIMIT_MB = 56

_BF = jnp.bfloat16
_F32 = jnp.float32


def _cparams(sem, vmem_mb=VMEM_LIMIT_MB):
    return pltpu.CompilerParams(dimension_semantics=sem, vmem_limit_bytes=vmem_mb << 20)


def _tile(n, pref):
    t = min(n, pref)
    while n % t:
        t //= 2
    return t


def _dot(a, b):
    return jnp.dot(a, b, preferred_element_type=_F32)


def _dot_nt(a, b):
    return lax.dot_general(a, b, (((1,), (1,)), ((), ())), preferred_element_type=_F32)


def _silu(x):
    return x / (1.0 + jnp.exp(-x))


def _ada_body(c_ref, w_ref, b_ref, o_ref):
    part = _dot(_silu(c_ref[...]).astype(_BF), w_ref[0].astype(_BF))

    @pl.when(pl.program_id(1) == 0)
    def _():
        o_ref[0] = part + b_ref[0]

    @pl.when(pl.program_id(1) > 0)
    def _():
        o_ref[0] += part


def _ada(cvec, w_ada, b_ada):
    depth, d, n = w_ada.shape
    tk = _tile(d, 256)
    out = pl.pallas_call(
        _ada_body,
        grid=(depth, d // tk),
        in_specs=[pl.BlockSpec((MOD_ROWS, tk), lambda l, k: (0, k)),
                  pl.BlockSpec((1, tk, n), lambda l, k: (l, k, 0)),
                  pl.BlockSpec((1, 1, n), lambda l, k: (l, 0, 0))],
        out_specs=pl.BlockSpec((1, MOD_ROWS, n), lambda l, k: (l, 0, 0)),
        out_shape=jax.ShapeDtypeStruct((depth, MOD_ROWS, n), _F32),
        compiler_params=_cparams(("arbitrary", "arbitrary")),
    )(cvec, w_ada, b_ada.reshape(depth, 1, n))
    return out.reshape(depth * MOD_ROWS, 1, n)


def _mod_spec(d, layer, chunk, row_fn):
    return pl.BlockSpec((1, 1, d), lambda i, *_: (layer * MOD_ROWS + row_fn(i), 0, chunk))


def _modulate_body(x_ref, sh_ref, sc_ref, o_ref):
    o_ref[...] = (x_ref[...] * (1.0 + sc_ref[0]) + sh_ref[0]).astype(o_ref.dtype)


def _modulate(x2, mod3, layer, row_fn, tm):
    m, d = x2.shape
    return pl.pallas_call(
        _modulate_body,
        grid=(m // tm,),
        in_specs=[pl.BlockSpec((tm, d), lambda i: (i, 0)),
                  _mod_spec(d, layer, 0, row_fn), _mod_spec(d, layer, 1, row_fn)],
        out_specs=pl.BlockSpec((tm, d), lambda i: (i, 0)),
        out_shape=jax.ShapeDtypeStruct((m, d), _BF),
        compiler_params=_cparams(("arbitrary",)),
    )(x2, mod3, mod3)


def _mm_body(a_ref, w_ref, o_ref):
    o_ref[...] = _dot(a_ref[...], w_ref[...]).astype(o_ref.dtype)


def _mm(a, w, tm, tn):
    m, k = a.shape
    n = w.shape[1]
    return pl.pallas_call(
        _mm_body,
        grid=(m // tm, n // tn),
        in_specs=[pl.BlockSpec((tm, k), lambda i, j: (i, 0)),
                  pl.BlockSpec((k, tn), lambda i, j: (0, j))],
        out_specs=pl.BlockSpec((tm, tn), lambda i, j: (i, j)),
        out_shape=jax.ShapeDtypeStruct((m, n), _BF),
        compiler_params=_cparams(("arbitrary", "arbitrary")),
    )(a, w)


def _rope_tables(t, rot_dim, identity=False):
    pos = jnp.arange(t)
    row = (pos // GRID_W).astype(_F32)
    col = (pos % GRID_W).astype(_F32)
    nf = rot_dim // 4
    inv = ROPE_BASE ** (-jnp.arange(nf, dtype=_F32) / nf)
    ar = row[:, None] * inv[None]
    ac = col[:, None] * inv[None]
    ang = jnp.concatenate([ar, ar, ac, ac], -1)
    cos, sin = jnp.cos(ang), jnp.sin(ang)
    if identity:
        cos, sin = jnp.ones_like(cos), jnp.zeros_like(sin)
    first = (np.arange(rot_dim) % (rot_dim // 2)) < nf
    sin_a = jnp.where(first, -sin, 0.0)
    sin_b = jnp.where(first, 0.0, sin)
    pad = LANE - rot_dim
    if pad:
        cos = jnp.pad(cos, ((0, 0), (0, pad)), constant_values=1.0)
        sin_a = jnp.pad(sin_a, ((0, 0), (0, pad)))
        sin_b = jnp.pad(sin_b, ((0, 0), (0, pad)))
    return cos, sin_a, sin_b


def _rope(x, cos, sin_a, sin_b, nf):
    return x * cos + pltpu.roll(x, LANE - nf, 1) * sin_a + pltpu.roll(x, nf, 1) * sin_b


def _rms(x, g):
    return x * lax.rsqrt(jnp.mean(x * x, -1, keepdims=True) + RMS_EPS) * g


def _mla_q_body(p_ref, g_ref, w_ref, cos_ref, sa_ref, sb_ref, o_ref):
    xb = _rms(p_ref[...].astype(_F32), g_ref[...]).astype(_BF)
    cos, sa, sb = cos_ref[...], sa_ref[...], sb_ref[...]
    for h in range(MLA_HEADS):
        acc = _dot(xb, w_ref[:, h * MLA_QK_PAD:(h + 1) * MLA_QK_PAD])
        o_ref[0, h, :, 0:LANE] = acc[:, :LANE].astype(_BF)
        o_ref[0, h, :, LANE:] = _rope(acc[:, LANE:], cos, sa, sb, MLA_ROPE // 4).astype(_BF)


def _mla_q(p, g, w, tables, b, t, tm):
    nb = t // tm
    tab_spec = pl.BlockSpec((tm, LANE), lambda i: (i % nb, 0))
    return pl.pallas_call(
        _mla_q_body,
        grid=(b * nb,),
        in_specs=[pl.BlockSpec((tm, MLA_Q_RANK), lambda i: (i, CB_CQ // 4)),
                  pl.BlockSpec((1, MLA_Q_RANK), lambda i: (0, 0)),
                  pl.BlockSpec(w.shape, lambda i: (0, 0)),
                  tab_spec, tab_spec, tab_spec],
        out_specs=pl.BlockSpec((1, MLA_HEADS, tm, MLA_QK_PAD), lambda i: (i // nb, 0, i % nb, 0)),
        out_shape=jax.ShapeDtypeStruct((b, MLA_HEADS, t, MLA_QK_PAD), _BF),
        compiler_params=_cparams(("arbitrary",)),
    )(p, g, w, *tables)


VT_ROWS = HEAD_DIM + 16


def _values_t(v):
    n = v.shape[0]
    ones = (lax.broadcasted_iota(jnp.int32, (VT_ROWS - HEAD_DIM, n), 0) == 0).astype(_BF)
    return jnp.concatenate([v.astype(_F32).T.astype(_BF), ones], 0)


def _mla_kv_body(ckv_ref, kpe_ref, g_ref, w_ref, cos_ref, sa_ref, sb_ref, k_ref, vt_ref):
    xb = _rms(ckv_ref[...].astype(_F32), g_ref[...]).astype(_BF)
    pe = _rope(kpe_ref[...].astype(_F32), cos_ref[...], sa_ref[...], sb_ref[...], MLA_ROPE // 4).astype(_BF)
    width = MLA_NOPE + MLA_V
    for h in range(MLA_HEADS):
        acc = _dot(xb, w_ref[:, h * width:(h + 1) * width])
        k_ref[0, h, :, 0:LANE] = acc[:, :MLA_NOPE].astype(_BF)
        k_ref[0, h, :, LANE:] = pe
        vt_ref[0, h] = _values_t(acc[:, MLA_NOPE:])


def _mla_kv(p, g, w, tables, b, t, tm):
    nb = t // tm
    tab_spec = pl.BlockSpec((tm, LANE), lambda i: (i % nb, 0))
    return pl.pallas_call(
        _mla_kv_body,
        grid=(b * nb,),
        in_specs=[pl.BlockSpec((tm, MLA_KV_RANK), lambda i: (i, CB_CKV // 4)),
                  pl.BlockSpec((tm, LANE), lambda i: (i, CB_KPE)),
                  pl.BlockSpec((1, MLA_KV_RANK), lambda i: (0, 0)),
                  pl.BlockSpec(w.shape, lambda i: (0, 0)),
                  tab_spec, tab_spec, tab_spec],
        out_specs=[pl.BlockSpec((1, MLA_HEADS, tm, MLA_QK_PAD), lambda i: (i // nb, 0, i % nb, 0)),
                   pl.BlockSpec((1, MLA_HEADS, VT_ROWS, tm), lambda i: (i // nb, 0, 0, i % nb))],
        out_shape=[jax.ShapeDtypeStruct((b, MLA_HEADS, t, MLA_QK_PAD), _BF),
                   jax.ShapeDtypeStruct((b, MLA_HEADS, VT_ROWS, t), _BF)],
        compiler_params=_cparams(("arbitrary",)),
    )(p, p, g, w, *tables)


def _skewed(n, stages):
    carry = [None] * n
    for step in range(n + len(stages) - 1):
        for s, stage in enumerate(stages):
            j = step - s
            if 0 <= j < n:
                carry[j] = stage(j, carry[j])


def _attn_stages(scores_fn, values_fn, store_fn):
    def s_scores(j, _):
        return scores_fn(j)

    def s_max(j, sc):
        return sc, functools.reduce(jnp.maximum, [jnp.max(s, 0, keepdims=True) for s in sc])

    def s_exp(j, c):
        sc, m = c
        return [jnp.exp2(s - m).astype(_BF) for s in sc]

    def s_pv(j, p):
        acc = functools.reduce(jnp.add, [_dot(vt, pi) for vt, pi in zip(values_fn(j), p)])
        store_fn(j, (acc[0:HEAD_DIM, :] / acc[HEAD_DIM:HEAD_DIM + 1, :]).T)

    return [s_scores, s_max, s_exp, s_pv]


def _attn_body(*refs, nseg, nsub, v_row_major):
    q_ref, o_ref = refs[0], refs[-1]
    ks = [refs[1 + 2 * i][...] for i in range(nseg)]
    vts = [refs[2 + 2 * i][...] for i in range(nseg)]
    if v_row_major:
        vts = [_values_t(v) for v in vts]
    ts = q_ref.shape[0] // nsub
    rows = lambda j: slice(j * ts, (j + 1) * ts)

    def store(j, out):
        o_ref[rows(j), :] = out.astype(o_ref.dtype)

    _skewed(nsub, _attn_stages(lambda j: [_dot_nt(k, q_ref[rows(j), :]) for k in ks], lambda j: vts, store))


def _dense_attn(q, q_spec, kv, b, heads, tq_blocks, tq, dv, nsub, v_row_major=False):
    args, specs = [q], [q_spec]
    for k, ks, v, vs in kv:
        args += [k, v]
        specs += [ks, vs]
    return pl.pallas_call(
        functools.partial(_attn_body, nseg=len(kv), nsub=nsub, v_row_major=v_row_major),
        grid=(b, heads, tq_blocks),
        in_specs=specs,
        out_specs=pl.BlockSpec((tq, dv), lambda bi, h, i: (bi * tq_blocks + i, h)),
        out_shape=jax.ShapeDtypeStruct((b * tq_blocks * tq, heads * dv), _BF),
        compiler_params=_cparams(("arbitrary", "arbitrary", "arbitrary")),
    )(*args)


def _head4_spec(rows, cols):
    return pl.BlockSpec((None, None, rows, cols), lambda bi, h, i: (bi, h, 0, 0))


NA_GROUP = 4
NA_KEY_ROWS = 12


def _na_group_window(gi, rows):
    ngroups = rows // NA_GROUP
    if gi == 0:
        return 0, 0
    if gi == ngroups - 1:
        return 2, rows - NA_KEY_ROWS
    return 1, gi * NA_GROUP - NA_WIN_R // 2


NA_DR = 2 * NA_WIN_R - 1


def _na_col_bias(rpb):
    c = np.arange(GRID_W)
    col_start = np.clip(c - NA_WIN_C // 2, 0, GRID_W - NA_WIN_C)
    kc = np.arange(GRID_W)
    in_win = (kc[None, :] >= col_start[:, None]) & (kc[None, :] < col_start[:, None] + NA_WIN_C)
    dc = np.clip(kc[None, :] - c[:, None] + NA_WIN_C - 1, 0, 2 * NA_WIN_C - 2)
    col_sel = (dc[:, :, None] == np.arange(2 * NA_WIN_C - 1)).astype(np.float32)
    toep = jnp.einsum('hrj,cqj->hrqc', rpb.astype(_F32), col_sel, precision=lax.Precision.HIGHEST)
    toep = jnp.where(in_win.T[None, None], toep * LOG2E, NEG_INF)
    masked = jnp.full((rpb.shape[0], 1, GRID_W, GRID_W), NEG_INF, _F32)
    return jnp.concatenate([toep, masked], 1)


def _na_block_index():
    g = np.arange(NA_GROUP)[:, None]
    kr = np.arange(NA_KEY_ROWS)[None, :]
    half = NA_WIN_R // 2
    dr = np.stack([kr - g + NA_WIN_R - 1, kr - g + NA_WIN_R - 1 - half, kr - g - 1])
    w0 = np.stack([0 * g, g, half + 0 * g])
    row_ok = (kr[None] >= w0) & (kr[None] < w0 + NA_WIN_R)
    assert np.all((dr[row_ok] >= 0) & (dr[row_ok] < NA_DR))
    return np.where(row_ok, dr, NA_DR)


def _na_body(q_ref, k_ref, v_ref, kc_ref, vc_ref, cb_ref, o_ref, bias_ref, vt_ref, *, rows):
    @pl.when(pl.program_id(1) == 0)
    def _():
        idx = _na_block_index()
        for typ in range(3):
            for kr in range(NA_KEY_ROWS):
                for gp in range(NA_GROUP // 2):
                    pair = jnp.concatenate([cb_ref[int(idx[typ, 2 * gp, kr])], cb_ref[int(idx[typ, 2 * gp + 1, kr])]], 1)
                    bias_ref[typ, kr * GRID_W:(kr + 1) * GRID_W, gp * 2 * GRID_W:(gp + 1) * 2 * GRID_W] = pair

    vt_ref[...] = _values_t(v_ref[...])
    kc = kc_ref[...]
    vct = _values_t(vc_ref[...])
    gq = NA_GROUP * GRID_W

    def band(gi):
        ws = _na_group_window(gi, rows)[1]
        return slice(ws * GRID_W, (ws + NA_KEY_ROWS) * GRID_W)

    def scores(gi):
        q = q_ref[gi * gq:(gi + 1) * gq, :]
        return [_dot_nt(k_ref[band(gi), :], q) + bias_ref[_na_group_window(gi, rows)[0]], _dot_nt(kc, q)]

    def store(gi, out):
        o_ref[gi * gq:(gi + 1) * gq, :] = out.astype(o_ref.dtype)

    _skewed(rows // NA_GROUP, _attn_stages(scores, lambda gi: [vt_ref[:, band(gi)], vct], store))


def _na_attn(p, pc, bias, b, t, c):
    rows = t // GRID_W
    assert t % GRID_W == 0 and rows % NA_GROUP == 0 and rows >= NA_KEY_ROWS + NA_GROUP and 2 * GRID_W == LANE
    return pl.pallas_call(
        functools.partial(_na_body, rows=rows),
        grid=(NA_HEADS, b),
        in_specs=[pl.BlockSpec((t, LANE), lambda h, bi: (bi, CB_NAQ + h)),
                  pl.BlockSpec((t, LANE), lambda h, bi: (bi, CB_NAK + h)),
                  pl.BlockSpec((t, LANE), lambda h, bi: (bi, CB_NAV + h)),
                  pl.BlockSpec((c, LANE), lambda h, bi: (bi, CB_NAK + h)),
                  pl.BlockSpec((c, LANE), lambda h, bi: (bi, CB_NAV + h)),
                  pl.BlockSpec((None, NA_DR + 1, GRID_W, GRID_W), lambda h, bi: (h, 0, 0, 0))],
        out_specs=pl.BlockSpec((t, LANE), lambda h, bi: (bi, h)),
        out_shape=jax.ShapeDtypeStruct((b * t, NA_WIDTH), _BF),
        scratch_shapes=[pltpu.VMEM((3, NA_KEY_ROWS * GRID_W, NA_GROUP * GRID_W), _F32),
                        pltpu.VMEM((VT_ROWS, t), _BF)],
        compiler_params=_cparams(("arbitrary", "arbitrary")),
    )(p, p, p, pc, pc, bias)


def _ret_body(lg_ref, q_ref, k_ref, v_ref, g_ref, s0f_ref, s0b_ref, cos_ref, sa_ref, sb_ref,
              o_ref, sf_ref, sbo_ref, kr_scr, u_hist, st_hist, *, n):
    h = pl.program_id(0)
    lgf = lg_ref[0, h]
    lgb = lg_ref[1, h]
    L = RET_CHUNK
    pos_c = lax.broadcasted_iota(jnp.int32, (L, 1), 0).astype(_F32)
    pos_r = lax.broadcasted_iota(jnp.int32, (1, L), 1).astype(_F32)
    diff = pos_c - pos_r
    decay = jnp.where(diff > 0, jnp.exp(jnp.maximum(diff, 0.0) * lgf),
                      jnp.where(diff < 0, jnp.exp(jnp.maximum(-diff, 0.0) * lgb), 2.0))
    qdec_f = jnp.exp((pos_c + 1.0) * lgf)
    kdec_f = jnp.exp((L - 1.0 - pos_c) * lgf)
    qdec_b = jnp.exp((L - pos_c) * lgb)
    kdec_b = jnp.exp(pos_c * lgb)
    cd_f = jnp.exp(L * lgf)
    cd_b = jnp.exp(L * lgb)
    nf = HEAD_DIM // 4

    def rows_of(i):
        return slice(i * L, (i + 1) * L)

    def roped(ref, i):
        r = rows_of(i)
        return _rope(ref[r, :].astype(_F32), cos_ref[r, :], sa_ref[r, :], sb_ref[r, :], nf)

    for i in range(n):
        k = roped(k_ref, i)
        kr_scr[rows_of(i), :] = k.astype(_BF)
        kd = jnp.concatenate([k * kdec_f, k * kdec_b], 1).astype(_BF)
        u_hist[i] = lax.dot_general(kd, v_ref[rows_of(i), :], (((0,), (0,)), ((), ())),
                                    preferred_element_type=_F32)

    state = s0f_ref[...]
    for i in range(n):
        st_hist[i, 0:HEAD_DIM, :] = state.astype(_BF)
        state = cd_f * state + u_hist[i, 0:HEAD_DIM, :]
    sf_ref[...] = state
    state = s0b_ref[...]
    for i in reversed(range(n)):
        st_hist[i, HEAD_DIM:, :] = state.astype(_BF)
        state = cd_b * state + u_hist[i, HEAD_DIM:, :]
    sbo_ref[...] = state

    for i in range(n):
        r = rows_of(i)
        q = roped(q_ref, i)
        a = _dot_nt(q.astype(_BF), kr_scr[r, :]) * decay
        qd = jnp.concatenate([q * qdec_f, q * qdec_b], 1).astype(_BF)
        o = _dot(a.astype(_BF), v_ref[r, :]) + _dot(qd, st_hist[i])
        mu = jnp.mean(o, -1, keepdims=True)
        oc = o - mu
        var = jnp.mean(oc * oc, -1, keepdims=True)
        y = oc * lax.rsqrt(var + LN_EPS) * _silu(g_ref[r, :].astype(_F32))
        o_ref[r, :] = y.astype(o_ref.dtype)


def _retention(p, lg, s0f, s0b, tables, b, t):
    n = t // RET_CHUNK
    col = lambda cb: pl.BlockSpec((t, LANE), lambda h, bi: (bi, cb + h))
    st_spec = pl.BlockSpec((None, None, HEAD_DIM, HEAD_DIM), lambda h, bi: (bi, h, 0, 0))
    tab_spec = pl.BlockSpec((t, LANE), lambda h, bi: (0, 0))
    st_shape = jax.ShapeDtypeStruct((b, RET_HEADS, HEAD_DIM, HEAD_DIM), _F32)
    return pl.pallas_call(
        functools.partial(_ret_body, n=n),
        grid=(RET_HEADS, b),
        in_specs=[pl.BlockSpec(memory_space=pltpu.SMEM),
                  col(CB_RQ), col(CB_RK), col(CB_RV), col(CB_RG), st_spec, st_spec,
                  tab_spec, tab_spec, tab_spec],
        out_specs=[pl.BlockSpec((t, LANE), lambda h, bi: (bi, h)), st_spec, st_spec],
        out_shape=[jax.ShapeDtypeStruct((b * t, RET_WIDTH), _BF), st_shape, st_shape],
        scratch_shapes=[pltpu.VMEM((t, HEAD_DIM), _BF),
                        pltpu.VMEM((n, 2 * HEAD_DIM, HEAD_DIM), _F32),
                        pltpu.VMEM((n, 2 * HEAD_DIM, HEAD_DIM), _BF)],
        compiler_params=_cparams(("arbitrary", "arbitrary")),
    )(lg, p, p, p, p, s0f, s0b, *tables)


def _ln_epilogue(y, rows, x_ref, gate_ref, lng_ref, lnb_ref, alpha, xo_ref, ho_ref, sh_ref, sc_ref):
    z = alpha * x_ref[rows, :] + gate_ref[0] * y
    mu = jnp.mean(z, -1, keepdims=True)
    zc = z - mu
    var = jnp.mean(zc * zc, -1, keepdims=True)
    xn = zc * lax.rsqrt(var + LN_EPS) * lng_ref[...] + lnb_ref[...]
    xo_ref[rows, :] = xn
    if ho_ref is not None:
        ho_ref[rows, :] = (xn * (1.0 + sc_ref[0]) + sh_ref[0]).astype(ho_ref.dtype)


def _proj_ln_body(*refs, n_in, alpha, with_next, nsub):
    a_refs, refs = refs[:n_in], refs[n_in:]
    if with_next:
        w_ref, x_ref, gate_ref, sh_ref, sc_ref, lng_ref, lnb_ref, xo_ref, ho_ref = refs
    else:
        w_ref, x_ref, gate_ref, lng_ref, lnb_ref, xo_ref = refs
        sh_ref = sc_ref = ho_ref = None
    ts = x_ref.shape[0] // nsub
    for j in range(nsub):
        r = slice(j * ts, (j + 1) * ts)
        a = a_refs[0][r, :] if n_in == 1 else jnp.concatenate([ar[r, :] for ar in a_refs], 1)
        _ln_epilogue(_dot(a, w_ref[...]), r, x_ref, gate_ref, lng_ref, lnb_ref, alpha, xo_ref, ho_ref, sh_ref, sc_ref)


def _proj_ln(a_list, w, x2, mod3, layer, gate_chunk, row_fn, lng, lnb, alpha, tm, nsub, with_next):
    m, d = x2.shape
    row = lambda width: pl.BlockSpec((tm, width), lambda i: (i, 0))
    vec = pl.BlockSpec((1, d), lambda i: (0, 0))
    w_spec = pl.BlockSpec((None,) + w.shape[1:], lambda i: (layer, 0, 0), pipeline_mode=pl.Buffered(1))
    in_specs = [row(a.shape[1]) for a in a_list] + [w_spec, row(d), _mod_spec(d, layer, gate_chunk, row_fn)]
    args = list(a_list) + [w, x2, mod3]
    if with_next:
        nxt = (layer, gate_chunk + 1) if gate_chunk < 5 else (layer + 1, 0)
        in_specs += [_mod_spec(d, nxt[0], nxt[1], row_fn), _mod_spec(d, nxt[0], nxt[1] + 1, row_fn)]
        args += [mod3, mod3]
    in_specs += [vec, vec]
    args += [lng, lnb]
    out_specs = [row(d)] + ([row(d)] if with_next else [])
    out_shape = [jax.ShapeDtypeStruct((m, d), _F32)] + ([jax.ShapeDtypeStruct((m, d), _BF)] if with_next else [])
    out = pl.pallas_call(
        functools.partial(_proj_ln_body, n_in=len(a_list), alpha=alpha, with_next=with_next, nsub=nsub),
        grid=(m // tm,),
        in_specs=in_specs, out_specs=out_specs, out_shape=out_shape,
        compiler_params=_cparams(("arbitrary",)),
    )(*args)
    return (out[0], out[1]) if with_next else (out[0], None)


def _glu_chunks(t):
    rc = _tile(t, 512)
    return [(i * rc, rc) for i in range(t // rc)]


def _glu_body(h_ref, wa_ref, wu_ref, cw_ref, cb_ref, o_ref, a_scr, wab, wub, *, t):
    @pl.when(pl.program_id(1) == 0)
    def _():
        wab[...] = wa_ref[...].astype(_BF)
        wub[...] = wu_ref[...].astype(_BF)

    chunks = _glu_chunks(t)
    nchunk = len(chunks)
    halo = 8
    tn = o_ref.shape[1]
    zeros = jnp.zeros((halo, tn), _F32)
    a_scr[0:halo, :] = zeros
    a_scr[halo + t:, :] = zeros
    w0, w1, w2 = cw_ref[0:1, :], cw_ref[1:2, :], cw_ref[2:3, :]
    bias = cb_ref[...]
    u_prev = None
    for c in range(nchunk + 1):
        u_cur = None
        if c < nchunk:
            r0, rc = chunks[c]
            hc = h_ref[r0:r0 + rc, :]
            a_scr[halo + r0:halo + r0 + rc, :] = _dot(hc, wab[...])
            u_cur = _dot(hc, wub[...])
        if c > 0:
            r0, rc = chunks[c - 1]
            base = halo + r0
            lo = a_scr[base - 1:base - 1 + rc, :]
            mid = a_scr[base:base + rc, :]
            hi = a_scr[base + 1:base + 1 + rc, :]
            acc = bias + lo * w0 + mid * w1 + hi * w2
            o_ref[r0:r0 + rc, :] = (_silu(acc) * u_prev).astype(o_ref.dtype)
        u_prev = u_cur


def _glu(h, w_up, layer, conv_w, conv_b, b, t, tn):
    d = h.shape[1]
    dff = conv_w.shape[1]
    nj = dff // tn
    return pl.pallas_call(
        functools.partial(_glu_body, t=t),
        grid=(nj, b),
        in_specs=[pl.BlockSpec((t, d), lambda j, bi: (bi, 0)),
                  pl.BlockSpec((None, d, tn), lambda j, bi: (layer, 0, j)),
                  pl.BlockSpec((None, d, tn), lambda j, bi: (layer, 0, nj + j)),
                  pl.BlockSpec((CONV_W, tn), lambda j, bi: (0, j)),
                  pl.BlockSpec((1, tn), lambda j, bi: (0, j))],
        out_specs=pl.BlockSpec((t, tn), lambda j, bi: (bi, j)),
        out_shape=jax.ShapeDtypeStruct((b * t, dff), _BF),
        scratch_shapes=[pltpu.VMEM((t + 16, tn), _F32), pltpu.VMEM((d, tn), _BF), pltpu.VMEM((d, tn), _BF)],
        compiler_params=_cparams(("arbitrary", "arbitrary")),
    )(h, w_up, w_up, conv_w, conv_b)


def _pack_w_in_body(w_ref, o_ref):
    src = dict(zip(("na_q", "na_k", "na_v", "cq", "ckv", "kpe", "r_q", "r_k", "r_v", "r_g"),
                   [int(i) for i in np.cumsum((0,) + IN_SPLITS[:-1])]))
    q_scale = HEAD_DIM ** -0.5 * LOG2E
    k_scale = HEAD_DIM ** -0.5

    def put(dst_block, name, width, mul=None):
        x = w_ref[:, src[name]:src[name] + width]
        if mul is not None:
            x = x * mul
        o_ref[:, dst_block * LANE:dst_block * LANE + width] = x.astype(o_ref.dtype)

    put(CB_CQ, "cq", MLA_Q_RANK + MLA_KV_RANK)
    put(CB_NAQ, "na_q", NA_WIDTH, q_scale)
    put(CB_NAK, "na_k", 2 * NA_WIDTH)
    put(CB_RQ, "r_q", RET_WIDTH)
    put(CB_RK, "r_k", RET_WIDTH, k_scale)
    put(CB_RV, "r_v", 2 * RET_WIDTH)
    rows = w_ref.shape[0]
    tail = jnp.concatenate([w_ref[:, src["kpe"]:src["kpe"] + MLA_ROPE],
                            jnp.zeros((rows, PACKED_IN_WIDTH - CB_KPE * LANE - MLA_ROPE), _F32)], 1)
    o_ref[:, CB_KPE * LANE:] = tail.astype(o_ref.dtype)


def _pack_w_in(w_in, layer):
    depth, d, n = w_in.shape
    tk = _tile(d, 256)
    nk = d // tk
    return pl.pallas_call(
        _pack_w_in_body,
        grid=(nk,),
        in_specs=[pl.BlockSpec((tk, n), lambda i: (layer * nk + i, 0))],
        out_specs=pl.BlockSpec((tk, PACKED_IN_WIDTH), lambda i: (i, 0)),
        out_shape=jax.ShapeDtypeStruct((d, PACKED_IN_WIDTH), _BF),
        compiler_params=_cparams(("arbitrary",)),
    )(w_in.reshape(depth * d, n))


def _pack_w_uq(w):
    r = w.shape[0]
    w = w.reshape(r, MLA_HEADS, MLA_NOPE + MLA_ROPE) * ((MLA_NOPE + MLA_ROPE) ** -0.5 * LOG2E)
    w = jnp.pad(w, ((0, 0), (0, 0), (0, MLA_QK_PAD - MLA_NOPE - MLA_ROPE)))
    return w.reshape(r, MLA_HEADS * MLA_QK_PAD).astype(_BF)


def kernel(x, c, ctx, c_ctx, w_ada, b_ada, w_in, mla_q_norm, mla_kv_norm, w_uq, w_ukv, na_rpb, ret_decay,
           w_o, ln1_g, ln1_b, w_up, conv_w, conv_b, w_down, ln2_g, ln2_b):
    b, t, d = x.shape
    cl = ctx.shape[1]
    depth = w_ada.shape[0]
    dff = conv_w.shape[-1]
    assert b + 1 <= MOD_ROWS
    alpha = (2 * depth) ** 0.25
    ctx_row = b

    cvec = jnp.concatenate([c, c_ctx[None], jnp.zeros((MOD_ROWS - b - 1, d), c.dtype)], 0)
    mod3 = _ada(cvec, w_ada, b_ada)

    tm_l = _tile(t, 1024)
    tm_c = _tile(cl, 1024)
    lat_row = lambda tm: (lambda i: i // (t // tm))
    ctx_rowf = lambda i: ctx_row

    rope_pe = _rope_tables(t, MLA_ROPE)
    rope_ret = _rope_tables(t, HEAD_DIM)
    id_pe = _rope_tables(cl, MLA_ROPE, identity=True)
    id_ret = _rope_tables(cl, HEAD_DIM, identity=True)

    xf = x.reshape(b * t, d)
    cf = ctx.reshape(b * cl, d)
    h1 = _modulate(xf, mod3, 0, lat_row(tm_l), tm_l)
    hc1 = _modulate(cf, mod3, 0, ctx_rowf, tm_c)

    tq = _tile(t, 2048)
    tqc = _tile(cl, 256)
    tn_ff = _tile(dff, 512)
    zero_state = jnp.zeros((b, RET_HEADS, HEAD_DIM, HEAD_DIM), _F32)
    w_o_b = w_o.astype(_BF)
    w_down_b = w_down.astype(_BF)

    for l in range(depth):
        last = l == depth - 1
        w_in_p = _pack_w_in(w_in, l)
        w_uq_p = _pack_w_uq(w_uq[l])
        w_ukv_p = w_ukv[l].astype(_BF)
        gq = mla_q_norm[l][None]
        gkv = mla_kv_norm[l][None]
        lg = jnp.log1p(-jnp.exp2(ret_decay[l].astype(_F32)))
        bias = _na_col_bias(na_rpb[l])

        p = _mm(h1, w_in_p, tm_l, 1024)
        pc = _mm(hc1, w_in_p, tm_c, 1024)

        kc_m, vc_m = _mla_kv(pc, gkv, w_ukv_p, id_pe, b, cl, _tile(cl, 512))
        kl_m, vl_m = _mla_kv(p, gkv, w_ukv_p, rope_pe, b, t, _tile(t, 512))
        q_m = _mla_q(p, gq, w_uq_p, rope_pe, b, t, _tile(t, 512))
        y_mla = _dense_attn(
            q_m, pl.BlockSpec((None, None, tq, MLA_QK_PAD), lambda bi, h, i: (bi, h, i, 0)),
            [(kc_m, _head4_spec(cl, MLA_QK_PAD), vc_m, _head4_spec(VT_ROWS, cl)),
             (kl_m, _head4_spec(t, MLA_QK_PAD), vl_m, _head4_spec(VT_ROWS, t))],
            b, MLA_HEADS, t // tq, tq, MLA_V, nsub=tq // _tile(tq, 256))

        y_na = _na_attn(p, pc, bias, b, t, cl)

        yc_ret, s_f, s_b = _retention(pc, lg, zero_state, zero_state, id_ret, b, cl)
        y_ret, _, _ = _retention(p, lg, s_f, s_b, rope_ret, b, t)

        tm_o = _tile(t, 512)
        x_new, h2 = _proj_ln([y_na, y_mla, y_ret], w_o_b, xf, mod3, l, 2, lat_row(tm_o),
                             ln1_g[l][None], ln1_b[l][None], alpha, tm_o, tm_o // _tile(tm_o, 256), with_next=True)

        tm_d = _tile(t, 256)

        if not last:
            qc_m = _mla_q(pc, gq, w_uq_p, id_pe, b, cl, _tile(cl, 512))
            yc_mla = _dense_attn(
                qc_m, pl.BlockSpec((None, None, tqc, MLA_QK_PAD), lambda bi, h, i: (bi, h, i, 0)),
                [(kc_m, _head4_spec(cl, MLA_QK_PAD), vc_m, _head4_spec(VT_ROWS, cl))],
                b, MLA_HEADS, cl // tqc, tqc, MLA_V, nsub=1)
            nbc = cl // tqc
            yc_na = _dense_attn(
                pc, pl.BlockSpec((tqc, LANE), lambda bi, h, i: (bi * nbc + i, CB_NAQ + h)),
                [(pc, pl.BlockSpec((cl, LANE), lambda bi, h, i: (bi, CB_NAK + h)),
                  pc, pl.BlockSpec((cl, LANE), lambda bi, h, i: (bi, CB_NAV + h)))],
                b, NA_HEADS, nbc, tqc, HEAD_DIM, nsub=1, v_row_major=True)
            tm_oc = _tile(cl, 256)
            c_new, hc2 = _proj_ln([yc_na, yc_mla, yc_ret], w_o_b, cf, mod3, l, 2, ctx_rowf,
                                  ln1_g[l][None], ln1_b[l][None], alpha, tm_oc, 1, with_next=True)

        g_l = _glu(h2, w_up, l, conv_w[l], conv_b[l][None], b, t, tn_ff)
        xf, h1 = _proj_ln([g_l], w_down_b, x_new, mod3, l, 5, lat_row(tm_d), ln2_g[l][None], ln2_b[l][None],
                          alpha, tm_d, 1, with_next=not last)
        if not last:
            g_c = _glu(hc2, w_up, l, conv_w[l], conv_b[l][None], b, cl, tn_ff)
            tm_dc = _tile(cl, 256)
            cf, hc1 = _proj_ln([g_c], w_down_b, c_new, mod3, l, 5, ctx_rowf, ln2_g[l][None], ln2_b[l][None],
                               alpha, tm_dc, 1, with_next=True)

    return xf.reshape(b, t, d)
```

```python
import functools

import numpy as np
import jax
import jax.numpy as jnp
from jax import lax
from jax.experimental import pallas as pl
from jax.experimental.pallas import tpu as pltpu

GRID_W = 64
HEAD_DIM = 128
NA_HEADS = 6
MLA_HEADS = 5
RET_HEADS = 5
NA_WIDTH = NA_HEADS * HEAD_DIM
MLA_WIDTH = MLA_HEADS * HEAD_DIM
RET_WIDTH = RET_HEADS * HEAD_DIM
NA_WIN_R = 8
NA_WIN_C = 16
MLA_Q_RANK = 512
MLA_KV_RANK = 512
MLA_NOPE = 128
MLA_ROPE = 64
MLA_V = 128
MLA_QK_PAD = 256
RET_CHUNK = 128
CONV_W = 3
ROPE_BASE = 10000.0
LN_EPS = 1e-5
RMS_EPS = 1e-6
NEG_INF = -1e30
LOG2E = 1.4426950408889634
IN_SPLITS = (NA_WIDTH, NA_WIDTH, NA_WIDTH, MLA_Q_RANK, MLA_KV_RANK, MLA_ROPE,
             RET_WIDTH, RET_WIDTH, RET_WIDTH, RET_WIDTH)

LANE = 128
CB_CQ, CB_CKV, CB_NAQ, CB_NAK, CB_NAV = 0, 4, 8, 14, 20
CB_RQ, CB_RK, CB_RV, CB_RG, CB_KPE = 26, 31, 36, 41, 46
PACKED_IN_WIDTH = 48 * LANE
MOD_ROWS = 8
VMEM_LIMIT_MB = 56

_BF = jnp.bfloat16
_F32 = jnp.float32


def _cparams(sem, vmem_mb=VMEM_LIMIT_MB):
    return pltpu.CompilerParams(dimension_semantics=sem, vmem_limit_bytes=vmem_mb << 20)


def _tile(n, pref):
    t = min(n, pref)
    while n % t:
        t //= 2
    return t


def _dot(a, b):
    return jnp.dot(a, b, preferred_element_type=_F32)


def _dot_nt(a, b):
    return lax.dot_general(a, b, (((1,), (1,)), ((), ())), preferred_element_type=_F32)


def _silu(x):
    return x / (1.0 + jnp.exp(-x))


def _ada_body(c_ref, w_ref, b_ref, o_ref):
    part = _dot(_silu(c_ref[...]).astype(_BF), w_ref[0].astype(_BF))

    @pl.when(pl.program_id(1) == 0)
    def _():
        o_ref[0] = part + b_ref[0]

    @pl.when(pl.program_id(1) > 0)
    def _():
        o_ref[0] += part


def _ada(cvec, w_ada, b_ada):
    depth, d, n = w_ada.shape
    tk = _tile(d, 256)
    out = pl.pallas_call(
        _ada_body,
        grid=(depth, d // tk),
        in_specs=[pl.BlockSpec((MOD_ROWS, tk), lambda l, k: (0, k)),
                  pl.BlockSpec((1, tk, n), lambda l, k: (l, k, 0)),
                  pl.BlockSpec((1, 1, n), lambda l, k: (l, 0, 0))],
        out_specs=pl.BlockSpec((1, MOD_ROWS, n), lambda l, k: (l, 0, 0)),
        out_shape=jax.ShapeDtypeStruct((depth, MOD_ROWS, n), _F32),
        compiler_params=_cparams(("arbitrary", "arbitrary")),
    )(cvec, w_ada, b_ada.reshape(depth, 1, n))
    return out.reshape(depth * MOD_ROWS, 1, n)


def _mod_spec(d, layer, chunk, row_fn):
    return pl.BlockSpec((1, 1, d), lambda i, *_: (layer * MOD_ROWS + row_fn(i), 0, chunk))


def _modulate_body(x_ref, sh_ref, sc_ref, o_ref):
    o_ref[...] = (x_ref[...] * (1.0 + sc_ref[0]) + sh_ref[0]).astype(o_ref.dtype)


def _modulate(x2, mod3, layer, row_fn, tm):
    m, d = x2.shape
    return pl.pallas_call(
        _modulate_body,
        grid=(m // tm,),
        in_specs=[pl.BlockSpec((tm, d), lambda i: (i, 0)),
                  _mod_spec(d, layer, 0, row_fn), _mod_spec(d, layer, 1, row_fn)],
        out_specs=pl.BlockSpec((tm, d), lambda i: (i, 0)),
        out_shape=jax.ShapeDtypeStruct((m, d), _BF),
        compiler_params=_cparams(("arbitrary",)),
    )(x2, mod3, mod3)


def _mm_body(a_ref, w_ref, o_ref):
    o_ref[...] = _dot(a_ref[...], w_ref[...]).astype(o_ref.dtype)


def _mm(a, w, tm):
    m, k = a.shape
    nj, _, tn = w.shape
    return pl.pallas_call(
        _mm_body,
        grid=(m // tm, nj),
        in_specs=[pl.BlockSpec((tm, k), lambda i, j: (i, 0)),
                  pl.BlockSpec((None, k, tn), lambda i, j: (j, 0, 0))],
        out_specs=pl.BlockSpec((tm, tn), lambda i, j: (i, j)),
        out_shape=jax.ShapeDtypeStruct((m, nj * tn), _BF),
        compiler_params=_cparams(("arbitrary", "arbitrary")),
    )(a, w)


def _rope_tables(t, rot_dim, identity=False):
    pos = jnp.arange(t)
    row = (pos // GRID_W).astype(_F32)
    col = (pos % GRID_W).astype(_F32)
    nf = rot_dim // 4
    inv = ROPE_BASE ** (-jnp.arange(nf, dtype=_F32) / nf)
    ar = row[:, None] * inv[None]
    ac = col[:, None] * inv[None]
    ang = jnp.concatenate([ar, ar, ac, ac], -1)
    cos, sin = jnp.cos(ang), jnp.sin(ang)
    if identity:
        cos, sin = jnp.ones_like(cos), jnp.zeros_like(sin)
    first = (np.arange(rot_dim) % (rot_dim // 2)) < nf
    sin_a = jnp.where(first, -sin, 0.0)
    sin_b = jnp.where(first, 0.0, sin)
    pad = LANE - rot_dim
    if pad:
        cos = jnp.pad(cos, ((0, 0), (0, pad)), constant_values=1.0)
        sin_a = jnp.pad(sin_a, ((0, 0), (0, pad)))
        sin_b = jnp.pad(sin_b, ((0, 0), (0, pad)))
    return cos, sin_a, sin_b


def _rope(x, cos, sin_a, sin_b, nf):
    return x * cos + pltpu.roll(x, LANE - nf, 1) * sin_a + pltpu.roll(x, nf, 1) * sin_b


def _rms(x, g):
    return x * lax.rsqrt(jnp.mean(x * x, -1, keepdims=True) + RMS_EPS) * g


def _mla_q_body(p_ref, g_ref, w_ref, cos_ref, sa_ref, sb_ref, o_ref):
    xb = _rms(p_ref[...].astype(_F32), g_ref[...]).astype(_BF)
    cos, sa, sb = cos_ref[...], sa_ref[...], sb_ref[...]
    for h in range(MLA_HEADS):
        acc = _dot(xb, w_ref[:, h * MLA_QK_PAD:(h + 1) * MLA_QK_PAD])
        o_ref[0, h, :, 0:LANE] = acc[:, :LANE].astype(_BF)
        o_ref[0, h, :, LANE:] = _rope(acc[:, LANE:], cos, sa, sb, MLA_ROPE // 4).astype(_BF)


def _mla_q(p, g, w, tables, b, t, tm):
    nb = t // tm
    tab_spec = pl.BlockSpec((tm, LANE), lambda i: (i % nb, 0))
    return pl.pallas_call(
        _mla_q_body,
        grid=(b * nb,),
        in_specs=[pl.BlockSpec((tm, MLA_Q_RANK), lambda i: (i, CB_CQ // 4)),
                  pl.BlockSpec((1, MLA_Q_RANK), lambda i: (0, 0)),
                  pl.BlockSpec(w.shape, lambda i: (0, 0)),
                  tab_spec, tab_spec, tab_spec],
        out_specs=pl.BlockSpec((1, MLA_HEADS, tm, MLA_QK_PAD), lambda i: (i // nb, 0, i % nb, 0)),
        out_shape=jax.ShapeDtypeStruct((b, MLA_HEADS, t, MLA_QK_PAD), _BF),
        compiler_params=_cparams(("arbitrary",)),
    )(p, g, w, *tables)


VT_ROWS = HEAD_DIM + 16


def _values_t(v):
    n = v.shape[0]
    ones = (lax.broadcasted_iota(jnp.int32, (VT_ROWS - HEAD_DIM, n), 0) == 0).astype(_BF)
    return jnp.concatenate([v.astype(_F32).T.astype(_BF), ones], 0)


def _mla_kv_body(ckv_ref, kpe_ref, g_ref, w_ref, cos_ref, sa_ref, sb_ref, k_ref, vt_ref):
    xb = _rms(ckv_ref[...].astype(_F32), g_ref[...]).astype(_BF)
    pe = _rope(kpe_ref[...].astype(_F32), cos_ref[...], sa_ref[...], sb_ref[...], MLA_ROPE // 4).astype(_BF)
    width = MLA_NOPE + MLA_V
    for h in range(MLA_HEADS):
        acc = _dot(xb, w_ref[:, h * width:(h + 1) * width])
        k_ref[0, h, :, 0:LANE] = acc[:, :MLA_NOPE].astype(_BF)
        k_ref[0, h, :, LANE:] = pe
        vt_ref[0, h] = _values_t(acc[:, MLA_NOPE:])


def _mla_kv(p, g, w, tables, b, t, tm):
    nb = t // tm
    tab_spec = pl.BlockSpec((tm, LANE), lambda i: (i % nb, 0))
    return pl.pallas_call(
        _mla_kv_body,
        grid=(b * nb,),
        in_specs=[pl.BlockSpec((tm, MLA_KV_RANK), lambda i: (i, CB_CKV // 4)),
                  pl.BlockSpec((tm, LANE), lambda i: (i, CB_KPE)),
                  pl.BlockSpec((1, MLA_KV_RANK), lambda i: (0, 0)),
                  pl.BlockSpec(w.shape, lambda i: (0, 0)),
                  tab_spec, tab_spec, tab_spec],
        out_specs=[pl.BlockSpec((1, MLA_HEADS, tm, MLA_QK_PAD), lambda i: (i // nb, 0, i % nb, 0)),
                   pl.BlockSpec((1, MLA_HEADS, VT_ROWS, tm), lambda i: (i // nb, 0, 0, i % nb))],
        out_shape=[jax.ShapeDtypeStruct((b, MLA_HEADS, t, MLA_QK_PAD), _BF),
                   jax.ShapeDtypeStruct((b, MLA_HEADS, VT_ROWS, t), _BF)],
        compiler_params=_cparams(("arbitrary",)),
    )(p, p, g, w, *tables)


def _skewed(n, stages):
    carry = [None] * n
    for step in range(n + len(stages) - 1):
        for s, stage in enumerate(stages):
            j = step - s
            if 0 <= j < n:
                carry[j] = stage(j, carry[j])


def _attn_stages(scores_fn, values_fn, store_fn):
    def s_scores(j, _):
        return scores_fn(j)

    def s_max(j, sc):
        return sc, functools.reduce(jnp.maximum, [jnp.max(s, 0, keepdims=True) for s in sc])

    def s_exp(j, c):
        sc, m = c
        return [jnp.exp2(s - m).astype(_BF) for s in sc]

    def s_pv(j, p):
        acc = functools.reduce(jnp.add, [_dot(vt, pi) for vt, pi in zip(values_fn(j), p)])
        store_fn(j, (acc[0:HEAD_DIM, :] / acc[HEAD_DIM:HEAD_DIM + 1, :]).T)

    return [s_scores, s_max, s_exp, s_pv]


def _attn_body(*refs, nseg, nsub, v_row_major):
    q_ref, o_ref = refs[0], refs[-1]
    ks = [refs[1 + 2 * i][...] for i in range(nseg)]
    vts = [refs[2 + 2 * i][...] for i in range(nseg)]
    if v_row_major:
        vts = [_values_t(v) for v in vts]
    ts = q_ref.shape[0] // nsub
    rows = lambda j: slice(j * ts, (j + 1) * ts)

    def store(j, out):
        o_ref[rows(j), :] = out.astype(o_ref.dtype)

    _skewed(nsub, _attn_stages(lambda j: [_dot_nt(k, q_ref[rows(j), :]) for k in ks], lambda j: vts, store))


def _dense_attn(q, q_spec, kv, b, heads, tq_blocks, tq, dv, nsub, v_row_major=False):
    args, specs = [q], [q_spec]
    for k, ks, v, vs in kv:
        args += [k, v]
        specs += [ks, vs]
    return pl.pallas_call(
        functools.partial(_attn_body, nseg=len(kv), nsub=nsub, v_row_major=v_row_major),
        grid=(b, heads, tq_blocks),
        in_specs=specs,
        out_specs=pl.BlockSpec((tq, dv), lambda bi, h, i: (bi * tq_blocks + i, h)),
        out_shape=jax.ShapeDtypeStruct((b * tq_blocks * tq, heads * dv), _BF),
        compiler_params=_cparams(("arbitrary", "arbitrary", "arbitrary")),
    )(*args)


def _head4_spec(rows, cols):
    return pl.BlockSpec((None, None, rows, cols), lambda bi, h, i: (bi, h, 0, 0))


NA_GROUP = 4
NA_KEY_ROWS = 12


def _na_group_window(gi, rows):
    ngroups = rows // NA_GROUP
    if gi == 0:
        return 0, 0
    if gi == ngroups - 1:
        return 2, rows - NA_KEY_ROWS
    return 1, gi * NA_GROUP - NA_WIN_R // 2


NA_DR = 2 * NA_WIN_R - 1


def _na_col_bias(rpb):
    c = np.arange(GRID_W)
    col_start = np.clip(c - NA_WIN_C // 2, 0, GRID_W - NA_WIN_C)
    kc = np.arange(GRID_W)
    in_win = (kc[None, :] >= col_start[:, None]) & (kc[None, :] < col_start[:, None] + NA_WIN_C)
    dc = np.clip(kc[None, :] - c[:, None] + NA_WIN_C - 1, 0, 2 * NA_WIN_C - 2)
    col_sel = (dc[:, :, None] == np.arange(2 * NA_WIN_C - 1)).astype(np.float32)
    toep = jnp.einsum('hrj,cqj->hrqc', rpb.astype(_F32), col_sel, precision=lax.Precision.HIGHEST)
    toep = jnp.where(in_win.T[None, None], toep * LOG2E, NEG_INF)
    masked = jnp.full((rpb.shape[0], 1, GRID_W, GRID_W), NEG_INF, _F32)
    return jnp.concatenate([toep, masked], 1)


def _na_block_index():
    g = np.arange(NA_GROUP)[:, None]
    kr = np.arange(NA_KEY_ROWS)[None, :]
    half = NA_WIN_R // 2
    dr = np.stack([kr - g + NA_WIN_R - 1, kr - g + NA_WIN_R - 1 - half, kr - g - 1])
    w0 = np.stack([0 * g, g, half + 0 * g])
    row_ok = (kr[None] >= w0) & (kr[None] < w0 + NA_WIN_R)
    assert np.all((dr[row_ok] >= 0) & (dr[row_ok] < NA_DR))
    return np.where(row_ok, dr, NA_DR)


def _na_body(q_ref, k_ref, v_ref, kc_ref, vc_ref, cb_ref, o_ref, bias_ref, vt_ref, *, rows):
    @pl.when(pl.program_id(1) == 0)
    def _():
        idx = _na_block_index()
        for typ in range(3):
            for kr in range(NA_KEY_ROWS):
                for gp in range(NA_GROUP // 2):
                    pair = jnp.concatenate([cb_ref[int(idx[typ, 2 * gp, kr])], cb_ref[int(idx[typ, 2 * gp + 1, kr])]], 1)
                    bias_ref[typ, kr * GRID_W:(kr + 1) * GRID_W, gp * 2 * GRID_W:(gp + 1) * 2 * GRID_W] = pair

    vt_ref[...] = _values_t(v_ref[...])
    kc = kc_ref[...]
    vct = _values_t(vc_ref[...])
    gq = NA_GROUP * GRID_W

    def band(gi):
        ws = _na_group_window(gi, rows)[1]
        return slice(ws * GRID_W, (ws + NA_KEY_ROWS) * GRID_W)

    def scores(gi):
        q = q_ref[gi * gq:(gi + 1) * gq, :]
        return [_dot_nt(k_ref[band(gi), :], q) + bias_ref[_na_group_window(gi, rows)[0]], _dot_nt(kc, q)]

    def store(gi, out):
        o_ref[gi * gq:(gi + 1) * gq, :] = out.astype(o_ref.dtype)

    _skewed(rows // NA_GROUP, _attn_stages(scores, lambda gi: [vt_ref[:, band(gi)], vct], store))


def _na_attn(p, pc, bias, b, t, c):
    rows = t // GRID_W
    assert t % GRID_W == 0 and rows % NA_GROUP == 0 and rows >= NA_KEY_ROWS + NA_GROUP and 2 * GRID_W == LANE
    return pl.pallas_call(
        functools.partial(_na_body, rows=rows),
        grid=(NA_HEADS, b),
        in_specs=[pl.BlockSpec((t, LANE), lambda h, bi: (bi, CB_NAQ + h)),
                  pl.BlockSpec((t, LANE), lambda h, bi: (bi, CB_NAK + h)),
                  pl.BlockSpec((t, LANE), lambda h, bi: (bi, CB_NAV + h)),
                  pl.BlockSpec((c, LANE), lambda h, bi: (bi, CB_NAK + h)),
                  pl.BlockSpec((c, LANE), lambda h, bi: (bi, CB_NAV + h)),
                  pl.BlockSpec((None, NA_DR + 1, GRID_W, GRID_W), lambda h, bi: (h, 0, 0, 0))],
        out_specs=pl.BlockSpec((t, LANE), lambda h, bi: (bi, h)),
        out_shape=jax.ShapeDtypeStruct((b * t, NA_WIDTH), _BF),
        scratch_shapes=[pltpu.VMEM((3, NA_KEY_ROWS * GRID_W, NA_GROUP * GRID_W), _F32),
                        pltpu.VMEM((VT_ROWS, t), _BF)],
        compiler_params=_cparams(("arbitrary", "arbitrary")),
    )(p, p, p, pc, pc, bias)


def _ret_body(lg_ref, q_ref, k_ref, v_ref, g_ref, s0f_ref, s0b_ref, cos_ref, sa_ref, sb_ref,
              o_ref, sf_ref, sbo_ref, kr_scr, u_hist, st_hist, *, n):
    h = pl.program_id(0)
    lgf = lg_ref[0, h]
    lgb = lg_ref[1, h]
    L = RET_CHUNK
    pos_c = lax.broadcasted_iota(jnp.int32, (L, 1), 0).astype(_F32)
    pos_r = lax.broadcasted_iota(jnp.int32, (1, L), 1).astype(_F32)
    diff = pos_c - pos_r
    decay = jnp.where(diff > 0, jnp.exp(jnp.maximum(diff, 0.0) * lgf),
                      jnp.where(diff < 0, jnp.exp(jnp.maximum(-diff, 0.0) * lgb), 2.0))
    qdec_f = jnp.exp((pos_c + 1.0) * lgf)
    kdec_f = jnp.exp((L - 1.0 - pos_c) * lgf)
    qdec_b = jnp.exp((L - pos_c) * lgb)
    kdec_b = jnp.exp(pos_c * lgb)
    cd_f = jnp.exp(L * lgf)
    cd_b = jnp.exp(L * lgb)
    nf = HEAD_DIM // 4

    def rows_of(i):
        return slice(i * L, (i + 1) * L)

    def roped(ref, i):
        r = rows_of(i)
        return _rope(ref[r, :].astype(_F32), cos_ref[r, :], sa_ref[r, :], sb_ref[r, :], nf)

    for i in range(n):
        k = roped(k_ref, i)
        kr_scr[rows_of(i), :] = k.astype(_BF)
        kd = jnp.concatenate([k * kdec_f, k * kdec_b], 1).astype(_BF)
        u_hist[i] = lax.dot_general(kd, v_ref[rows_of(i), :], (((0,), (0,)), ((), ())),
                                    preferred_element_type=_F32)

    state = s0f_ref[...]
    for i in range(n):
        st_hist[i, 0:HEAD_DIM, :] = state.astype(_BF)
        state = cd_f * state + u_hist[i, 0:HEAD_DIM, :]
    sf_ref[...] = state
    state = s0b_ref[...]
    for i in reversed(range(n)):
        st_hist[i, HEAD_DIM:, :] = state.astype(_BF)
        state = cd_b * state + u_hist[i, HEAD_DIM:, :]
    sbo_ref[...] = state

    for i in range(n):
        r = rows_of(i)
        q = roped(q_ref, i)
        a = _dot_nt(q.astype(_BF), kr_scr[r, :]) * decay
        qd = jnp.concatenate([q * qdec_f, q * qdec_b], 1).astype(_BF)
        o = _dot(a.astype(_BF), v_ref[r, :]) + _dot(qd, st_hist[i])
        mu = jnp.mean(o, -1, keepdims=True)
        oc = o - mu
        var = jnp.mean(oc * oc, -1, keepdims=True)
        y = oc * lax.rsqrt(var + LN_EPS) * _silu(g_ref[r, :].astype(_F32))
        o_ref[r, :] = y.astype(o_ref.dtype)


def _retention(p, lg, s0f, s0b, tables, b, t):
    n = t // RET_CHUNK
    col = lambda cb: pl.BlockSpec((t, LANE), lambda h, bi: (bi, cb + h))
    st_spec = pl.BlockSpec((None, None, HEAD_DIM, HEAD_DIM), lambda h, bi: (bi, h, 0, 0))
    tab_spec = pl.BlockSpec((t, LANE), lambda h, bi: (0, 0))
    st_shape = jax.ShapeDtypeStruct((b, RET_HEADS, HEAD_DIM, HEAD_DIM), _F32)
    return pl.pallas_call(
        functools.partial(_ret_body, n=n),
        grid=(RET_HEADS, b),
        in_specs=[pl.BlockSpec(memory_space=pltpu.SMEM),
                  col(CB_RQ), col(CB_RK), col(CB_RV), col(CB_RG), st_spec, st_spec,
                  tab_spec, tab_spec, tab_spec],
        out_specs=[pl.BlockSpec((t, LANE), lambda h, bi: (bi, h)), st_spec, st_spec],
        out_shape=[jax.ShapeDtypeStruct((b * t, RET_WIDTH), _BF), st_shape, st_shape],
        scratch_shapes=[pltpu.VMEM((t, HEAD_DIM), _BF),
                        pltpu.VMEM((n, 2 * HEAD_DIM, HEAD_DIM), _F32),
                        pltpu.VMEM((n, 2 * HEAD_DIM, HEAD_DIM), _BF)],
        compiler_params=_cparams(("arbitrary", "arbitrary")),
    )(lg, p, p, p, p, s0f, s0b, *tables)


def _ln_epilogue(y, rows, x_ref, gate_ref, lng_ref, lnb_ref, alpha, xo_ref, ho_ref, sh_ref, sc_ref):
    z = alpha * x_ref[rows, :] + gate_ref[0] * y
    mu = jnp.mean(z, -1, keepdims=True)
    zc = z - mu
    var = jnp.mean(zc * zc, -1, keepdims=True)
    xn = zc * lax.rsqrt(var + LN_EPS) * lng_ref[...] + lnb_ref[...]
    xo_ref[rows, :] = xn
    if ho_ref is not None:
        ho_ref[rows, :] = (xn * (1.0 + sc_ref[0]) + sh_ref[0]).astype(ho_ref.dtype)


def _proj_ln_body(*refs, n_in, alpha, with_next, nsub):
    a_refs, refs = refs[:n_in], refs[n_in:]
    if with_next:
        w_ref, x_ref, gate_ref, sh_ref, sc_ref, lng_ref, lnb_ref, xo_ref, ho_ref = refs
    else:
        w_ref, x_ref, gate_ref, lng_ref, lnb_ref, xo_ref = refs
        sh_ref = sc_ref = ho_ref = None
    ts = x_ref.shape[0] // nsub
    for j in range(nsub):
        r = slice(j * ts, (j + 1) * ts)
        a = a_refs[0][r, :] if n_in == 1 else jnp.concatenate([ar[r, :] for ar in a_refs], 1)
        _ln_epilogue(_dot(a, w_ref[...]), r, x_ref, gate_ref, lng_ref, lnb_ref, alpha, xo_ref, ho_ref, sh_ref, sc_ref)


def _proj_ln(a_list, w, x2, mod3, layer, gate_chunk, row_fn, lng, lnb, alpha, tm, nsub, with_next):
    m, d = x2.shape
    row = lambda width: pl.BlockSpec((tm, width), lambda i: (i, 0))
    vec = pl.BlockSpec((1, d), lambda i: (0, 0))
    w_spec = pl.BlockSpec((None,) + w.shape[1:], lambda i: (layer, 0, 0), pipeline_mode=pl.Buffered(1))
    in_specs = [row(a.shape[1]) for a in a_list] + [w_spec, row(d), _mod_spec(d, layer, gate_chunk, row_fn)]
    args = list(a_list) + [w, x2, mod3]
    if with_next:
        nxt = (layer, gate_chunk + 1) if gate_chunk < 5 else (layer + 1, 0)
        in_specs += [_mod_spec(d, nxt[0], nxt[1], row_fn), _mod_spec(d, nxt[0], nxt[1] + 1, row_fn)]
        args += [mod3, mod3]
    in_specs += [vec, vec]
    args += [lng, lnb]
    out_specs = [row(d)] + ([row(d)] if with_next else [])
    out_shape = [jax.ShapeDtypeStruct((m, d), _F32)] + ([jax.ShapeDtypeStruct((m, d), _BF)] if with_next else [])
    out = pl.pallas_call(
        functools.partial(_proj_ln_body, n_in=len(a_list), alpha=alpha, with_next=with_next, nsub=nsub),
        grid=(m // tm,),
        in_specs=in_specs, out_specs=out_specs, out_shape=out_shape,
        compiler_params=_cparams(("arbitrary",)),
    )(*args)
    return (out[0], out[1]) if with_next else (out[0], None)


def _glu_chunks(t):
    rc = _tile(t, 512)
    return [(i * rc, rc) for i in range(t // rc)]


def _glu_body(h_ref, wa_ref, wu_ref, cw_ref, cb_ref, o_ref, *rest, t, cast):
    if cast:
        wab, wub, a_scr = rest

        @pl.when(pl.program_id(1) == 0)
        def _():
            wab[...] = wa_ref[...].astype(_BF)
            wub[...] = wu_ref[...].astype(_BF)
    else:
        (a_scr,) = rest
        wab, wub = wa_ref, wu_ref

    chunks = _glu_chunks(t)
    nchunk = len(chunks)
    halo = 8
    tn = o_ref.shape[1]
    zeros = jnp.zeros((halo, tn), _F32)
    a_scr[0:halo, :] = zeros
    a_scr[halo + t:, :] = zeros
    w0, w1, w2 = cw_ref[0:1, :], cw_ref[1:2, :], cw_ref[2:3, :]
    bias = cb_ref[...]
    u_prev = None
    for c in range(nchunk + 1):
        u_cur = None
        if c < nchunk:
            r0, rc = chunks[c]
            hc = h_ref[r0:r0 + rc, :]
            a_scr[halo + r0:halo + r0 + rc, :] = _dot(hc, wab[...])
            u_cur = _dot(hc, wub[...])
        if c > 0:
            r0, rc = chunks[c - 1]
            base = halo + r0
            lo = a_scr[base - 1:base - 1 + rc, :]
            mid = a_scr[base:base + rc, :]
            hi = a_scr[base + 1:base + 1 + rc, :]
            acc = bias + lo * w0 + mid * w1 + hi * w2
            o_ref[r0:r0 + rc, :] = (_silu(acc) * u_prev).astype(o_ref.dtype)
        u_prev = u_cur


def _glu(h, w_up, layer, conv_w, conv_b, b, t, tn, w_bf16=None):
    d = h.shape[1]
    dff = conv_w.shape[1]
    nj = dff // tn
    cast = w_bf16 is None
    g_spec = pl.BlockSpec((t, tn), lambda j, bi: (bi, j))
    g_shape = jax.ShapeDtypeStruct((b * t, dff), _BF)
    wb_spec = pl.BlockSpec((None, d, tn), lambda j, bi: (j, 0, 0))
    if cast:
        w_args = (w_up, w_up)
        w_specs = [pl.BlockSpec((None, d, tn), lambda j, bi: (layer, 0, j)),
                   pl.BlockSpec((None, d, tn), lambda j, bi: (layer, 0, nj + j))]
        wb_shape = jax.ShapeDtypeStruct((nj, d, tn), _BF)
        out_specs, out_shape = [g_spec, wb_spec, wb_spec], [g_shape, wb_shape, wb_shape]
    else:
        w_args, w_specs = tuple(w_bf16), [wb_spec, wb_spec]
        out_specs, out_shape = g_spec, g_shape
    out = pl.pallas_call(
        functools.partial(_glu_body, t=t, cast=cast),
        grid=(nj, b),
        in_specs=[pl.BlockSpec((t, d), lambda j, bi: (bi, 0)), *w_specs,
                  pl.BlockSpec((CONV_W, tn), lambda j, bi: (0, j)),
                  pl.BlockSpec((1, tn), lambda j, bi: (0, j))],
        out_specs=out_specs, out_shape=out_shape,
        scratch_shapes=[pltpu.VMEM((t + 16, tn), _F32)],
        compiler_params=_cparams(("arbitrary", "arbitrary")),
    )(h, *w_args, conv_w, conv_b)
    return (out[0], (out[1], out[2])) if cast else out


def _pack_w_in_body(w_ref, o_ref):
    src = dict(zip(("na_q", "na_k", "na_v", "cq", "ckv", "kpe", "r_q", "r_k", "r_v", "r_g"),
                   [int(i) for i in np.cumsum((0,) + IN_SPLITS[:-1])]))
    q_scale = HEAD_DIM ** -0.5 * LOG2E
    k_scale = HEAD_DIM ** -0.5

    tn = o_ref.shape[2]

    def store(col, x):
        width, done = x.shape[1], 0
        while done < width:
            blk, off = divmod(col + done, tn)
            step = min(width - done, tn - off)
            o_ref[blk, :, off:off + step] = x[:, done:done + step].astype(o_ref.dtype)
            done += step

    def put(dst_block, name, width, mul=None):
        x = w_ref[:, src[name]:src[name] + width]
        store(dst_block * LANE, x if mul is None else x * mul)

    put(CB_CQ, "cq", MLA_Q_RANK + MLA_KV_RANK)
    put(CB_NAQ, "na_q", NA_WIDTH, q_scale)
    put(CB_NAK, "na_k", 2 * NA_WIDTH)
    put(CB_RQ, "r_q", RET_WIDTH)
    put(CB_RK, "r_k", RET_WIDTH, k_scale)
    put(CB_RV, "r_v", 2 * RET_WIDTH)
    rows = w_ref.shape[0]
    tail = jnp.concatenate([w_ref[:, src["kpe"]:src["kpe"] + MLA_ROPE],
                            jnp.zeros((rows, PACKED_IN_WIDTH - CB_KPE * LANE - MLA_ROPE), _F32)], 1)
    store(CB_KPE * LANE, tail)


def _pack_w_in(w_in, layer, tn):
    depth, d, n = w_in.shape
    tk = _tile(d, 256)
    nk = d // tk
    nj = PACKED_IN_WIDTH // tn
    return pl.pallas_call(
        _pack_w_in_body,
        grid=(nk,),
        in_specs=[pl.BlockSpec((tk, n), lambda i: (layer * nk + i, 0))],
        out_specs=pl.BlockSpec((nj, tk, tn), lambda i: (0, i, 0)),
        out_shape=jax.ShapeDtypeStruct((nj, d, tn), _BF),
        compiler_params=_cparams(("arbitrary",)),
    )(w_in.reshape(depth * d, n))


def _pack_w_uq(w):
    r = w.shape[0]
    w = w.reshape(r, MLA_HEADS, MLA_NOPE + MLA_ROPE) * ((MLA_NOPE + MLA_ROPE) ** -0.5 * LOG2E)
    w = jnp.pad(w, ((0, 0), (0, 0), (0, MLA_QK_PAD - MLA_NOPE - MLA_ROPE)))
    return w.reshape(r, MLA_HEADS * MLA_QK_PAD).astype(_BF)


def kernel(x, c, ctx, c_ctx, w_ada, b_ada, w_in, mla_q_norm, mla_kv_norm, w_uq, w_ukv, na_rpb, ret_decay,
           w_o, ln1_g, ln1_b, w_up, conv_w, conv_b, w_down, ln2_g, ln2_b):
    b, t, d = x.shape
    cl = ctx.shape[1]
    depth = w_ada.shape[0]
    dff = conv_w.shape[-1]
    assert b + 1 <= MOD_ROWS
    alpha = (2 * depth) ** 0.25
    ctx_row = b

    cvec = jnp.concatenate([c, c_ctx[None], jnp.zeros((MOD_ROWS - b - 1, d), c.dtype)], 0)
    mod3 = _ada(cvec, w_ada, b_ada)

    tm_l = _tile(t, 1024)
    tm_c = _tile(cl, 1024)
    lat_row = lambda tm: (lambda i: i // (t // tm))
    ctx_rowf = lambda i: ctx_row

    rope_pe = _rope_tables(t, MLA_ROPE)
    rope_ret = _rope_tables(t, HEAD_DIM)
    id_pe = _rope_tables(cl, MLA_ROPE, identity=True)
    id_ret = _rope_tables(cl, HEAD_DIM, identity=True)

    xf = x.reshape(b * t, d)
    cf = ctx.reshape(b * cl, d)
    h1 = _modulate(xf, mod3, 0, lat_row(tm_l), tm_l)
    hc1 = _modulate(cf, mod3, 0, ctx_rowf, tm_c)

    tq = _tile(t, 2048)
    tqc = _tile(cl, 256)
    tn_ff = _tile(dff, 512)
    zero_state = jnp.zeros((b, RET_HEADS, HEAD_DIM, HEAD_DIM), _F32)
    w_o_b = w_o.astype(_BF)
    w_down_b = w_down.astype(_BF)

    for l in range(depth):
        last = l == depth - 1
        w_in_p = _pack_w_in(w_in, l, 1024)
        w_uq_p = _pack_w_uq(w_uq[l])
        w_ukv_p = w_ukv[l].astype(_BF)
        gq = mla_q_norm[l][None]
        gkv = mla_kv_norm[l][None]
        lg = jnp.log1p(-jnp.exp2(ret_decay[l].astype(_F32)))
        bias = _na_col_bias(na_rpb[l])

        p = _mm(h1, w_in_p, tm_l)
        pc = _mm(hc1, w_in_p, tm_c)

        kc_m, vc_m = _mla_kv(pc, gkv, w_ukv_p, id_pe, b, cl, _tile(cl, 512))
        kl_m, vl_m = _mla_kv(p, gkv, w_ukv_p, rope_pe, b, t, _tile(t, 512))
        q_m = _mla_q(p, gq, w_uq_p, rope_pe, b, t, _tile(t, 512))
        y_mla = _dense_attn(
            q_m, pl.BlockSpec((None, None, tq, MLA_QK_PAD), lambda bi, h, i: (bi, h, i, 0)),
            [(kc_m, _head4_spec(cl, MLA_QK_PAD), vc_m, _head4_spec(VT_ROWS, cl)),
             (kl_m, _head4_spec(t, MLA_QK_PAD), vl_m, _head4_spec(VT_ROWS, t))],
            b, MLA_HEADS, t // tq, tq, MLA_V, nsub=tq // _tile(tq, 256))

        y_na = _na_attn(p, pc, bias, b, t, cl)

        yc_ret, s_f, s_b = _retention(pc, lg, zero_state, zero_state, id_ret, b, cl)
        y_ret, _, _ = _retention(p, lg, s_f, s_b, rope_ret, b, t)

        tm_o = _tile(t, 512)
        x_new, h2 = _proj_ln([y_na, y_mla, y_ret], w_o_b, xf, mod3, l, 2, lat_row(tm_o),
                             ln1_g[l][None], ln1_b[l][None], alpha, tm_o, tm_o // _tile(tm_o, 256), with_next=True)

        tm_d = _tile(t, 256)

        if not last:
            qc_m = _mla_q(pc, gq, w_uq_p, id_pe, b, cl, _tile(cl, 512))
            yc_mla = _dense_attn(
                qc_m, pl.BlockSpec((None, None, tqc, MLA_QK_PAD), lambda bi, h, i: (bi, h, i, 0)),
                [(kc_m, _head4_spec(cl, MLA_QK_PAD), vc_m, _head4_spec(VT_ROWS, cl))],
                b, MLA_HEADS, cl // tqc, tqc, MLA_V, nsub=1)
            nbc = cl // tqc
            yc_na = _dense_attn(
                pc, pl.BlockSpec((tqc, LANE), lambda bi, h, i: (bi * nbc + i, CB_NAQ + h)),
                [(pc, pl.BlockSpec((cl, LANE), lambda bi, h, i: (bi, CB_NAK + h)),
                  pc, pl.BlockSpec((cl, LANE), lambda bi, h, i: (bi, CB_NAV + h)))],
                b, NA_HEADS, nbc, tqc, HEAD_DIM, nsub=1, v_row_major=True)
            tm_oc = _tile(cl, 256)
            c_new, hc2 = _proj_ln([yc_na, yc_mla, yc_ret], w_o_b, cf, mod3, l, 2, ctx_rowf,
                                  ln1_g[l][None], ln1_b[l][None], alpha, tm_oc, 1, with_next=True)

        g_l, w_up_b = _glu(h2, w_up, l, conv_w[l], conv_b[l][None], b, t, tn_ff)
        xf, h1 = _proj_ln([g_l], w_down_b, x_new, mod3, l, 5, lat_row(tm_d), ln2_g[l][None], ln2_b[l][None],
                          alpha, tm_d, 1, with_next=not last)
        if not last:
            g_c = _glu(hc2, w_up, l, conv_w[l], conv_b[l][None], b, cl, tn_ff, w_bf16=w_up_b)
            tm_dc = _tile(cl, 256)
            cf, hc1 = _proj_ln([g_c], w_down_b, c_new, mod3, l, 5, ctx_rowf, ln2_g[l][None], ln2_b[l][None],
                               alpha, tm_dc, 1, with_next=True)

    return xf.reshape(b, t, d)
```

```python
import functools

import numpy as np
import jax
import jax.numpy as jnp
from jax import lax
from jax.experimental import pallas as pl
from jax.experimental.pallas import tpu as pltpu

GRID_W = 64
HEAD_DIM = 128
NA_HEADS = 6
MLA_HEADS = 5
RET_HEADS = 5
NA_WIDTH = NA_HEADS * HEAD_DIM
MLA_WIDTH = MLA_HEADS * HEAD_DIM
RET_WIDTH = RET_HEADS * HEAD_DIM
NA_WIN_R = 8
NA_WIN_C = 16
MLA_Q_RANK = 512
MLA_KV_RANK = 512
MLA_NOPE = 128
MLA_ROPE = 64
MLA_V = 128
MLA_QK_PAD = 256
RET_CHUNK = 128
CONV_W = 3
ROPE_BASE = 10000.0
LN_EPS = 1e-5
RMS_EPS = 1e-6
NEG_INF = -1e30
LOG2E = 1.4426950408889634
IN_SPLITS = (NA_WIDTH, NA_WIDTH, NA_WIDTH, MLA_Q_RANK, MLA_KV_RANK, MLA_ROPE,
             RET_WIDTH, RET_WIDTH, RET_WIDTH, RET_WIDTH)

LANE = 128
CB_CQ, CB_CKV, CB_NAQ, CB_NAK, CB_NAV = 0, 4, 8, 14, 20
CB_RQ, CB_RK, CB_RV, CB_RG, CB_KPE = 26, 31, 36, 41, 46
PACKED_IN_WIDTH = 48 * LANE
MOD_ROWS = 8
VMEM_LIMIT_MB = 56

_BF = jnp.bfloat16
_F32 = jnp.float32


def _cparams(sem, vmem_mb=VMEM_LIMIT_MB):
    return pltpu.CompilerParams(dimension_semantics=sem, vmem_limit_bytes=vmem_mb << 20)


def _tile(n, pref):
    t = min(n, pref)
    while n % t:
        t //= 2
    return t


def _dot(a, b):
    return jnp.dot(a, b, preferred_element_type=_F32)


def _dot_nt(a, b):
    return lax.dot_general(a, b, (((1,), (1,)), ((), ())), preferred_element_type=_F32)


def _silu(x):
    return x / (1.0 + jnp.exp(-x))


def _ada_body(c_ref, w_ref, b_ref, o_ref):
    part = _dot(_silu(c_ref[...]).astype(_BF), w_ref[0].astype(_BF))

    @pl.when(pl.program_id(1) == 0)
    def _():
        o_ref[0] = part + b_ref[0]

    @pl.when(pl.program_id(1) > 0)
    def _():
        o_ref[0] += part


def _ada(cvec, w_ada, b_ada):
    depth, d, n = w_ada.shape
    tk = _tile(d, 256)
    out = pl.pallas_call(
        _ada_body,
        grid=(depth, d // tk),
        in_specs=[pl.BlockSpec((MOD_ROWS, tk), lambda l, k: (0, k)),
                  pl.BlockSpec((1, tk, n), lambda l, k: (l, k, 0)),
                  pl.BlockSpec((1, 1, n), lambda l, k: (l, 0, 0))],
        out_specs=pl.BlockSpec((1, MOD_ROWS, n), lambda l, k: (l, 0, 0)),
        out_shape=jax.ShapeDtypeStruct((depth, MOD_ROWS, n), _F32),
        compiler_params=_cparams(("arbitrary", "arbitrary")),
    )(cvec, w_ada, b_ada.reshape(depth, 1, n))
    return out.reshape(depth * MOD_ROWS, 1, n)


def _mod_spec(d, layer, chunk, row_fn):
    return pl.BlockSpec((1, 1, d), lambda i, *_: (layer * MOD_ROWS + row_fn(i), 0, chunk))


def _modulate_body(x_ref, sh_ref, sc_ref, o_ref):
    o_ref[...] = (x_ref[...] * (1.0 + sc_ref[0]) + sh_ref[0]).astype(o_ref.dtype)


def _modulate(x2, mod3, layer, row_fn, tm):
    m, d = x2.shape
    return pl.pallas_call(
        _modulate_body,
        grid=(m // tm,),
        in_specs=[pl.BlockSpec((tm, d), lambda i: (i, 0)),
                  _mod_spec(d, layer, 0, row_fn), _mod_spec(d, layer, 1, row_fn)],
        out_specs=pl.BlockSpec((tm, d), lambda i: (i, 0)),
        out_shape=jax.ShapeDtypeStruct((m, d), _BF),
        compiler_params=_cparams(("arbitrary",)),
    )(x2, mod3, mod3)


def _mm_body(a_ref, w_ref, o_ref):
    o_ref[...] = _dot(a_ref[...], w_ref[...]).astype(o_ref.dtype)


def _mm(a, w, tm):
    m, k = a.shape
    nj, _, tn = w.shape
    return pl.pallas_call(
        _mm_body,
        grid=(m // tm, nj),
        in_specs=[pl.BlockSpec((tm, k), lambda i, j: (i, 0)),
                  pl.BlockSpec((None, k, tn), lambda i, j: (j, 0, 0))],
        out_specs=pl.BlockSpec((tm, tn), lambda i, j: (i, j)),
        out_shape=jax.ShapeDtypeStruct((m, nj * tn), _BF),
        compiler_params=_cparams(("arbitrary", "arbitrary")),
    )(a, w)


def _rope_tables(t, rot_dim, identity=False):
    pos = jnp.arange(t)
    row = (pos // GRID_W).astype(_F32)
    col = (pos % GRID_W).astype(_F32)
    nf = rot_dim // 4
    inv = ROPE_BASE ** (-jnp.arange(nf, dtype=_F32) / nf)
    ar = row[:, None] * inv[None]
    ac = col[:, None] * inv[None]
    ang = jnp.concatenate([ar, ar, ac, ac], -1)
    cos, sin = jnp.cos(ang), jnp.sin(ang)
    if identity:
        cos, sin = jnp.ones_like(cos), jnp.zeros_like(sin)
    first = (np.arange(rot_dim) % (rot_dim // 2)) < nf
    sin_a = jnp.where(first, -sin, 0.0)
    sin_b = jnp.where(first, 0.0, sin)
    pad = LANE - rot_dim
    if pad:
        cos = jnp.pad(cos, ((0, 0), (0, pad)), constant_values=1.0)
        sin_a = jnp.pad(sin_a, ((0, 0), (0, pad)))
        sin_b = jnp.pad(sin_b, ((0, 0), (0, pad)))
    return cos, sin_a, sin_b


def _rope(x, cos, sin_a, sin_b, nf):
    return x * cos + pltpu.roll(x, LANE - nf, 1) * sin_a + pltpu.roll(x, nf, 1) * sin_b


def _rms(x, g):
    return x * lax.rsqrt(jnp.mean(x * x, -1, keepdims=True) + RMS_EPS) * g


def _mla_q_body(p_ref, g_ref, w_ref, cos_ref, sa_ref, sb_ref, o_ref):
    xb = _rms(p_ref[...].astype(_F32), g_ref[...]).astype(_BF)
    cos, sa, sb = cos_ref[...], sa_ref[...], sb_ref[...]
    for h in range(MLA_HEADS):
        acc = _dot(xb, w_ref[:, h * MLA_QK_PAD:(h + 1) * MLA_QK_PAD])
        o_ref[0, h, :, 0:LANE] = acc[:, :LANE].astype(_BF)
        o_ref[0, h, :, LANE:] = _rope(acc[:, LANE:], cos, sa, sb, MLA_ROPE // 4).astype(_BF)


def _mla_q(p, g, w, tables, b, t, tm):
    nb = t // tm
    tab_spec = pl.BlockSpec((tm, LANE), lambda i: (i % nb, 0))
    return pl.pallas_call(
        _mla_q_body,
        grid=(b * nb,),
        in_specs=[pl.BlockSpec((tm, MLA_Q_RANK), lambda i: (i, CB_CQ // 4)),
                  pl.BlockSpec((1, MLA_Q_RANK), lambda i: (0, 0)),
                  pl.BlockSpec(w.shape, lambda i: (0, 0)),
                  tab_spec, tab_spec, tab_spec],
        out_specs=pl.BlockSpec((1, MLA_HEADS, tm, MLA_QK_PAD), lambda i: (i // nb, 0, i % nb, 0)),
        out_shape=jax.ShapeDtypeStruct((b, MLA_HEADS, t, MLA_QK_PAD), _BF),
        compiler_params=_cparams(("arbitrary",)),
    )(p, g, w, *tables)


VT_ROWS = HEAD_DIM + 16


def _values_t(v):
    n = v.shape[0]
    ones = (lax.broadcasted_iota(jnp.int32, (VT_ROWS - HEAD_DIM, n), 0) == 0).astype(_BF)
    return jnp.concatenate([v.astype(_F32).T.astype(_BF), ones], 0)


def _mla_kv_body(ckv_ref, kpe_ref, g_ref, w_ref, cos_ref, sa_ref, sb_ref, k_ref, vt_ref):
    xb = _rms(ckv_ref[...].astype(_F32), g_ref[...]).astype(_BF)
    pe = _rope(kpe_ref[...].astype(_F32), cos_ref[...], sa_ref[...], sb_ref[...], MLA_ROPE // 4).astype(_BF)
    width = MLA_NOPE + MLA_V
    for h in range(MLA_HEADS):
        acc = _dot(xb, w_ref[:, h * width:(h + 1) * width])
        k_ref[0, h, :, 0:LANE] = acc[:, :MLA_NOPE].astype(_BF)
        k_ref[0, h, :, LANE:] = pe
        vt_ref[0, h] = _values_t(acc[:, MLA_NOPE:])


def _mla_kv(p, g, w, tables, b, t, tm):
    nb = t // tm
    tab_spec = pl.BlockSpec((tm, LANE), lambda i: (i % nb, 0))
    return pl.pallas_call(
        _mla_kv_body,
        grid=(b * nb,),
        in_specs=[pl.BlockSpec((tm, MLA_KV_RANK), lambda i: (i, CB_CKV // 4)),
                  pl.BlockSpec((tm, LANE), lambda i: (i, CB_KPE)),
                  pl.BlockSpec((1, MLA_KV_RANK), lambda i: (0, 0)),
                  pl.BlockSpec(w.shape, lambda i: (0, 0)),
                  tab_spec, tab_spec, tab_spec],
        out_specs=[pl.BlockSpec((1, MLA_HEADS, tm, MLA_QK_PAD), lambda i: (i // nb, 0, i % nb, 0)),
                   pl.BlockSpec((1, MLA_HEADS, VT_ROWS, tm), lambda i: (i // nb, 0, 0, i % nb))],
        out_shape=[jax.ShapeDtypeStruct((b, MLA_HEADS, t, MLA_QK_PAD), _BF),
                   jax.ShapeDtypeStruct((b, MLA_HEADS, VT_ROWS, t), _BF)],
        compiler_params=_cparams(("arbitrary",)),
    )(p, p, g, w, *tables)


def _skewed(n, stages):
    carry = [None] * n
    for step in range(n + len(stages) - 1):
        for s, stage in enumerate(stages):
            j = step - s
            if 0 <= j < n:
                carry[j] = stage(j, carry[j])


def _attn_stages(scores_fn, values_fn, store_fn):
    def s_scores(j, _):
        return scores_fn(j)

    def s_max(j, sc):
        return sc, functools.reduce(jnp.maximum, [jnp.max(s, 0, keepdims=True) for s in sc])

    def s_exp(j, c):
        sc, m = c
        return [jnp.exp2(s - m).astype(_BF) for s in sc]

    def s_pv(j, p):
        acc = functools.reduce(jnp.add, [_dot(vt, pi) for vt, pi in zip(values_fn(j), p)])
        store_fn(j, (acc[0:HEAD_DIM, :] / acc[HEAD_DIM:HEAD_DIM + 1, :]).T)

    return [s_scores, s_max, s_exp, s_pv]


def _attn_body(*refs, nseg, nsub, v_row_major):
    q_ref, o_ref = refs[0], refs[-1]
    ks = [refs[1 + 2 * i][...] for i in range(nseg)]
    vts = [refs[2 + 2 * i][...] for i in range(nseg)]
    if v_row_major:
        vts = [_values_t(v) for v in vts]
    ts = q_ref.shape[0] // nsub
    rows = lambda j: slice(j * ts, (j + 1) * ts)

    def store(j, out):
        o_ref[rows(j), :] = out.astype(o_ref.dtype)

    _skewed(nsub, _attn_stages(lambda j: [_dot_nt(k, q_ref[rows(j), :]) for k in ks], lambda j: vts, store))


def _dense_attn(q, q_spec, kv, b, heads, tq_blocks, tq, dv, nsub, v_row_major=False):
    args, specs = [q], [q_spec]
    for k, ks, v, vs in kv:
        args += [k, v]
        specs += [ks, vs]
    return pl.pallas_call(
        functools.partial(_attn_body, nseg=len(kv), nsub=nsub, v_row_major=v_row_major),
        grid=(b, heads, tq_blocks),
        in_specs=specs,
        out_specs=pl.BlockSpec((tq, dv), lambda bi, h, i: (bi * tq_blocks + i, h)),
        out_shape=jax.ShapeDtypeStruct((b * tq_blocks * tq, heads * dv), _BF),
        compiler_params=_cparams(("arbitrary", "arbitrary", "arbitrary")),
    )(*args)


def _head4_spec(rows, cols):
    return pl.BlockSpec((None, None, rows, cols), lambda bi, h, i: (bi, h, 0, 0))


NA_GROUP = 4
NA_KEY_ROWS = 12


def _na_group_window(gi, rows):
    ngroups = rows // NA_GROUP
    if gi == 0:
        return 0, 0
    if gi == ngroups - 1:
        return 2, rows - NA_KEY_ROWS
    return 1, gi * NA_GROUP - NA_WIN_R // 2


NA_DR = 2 * NA_WIN_R - 1


def _na_col_bias(rpb):
    c = np.arange(GRID_W)
    col_start = np.clip(c - NA_WIN_C // 2, 0, GRID_W - NA_WIN_C)
    kc = np.arange(GRID_W)
    in_win = (kc[None, :] >= col_start[:, None]) & (kc[None, :] < col_start[:, None] + NA_WIN_C)
    dc = np.clip(kc[None, :] - c[:, None] + NA_WIN_C - 1, 0, 2 * NA_WIN_C - 2)
    col_sel = (dc[:, :, None] == np.arange(2 * NA_WIN_C - 1)).astype(np.float32)
    toep = jnp.einsum('hrj,cqj->hrqc', rpb.astype(_F32), col_sel, precision=lax.Precision.HIGHEST)
    toep = jnp.where(in_win.T[None, None], toep * LOG2E, NEG_INF)
    masked = jnp.full((rpb.shape[0], 1, GRID_W, GRID_W), NEG_INF, _F32)
    return jnp.concatenate([toep, masked], 1)


def _na_block_index():
    g = np.arange(NA_GROUP)[:, None]
    kr = np.arange(NA_KEY_ROWS)[None, :]
    half = NA_WIN_R // 2
    dr = np.stack([kr - g + NA_WIN_R - 1, kr - g + NA_WIN_R - 1 - half, kr - g - 1])
    w0 = np.stack([0 * g, g, half + 0 * g])
    row_ok = (kr[None] >= w0) & (kr[None] < w0 + NA_WIN_R)
    assert np.all((dr[row_ok] >= 0) & (dr[row_ok] < NA_DR))
    return np.where(row_ok, dr, NA_DR)


def _na_body(q_ref, k_ref, v_ref, kc_ref, vc_ref, cb_ref, o_ref, bias_ref, vt_ref, *, rows):
    @pl.when(pl.program_id(1) == 0)
    def _():
        idx = _na_block_index()
        for typ in range(3):
            for kr in range(NA_KEY_ROWS):
                for gp in range(NA_GROUP // 2):
                    pair = jnp.concatenate([cb_ref[int(idx[typ, 2 * gp, kr])], cb_ref[int(idx[typ, 2 * gp + 1, kr])]], 1)
                    bias_ref[typ, kr * GRID_W:(kr + 1) * GRID_W, gp * 2 * GRID_W:(gp + 1) * 2 * GRID_W] = pair

    vt_ref[...] = _values_t(v_ref[...])
    kc = kc_ref[...]
    vct = _values_t(vc_ref[...])
    gq = NA_GROUP * GRID_W

    def band(gi):
        ws = _na_group_window(gi, rows)[1]
        return slice(ws * GRID_W, (ws + NA_KEY_ROWS) * GRID_W)

    def scores(gi):
        q = q_ref[gi * gq:(gi + 1) * gq, :]
        return [_dot_nt(k_ref[band(gi), :], q) + bias_ref[_na_group_window(gi, rows)[0]], _dot_nt(kc, q)]

    def store(gi, out):
        o_ref[gi * gq:(gi + 1) * gq, :] = out.astype(o_ref.dtype)

    _skewed(rows // NA_GROUP, _attn_stages(scores, lambda gi: [vt_ref[:, band(gi)], vct], store))


def _na_attn(p, pc, bias, b, t, c):
    rows = t // GRID_W
    assert t % GRID_W == 0 and rows % NA_GROUP == 0 and rows >= NA_KEY_ROWS + NA_GROUP and 2 * GRID_W == LANE
    return pl.pallas_call(
        functools.partial(_na_body, rows=rows),
        grid=(NA_HEADS, b),
        in_specs=[pl.BlockSpec((t, LANE), lambda h, bi: (bi, CB_NAQ + h)),
                  pl.BlockSpec((t, LANE), lambda h, bi: (bi, CB_NAK + h)),
                  pl.BlockSpec((t, LANE), lambda h, bi: (bi, CB_NAV + h)),
                  pl.BlockSpec((c, LANE), lambda h, bi: (bi, CB_NAK + h)),
                  pl.BlockSpec((c, LANE), lambda h, bi: (bi, CB_NAV + h)),
                  pl.BlockSpec((None, NA_DR + 1, GRID_W, GRID_W), lambda h, bi: (h, 0, 0, 0))],
        out_specs=pl.BlockSpec((t, LANE), lambda h, bi: (bi, h)),
        out_shape=jax.ShapeDtypeStruct((b * t, NA_WIDTH), _BF),
        scratch_shapes=[pltpu.VMEM((3, NA_KEY_ROWS * GRID_W, NA_GROUP * GRID_W), _F32),
                        pltpu.VMEM((VT_ROWS, t), _BF)],
        compiler_params=_cparams(("arbitrary", "arbitrary")),
    )(p, p, p, pc, pc, bias)


def _ret_body(lg_ref, q_ref, k_ref, v_ref, g_ref, s0f_ref, s0b_ref, cos_ref, sa_ref, sb_ref,
              o_ref, sf_ref, sbo_ref, kr_scr, u_hist, st_hist, *, n):
    h = pl.program_id(0)
    lgf = lg_ref[0, h]
    lgb = lg_ref[1, h]
    L = RET_CHUNK
    pos_c = lax.broadcasted_iota(jnp.int32, (L, 1), 0).astype(_F32)
    pos_r = lax.broadcasted_iota(jnp.int32, (1, L), 1).astype(_F32)
    diff = pos_c - pos_r
    decay = jnp.where(diff > 0, jnp.exp(jnp.maximum(diff, 0.0) * lgf),
                      jnp.where(diff < 0, jnp.exp(jnp.maximum(-diff, 0.0) * lgb), 2.0))
    qdec_f = jnp.exp((pos_c + 1.0) * lgf)
    kdec_f = jnp.exp((L - 1.0 - pos_c) * lgf)
    qdec_b = jnp.exp((L - pos_c) * lgb)
    kdec_b = jnp.exp(pos_c * lgb)
    cd_f = jnp.exp(L * lgf)
    cd_b = jnp.exp(L * lgb)
    nf = HEAD_DIM // 4

    def rows_of(i):
        return slice(i * L, (i + 1) * L)

    def roped(ref, i):
        r = rows_of(i)
        return _rope(ref[r, :].astype(_F32), cos_ref[r, :], sa_ref[r, :], sb_ref[r, :], nf)

    for i in range(n):
        k = roped(k_ref, i)
        kr_scr[rows_of(i), :] = k.astype(_BF)
        kd = jnp.concatenate([k * kdec_f, k * kdec_b], 1).astype(_BF)
        u_hist[i] = lax.dot_general(kd, v_ref[rows_of(i), :], (((0,), (0,)), ((), ())),
                                    preferred_element_type=_F32)

    state = s0f_ref[...]
    for i in range(n):
        st_hist[i, 0:HEAD_DIM, :] = state.astype(_BF)
        state = cd_f * state + u_hist[i, 0:HEAD_DIM, :]
    sf_ref[...] = state
    state = s0b_ref[...]
    for i in reversed(range(n)):
        st_hist[i, HEAD_DIM:, :] = state.astype(_BF)
        state = cd_b * state + u_hist[i, HEAD_DIM:, :]
    sbo_ref[...] = state

    for i in range(n):
        r = rows_of(i)
        q = roped(q_ref, i)
        a = _dot_nt(q.astype(_BF), kr_scr[r, :]) * decay
        qd = jnp.concatenate([q * qdec_f, q * qdec_b], 1).astype(_BF)
        o = _dot(a.astype(_BF), v_ref[r, :]) + _dot(qd, st_hist[i])
        mu = jnp.mean(o, -1, keepdims=True)
        oc = o - mu
        var = jnp.mean(oc * oc, -1, keepdims=True)
        y = oc * lax.rsqrt(var + LN_EPS) * _silu(g_ref[r, :].astype(_F32))
        o_ref[r, :] = y.astype(o_ref.dtype)


def _retention(p, lg, s0f, s0b, tables, b, t):
    n = t // RET_CHUNK
    col = lambda cb: pl.BlockSpec((t, LANE), lambda h, bi: (bi, cb + h))
    st_spec = pl.BlockSpec((None, None, HEAD_DIM, HEAD_DIM), lambda h, bi: (bi, h, 0, 0))
    tab_spec = pl.BlockSpec((t, LANE), lambda h, bi: (0, 0))
    st_shape = jax.ShapeDtypeStruct((b, RET_HEADS, HEAD_DIM, HEAD_DIM), _F32)
    return pl.pallas_call(
        functools.partial(_ret_body, n=n),
        grid=(RET_HEADS, b),
        in_specs=[pl.BlockSpec(memory_space=pltpu.SMEM),
                  col(CB_RQ), col(CB_RK), col(CB_RV), col(CB_RG), st_spec, st_spec,
                  tab_spec, tab_spec, tab_spec],
        out_specs=[pl.BlockSpec((t, LANE), lambda h, bi: (bi, h)), st_spec, st_spec],
        out_shape=[jax.ShapeDtypeStruct((b * t, RET_WIDTH), _BF), st_shape, st_shape],
        scratch_shapes=[pltpu.VMEM((t, HEAD_DIM), _BF),
                        pltpu.VMEM((n, 2 * HEAD_DIM, HEAD_DIM), _F32),
                        pltpu.VMEM((n, 2 * HEAD_DIM, HEAD_DIM), _BF)],
        compiler_params=_cparams(("arbitrary", "arbitrary")),
    )(lg, p, p, p, p, s0f, s0b, *tables)


def _ln_epilogue(y, rows, x_ref, gate_ref, lng_ref, lnb_ref, alpha, xo_ref, ho_ref, sh_ref, sc_ref):
    z = alpha * x_ref[rows, :] + gate_ref[0] * y
    mu = jnp.mean(z, -1, keepdims=True)
    zc = z - mu
    var = jnp.mean(zc * zc, -1, keepdims=True)
    xn = zc * lax.rsqrt(var + LN_EPS) * lng_ref[...] + lnb_ref[...]
    xo_ref[rows, :] = xn
    if ho_ref is not None:
        ho_ref[rows, :] = (xn * (1.0 + sc_ref[0]) + sh_ref[0]).astype(ho_ref.dtype)


def _proj_ln_body(*refs, n_in, alpha, with_next, nsub):
    a_refs, refs = refs[:n_in], refs[n_in:]
    if with_next:
        w_ref, x_ref, gate_ref, sh_ref, sc_ref, lng_ref, lnb_ref, xo_ref, ho_ref = refs
    else:
        w_ref, x_ref, gate_ref, lng_ref, lnb_ref, xo_ref = refs
        sh_ref = sc_ref = ho_ref = None
    ts = x_ref.shape[0] // nsub
    for j in range(nsub):
        r = slice(j * ts, (j + 1) * ts)
        a = a_refs[0][r, :] if n_in == 1 else jnp.concatenate([ar[r, :] for ar in a_refs], 1)
        _ln_epilogue(_dot(a, w_ref[...]), r, x_ref, gate_ref, lng_ref, lnb_ref, alpha, xo_ref, ho_ref, sh_ref, sc_ref)


def _proj_ln(a_list, w, x2, mod3, layer, gate_chunk, row_fn, lng, lnb, alpha, tm, nsub, with_next):
    m, d = x2.shape
    row = lambda width: pl.BlockSpec((tm, width), lambda i: (i, 0))
    vec = pl.BlockSpec((1, d), lambda i: (0, 0))
    w_spec = pl.BlockSpec((None,) + w.shape[1:], lambda i: (layer, 0, 0), pipeline_mode=pl.Buffered(1))
    in_specs = [row(a.shape[1]) for a in a_list] + [w_spec, row(d), _mod_spec(d, layer, gate_chunk, row_fn)]
    args = list(a_list) + [w, x2, mod3]
    if with_next:
        nxt = (layer, gate_chunk + 1) if gate_chunk < 5 else (layer + 1, 0)
        in_specs += [_mod_spec(d, nxt[0], nxt[1], row_fn), _mod_spec(d, nxt[0], nxt[1] + 1, row_fn)]
        args += [mod3, mod3]
    in_specs += [vec, vec]
    args += [lng, lnb]
    out_specs = [row(d)] + ([row(d)] if with_next else [])
    out_shape = [jax.ShapeDtypeStruct((m, d), _F32)] + ([jax.ShapeDtypeStruct((m, d), _BF)] if with_next else [])
    out = pl.pallas_call(
        functools.partial(_proj_ln_body, n_in=len(a_list), alpha=alpha, with_next=with_next, nsub=nsub),
        grid=(m // tm,),
        in_specs=in_specs, out_specs=out_specs, out_shape=out_shape,
        compiler_params=_cparams(("arbitrary",)),
    )(*args)
    return (out[0], out[1]) if with_next else (out[0], None)


def _glu_chunks(t):
    rc = _tile(t, 512)
    return [(i * rc, rc) for i in range(t // rc)]


def _glu_body(h_ref, wa_ref, wu_ref, cw_ref, cb_ref, o_ref, *rest, t, cast):
    if cast:
        wab, wub, a_scr = rest

        @pl.when(pl.program_id(1) == 0)
        def _():
            wab[...] = wa_ref[...].astype(_BF)
            wub[...] = wu_ref[...].astype(_BF)
    else:
        (a_scr,) = rest
        wab, wub = wa_ref, wu_ref

    chunks = _glu_chunks(t)
    nchunk = len(chunks)
    halo = 8
    tn = o_ref.shape[1]
    zeros = jnp.zeros((halo, tn), _F32)
    a_scr[0:halo, :] = zeros
    a_scr[halo + t:, :] = zeros
    w0, w1, w2 = cw_ref[0:1, :], cw_ref[1:2, :], cw_ref[2:3, :]
    bias = cb_ref[...]
    u_prev = None
    for c in range(nchunk + 1):
        u_cur = None
        if c < nchunk:
            r0, rc = chunks[c]
            hc = h_ref[r0:r0 + rc, :]
            a_scr[halo + r0:halo + r0 + rc, :] = _dot(hc, wab[...])
            u_cur = _dot(hc, wub[...])
        if c > 0:
            r0, rc = chunks[c - 1]
            base = halo + r0
            lo = a_scr[base - 1:base - 1 + rc, :]
            mid = a_scr[base:base + rc, :]
            hi = a_scr[base + 1:base + 1 + rc, :]
            acc = bias + lo * w0 + mid * w1 + hi * w2
            o_ref[r0:r0 + rc, :] = (_silu(acc) * u_prev).astype(o_ref.dtype)
        u_prev = u_cur


def _glu(h, w_up, layer, conv_w, conv_b, b, t, tn, w_bf16=None):
    d = h.shape[1]
    dff = conv_w.shape[1]
    nj = dff // tn
    cast = w_bf16 is None
    g_spec = pl.BlockSpec((t, tn), lambda j, bi: (bi, j))
    g_shape = jax.ShapeDtypeStruct((b * t, dff), _BF)
    wb_spec = pl.BlockSpec((None, d, tn), lambda j, bi: (j, 0, 0))
    if cast:
        w_args = (w_up, w_up)
        w_specs = [pl.BlockSpec((None, d, tn), lambda j, bi: (layer, 0, j)),
                   pl.BlockSpec((None, d, tn), lambda j, bi: (layer, 0, nj + j))]
        wb_shape = jax.ShapeDtypeStruct((nj, d, tn), _BF)
        out_specs, out_shape = [g_spec, wb_spec, wb_spec], [g_shape, wb_shape, wb_shape]
    else:
        w_args, w_specs = tuple(w_bf16), [wb_spec, wb_spec]
        out_specs, out_shape = g_spec, g_shape
    out = pl.pallas_call(
        functools.partial(_glu_body, t=t, cast=cast),
        grid=(nj, b),
        in_specs=[pl.BlockSpec((t, d), lambda j, bi: (bi, 0)), *w_specs,
                  pl.BlockSpec((CONV_W, tn), lambda j, bi: (0, j)),
                  pl.BlockSpec((1, tn), lambda j, bi: (0, j))],
        out_specs=out_specs, out_shape=out_shape,
        scratch_shapes=[pltpu.VMEM((t + 16, tn), _F32)],
        compiler_params=_cparams(("arbitrary", "arbitrary")),
    )(h, *w_args, conv_w, conv_b)
    return (out[0], (out[1], out[2])) if cast else out


PACK_SRC_ROWS = 64
PACK_CHUNKS = 4


def _pack_plan():
    src = dict(zip(("na_q", "na_k", "na_v", "cq", "ckv", "kpe", "r_q", "r_k", "r_v", "r_g"),
                   [int(i) for i in np.cumsum((0,) + IN_SPLITS[:-1])]))
    plan = np.zeros((PACKED_IN_WIDTH // LANE, 3), np.int32)
    segments = [(CB_CQ, "cq", MLA_Q_RANK + MLA_KV_RANK, 0), (CB_NAQ, "na_q", NA_WIDTH, 1),
                (CB_NAK, "na_k", 2 * NA_WIDTH, 0), (CB_RQ, "r_q", RET_WIDTH, 0), (CB_RK, "r_k", RET_WIDTH, 2),
                (CB_RV, "r_v", 2 * RET_WIDTH, 0), (CB_KPE, "kpe", MLA_ROPE, 0)]
    for dst, name, width, kind in segments:
        for off in range(0, width, LANE):
            row = src[name] + off
            assert row % PACK_SRC_ROWS == 0
            plan[dst + off // LANE] = (row // PACK_SRC_ROWS, kind, min(LANE, width - off))
    return plan


def _pack_w_in_body(plan_ref, *refs):
    q_scale = HEAD_DIM ** -0.5 * LOG2E
    k_scale = HEAD_DIM ** -0.5
    o_ref = refs[-1]
    row = lax.broadcasted_iota(jnp.int32, (LANE, 1), 0)
    for c in range(PACK_CHUNKS):
        chunk = pl.program_id(0) * PACK_CHUNKS + c
        kind, valid = plan_ref[chunk, 1], plan_ref[chunk, 2]
        scale = jnp.where(kind == 1, q_scale, jnp.where(kind == 2, k_scale, 1.0))
        x = jnp.concatenate([refs[2 * c][...], refs[2 * c + 1][...]], 0)
        x = jnp.where(row < valid, x * scale, 0.0)
        o_ref[:, c * LANE:(c + 1) * LANE] = x.T.astype(o_ref.dtype)


def _pack_w_in(w_in, layer, tn):
    depth, d, n = w_in.shape
    w_t = jnp.swapaxes(w_in, 1, 2)
    step_cols = PACK_CHUNKS * LANE
    per_block = tn // step_cols

    def src_spec(c, second):
        return pl.BlockSpec((None, PACK_SRC_ROWS, d),
                            lambda j, plan: (layer, plan[j * PACK_CHUNKS + c, 0] + second, 0))

    return pl.pallas_call(
        _pack_w_in_body,
        grid_spec=pltpu.PrefetchScalarGridSpec(
            num_scalar_prefetch=1,
            grid=(PACKED_IN_WIDTH // step_cols,),
            in_specs=[src_spec(c, second) for c in range(PACK_CHUNKS) for second in (0, 1)],
            out_specs=pl.BlockSpec((None, d, step_cols), lambda j, plan: (j // per_block, 0, j % per_block))),
        out_shape=jax.ShapeDtypeStruct((PACKED_IN_WIDTH // tn, d, tn), _BF),
        compiler_params=_cparams(("arbitrary",)),
    )(jnp.asarray(_pack_plan()), *([w_t] * (2 * PACK_CHUNKS)))


def _pack_w_uq(w):
    r = w.shape[0]
    w = w.reshape(r, MLA_HEADS, MLA_NOPE + MLA_ROPE) * ((MLA_NOPE + MLA_ROPE) ** -0.5 * LOG2E)
    w = jnp.pad(w, ((0, 0), (0, 0), (0, MLA_QK_PAD - MLA_NOPE - MLA_ROPE)))
    return w.reshape(r, MLA_HEADS * MLA_QK_PAD).astype(_BF)


def kernel(x, c, ctx, c_ctx, w_ada, b_ada, w_in, mla_q_norm, mla_kv_norm, w_uq, w_ukv, na_rpb, ret_decay,
           w_o, ln1_g, ln1_b, w_up, conv_w, conv_b, w_down, ln2_g, ln2_b):
    b, t, d = x.shape
    cl = ctx.shape[1]
    depth = w_ada.shape[0]
    dff = conv_w.shape[-1]
    assert b + 1 <= MOD_ROWS
    alpha = (2 * depth) ** 0.25
    ctx_row = b

    cvec = jnp.concatenate([c, c_ctx[None], jnp.zeros((MOD_ROWS - b - 1, d), c.dtype)], 0)
    mod3 = _ada(cvec, w_ada, b_ada)

    tm_l = _tile(t, 1024)
    tm_c = _tile(cl, 1024)
    lat_row = lambda tm: (lambda i: i // (t // tm))
    ctx_rowf = lambda i: ctx_row

    rope_pe = _rope_tables(t, MLA_ROPE)
    rope_ret = _rope_tables(t, HEAD_DIM)
    id_pe = _rope_tables(cl, MLA_ROPE, identity=True)
    id_ret = _rope_tables(cl, HEAD_DIM, identity=True)

    xf = x.reshape(b * t, d)
    cf = ctx.reshape(b * cl, d)
    h1 = _modulate(xf, mod3, 0, lat_row(tm_l), tm_l)
    hc1 = _modulate(cf, mod3, 0, ctx_rowf, tm_c)

    tq = _tile(t, 2048)
    tqc = _tile(cl, 256)
    tn_ff = _tile(dff, 512)
    zero_state = jnp.zeros((b, RET_HEADS, HEAD_DIM, HEAD_DIM), _F32)
    w_o_b = w_o.astype(_BF)
    w_down_b = w_down.astype(_BF)

    for l in range(depth):
        last = l == depth - 1
        w_in_p = _pack_w_in(w_in, l, 1024)
        w_uq_p = _pack_w_uq(w_uq[l])
        w_ukv_p = w_ukv[l].astype(_BF)
        gq = mla_q_norm[l][None]
        gkv = mla_kv_norm[l][None]
        lg = jnp.log1p(-jnp.exp2(ret_decay[l].astype(_F32)))
        bias = _na_col_bias(na_rpb[l])

        p = _mm(h1, w_in_p, tm_l)
        pc = _mm(hc1, w_in_p, tm_c)

        kc_m, vc_m = _mla_kv(pc, gkv, w_ukv_p, id_pe, b, cl, _tile(cl, 512))
        kl_m, vl_m = _mla_kv(p, gkv, w_ukv_p, rope_pe, b, t, _tile(t, 512))
        q_m = _mla_q(p, gq, w_uq_p, rope_pe, b, t, _tile(t, 512))
        y_mla = _dense_attn(
            q_m, pl.BlockSpec((None, None, tq, MLA_QK_PAD), lambda bi, h, i: (bi, h, i, 0)),
            [(kc_m, _head4_spec(cl, MLA_QK_PAD), vc_m, _head4_spec(VT_ROWS, cl)),
             (kl_m, _head4_spec(t, MLA_QK_PAD), vl_m, _head4_spec(VT_ROWS, t))],
            b, MLA_HEADS, t // tq, tq, MLA_V, nsub=tq // _tile(tq, 256))

        y_na = _na_attn(p, pc, bias, b, t, cl)

        yc_ret, s_f, s_b = _retention(pc, lg, zero_state, zero_state, id_ret, b, cl)
        y_ret, _, _ = _retention(p, lg, s_f, s_b, rope_ret, b, t)

        tm_o = _tile(t, 512)
        x_new, h2 = _proj_ln([y_na, y_mla, y_ret], w_o_b, xf, mod3, l, 2, lat_row(tm_o),
                             ln1_g[l][None], ln1_b[l][None], alpha, tm_o, tm_o // _tile(tm_o, 256), with_next=True)

        tm_d = _tile(t, 256)

        if not last:
            qc_m = _mla_q(pc, gq, w_uq_p, id_pe, b, cl, _tile(cl, 512))
            yc_mla = _dense_attn(
                qc_m, pl.BlockSpec((None, None, tqc, MLA_QK_PAD), lambda bi, h, i: (bi, h, i, 0)),
                [(kc_m, _head4_spec(cl, MLA_QK_PAD), vc_m, _head4_spec(VT_ROWS, cl))],
                b, MLA_HEADS, cl // tqc, tqc, MLA_V, nsub=1)
            nbc = cl // tqc
            yc_na = _dense_attn(
                pc, pl.BlockSpec((tqc, LANE), lambda bi, h, i: (bi * nbc + i, CB_NAQ + h)),
                [(pc, pl.BlockSpec((cl, LANE), lambda bi, h, i: (bi, CB_NAK + h)),
                  pc, pl.BlockSpec((cl, LANE), lambda bi, h, i: (bi, CB_NAV + h)))],
                b, NA_HEADS, nbc, tqc, HEAD_DIM, nsub=1, v_row_major=True)
            tm_oc = _tile(cl, 256)
            c_new, hc2 = _proj_ln([yc_na, yc_mla, yc_ret], w_o_b, cf, mod3, l, 2, ctx_rowf,
                                  ln1_g[l][None], ln1_b[l][None], alpha, tm_oc, 1, with_next=True)

        g_l, w_up_b = _glu(h2, w_up, l, conv_w[l], conv_b[l][None], b, t, tn_ff)
        xf, h1 = _proj_ln([g_l], w_down_b, x_new, mod3, l, 5, lat_row(tm_d), ln2_g[l][None], ln2_b[l][None],
                          alpha, tm_d, 1, with_next=not last)
        if not last:
            g_c = _glu(hc2, w_up, l, conv_w[l], conv_b[l][None], b, cl, tn_ff, w_bf16=w_up_b)
            tm_dc = _tile(cl, 256)
            cf, hc1 = _proj_ln([g_c], w_down_b, c_new, mod3, l, 5, ctx_rowf, ln2_g[l][None], ln2_b[l][None],
                               alpha, tm_dc, 1, with_next=True)

    return xf.reshape(b, t, d)
```

```python
import functools

import numpy as np
import jax
import jax.numpy as jnp
from jax import lax
from jax.experimental import pallas as pl
from jax.experimental.pallas import tpu as pltpu

GRID_W = 64
HEAD_DIM = 128
NA_HEADS = 6
MLA_HEADS = 5
RET_HEADS = 5
NA_WIDTH = NA_HEADS * HEAD_DIM
MLA_WIDTH = MLA_HEADS * HEAD_DIM
RET_WIDTH = RET_HEADS * HEAD_DIM
NA_WIN_R = 8
NA_WIN_C = 16
MLA_Q_RANK = 512
MLA_KV_RANK = 512
MLA_NOPE = 128
MLA_ROPE = 64
MLA_V = 128
MLA_QK_PAD = 256
RET_CHUNK = 128
CONV_W = 3
ROPE_BASE = 10000.0
LN_EPS = 1e-5
RMS_EPS = 1e-6
NEG_INF = -1e30
LOG2E = 1.4426950408889634
IN_SPLITS = (NA_WIDTH, NA_WIDTH, NA_WIDTH, MLA_Q_RANK, MLA_KV_RANK, MLA_ROPE,
             RET_WIDTH, RET_WIDTH, RET_WIDTH, RET_WIDTH)

LANE = 128
CB_CQ, CB_CKV, CB_NAQ, CB_NAK, CB_NAV = 0, 4, 8, 14, 20
CB_RQ, CB_RK, CB_RV, CB_RG, CB_KPE = 26, 31, 36, 41, 46
PACKED_IN_WIDTH = 48 * LANE
MOD_ROWS = 8
VMEM_LIMIT_MB = 56

_BF = jnp.bfloat16
_F32 = jnp.float32


def _cparams(sem, vmem_mb=VMEM_LIMIT_MB):
    return pltpu.CompilerParams(dimension_semantics=sem, vmem_limit_bytes=vmem_mb << 20)


def _tile(n, pref):
    t = min(n, pref)
    while n % t:
        t //= 2
    return t


def _dot(a, b):
    return jnp.dot(a, b, preferred_element_type=_F32)


def _dot_nt(a, b):
    return lax.dot_general(a, b, (((1,), (1,)), ((), ())), preferred_element_type=_F32)


def _silu(x):
    return x / (1.0 + jnp.exp(-x))


def _ada_body(c_ref, w_ref, b_ref, o_ref):
    part = _dot(_silu(c_ref[...]).astype(_BF), w_ref[0].astype(_BF))

    @pl.when(pl.program_id(1) == 0)
    def _():
        o_ref[0] = part + b_ref[0]

    @pl.when(pl.program_id(1) > 0)
    def _():
        o_ref[0] += part


def _ada(cvec, w_ada, b_ada):
    depth, d, n = w_ada.shape
    tk = _tile(d, 256)
    out = pl.pallas_call(
        _ada_body,
        grid=(depth, d // tk),
        in_specs=[pl.BlockSpec((MOD_ROWS, tk), lambda l, k: (0, k)),
                  pl.BlockSpec((1, tk, n), lambda l, k: (l, k, 0)),
                  pl.BlockSpec((1, 1, n), lambda l, k: (l, 0, 0))],
        out_specs=pl.BlockSpec((1, MOD_ROWS, n), lambda l, k: (l, 0, 0)),
        out_shape=jax.ShapeDtypeStruct((depth, MOD_ROWS, n), _F32),
        compiler_params=_cparams(("arbitrary", "arbitrary")),
    )(cvec, w_ada, b_ada.reshape(depth, 1, n))
    return out.reshape(depth * MOD_ROWS, 1, n)


def _mod_spec(d, layer, chunk, row_fn):
    return pl.BlockSpec((1, 1, d), lambda i, *_: (layer * MOD_ROWS + row_fn(i), 0, chunk))


def _in_proj_body(*refs, n_lat, modulated):
    if modulated:
        lat_ref, ctx_ref, shl_ref, scl_ref, shc_ref, scc_ref, w_ref, o_ref, a_scr = refs
    else:
        lat_ref, ctx_ref, w_ref, o_ref, a_scr = refs
    i = pl.program_id(0)

    def stage(src_ref, sh_ref, sc_ref):
        x = src_ref[...]
        if modulated:
            x = x * (1.0 + sc_ref[0]) + sh_ref[0]
        a_scr[...] = x.astype(a_scr.dtype)

    @pl.when((pl.program_id(1) == 0) & (i < n_lat))
    def _():
        stage(lat_ref, shl_ref if modulated else None, scl_ref if modulated else None)

    @pl.when((pl.program_id(1) == 0) & (i >= n_lat))
    def _():
        stage(ctx_ref, shc_ref if modulated else None, scc_ref if modulated else None)

    o_ref[...] = _dot(a_scr[...], w_ref[...]).astype(o_ref.dtype)


def _in_proj(lat, ctx, w, tm, t, mod=None):
    d = lat.shape[1]
    nj, _, tn = w.shape
    n_lat, n_ctx = lat.shape[0] // tm, ctx.shape[0] // tm
    lat_i = lambda i: jnp.minimum(i, n_lat - 1)
    in_specs = [pl.BlockSpec((tm, d), lambda i, j: (lat_i(i), 0)),
                pl.BlockSpec((tm, d), lambda i, j: (jnp.maximum(i - n_lat, 0), 0), pipeline_mode=pl.Buffered(1))]
    args = [lat, ctx]
    if mod is not None:
        mod3, layer, ctx_row = mod
        lat_row = lambda i: lat_i(i) // (t // tm)
        in_specs += [_mod_spec(d, layer, 0, lat_row), _mod_spec(d, layer, 1, lat_row),
                     _mod_spec(d, layer, 0, lambda i: ctx_row), _mod_spec(d, layer, 1, lambda i: ctx_row)]
        args += [mod3] * 4
    return pl.pallas_call(
        functools.partial(_in_proj_body, n_lat=n_lat, modulated=mod is not None),
        grid=(n_lat + n_ctx, nj),
        in_specs=in_specs + [pl.BlockSpec((None, d, tn), lambda i, j: (j, 0, 0))],
        out_specs=pl.BlockSpec((tm, tn), lambda i, j: (i, j)),
        out_shape=jax.ShapeDtypeStruct(((n_lat + n_ctx) * tm, nj * tn), _BF),
        scratch_shapes=[pltpu.VMEM((tm, d), _BF)],
        compiler_params=_cparams(("arbitrary", "arbitrary")),
    )(*args, w)


def _rope_tables(t, rot_dim, identity=False):
    pos = jnp.arange(t)
    row = (pos // GRID_W).astype(_F32)
    col = (pos % GRID_W).astype(_F32)
    nf = rot_dim // 4
    inv = ROPE_BASE ** (-jnp.arange(nf, dtype=_F32) / nf)
    ar = row[:, None] * inv[None]
    ac = col[:, None] * inv[None]
    ang = jnp.concatenate([ar, ar, ac, ac], -1)
    cos, sin = jnp.cos(ang), jnp.sin(ang)
    if identity:
        cos, sin = jnp.ones_like(cos), jnp.zeros_like(sin)
    first = (np.arange(rot_dim) % (rot_dim // 2)) < nf
    sin_a = jnp.where(first, -sin, 0.0)
    sin_b = jnp.where(first, 0.0, sin)
    pad = LANE - rot_dim
    if pad:
        cos = jnp.pad(cos, ((0, 0), (0, pad)), constant_values=1.0)
        sin_a = jnp.pad(sin_a, ((0, 0), (0, pad)))
        sin_b = jnp.pad(sin_b, ((0, 0), (0, pad)))
    return cos, sin_a, sin_b


def _rope(x, cos, sin_a, sin_b, nf):
    return x * cos + pltpu.roll(x, LANE - nf, 1) * sin_a + pltpu.roll(x, nf, 1) * sin_b


def _rms(x, g):
    return x * lax.rsqrt(jnp.mean(x * x, -1, keepdims=True) + RMS_EPS) * g


def _mla_q_body(p_ref, g_ref, w_ref, cos_ref, sa_ref, sb_ref, o_ref):
    xb = _rms(p_ref[...].astype(_F32), g_ref[...]).astype(_BF)
    cos, sa, sb = cos_ref[...], sa_ref[...], sb_ref[...]
    for h in range(MLA_HEADS):
        acc = _dot(xb, w_ref[:, h * MLA_QK_PAD:(h + 1) * MLA_QK_PAD])
        o_ref[0, h, :, 0:LANE] = acc[:, :LANE].astype(_BF)
        o_ref[0, h, :, LANE:] = _rope(acc[:, LANE:], cos, sa, sb, MLA_ROPE // 4).astype(_BF)


def _mla_q(p, g, w, tables, b, t, tm, row0=0):
    nb = t // tm
    blk0 = row0 // tm
    tab_spec = pl.BlockSpec((tm, LANE), lambda i: (i % nb, 0))
    return pl.pallas_call(
        _mla_q_body,
        grid=(b * nb,),
        in_specs=[pl.BlockSpec((tm, MLA_Q_RANK), lambda i: (blk0 + i, CB_CQ // 4)),
                  pl.BlockSpec((1, MLA_Q_RANK), lambda i: (0, 0)),
                  pl.BlockSpec(w.shape, lambda i: (0, 0)),
                  tab_spec, tab_spec, tab_spec],
        out_specs=pl.BlockSpec((1, MLA_HEADS, tm, MLA_QK_PAD), lambda i: (i // nb, 0, i % nb, 0)),
        out_shape=jax.ShapeDtypeStruct((b, MLA_HEADS, t, MLA_QK_PAD), _BF),
        compiler_params=_cparams(("arbitrary",)),
    )(p, g, w, *tables)


VT_ROWS = HEAD_DIM + 16


def _values_t(v):
    n = v.shape[0]
    ones = (lax.broadcasted_iota(jnp.int32, (VT_ROWS - HEAD_DIM, n), 0) == 0).astype(_BF)
    return jnp.concatenate([v.astype(_F32).T.astype(_BF), ones], 0)


def _mla_kv_body(ckv_ref, kpe_ref, g_ref, w_ref, cos_ref, sa_ref, sb_ref, k_ref, vt_ref):
    xb = _rms(ckv_ref[...].astype(_F32), g_ref[...]).astype(_BF)
    pe = _rope(kpe_ref[...].astype(_F32), cos_ref[...], sa_ref[...], sb_ref[...], MLA_ROPE // 4).astype(_BF)
    width = MLA_NOPE + MLA_V
    for h in range(MLA_HEADS):
        acc = _dot(xb, w_ref[:, h * width:(h + 1) * width])
        k_ref[0, h, :, 0:LANE] = acc[:, :MLA_NOPE].astype(_BF)
        k_ref[0, h, :, LANE:] = pe
        vt_ref[0, h] = _values_t(acc[:, MLA_NOPE:])


def _mla_kv(p, g, w, tables, b, t, tm, row0=0):
    nb = t // tm
    blk0 = row0 // tm
    tab_spec = pl.BlockSpec((tm, LANE), lambda i: (i % nb, 0))
    return pl.pallas_call(
        _mla_kv_body,
        grid=(b * nb,),
        in_specs=[pl.BlockSpec((tm, MLA_KV_RANK), lambda i: (blk0 + i, CB_CKV // 4)),
                  pl.BlockSpec((tm, LANE), lambda i: (blk0 + i, CB_KPE)),
                  pl.BlockSpec((1, MLA_KV_RANK), lambda i: (0, 0)),
                  pl.BlockSpec(w.shape, lambda i: (0, 0)),
                  tab_spec, tab_spec, tab_spec],
        out_specs=[pl.BlockSpec((1, MLA_HEADS, tm, MLA_QK_PAD), lambda i: (i // nb, 0, i % nb, 0)),
                   pl.BlockSpec((1, MLA_HEADS, VT_ROWS, tm), lambda i: (i // nb, 0, 0, i % nb))],
        out_shape=[jax.ShapeDtypeStruct((b, MLA_HEADS, t, MLA_QK_PAD), _BF),
                   jax.ShapeDtypeStruct((b, MLA_HEADS, VT_ROWS, t), _BF)],
        compiler_params=_cparams(("arbitrary",)),
    )(p, p, g, w, *tables)


def _skewed(n, stages):
    carry = [None] * n
    for step in range(n + len(stages) - 1):
        for s, stage in enumerate(stages):
            j = step - s
            if 0 <= j < n:
                carry[j] = stage(j, carry[j])


def _attn_stages(scores_fn, values_fn, store_fn):
    def s_scores(j, _):
        return scores_fn(j)

    def s_max(j, sc):
        return sc, functools.reduce(jnp.maximum, [jnp.max(s, 0, keepdims=True) for s in sc])

    def s_exp(j, c):
        sc, m = c
        return [jnp.exp2(s - m).astype(_BF) for s in sc]

    def s_pv(j, p):
        acc = functools.reduce(jnp.add, [_dot(vt, pi) for vt, pi in zip(values_fn(j), p)])
        store_fn(j, (acc[0:HEAD_DIM, :] / acc[HEAD_DIM:HEAD_DIM + 1, :]).T)

    return [s_scores, s_max, s_exp, s_pv]


def _attn_body(*refs, nseg, nsub, v_row_major):
    q_ref, o_ref = refs[0], refs[-1]
    ks = [refs[1 + 2 * i][...] for i in range(nseg)]
    vts = [refs[2 + 2 * i][...] for i in range(nseg)]
    if v_row_major:
        vts = [_values_t(v) for v in vts]
    ts = q_ref.shape[0] // nsub
    rows = lambda j: slice(j * ts, (j + 1) * ts)

    def store(j, out):
        o_ref[rows(j), :] = out.astype(o_ref.dtype)

    _skewed(nsub, _attn_stages(lambda j: [_dot_nt(k, q_ref[rows(j), :]) for k in ks], lambda j: vts, store))


def _dense_attn(q, q_spec, kv, b, heads, tq_blocks, tq, dv, nsub, v_row_major=False):
    args, specs = [q], [q_spec]
    for k, ks, v, vs in kv:
        args += [k, v]
        specs += [ks, vs]
    return pl.pallas_call(
        functools.partial(_attn_body, nseg=len(kv), nsub=nsub, v_row_major=v_row_major),
        grid=(b, heads, tq_blocks),
        in_specs=specs,
        out_specs=pl.BlockSpec((tq, dv), lambda bi, h, i: (bi * tq_blocks + i, h)),
        out_shape=jax.ShapeDtypeStruct((b * tq_blocks * tq, heads * dv), _BF),
        compiler_params=_cparams(("arbitrary", "arbitrary", "arbitrary")),
    )(*args)


def _head4_spec(rows, cols):
    return pl.BlockSpec((None, None, rows, cols), lambda bi, h, i: (bi, h, 0, 0))


NA_GROUP = 4
NA_KEY_ROWS = 12


def _na_group_window(gi, rows):
    ngroups = rows // NA_GROUP
    if gi == 0:
        return 0, 0
    if gi == ngroups - 1:
        return 2, rows - NA_KEY_ROWS
    return 1, gi * NA_GROUP - NA_WIN_R // 2


NA_DR = 2 * NA_WIN_R - 1


def _na_col_bias(rpb):
    c = np.arange(GRID_W)
    col_start = np.clip(c - NA_WIN_C // 2, 0, GRID_W - NA_WIN_C)
    kc = np.arange(GRID_W)
    in_win = (kc[None, :] >= col_start[:, None]) & (kc[None, :] < col_start[:, None] + NA_WIN_C)
    dc = np.clip(kc[None, :] - c[:, None] + NA_WIN_C - 1, 0, 2 * NA_WIN_C - 2)
    col_sel = (dc[:, :, None] == np.arange(2 * NA_WIN_C - 1)).astype(np.float32)
    toep = jnp.einsum('hrj,cqj->hrqc', rpb.astype(_F32), col_sel, precision=lax.Precision.HIGHEST)
    toep = jnp.where(in_win.T[None, None], toep * LOG2E, NEG_INF)
    masked = jnp.full((rpb.shape[0], 1, GRID_W, GRID_W), NEG_INF, _F32)
    return jnp.concatenate([toep, masked], 1)


def _na_block_index():
    g = np.arange(NA_GROUP)[:, None]
    kr = np.arange(NA_KEY_ROWS)[None, :]
    half = NA_WIN_R // 2
    dr = np.stack([kr - g + NA_WIN_R - 1, kr - g + NA_WIN_R - 1 - half, kr - g - 1])
    w0 = np.stack([0 * g, g, half + 0 * g])
    row_ok = (kr[None] >= w0) & (kr[None] < w0 + NA_WIN_R)
    assert np.all((dr[row_ok] >= 0) & (dr[row_ok] < NA_DR))
    return np.where(row_ok, dr, NA_DR)


def _na_body(q_ref, k_ref, v_ref, kc_ref, vc_ref, cb_ref, o_ref, bias_ref, vt_ref, *, rows):
    @pl.when(pl.program_id(1) == 0)
    def _():
        idx = _na_block_index()
        for typ in range(3):
            for kr in range(NA_KEY_ROWS):
                for gp in range(NA_GROUP // 2):
                    pair = jnp.concatenate([cb_ref[int(idx[typ, 2 * gp, kr])], cb_ref[int(idx[typ, 2 * gp + 1, kr])]], 1)
                    bias_ref[typ, kr * GRID_W:(kr + 1) * GRID_W, gp * 2 * GRID_W:(gp + 1) * 2 * GRID_W] = pair

    vt_ref[...] = _values_t(v_ref[...])
    kc = kc_ref[...]
    vct = _values_t(vc_ref[...])
    gq = NA_GROUP * GRID_W

    def band(gi):
        ws = _na_group_window(gi, rows)[1]
        return slice(ws * GRID_W, (ws + NA_KEY_ROWS) * GRID_W)

    def scores(gi):
        q = q_ref[gi * gq:(gi + 1) * gq, :]
        return [_dot_nt(k_ref[band(gi), :], q) + bias_ref[_na_group_window(gi, rows)[0]], _dot_nt(kc, q)]

    def store(gi, out):
        o_ref[gi * gq:(gi + 1) * gq, :] = out.astype(o_ref.dtype)

    _skewed(rows // NA_GROUP, _attn_stages(scores, lambda gi: [vt_ref[:, band(gi)], vct], store))


def _na_attn(p, bias, b, t, c):
    rows = t // GRID_W
    cblk0 = b * t // c
    assert t % GRID_W == 0 and rows % NA_GROUP == 0 and rows >= NA_KEY_ROWS + NA_GROUP and 2 * GRID_W == LANE
    return pl.pallas_call(
        functools.partial(_na_body, rows=rows),
        grid=(NA_HEADS, b),
        in_specs=[pl.BlockSpec((t, LANE), lambda h, bi: (bi, CB_NAQ + h)),
                  pl.BlockSpec((t, LANE), lambda h, bi: (bi, CB_NAK + h)),
                  pl.BlockSpec((t, LANE), lambda h, bi: (bi, CB_NAV + h)),
                  pl.BlockSpec((c, LANE), lambda h, bi: (cblk0 + bi, CB_NAK + h)),
                  pl.BlockSpec((c, LANE), lambda h, bi: (cblk0 + bi, CB_NAV + h)),
                  pl.BlockSpec((None, NA_DR + 1, GRID_W, GRID_W), lambda h, bi: (h, 0, 0, 0))],
        out_specs=pl.BlockSpec((t, LANE), lambda h, bi: (bi, h)),
        out_shape=jax.ShapeDtypeStruct((b * t, NA_WIDTH), _BF),
        scratch_shapes=[pltpu.VMEM((3, NA_KEY_ROWS * GRID_W, NA_GROUP * GRID_W), _F32),
                        pltpu.VMEM((VT_ROWS, t), _BF)],
        compiler_params=_cparams(("arbitrary", "arbitrary")),
    )(p, p, p, p, p, bias)


def _ret_body(lg_ref, q_ref, k_ref, v_ref, g_ref, s0f_ref, s0b_ref, cos_ref, sa_ref, sb_ref,
              o_ref, sf_ref, sbo_ref, kr_scr, u_hist, st_hist, *, n):
    h = pl.program_id(0)
    lgf = lg_ref[0, h]
    lgb = lg_ref[1, h]
    L = RET_CHUNK
    pos_c = lax.broadcasted_iota(jnp.int32, (L, 1), 0).astype(_F32)
    pos_r = lax.broadcasted_iota(jnp.int32, (1, L), 1).astype(_F32)
    diff = pos_c - pos_r
    decay = jnp.where(diff > 0, jnp.exp(jnp.maximum(diff, 0.0) * lgf),
                      jnp.where(diff < 0, jnp.exp(jnp.maximum(-diff, 0.0) * lgb), 2.0))
    qdec_f = jnp.exp((pos_c + 1.0) * lgf)
    kdec_f = jnp.exp((L - 1.0 - pos_c) * lgf)
    qdec_b = jnp.exp((L - pos_c) * lgb)
    kdec_b = jnp.exp(pos_c * lgb)
    cd_f = jnp.exp(L * lgf)
    cd_b = jnp.exp(L * lgb)
    nf = HEAD_DIM // 4

    def rows_of(i):
        return slice(i * L, (i + 1) * L)

    def roped(ref, i):
        r = rows_of(i)
        return _rope(ref[r, :].astype(_F32), cos_ref[r, :], sa_ref[r, :], sb_ref[r, :], nf)

    for i in range(n):
        k = roped(k_ref, i)
        kr_scr[rows_of(i), :] = k.astype(_BF)
        kd = jnp.concatenate([k * kdec_f, k * kdec_b], 1).astype(_BF)
        u_hist[i] = lax.dot_general(kd, v_ref[rows_of(i), :], (((0,), (0,)), ((), ())),
                                    preferred_element_type=_F32)

    state = s0f_ref[...]
    for i in range(n):
        st_hist[i, 0:HEAD_DIM, :] = state.astype(_BF)
        state = cd_f * state + u_hist[i, 0:HEAD_DIM, :]
    sf_ref[...] = state
    state = s0b_ref[...]
    for i in reversed(range(n)):
        st_hist[i, HEAD_DIM:, :] = state.astype(_BF)
        state = cd_b * state + u_hist[i, HEAD_DIM:, :]
    sbo_ref[...] = state

    for i in range(n):
        r = rows_of(i)
        q = roped(q_ref, i)
        a = _dot_nt(q.astype(_BF), kr_scr[r, :]) * decay
        qd = jnp.concatenate([q * qdec_f, q * qdec_b], 1).astype(_BF)
        o = _dot(a.astype(_BF), v_ref[r, :]) + _dot(qd, st_hist[i])
        mu = jnp.mean(o, -1, keepdims=True)
        oc = o - mu
        var = jnp.mean(oc * oc, -1, keepdims=True)
        y = oc * lax.rsqrt(var + LN_EPS) * _silu(g_ref[r, :].astype(_F32))
        o_ref[r, :] = y.astype(o_ref.dtype)


def _retention(p, lg, s0f, s0b, tables, b, t, row0=0):
    n = t // RET_CHUNK
    blk0 = row0 // t
    col = lambda cb: pl.BlockSpec((t, LANE), lambda h, bi: (blk0 + bi, cb + h))
    st_spec = pl.BlockSpec((None, None, HEAD_DIM, HEAD_DIM), lambda h, bi: (bi, h, 0, 0))
    tab_spec = pl.BlockSpec((t, LANE), lambda h, bi: (0, 0))
    st_shape = jax.ShapeDtypeStruct((b, RET_HEADS, HEAD_DIM, HEAD_DIM), _F32)
    return pl.pallas_call(
        functools.partial(_ret_body, n=n),
        grid=(RET_HEADS, b),
        in_specs=[pl.BlockSpec(memory_space=pltpu.SMEM),
                  col(CB_RQ), col(CB_RK), col(CB_RV), col(CB_RG), st_spec, st_spec,
                  tab_spec, tab_spec, tab_spec],
        out_specs=[pl.BlockSpec((t, LANE), lambda h, bi: (bi, h)), st_spec, st_spec],
        out_shape=[jax.ShapeDtypeStruct((b * t, RET_WIDTH), _BF), st_shape, st_shape],
        scratch_shapes=[pltpu.VMEM((t, HEAD_DIM), _BF),
                        pltpu.VMEM((n, 2 * HEAD_DIM, HEAD_DIM), _F32),
                        pltpu.VMEM((n, 2 * HEAD_DIM, HEAD_DIM), _BF)],
        compiler_params=_cparams(("arbitrary", "arbitrary")),
    )(lg, p, p, p, p, s0f, s0b, *tables)


def _ln_epilogue(y, rows, x_ref, gate_ref, lng_ref, lnb_ref, alpha, xo_ref, ho_ref, sh_ref, sc_ref):
    z = alpha * x_ref[rows, :] + gate_ref[0] * y
    mu = jnp.mean(z, -1, keepdims=True)
    zc = z - mu
    var = jnp.mean(zc * zc, -1, keepdims=True)
    xn = zc * lax.rsqrt(var + LN_EPS) * lng_ref[...] + lnb_ref[...]
    xo_ref[rows, :] = xn
    if ho_ref is not None:
        ho_ref[rows, :] = (xn * (1.0 + sc_ref[0]) + sh_ref[0]).astype(ho_ref.dtype)


def _proj_ln_body(*refs, n_in, alpha, with_next, nsub):
    a_refs, refs = refs[:n_in], refs[n_in:]
    if with_next:
        w_ref, x_ref, gate_ref, sh_ref, sc_ref, lng_ref, lnb_ref, xo_ref, ho_ref = refs
    else:
        w_ref, x_ref, gate_ref, lng_ref, lnb_ref, xo_ref = refs
        sh_ref = sc_ref = ho_ref = None
    ts = x_ref.shape[0] // nsub
    for j in range(nsub):
        r = slice(j * ts, (j + 1) * ts)
        a = a_refs[0][r, :] if n_in == 1 else jnp.concatenate([ar[r, :] for ar in a_refs], 1)
        _ln_epilogue(_dot(a, w_ref[...]), r, x_ref, gate_ref, lng_ref, lnb_ref, alpha, xo_ref, ho_ref, sh_ref, sc_ref)


def _proj_ln(a_list, w, x2, mod3, layer, gate_chunk, row_fn, lng, lnb, alpha, tm, nsub, with_next):
    m, d = x2.shape
    row = lambda width: pl.BlockSpec((tm, width), lambda i: (i, 0))
    vec = pl.BlockSpec((1, d), lambda i: (0, 0))
    w_spec = pl.BlockSpec((None,) + w.shape[1:], lambda i: (layer, 0, 0), pipeline_mode=pl.Buffered(1))
    in_specs = [row(a.shape[1]) for a in a_list] + [w_spec, row(d), _mod_spec(d, layer, gate_chunk, row_fn)]
    args = list(a_list) + [w, x2, mod3]
    if with_next:
        nxt = (layer, gate_chunk + 1) if gate_chunk < 5 else (layer + 1, 0)
        in_specs += [_mod_spec(d, nxt[0], nxt[1], row_fn), _mod_spec(d, nxt[0], nxt[1] + 1, row_fn)]
        args += [mod3, mod3]
    in_specs += [vec, vec]
    args += [lng, lnb]
    out_specs = [row(d)] + ([row(d)] if with_next else [])
    out_shape = [jax.ShapeDtypeStruct((m, d), _F32)] + ([jax.ShapeDtypeStruct((m, d), _BF)] if with_next else [])
    out = pl.pallas_call(
        functools.partial(_proj_ln_body, n_in=len(a_list), alpha=alpha, with_next=with_next, nsub=nsub),
        grid=(m // tm,),
        in_specs=in_specs, out_specs=out_specs, out_shape=out_shape,
        compiler_params=_cparams(("arbitrary",)),
    )(*args)
    return (out[0], out[1]) if with_next else (out[0], None)


def _glu_chunks(t):
    rc = _tile(t, 512)
    return [(i * rc, rc) for i in range(t // rc)]


def _glu_body(h_ref, wa_ref, wu_ref, cw_ref, cb_ref, o_ref, *rest, t, cast):
    if cast:
        wab, wub, a_scr = rest

        @pl.when(pl.program_id(1) == 0)
        def _():
            wab[...] = wa_ref[...].astype(_BF)
            wub[...] = wu_ref[...].astype(_BF)
    else:
        (a_scr,) = rest
        wab, wub = wa_ref, wu_ref

    chunks = _glu_chunks(t)
    nchunk = len(chunks)
    halo = 8
    tn = o_ref.shape[1]
    zeros = jnp.zeros((halo, tn), _F32)
    a_scr[0:halo, :] = zeros
    a_scr[halo + t:, :] = zeros
    w0, w1, w2 = cw_ref[0:1, :], cw_ref[1:2, :], cw_ref[2:3, :]
    bias = cb_ref[...]
    u_prev = None
    for c in range(nchunk + 1):
        u_cur = None
        if c < nchunk:
            r0, rc = chunks[c]
            hc = h_ref[r0:r0 + rc, :]
            a_scr[halo + r0:halo + r0 + rc, :] = _dot(hc, wab[...])
            u_cur = _dot(hc, wub[...])
        if c > 0:
            r0, rc = chunks[c - 1]
            base = halo + r0
            lo = a_scr[base - 1:base - 1 + rc, :]
            mid = a_scr[base:base + rc, :]
            hi = a_scr[base + 1:base + 1 + rc, :]
            acc = bias + lo * w0 + mid * w1 + hi * w2
            o_ref[r0:r0 + rc, :] = (_silu(acc) * u_prev).astype(o_ref.dtype)
        u_prev = u_cur


def _glu(h, w_up, layer, conv_w, conv_b, b, t, tn, w_bf16=None):
    d = h.shape[1]
    dff = conv_w.shape[1]
    nj = dff // tn
    cast = w_bf16 is None
    g_spec = pl.BlockSpec((t, tn), lambda j, bi: (bi, j))
    g_shape = jax.ShapeDtypeStruct((b * t, dff), _BF)
    wb_spec = pl.BlockSpec((None, d, tn), lambda j, bi: (j, 0, 0))
    if cast:
        w_args = (w_up, w_up)
        w_specs = [pl.BlockSpec((None, d, tn), lambda j, bi: (layer, 0, j)),
                   pl.BlockSpec((None, d, tn), lambda j, bi: (layer, 0, nj + j))]
        wb_shape = jax.ShapeDtypeStruct((nj, d, tn), _BF)
        out_specs, out_shape = [g_spec, wb_spec, wb_spec], [g_shape, wb_shape, wb_shape]
    else:
        w_args, w_specs = tuple(w_bf16), [wb_spec, wb_spec]
        out_specs, out_shape = g_spec, g_shape
    out = pl.pallas_call(
        functools.partial(_glu_body, t=t, cast=cast),
        grid=(nj, b),
        in_specs=[pl.BlockSpec((t, d), lambda j, bi: (bi, 0)), *w_specs,
                  pl.BlockSpec((CONV_W, tn), lambda j, bi: (0, j)),
                  pl.BlockSpec((1, tn), lambda j, bi: (0, j))],
        out_specs=out_specs, out_shape=out_shape,
        scratch_shapes=[pltpu.VMEM((t + 16, tn), _F32)],
        compiler_params=_cparams(("arbitrary", "arbitrary")),
    )(h, *w_args, conv_w, conv_b)
    return (out[0], (out[1], out[2])) if cast else out


PACK_SRC_ROWS = 64
PACK_CHUNKS = 4


def _pack_plan():
    src = dict(zip(("na_q", "na_k", "na_v", "cq", "ckv", "kpe", "r_q", "r_k", "r_v", "r_g"),
                   [int(i) for i in np.cumsum((0,) + IN_SPLITS[:-1])]))
    plan = np.zeros((PACKED_IN_WIDTH // LANE, 3), np.int32)
    segments = [(CB_CQ, "cq", MLA_Q_RANK + MLA_KV_RANK, 0), (CB_NAQ, "na_q", NA_WIDTH, 1),
                (CB_NAK, "na_k", 2 * NA_WIDTH, 0), (CB_RQ, "r_q", RET_WIDTH, 0), (CB_RK, "r_k", RET_WIDTH, 2),
                (CB_RV, "r_v", 2 * RET_WIDTH, 0), (CB_KPE, "kpe", MLA_ROPE, 0)]
    for dst, name, width, kind in segments:
        for off in range(0, width, LANE):
            row = src[name] + off
            assert row % PACK_SRC_ROWS == 0
            plan[dst + off // LANE] = (row // PACK_SRC_ROWS, kind, min(LANE, width - off))
    return plan


def _pack_w_in_body(plan_ref, *refs):
    q_scale = HEAD_DIM ** -0.5 * LOG2E
    k_scale = HEAD_DIM ** -0.5
    o_ref = refs[-1]
    row = lax.broadcasted_iota(jnp.int32, (LANE, 1), 0)
    for c in range(PACK_CHUNKS):
        chunk = pl.program_id(0) * PACK_CHUNKS + c
        kind, valid = plan_ref[chunk, 1], plan_ref[chunk, 2]
        scale = jnp.where(kind == 1, q_scale, jnp.where(kind == 2, k_scale, 1.0))
        x = jnp.concatenate([refs[2 * c][...], refs[2 * c + 1][...]], 0)
        x = jnp.where(row < valid, x * scale, 0.0)
        o_ref[:, c * LANE:(c + 1) * LANE] = x.T.astype(o_ref.dtype)


def _pack_w_in(w_in, layer, tn):
    depth, d, n = w_in.shape
    w_t = jnp.swapaxes(w_in, 1, 2)
    step_cols = PACK_CHUNKS * LANE
    per_block = tn // step_cols

    def src_spec(c, second):
        return pl.BlockSpec((None, PACK_SRC_ROWS, d),
                            lambda j, plan: (layer, plan[j * PACK_CHUNKS + c, 0] + second, 0))

    return pl.pallas_call(
        _pack_w_in_body,
        grid_spec=pltpu.PrefetchScalarGridSpec(
            num_scalar_prefetch=1,
            grid=(PACKED_IN_WIDTH // step_cols,),
            in_specs=[src_spec(c, second) for c in range(PACK_CHUNKS) for second in (0, 1)],
            out_specs=pl.BlockSpec((None, d, step_cols), lambda j, plan: (j // per_block, 0, j % per_block))),
        out_shape=jax.ShapeDtypeStruct((PACKED_IN_WIDTH // tn, d, tn), _BF),
        compiler_params=_cparams(("arbitrary",)),
    )(jnp.asarray(_pack_plan()), *([w_t] * (2 * PACK_CHUNKS)))


def _pack_w_uq(w):
    r = w.shape[0]
    w = w.reshape(r, MLA_HEADS, MLA_NOPE + MLA_ROPE) * ((MLA_NOPE + MLA_ROPE) ** -0.5 * LOG2E)
    w = jnp.pad(w, ((0, 0), (0, 0), (0, MLA_QK_PAD - MLA_NOPE - MLA_ROPE)))
    return w.reshape(r, MLA_HEADS * MLA_QK_PAD).astype(_BF)


def kernel(x, c, ctx, c_ctx, w_ada, b_ada, w_in, mla_q_norm, mla_kv_norm, w_uq, w_ukv, na_rpb, ret_decay,
           w_o, ln1_g, ln1_b, w_up, conv_w, conv_b, w_down, ln2_g, ln2_b):
    b, t, d = x.shape
    cl = ctx.shape[1]
    depth = w_ada.shape[0]
    dff = conv_w.shape[-1]
    assert b + 1 <= MOD_ROWS
    alpha = (2 * depth) ** 0.25
    ctx_row = b

    cvec = jnp.concatenate([c, c_ctx[None], jnp.zeros((MOD_ROWS - b - 1, d), c.dtype)], 0)
    mod3 = _ada(cvec, w_ada, b_ada)

    tm_in = _tile(b * cl, _tile(t, 1024))
    ctx0 = b * t
    lat_row = lambda tm: (lambda i: i // (t // tm))
    ctx_rowf = lambda i: ctx_row

    rope_pe = _rope_tables(t, MLA_ROPE)
    rope_ret = _rope_tables(t, HEAD_DIM)
    id_pe = _rope_tables(cl, MLA_ROPE, identity=True)
    id_ret = _rope_tables(cl, HEAD_DIM, identity=True)

    xf = x.reshape(b * t, d)
    cf = ctx.reshape(b * cl, d)
    h1, hc1 = xf, cf

    tq = _tile(t, 2048)
    tqc = _tile(cl, 256)
    tn_ff = _tile(dff, 512)
    zero_state = jnp.zeros((b, RET_HEADS, HEAD_DIM, HEAD_DIM), _F32)
    w_o_b = w_o.astype(_BF)
    w_down_b = w_down.astype(_BF)

    for l in range(depth):
        last = l == depth - 1
        w_in_p = _pack_w_in(w_in, l, 1024)
        w_uq_p = _pack_w_uq(w_uq[l])
        w_ukv_p = w_ukv[l].astype(_BF)
        gq = mla_q_norm[l][None]
        gkv = mla_kv_norm[l][None]
        lg = jnp.log1p(-jnp.exp2(ret_decay[l].astype(_F32)))
        bias = _na_col_bias(na_rpb[l])

        p = _in_proj(h1, hc1, w_in_p, tm_in, t, mod=(mod3, 0, ctx_row) if l == 0 else None)

        kc_m, vc_m = _mla_kv(p, gkv, w_ukv_p, id_pe, b, cl, _tile(cl, 512), row0=ctx0)
        kl_m, vl_m = _mla_kv(p, gkv, w_ukv_p, rope_pe, b, t, _tile(t, 512))
        q_m = _mla_q(p, gq, w_uq_p, rope_pe, b, t, _tile(t, 512))
        y_mla = _dense_attn(
            q_m, pl.BlockSpec((None, None, tq, MLA_QK_PAD), lambda bi, h, i: (bi, h, i, 0)),
            [(kc_m, _head4_spec(cl, MLA_QK_PAD), vc_m, _head4_spec(VT_ROWS, cl)),
             (kl_m, _head4_spec(t, MLA_QK_PAD), vl_m, _head4_spec(VT_ROWS, t))],
            b, MLA_HEADS, t // tq, tq, MLA_V, nsub=tq // _tile(tq, 256))

        y_na = _na_attn(p, bias, b, t, cl)

        yc_ret, s_f, s_b = _retention(p, lg, zero_state, zero_state, id_ret, b, cl, row0=ctx0)
        y_ret, _, _ = _retention(p, lg, s_f, s_b, rope_ret, b, t)

        tm_o = _tile(t, 512)
        x_new, h2 = _proj_ln([y_na, y_mla, y_ret], w_o_b, xf, mod3, l, 2, lat_row(tm_o),
                             ln1_g[l][None], ln1_b[l][None], alpha, tm_o, tm_o // _tile(tm_o, 256), with_next=True)

        tm_d = _tile(t, 256)

        if not last:
            qc_m = _mla_q(p, gq, w_uq_p, id_pe, b, cl, _tile(cl, 512), row0=ctx0)
            yc_mla = _dense_attn(
                qc_m, pl.BlockSpec((None, None, tqc, MLA_QK_PAD), lambda bi, h, i: (bi, h, i, 0)),
                [(kc_m, _head4_spec(cl, MLA_QK_PAD), vc_m, _head4_spec(VT_ROWS, cl))],
                b, MLA_HEADS, cl // tqc, tqc, MLA_V, nsub=1)
            nbc = cl // tqc
            cq0, ck0 = ctx0 // tqc, ctx0 // cl
            yc_na = _dense_attn(
                p, pl.BlockSpec((tqc, LANE), lambda bi, h, i: (cq0 + bi * nbc + i, CB_NAQ + h)),
                [(p, pl.BlockSpec((cl, LANE), lambda bi, h, i: (ck0 + bi, CB_NAK + h)),
                  p, pl.BlockSpec((cl, LANE), lambda bi, h, i: (ck0 + bi, CB_NAV + h)))],
                b, NA_HEADS, nbc, tqc, HEAD_DIM, nsub=1, v_row_major=True)
            tm_oc = _tile(cl, 256)
            c_new, hc2 = _proj_ln([yc_na, yc_mla, yc_ret], w_o_b, cf, mod3, l, 2, ctx_rowf,
                                  ln1_g[l][None], ln1_b[l][None], alpha, tm_oc, 1, with_next=True)

        g_l, w_up_b = _glu(h2, w_up, l, conv_w[l], conv_b[l][None], b, t, tn_ff)
        xf, h1 = _proj_ln([g_l], w_down_b, x_new, mod3, l, 5, lat_row(tm_d), ln2_g[l][None], ln2_b[l][None],
                          alpha, tm_d, 1, with_next=not last)
        if not last:
            g_c = _glu(hc2, w_up, l, conv_w[l], conv_b[l][None], b, cl, tn_ff, w_bf16=w_up_b)
            tm_dc = _tile(cl, 256)
            cf, hc1 = _proj_ln([g_c], w_down_b, c_new, mod3, l, 5, ctx_rowf, ln2_g[l][None], ln2_b[l][None],
                               alpha, tm_dc, 1, with_next=True)

    return xf.reshape(b, t, d)
```

```python
import functools

import numpy as np
import jax
import jax.numpy as jnp
from jax import lax
from jax.experimental import pallas as pl
from jax.experimental.pallas import tpu as pltpu

GRID_W = 64
HEAD_DIM = 128
NA_HEADS = 6
MLA_HEADS = 5
RET_HEADS = 5
NA_WIDTH = NA_HEADS * HEAD_DIM
MLA_WIDTH = MLA_HEADS * HEAD_DIM
RET_WIDTH = RET_HEADS * HEAD_DIM
NA_WIN_R = 8
NA_WIN_C = 16
MLA_Q_RANK = 512
MLA_KV_RANK = 512
MLA_NOPE = 128
MLA_ROPE = 64
MLA_V = 128
MLA_QK_PAD = 256
RET_CHUNK = 128
CONV_W = 3
ROPE_BASE = 10000.0
LN_EPS = 1e-5
RMS_EPS = 1e-6
NEG_INF = -1e30
LOG2E = 1.4426950408889634
IN_SPLITS = (NA_WIDTH, NA_WIDTH, NA_WIDTH, MLA_Q_RANK, MLA_KV_RANK, MLA_ROPE,
             RET_WIDTH, RET_WIDTH, RET_WIDTH, RET_WIDTH)

LANE = 128
CB_CQ, CB_CKV, CB_NAQ, CB_NAK, CB_NAV = 0, 4, 8, 14, 20
CB_RQ, CB_RK, CB_RV, CB_RG, CB_KPE = 26, 31, 36, 41, 46
PACKED_IN_WIDTH = 48 * LANE
MOD_ROWS = 8
VMEM_LIMIT_MB = 56

_BF = jnp.bfloat16
_F32 = jnp.float32


def _cparams(sem, vmem_mb=VMEM_LIMIT_MB):
    return pltpu.CompilerParams(dimension_semantics=sem, vmem_limit_bytes=vmem_mb << 20)


def _tile(n, pref):
    t = min(n, pref)
    while n % t:
        t //= 2
    return t


def _dot(a, b):
    return jnp.dot(a, b, preferred_element_type=_F32)


def _dot_nt(a, b):
    return lax.dot_general(a, b, (((1,), (1,)), ((), ())), preferred_element_type=_F32)


def _silu(x):
    return x / (1.0 + jnp.exp(-x))


def _ada_body(c_ref, w_ref, b_ref, o_ref):
    part = _dot(_silu(c_ref[...]).astype(_BF), w_ref[0].astype(_BF))

    @pl.when(pl.program_id(1) == 0)
    def _():
        o_ref[0] = part + b_ref[0]

    @pl.when(pl.program_id(1) > 0)
    def _():
        o_ref[0] += part


def _ada(cvec, w_ada, b_ada):
    depth, d, n = w_ada.shape
    tk = _tile(d, 256)
    out = pl.pallas_call(
        _ada_body,
        grid=(depth, d // tk),
        in_specs=[pl.BlockSpec((MOD_ROWS, tk), lambda l, k: (0, k)),
                  pl.BlockSpec((1, tk, n), lambda l, k: (l, k, 0)),
                  pl.BlockSpec((1, 1, n), lambda l, k: (l, 0, 0))],
        out_specs=pl.BlockSpec((1, MOD_ROWS, n), lambda l, k: (l, 0, 0)),
        out_shape=jax.ShapeDtypeStruct((depth, MOD_ROWS, n), _F32),
        compiler_params=_cparams(("arbitrary", "arbitrary")),
    )(cvec, w_ada, b_ada.reshape(depth, 1, n))
    return out.reshape(depth * MOD_ROWS, 1, n)


def _mod_spec(d, layer, chunk, row_fn):
    return pl.BlockSpec((1, 1, d), lambda i, *_: (layer * MOD_ROWS + row_fn(i), 0, chunk))


def _in_proj_body(*refs, n_lat, modulated):
    if modulated:
        lat_ref, ctx_ref, shl_ref, scl_ref, shc_ref, scc_ref, w_ref, o_ref, a_scr = refs
    else:
        lat_ref, ctx_ref, w_ref, o_ref, a_scr = refs
    i = pl.program_id(0)

    def stage(src_ref, sh_ref, sc_ref):
        x = src_ref[...]
        if modulated:
            x = x * (1.0 + sc_ref[0]) + sh_ref[0]
        a_scr[...] = x.astype(a_scr.dtype)

    @pl.when((pl.program_id(1) == 0) & (i < n_lat))
    def _():
        stage(lat_ref, shl_ref if modulated else None, scl_ref if modulated else None)

    @pl.when((pl.program_id(1) == 0) & (i >= n_lat))
    def _():
        stage(ctx_ref, shc_ref if modulated else None, scc_ref if modulated else None)

    o_ref[...] = _dot(a_scr[...], w_ref[...]).astype(o_ref.dtype)


def _in_proj(lat, ctx, w, tm, t, mod=None):
    d = lat.shape[1]
    nj, _, tn = w.shape
    n_lat, n_ctx = lat.shape[0] // tm, ctx.shape[0] // tm
    lat_i = lambda i: jnp.minimum(i, n_lat - 1)
    in_specs = [pl.BlockSpec((tm, d), lambda i, j: (lat_i(i), 0)),
                pl.BlockSpec((tm, d), lambda i, j: (jnp.maximum(i - n_lat, 0), 0), pipeline_mode=pl.Buffered(1))]
    args = [lat, ctx]
    if mod is not None:
        mod3, layer, ctx_row = mod
        lat_row = lambda i: lat_i(i) // (t // tm)
        in_specs += [_mod_spec(d, layer, 0, lat_row), _mod_spec(d, layer, 1, lat_row),
                     _mod_spec(d, layer, 0, lambda i: ctx_row), _mod_spec(d, layer, 1, lambda i: ctx_row)]
        args += [mod3] * 4
    return pl.pallas_call(
        functools.partial(_in_proj_body, n_lat=n_lat, modulated=mod is not None),
        grid=(n_lat + n_ctx, nj),
        in_specs=in_specs + [pl.BlockSpec((None, d, tn), lambda i, j: (j, 0, 0))],
        out_specs=pl.BlockSpec((tm, tn), lambda i, j: (i, j)),
        out_shape=jax.ShapeDtypeStruct(((n_lat + n_ctx) * tm, nj * tn), _BF),
        scratch_shapes=[pltpu.VMEM((tm, d), _BF)],
        compiler_params=_cparams(("arbitrary", "arbitrary")),
    )(*args, w)


def _rope_tables(t, rot_dim, identity=False):
    pos = np.arange(t)
    row = (pos // GRID_W).astype(np.float32)
    col = (pos % GRID_W).astype(np.float32)
    nf = rot_dim // 4
    inv = (ROPE_BASE ** (-np.arange(nf, dtype=np.float32) / nf)).astype(np.float32)
    ar = row[:, None] * inv[None]
    ac = col[:, None] * inv[None]
    ang = np.concatenate([ar, ar, ac, ac], -1)
    cos, sin = np.cos(ang), np.sin(ang)
    if identity:
        cos, sin = np.ones_like(cos), np.zeros_like(sin)
    first = (np.arange(rot_dim) % (rot_dim // 2)) < nf
    sin_a = np.where(first, -sin, 0.0)
    sin_b = np.where(first, 0.0, sin)
    pad = LANE - rot_dim
    if pad:
        cos = np.pad(cos, ((0, 0), (0, pad)), constant_values=1.0)
        sin_a = np.pad(sin_a, ((0, 0), (0, pad)))
        sin_b = np.pad(sin_b, ((0, 0), (0, pad)))
    return tuple(jnp.asarray(a, _F32) for a in (cos, sin_a, sin_b))


def _rope(x, cos, sin_a, sin_b, nf):
    return x * cos + pltpu.roll(x, LANE - nf, 1) * sin_a + pltpu.roll(x, nf, 1) * sin_b


def _rms(x, g):
    return x * lax.rsqrt(jnp.mean(x * x, -1, keepdims=True) + RMS_EPS) * g


def _mla_q_body(p_ref, g_ref, w_ref, cos_ref, sa_ref, sb_ref, o_ref):
    xb = _rms(p_ref[...].astype(_F32), g_ref[...]).astype(_BF)
    cos, sa, sb = cos_ref[...], sa_ref[...], sb_ref[...]
    for h in range(MLA_HEADS):
        acc = _dot(xb, w_ref[:, h * MLA_QK_PAD:(h + 1) * MLA_QK_PAD])
        o_ref[0, h, :, 0:LANE] = acc[:, :LANE].astype(_BF)
        o_ref[0, h, :, LANE:] = _rope(acc[:, LANE:], cos, sa, sb, MLA_ROPE // 4).astype(_BF)


def _mla_q(p, g, w, tables, b, t, tm, row0=0):
    nb = t // tm
    blk0 = row0 // tm
    tab_spec = pl.BlockSpec((tm, LANE), lambda i: (i % nb, 0))
    return pl.pallas_call(
        _mla_q_body,
        grid=(b * nb,),
        in_specs=[pl.BlockSpec((tm, MLA_Q_RANK), lambda i: (blk0 + i, CB_CQ // 4)),
                  pl.BlockSpec((1, MLA_Q_RANK), lambda i: (0, 0)),
                  pl.BlockSpec(w.shape, lambda i: (0, 0)),
                  tab_spec, tab_spec, tab_spec],
        out_specs=pl.BlockSpec((1, MLA_HEADS, tm, MLA_QK_PAD), lambda i: (i // nb, 0, i % nb, 0)),
        out_shape=jax.ShapeDtypeStruct((b, MLA_HEADS, t, MLA_QK_PAD), _BF),
        compiler_params=_cparams(("arbitrary",)),
    )(p, g, w, *tables)


VT_ROWS = HEAD_DIM + 16


def _values_t(v):
    n = v.shape[0]
    ones = (lax.broadcasted_iota(jnp.int32, (VT_ROWS - HEAD_DIM, n), 0) == 0).astype(_BF)
    return jnp.concatenate([v.astype(_F32).T.astype(_BF), ones], 0)


def _mla_kv_body(ckv_ref, kpe_ref, g_ref, w_ref, cos_ref, sa_ref, sb_ref, k_ref, vt_ref):
    xb = _rms(ckv_ref[...].astype(_F32), g_ref[...]).astype(_BF)
    pe = _rope(kpe_ref[...].astype(_F32), cos_ref[...], sa_ref[...], sb_ref[...], MLA_ROPE // 4).astype(_BF)
    width = MLA_NOPE + MLA_V
    for h in range(MLA_HEADS):
        acc = _dot(xb, w_ref[:, h * width:(h + 1) * width])
        k_ref[0, h, :, 0:LANE] = acc[:, :MLA_NOPE].astype(_BF)
        k_ref[0, h, :, LANE:] = pe
        vt_ref[0, h] = _values_t(acc[:, MLA_NOPE:])


def _mla_kv(p, g, w, tables, b, t, tm, row0=0):
    nb = t // tm
    blk0 = row0 // tm
    tab_spec = pl.BlockSpec((tm, LANE), lambda i: (i % nb, 0))
    return pl.pallas_call(
        _mla_kv_body,
        grid=(b * nb,),
        in_specs=[pl.BlockSpec((tm, MLA_KV_RANK), lambda i: (blk0 + i, CB_CKV // 4)),
                  pl.BlockSpec((tm, LANE), lambda i: (blk0 + i, CB_KPE)),
                  pl.BlockSpec((1, MLA_KV_RANK), lambda i: (0, 0)),
                  pl.BlockSpec(w.shape, lambda i: (0, 0)),
                  tab_spec, tab_spec, tab_spec],
        out_specs=[pl.BlockSpec((1, MLA_HEADS, tm, MLA_QK_PAD), lambda i: (i // nb, 0, i % nb, 0)),
                   pl.BlockSpec((1, MLA_HEADS, VT_ROWS, tm), lambda i: (i // nb, 0, 0, i % nb))],
        out_shape=[jax.ShapeDtypeStruct((b, MLA_HEADS, t, MLA_QK_PAD), _BF),
                   jax.ShapeDtypeStruct((b, MLA_HEADS, VT_ROWS, t), _BF)],
        compiler_params=_cparams(("arbitrary",)),
    )(p, p, g, w, *tables)


def _skewed(n, stages):
    carry = [None] * n
    for step in range(n + len(stages) - 1):
        for s, stage in enumerate(stages):
            j = step - s
            if 0 <= j < n:
                carry[j] = stage(j, carry[j])


def _attn_stages(scores_fn, values_fn, store_fn):
    def s_scores(j, _):
        return scores_fn(j)

    def s_max(j, sc):
        return sc, functools.reduce(jnp.maximum, [jnp.max(s, 0, keepdims=True) for s in sc])

    def s_exp(j, c):
        sc, m = c
        return [jnp.exp2(s - m).astype(_BF) for s in sc]

    def s_pv(j, p):
        acc = functools.reduce(jnp.add, [_dot(vt, pi) for vt, pi in zip(values_fn(j), p)])
        store_fn(j, (acc[0:HEAD_DIM, :] / acc[HEAD_DIM:HEAD_DIM + 1, :]).T)

    return [s_scores, s_max, s_exp, s_pv]


def _attn_body(*refs, nseg, nsub, v_row_major):
    q_ref, o_ref = refs[0], refs[-1]
    ks = [refs[1 + 2 * i][...] for i in range(nseg)]
    vts = [refs[2 + 2 * i][...] for i in range(nseg)]
    if v_row_major:
        vts = [_values_t(v) for v in vts]
    ts = q_ref.shape[0] // nsub
    rows = lambda j: slice(j * ts, (j + 1) * ts)

    def store(j, out):
        o_ref[rows(j), :] = out.astype(o_ref.dtype)

    _skewed(nsub, _attn_stages(lambda j: [_dot_nt(k, q_ref[rows(j), :]) for k in ks], lambda j: vts, store))


def _dense_attn(q, q_spec, kv, b, heads, tq_blocks, tq, dv, nsub, v_row_major=False):
    args, specs = [q], [q_spec]
    for k, ks, v, vs in kv:
        args += [k, v]
        specs += [ks, vs]
    return pl.pallas_call(
        functools.partial(_attn_body, nseg=len(kv), nsub=nsub, v_row_major=v_row_major),
        grid=(b, heads, tq_blocks),
        in_specs=specs,
        out_specs=pl.BlockSpec((tq, dv), lambda bi, h, i: (bi * tq_blocks + i, h)),
        out_shape=jax.ShapeDtypeStruct((b * tq_blocks * tq, heads * dv), _BF),
        compiler_params=_cparams(("arbitrary", "arbitrary", "arbitrary")),
    )(*args)


def _head4_spec(rows, cols):
    return pl.BlockSpec((None, None, rows, cols), lambda bi, h, i: (bi, h, 0, 0))


NA_GROUP = 4
NA_KEY_ROWS = 12


def _na_group_window(gi, rows):
    ngroups = rows // NA_GROUP
    if gi == 0:
        return 0, 0
    if gi == ngroups - 1:
        return 2, rows - NA_KEY_ROWS
    return 1, gi * NA_GROUP - NA_WIN_R // 2


NA_DR = 2 * NA_WIN_R - 1


def _na_col_bias(rpb):
    c = np.arange(GRID_W)
    col_start = np.clip(c - NA_WIN_C // 2, 0, GRID_W - NA_WIN_C)
    kc = np.arange(GRID_W)
    in_win = (kc[None, :] >= col_start[:, None]) & (kc[None, :] < col_start[:, None] + NA_WIN_C)
    dc = np.clip(kc[None, :] - c[:, None] + NA_WIN_C - 1, 0, 2 * NA_WIN_C - 2)
    col_sel = (dc[:, :, None] == np.arange(2 * NA_WIN_C - 1)).astype(np.float32)
    toep = jnp.einsum('hrj,cqj->hrqc', rpb.astype(_F32), col_sel, precision=lax.Precision.HIGHEST)
    toep = jnp.where(in_win.T[None, None], toep * LOG2E, NEG_INF)
    masked = jnp.full((rpb.shape[0], 1, GRID_W, GRID_W), NEG_INF, _F32)
    return jnp.concatenate([toep, masked], 1)


def _na_block_index():
    g = np.arange(NA_GROUP)[:, None]
    kr = np.arange(NA_KEY_ROWS)[None, :]
    half = NA_WIN_R // 2
    dr = np.stack([kr - g + NA_WIN_R - 1, kr - g + NA_WIN_R - 1 - half, kr - g - 1])
    w0 = np.stack([0 * g, g, half + 0 * g])
    row_ok = (kr[None] >= w0) & (kr[None] < w0 + NA_WIN_R)
    assert np.all((dr[row_ok] >= 0) & (dr[row_ok] < NA_DR))
    return np.where(row_ok, dr, NA_DR)


def _na_body(q_ref, k_ref, v_ref, kc_ref, vc_ref, cb_ref, o_ref, bias_ref, vt_ref, *, rows):
    @pl.when(pl.program_id(1) == 0)
    def _():
        idx = _na_block_index()
        for typ in range(3):
            for kr in range(NA_KEY_ROWS):
                for gp in range(NA_GROUP // 2):
                    pair = jnp.concatenate([cb_ref[int(idx[typ, 2 * gp, kr])], cb_ref[int(idx[typ, 2 * gp + 1, kr])]], 1)
                    bias_ref[typ, kr * GRID_W:(kr + 1) * GRID_W, gp * 2 * GRID_W:(gp + 1) * 2 * GRID_W] = pair

    vt_ref[...] = _values_t(v_ref[...])
    kc = kc_ref[...]
    vct = _values_t(vc_ref[...])
    gq = NA_GROUP * GRID_W

    def band(gi):
        ws = _na_group_window(gi, rows)[1]
        return slice(ws * GRID_W, (ws + NA_KEY_ROWS) * GRID_W)

    def scores(gi):
        q = q_ref[gi * gq:(gi + 1) * gq, :]
        return [_dot_nt(k_ref[band(gi), :], q) + bias_ref[_na_group_window(gi, rows)[0]], _dot_nt(kc, q)]

    def store(gi, out):
        o_ref[gi * gq:(gi + 1) * gq, :] = out.astype(o_ref.dtype)

    _skewed(rows // NA_GROUP, _attn_stages(scores, lambda gi: [vt_ref[:, band(gi)], vct], store))


def _na_attn(p, bias, b, t, c):
    rows = t // GRID_W
    cblk0 = b * t // c
    assert t % GRID_W == 0 and rows % NA_GROUP == 0 and rows >= NA_KEY_ROWS + NA_GROUP and 2 * GRID_W == LANE
    return pl.pallas_call(
        functools.partial(_na_body, rows=rows),
        grid=(NA_HEADS, b),
        in_specs=[pl.BlockSpec((t, LANE), lambda h, bi: (bi, CB_NAQ + h)),
                  pl.BlockSpec((t, LANE), lambda h, bi: (bi, CB_NAK + h)),
                  pl.BlockSpec((t, LANE), lambda h, bi: (bi, CB_NAV + h)),
                  pl.BlockSpec((c, LANE), lambda h, bi: (cblk0 + bi, CB_NAK + h)),
                  pl.BlockSpec((c, LANE), lambda h, bi: (cblk0 + bi, CB_NAV + h)),
                  pl.BlockSpec((None, NA_DR + 1, GRID_W, GRID_W), lambda h, bi: (h, 0, 0, 0))],
        out_specs=pl.BlockSpec((t, LANE), lambda h, bi: (bi, h)),
        out_shape=jax.ShapeDtypeStruct((b * t, NA_WIDTH), _BF),
        scratch_shapes=[pltpu.VMEM((3, NA_KEY_ROWS * GRID_W, NA_GROUP * GRID_W), _F32),
                        pltpu.VMEM((VT_ROWS, t), _BF)],
        compiler_params=_cparams(("arbitrary", "arbitrary")),
    )(p, p, p, p, p, bias)


def _ret_body(lg_ref, q_ref, k_ref, v_ref, g_ref, s0f_ref, s0b_ref, cos_ref, sa_ref, sb_ref,
              o_ref, sf_ref, sbo_ref, kr_scr, u_hist, st_hist, *, n):
    h = pl.program_id(0)
    lgf = lg_ref[0, h]
    lgb = lg_ref[1, h]
    L = RET_CHUNK
    pos_c = lax.broadcasted_iota(jnp.int32, (L, 1), 0).astype(_F32)
    pos_r = lax.broadcasted_iota(jnp.int32, (1, L), 1).astype(_F32)
    diff = pos_c - pos_r
    decay = jnp.where(diff > 0, jnp.exp(jnp.maximum(diff, 0.0) * lgf),
                      jnp.where(diff < 0, jnp.exp(jnp.maximum(-diff, 0.0) * lgb), 2.0))
    qdec_f = jnp.exp((pos_c + 1.0) * lgf)
    kdec_f = jnp.exp((L - 1.0 - pos_c) * lgf)
    qdec_b = jnp.exp((L - pos_c) * lgb)
    kdec_b = jnp.exp(pos_c * lgb)
    cd_f = jnp.exp(L * lgf)
    cd_b = jnp.exp(L * lgb)
    nf = HEAD_DIM // 4

    def rows_of(i):
        return slice(i * L, (i + 1) * L)

    def roped(ref, i):
        r = rows_of(i)
        return _rope(ref[r, :].astype(_F32), cos_ref[r, :], sa_ref[r, :], sb_ref[r, :], nf)

    for i in range(n):
        k = roped(k_ref, i)
        kr_scr[rows_of(i), :] = k.astype(_BF)
        kd = jnp.concatenate([k * kdec_f, k * kdec_b], 1).astype(_BF)
        u_hist[i] = lax.dot_general(kd, v_ref[rows_of(i), :], (((0,), (0,)), ((), ())),
                                    preferred_element_type=_F32)

    state = s0f_ref[...]
    for i in range(n):
        st_hist[i, 0:HEAD_DIM, :] = state.astype(_BF)
        state = cd_f * state + u_hist[i, 0:HEAD_DIM, :]
    sf_ref[...] = state
    state = s0b_ref[...]
    for i in reversed(range(n)):
        st_hist[i, HEAD_DIM:, :] = state.astype(_BF)
        state = cd_b * state + u_hist[i, HEAD_DIM:, :]
    sbo_ref[...] = state

    for i in range(n):
        r = rows_of(i)
        q = roped(q_ref, i)
        a = _dot_nt(q.astype(_BF), kr_scr[r, :]) * decay
        qd = jnp.concatenate([q * qdec_f, q * qdec_b], 1).astype(_BF)
        o = _dot(a.astype(_BF), v_ref[r, :]) + _dot(qd, st_hist[i])
        mu = jnp.mean(o, -1, keepdims=True)
        oc = o - mu
        var = jnp.mean(oc * oc, -1, keepdims=True)
        y = oc * lax.rsqrt(var + LN_EPS) * _silu(g_ref[r, :].astype(_F32))
        o_ref[r, :] = y.astype(o_ref.dtype)


def _retention(p, lg, s0f, s0b, tables, b, t, row0=0):
    n = t // RET_CHUNK
    blk0 = row0 // t
    col = lambda cb: pl.BlockSpec((t, LANE), lambda h, bi: (blk0 + bi, cb + h))
    st_spec = pl.BlockSpec((None, None, HEAD_DIM, HEAD_DIM), lambda h, bi: (bi, h, 0, 0))
    tab_spec = pl.BlockSpec((t, LANE), lambda h, bi: (0, 0))
    st_shape = jax.ShapeDtypeStruct((b, RET_HEADS, HEAD_DIM, HEAD_DIM), _F32)
    return pl.pallas_call(
        functools.partial(_ret_body, n=n),
        grid=(RET_HEADS, b),
        in_specs=[pl.BlockSpec(memory_space=pltpu.SMEM),
                  col(CB_RQ), col(CB_RK), col(CB_RV), col(CB_RG), st_spec, st_spec,
                  tab_spec, tab_spec, tab_spec],
        out_specs=[pl.BlockSpec((t, LANE), lambda h, bi: (bi, h)), st_spec, st_spec],
        out_shape=[jax.ShapeDtypeStruct((b * t, RET_WIDTH), _BF), st_shape, st_shape],
        scratch_shapes=[pltpu.VMEM((t, HEAD_DIM), _BF),
                        pltpu.VMEM((n, 2 * HEAD_DIM, HEAD_DIM), _F32),
                        pltpu.VMEM((n, 2 * HEAD_DIM, HEAD_DIM), _BF)],
        compiler_params=_cparams(("arbitrary", "arbitrary")),
    )(lg, p, p, p, p, s0f, s0b, *tables)


def _ln_epilogue(y, rows, x_ref, gate_ref, lng_ref, lnb_ref, alpha, xo_ref, ho_ref, sh_ref, sc_ref):
    z = alpha * x_ref[rows, :] + gate_ref[0] * y
    mu = jnp.mean(z, -1, keepdims=True)
    zc = z - mu
    var = jnp.mean(zc * zc, -1, keepdims=True)
    xn = zc * lax.rsqrt(var + LN_EPS) * lng_ref[...] + lnb_ref[...]
    xo_ref[rows, :] = xn
    if ho_ref is not None:
        ho_ref[rows, :] = (xn * (1.0 + sc_ref[0]) + sh_ref[0]).astype(ho_ref.dtype)


def _proj_ln_body(*refs, n_in, alpha, with_next, nsub):
    a_refs, refs = refs[:n_in], refs[n_in:]
    if with_next:
        w_ref, x_ref, gate_ref, sh_ref, sc_ref, lng_ref, lnb_ref, xo_ref, ho_ref = refs
    else:
        w_ref, x_ref, gate_ref, lng_ref, lnb_ref, xo_ref = refs
        sh_ref = sc_ref = ho_ref = None
    ts = x_ref.shape[0] // nsub
    for j in range(nsub):
        r = slice(j * ts, (j + 1) * ts)
        a = a_refs[0][r, :] if n_in == 1 else jnp.concatenate([ar[r, :] for ar in a_refs], 1)
        _ln_epilogue(_dot(a, w_ref[...]), r, x_ref, gate_ref, lng_ref, lnb_ref, alpha, xo_ref, ho_ref, sh_ref, sc_ref)


def _proj_ln(a_list, w, x2, mod3, layer, gate_chunk, row_fn, lng, lnb, alpha, tm, nsub, with_next):
    m, d = x2.shape
    row = lambda width: pl.BlockSpec((tm, width), lambda i: (i, 0))
    vec = pl.BlockSpec((1, d), lambda i: (0, 0))
    w_spec = pl.BlockSpec((None,) + w.shape[1:], lambda i: (layer, 0, 0), pipeline_mode=pl.Buffered(1))
    in_specs = [row(a.shape[1]) for a in a_list] + [w_spec, row(d), _mod_spec(d, layer, gate_chunk, row_fn)]
    args = list(a_list) + [w, x2, mod3]
    if with_next:
        nxt = (layer, gate_chunk + 1) if gate_chunk < 5 else (layer + 1, 0)
        in_specs += [_mod_spec(d, nxt[0], nxt[1], row_fn), _mod_spec(d, nxt[0], nxt[1] + 1, row_fn)]
        args += [mod3, mod3]
    in_specs += [vec, vec]
    args += [lng, lnb]
    out_specs = [row(d)] + ([row(d)] if with_next else [])
    out_shape = [jax.ShapeDtypeStruct((m, d), _F32)] + ([jax.ShapeDtypeStruct((m, d), _BF)] if with_next else [])
    out = pl.pallas_call(
        functools.partial(_proj_ln_body, n_in=len(a_list), alpha=alpha, with_next=with_next, nsub=nsub),
        grid=(m // tm,),
        in_specs=in_specs, out_specs=out_specs, out_shape=out_shape,
        compiler_params=_cparams(("arbitrary",)),
    )(*args)
    return (out[0], out[1]) if with_next else (out[0], None)


def _glu_chunks(t):
    rc = _tile(t, 512)
    return [(i * rc, rc) for i in range(t // rc)]


def _glu_body(h_ref, wa_ref, wu_ref, cw_ref, cb_ref, o_ref, *rest, t, cast):
    if cast:
        wab, wub, a_scr = rest

        @pl.when(pl.program_id(1) == 0)
        def _():
            wab[...] = wa_ref[...].astype(_BF)
            wub[...] = wu_ref[...].astype(_BF)
    else:
        (a_scr,) = rest
        wab, wub = wa_ref, wu_ref

    nseq = a_scr.shape[0]
    units = [(q, r0, rc) for q in range(nseq) for r0, rc in _glu_chunks(t)]
    halo = 8
    tn = o_ref.shape[1]
    zeros = jnp.zeros((halo, tn), _F32)
    for q in range(nseq):
        a_scr[q, 0:halo, :] = zeros
        a_scr[q, halo + t:, :] = zeros
    w0, w1, w2 = cw_ref[0:1, :], cw_ref[1:2, :], cw_ref[2:3, :]
    bias = cb_ref[...]
    u_prev = None
    for i in range(len(units) + 1):
        u_cur = None
        if i < len(units):
            q, r0, rc = units[i]
            hc = h_ref[q * t + r0:q * t + r0 + rc, :]
            a_scr[q, halo + r0:halo + r0 + rc, :] = _dot(hc, wab[...])
            u_cur = _dot(hc, wub[...])
        if i > 0:
            q, r0, rc = units[i - 1]
            base = halo + r0
            lo = a_scr[q, base - 1:base - 1 + rc, :]
            mid = a_scr[q, base:base + rc, :]
            hi = a_scr[q, base + 1:base + 1 + rc, :]
            acc = bias + lo * w0 + mid * w1 + hi * w2
            o_ref[q * t + r0:q * t + r0 + rc, :] = (_silu(acc) * u_prev).astype(o_ref.dtype)
        u_prev = u_cur


def _glu(h, w_up, layer, conv_w, conv_b, b, t, tn, nseq=1, w_bf16=None):
    d = h.shape[1]
    dff = conv_w.shape[1]
    nj = dff // tn
    cast = w_bf16 is None
    g_spec = pl.BlockSpec((nseq * t, tn), lambda j, bi: (bi, j))
    g_shape = jax.ShapeDtypeStruct((b * t, dff), _BF)
    wb_spec = pl.BlockSpec((None, d, tn), lambda j, bi: (j, 0, 0))
    if cast:
        w_args = (w_up, w_up)
        w_specs = [pl.BlockSpec((None, d, tn), lambda j, bi: (layer, 0, j)),
                   pl.BlockSpec((None, d, tn), lambda j, bi: (layer, 0, nj + j))]
        wb_shape = jax.ShapeDtypeStruct((nj, d, tn), _BF)
        out_specs, out_shape = [g_spec, wb_spec, wb_spec], [g_shape, wb_shape, wb_shape]
    else:
        w_args, w_specs = tuple(w_bf16), [wb_spec, wb_spec]
        out_specs, out_shape = g_spec, g_shape
    out = pl.pallas_call(
        functools.partial(_glu_body, t=t, cast=cast),
        grid=(nj, b // nseq),
        in_specs=[pl.BlockSpec((nseq * t, d), lambda j, bi: (bi, 0)), *w_specs,
                  pl.BlockSpec((CONV_W, tn), lambda j, bi: (0, j)),
                  pl.BlockSpec((1, tn), lambda j, bi: (0, j))],
        out_specs=out_specs, out_shape=out_shape,
        scratch_shapes=[pltpu.VMEM((nseq, t + 16, tn), _F32)],
        compiler_params=_cparams(("arbitrary", "arbitrary")),
    )(h, *w_args, conv_w, conv_b)
    return (out[0], (out[1], out[2])) if cast else out


PACK_SRC_ROWS = 64
PACK_CHUNKS = 4


def _pack_plan():
    src = dict(zip(("na_q", "na_k", "na_v", "cq", "ckv", "kpe", "r_q", "r_k", "r_v", "r_g"),
                   [int(i) for i in np.cumsum((0,) + IN_SPLITS[:-1])]))
    plan = np.zeros((PACKED_IN_WIDTH // LANE, 3), np.int32)
    segments = [(CB_CQ, "cq", MLA_Q_RANK + MLA_KV_RANK, 0), (CB_NAQ, "na_q", NA_WIDTH, 1),
                (CB_NAK, "na_k", 2 * NA_WIDTH, 0), (CB_RQ, "r_q", RET_WIDTH, 0), (CB_RK, "r_k", RET_WIDTH, 2),
                (CB_RV, "r_v", 2 * RET_WIDTH, 0), (CB_KPE, "kpe", MLA_ROPE, 0)]
    for dst, name, width, kind in segments:
        for off in range(0, width, LANE):
            row = src[name] + off
            assert row % PACK_SRC_ROWS == 0
            plan[dst + off // LANE] = (row // PACK_SRC_ROWS, kind, min(LANE, width - off))
    return plan


def _pack_w_in_body(plan_ref, *refs):
    q_scale = HEAD_DIM ** -0.5 * LOG2E
    k_scale = HEAD_DIM ** -0.5
    o_ref = refs[-1]
    row = lax.broadcasted_iota(jnp.int32, (LANE, 1), 0)
    for c in range(PACK_CHUNKS):
        chunk = pl.program_id(0) * PACK_CHUNKS + c
        kind, valid = plan_ref[chunk, 1], plan_ref[chunk, 2]
        scale = jnp.where(kind == 1, q_scale, jnp.where(kind == 2, k_scale, 1.0))
        x = jnp.concatenate([refs[2 * c][...], refs[2 * c + 1][...]], 0)
        x = jnp.where(row < valid, x * scale, 0.0)
        o_ref[:, c * LANE:(c + 1) * LANE] = x.T.astype(o_ref.dtype)


def _pack_w_in(w_in, layer, tn):
    depth, d, n = w_in.shape
    w_t = jnp.swapaxes(w_in, 1, 2)
    step_cols = PACK_CHUNKS * LANE
    per_block = tn // step_cols

    def src_spec(c, second):
        return pl.BlockSpec((None, PACK_SRC_ROWS, d),
                            lambda j, plan: (layer, plan[j * PACK_CHUNKS + c, 0] + second, 0))

    return pl.pallas_call(
        _pack_w_in_body,
        grid_spec=pltpu.PrefetchScalarGridSpec(
            num_scalar_prefetch=1,
            grid=(PACKED_IN_WIDTH // step_cols,),
            in_specs=[src_spec(c, second) for c in range(PACK_CHUNKS) for second in (0, 1)],
            out_specs=pl.BlockSpec((None, d, step_cols), lambda j, plan: (j // per_block, 0, j % per_block))),
        out_shape=jax.ShapeDtypeStruct((PACKED_IN_WIDTH // tn, d, tn), _BF),
        compiler_params=_cparams(("arbitrary",)),
    )(jnp.asarray(_pack_plan()), *([w_t] * (2 * PACK_CHUNKS)))


def _pack_w_uq(w):
    r = w.shape[0]
    w = w.reshape(r, MLA_HEADS, MLA_NOPE + MLA_ROPE) * ((MLA_NOPE + MLA_ROPE) ** -0.5 * LOG2E)
    w = jnp.pad(w, ((0, 0), (0, 0), (0, MLA_QK_PAD - MLA_NOPE - MLA_ROPE)))
    return w.reshape(r, MLA_HEADS * MLA_QK_PAD).astype(_BF)


def kernel(x, c, ctx, c_ctx, w_ada, b_ada, w_in, mla_q_norm, mla_kv_norm, w_uq, w_ukv, na_rpb, ret_decay,
           w_o, ln1_g, ln1_b, w_up, conv_w, conv_b, w_down, ln2_g, ln2_b):
    b, t, d = x.shape
    cl = ctx.shape[1]
    depth = w_ada.shape[0]
    dff = conv_w.shape[-1]
    assert b + 1 <= MOD_ROWS
    alpha = (2 * depth) ** 0.25
    ctx_row = b

    cvec = jnp.concatenate([c, c_ctx[None], jnp.zeros((MOD_ROWS - b - 1, d), c.dtype)], 0)
    mod3 = _ada(cvec, w_ada, b_ada)

    tm_in = _tile(b * cl, _tile(t, 1024))
    ctx0 = b * t
    lat_row = lambda tm: (lambda i: i // (t // tm))
    ctx_rowf = lambda i: ctx_row

    rope_pe = _rope_tables(t, MLA_ROPE)
    rope_ret = _rope_tables(t, HEAD_DIM)
    id_pe = _rope_tables(cl, MLA_ROPE, identity=True)
    id_ret = _rope_tables(cl, HEAD_DIM, identity=True)

    xf = x.reshape(b * t, d)
    cf = ctx.reshape(b * cl, d)
    h1, hc1 = xf, cf

    tq = _tile(t, 2048)
    tqc = _tile(cl, 256)
    tn_ff = _tile(dff, 512)
    zero_state = jnp.zeros((b, RET_HEADS, HEAD_DIM, HEAD_DIM), _F32)
    w_o_b = w_o.astype(_BF)
    w_down_b = w_down.astype(_BF)

    for l in range(depth):
        last = l == depth - 1
        w_in_p = _pack_w_in(w_in, l, 1024)
        w_uq_p = _pack_w_uq(w_uq[l])
        w_ukv_p = w_ukv[l].astype(_BF)
        gq = mla_q_norm[l][None]
        gkv = mla_kv_norm[l][None]
        lg = jnp.log1p(-jnp.exp2(ret_decay[l].astype(_F32)))
        bias = _na_col_bias(na_rpb[l])

        p = _in_proj(h1, hc1, w_in_p, tm_in, t, mod=(mod3, 0, ctx_row) if l == 0 else None)

        kc_m, vc_m = _mla_kv(p, gkv, w_ukv_p, id_pe, b, cl, _tile(cl, 512), row0=ctx0)
        kl_m, vl_m = _mla_kv(p, gkv, w_ukv_p, rope_pe, b, t, _tile(t, 512))
        q_m = _mla_q(p, gq, w_uq_p, rope_pe, b, t, _tile(t, 512))
        y_mla = _dense_attn(
            q_m, pl.BlockSpec((None, None, tq, MLA_QK_PAD), lambda bi, h, i: (bi, h, i, 0)),
            [(kc_m, _head4_spec(cl, MLA_QK_PAD), vc_m, _head4_spec(VT_ROWS, cl)),
             (kl_m, _head4_spec(t, MLA_QK_PAD), vl_m, _head4_spec(VT_ROWS, t))],
            b, MLA_HEADS, t // tq, tq, MLA_V, nsub=tq // _tile(tq, 256))

        y_na = _na_attn(p, bias, b, t, cl)

        yc_ret, s_f, s_b = _retention(p, lg, zero_state, zero_state, id_ret, b, cl, row0=ctx0)
        y_ret, _, _ = _retention(p, lg, s_f, s_b, rope_ret, b, t)

        tm_o = _tile(t, 512)
        x_new, h2 = _proj_ln([y_na, y_mla, y_ret], w_o_b, xf, mod3, l, 2, lat_row(tm_o),
                             ln1_g[l][None], ln1_b[l][None], alpha, tm_o, tm_o // _tile(tm_o, 256), with_next=True)

        tm_d = _tile(t, 256)

        if not last:
            qc_m = _mla_q(p, gq, w_uq_p, id_pe, b, cl, _tile(cl, 512), row0=ctx0)
            yc_mla = _dense_attn(
                qc_m, pl.BlockSpec((None, None, tqc, MLA_QK_PAD), lambda bi, h, i: (bi, h, i, 0)),
                [(kc_m, _head4_spec(cl, MLA_QK_PAD), vc_m, _head4_spec(VT_ROWS, cl))],
                b, MLA_HEADS, cl // tqc, tqc, MLA_V, nsub=1)
            nbc = cl // tqc
            cq0, ck0 = ctx0 // tqc, ctx0 // cl
            yc_na = _dense_attn(
                p, pl.BlockSpec((tqc, LANE), lambda bi, h, i: (cq0 + bi * nbc + i, CB_NAQ + h)),
                [(p, pl.BlockSpec((cl, LANE), lambda bi, h, i: (ck0 + bi, CB_NAK + h)),
                  p, pl.BlockSpec((cl, LANE), lambda bi, h, i: (ck0 + bi, CB_NAV + h)))],
                b, NA_HEADS, nbc, tqc, HEAD_DIM, nsub=1, v_row_major=True)
            tm_oc = _tile(cl, 256)
            c_new, hc2 = _proj_ln([yc_na, yc_mla, yc_ret], w_o_b, cf, mod3, l, 2, ctx_rowf,
                                  ln1_g[l][None], ln1_b[l][None], alpha, tm_oc, 1, with_next=True)

        g_l, w_up_b = _glu(h2, w_up, l, conv_w[l], conv_b[l][None], b, t, tn_ff)
        xf, h1 = _proj_ln([g_l], w_down_b, x_new, mod3, l, 5, lat_row(tm_d), ln2_g[l][None], ln2_b[l][None],
                          alpha, tm_d, 1, with_next=not last)
        if not last:
            g_c = _glu(hc2, w_up, l, conv_w[l], conv_b[l][None], b, cl, tn_ff, nseq=b, w_bf16=w_up_b)
            tm_dc = _tile(cl, 256)
            cf, hc1 = _proj_ln([g_c], w_down_b, c_new, mod3, l, 5, ctx_rowf, ln2_g[l][None], ln2_b[l][None],
                               alpha, tm_dc, 1, with_next=True)

    return xf.reshape(b, t, d)
```

```python
import functools

import numpy as np
import jax
import jax.numpy as jnp
from jax import lax
from jax.experimental import pallas as pl
from jax.experimental.pallas import tpu as pltpu

GRID_W = 64
HEAD_DIM = 128
NA_HEADS = 6
MLA_HEADS = 5
RET_HEADS = 5
NA_WIDTH = NA_HEADS * HEAD_DIM
MLA_WIDTH = MLA_HEADS * HEAD_DIM
RET_WIDTH = RET_HEADS * HEAD_DIM
NA_WIN_R = 8
NA_WIN_C = 16
MLA_Q_RANK = 512
MLA_KV_RANK = 512
MLA_NOPE = 128
MLA_ROPE = 64
MLA_V = 128
MLA_QK_PAD = 256
RET_CHUNK = 128
CONV_W = 3
ROPE_BASE = 10000.0
LN_EPS = 1e-5
RMS_EPS = 1e-6
NEG_INF = -1e30
LOG2E = 1.4426950408889634
IN_SPLITS = (NA_WIDTH, NA_WIDTH, NA_WIDTH, MLA_Q_RANK, MLA_KV_RANK, MLA_ROPE,
             RET_WIDTH, RET_WIDTH, RET_WIDTH, RET_WIDTH)

LANE = 128
CB_CQ, CB_CKV, CB_NAQ, CB_NAK, CB_NAV = 0, 4, 8, 14, 20
CB_RQ, CB_RK, CB_RV, CB_RG, CB_KPE = 26, 31, 36, 41, 46
PACKED_IN_WIDTH = 48 * LANE
MOD_ROWS = 8
VMEM_LIMIT_MB = 56

_BF = jnp.bfloat16
_F32 = jnp.float32


def _cparams(sem, vmem_mb=VMEM_LIMIT_MB):
    return pltpu.CompilerParams(dimension_semantics=sem, vmem_limit_bytes=vmem_mb << 20)


def _tile(n, pref):
    t = min(n, pref)
    while n % t:
        t //= 2
    return t


def _dot(a, b):
    return jnp.dot(a, b, preferred_element_type=_F32)


def _dot_nt(a, b):
    return lax.dot_general(a, b, (((1,), (1,)), ((), ())), preferred_element_type=_F32)


def _silu(x):
    return x / (1.0 + jnp.exp(-x))


def _ada_body(c_ref, w_ref, b_ref, o_ref):
    part = _dot(_silu(c_ref[...]).astype(_BF), w_ref[0].astype(_BF))

    @pl.when(pl.program_id(1) == 0)
    def _():
        o_ref[0] = part + b_ref[0]

    @pl.when(pl.program_id(1) > 0)
    def _():
        o_ref[0] += part


def _ada(cvec, w_ada, b_ada):
    depth, d, n = w_ada.shape
    tk = _tile(d, 256)
    out = pl.pallas_call(
        _ada_body,
        grid=(depth, d // tk),
        in_specs=[pl.BlockSpec((MOD_ROWS, tk), lambda l, k: (0, k)),
                  pl.BlockSpec((1, tk, n), lambda l, k: (l, k, 0)),
                  pl.BlockSpec((1, 1, n), lambda l, k: (l, 0, 0))],
        out_specs=pl.BlockSpec((1, MOD_ROWS, n), lambda l, k: (l, 0, 0)),
        out_shape=jax.ShapeDtypeStruct((depth, MOD_ROWS, n), _F32),
        compiler_params=_cparams(("arbitrary", "arbitrary")),
    )(cvec, w_ada, b_ada.reshape(depth, 1, n))
    return out.reshape(depth * MOD_ROWS, 1, n)


def _mod_spec(d, layer, chunk, row_fn):
    return pl.BlockSpec((1, 1, d), lambda i, *_: (layer * MOD_ROWS + row_fn(i), 0, chunk))


def _in_proj_body(*refs, n_lat, modulated):
    if modulated:
        lat_ref, ctx_ref, shl_ref, scl_ref, shc_ref, scc_ref, w_ref, o_ref, a_scr = refs
    else:
        lat_ref, ctx_ref, w_ref, o_ref, a_scr = refs
    i = pl.program_id(0)

    def stage(src_ref, sh_ref, sc_ref):
        x = src_ref[...]
        if modulated:
            x = x * (1.0 + sc_ref[0]) + sh_ref[0]
        a_scr[...] = x.astype(a_scr.dtype)

    @pl.when((pl.program_id(1) == 0) & (i < n_lat))
    def _():
        stage(lat_ref, shl_ref if modulated else None, scl_ref if modulated else None)

    @pl.when((pl.program_id(1) == 0) & (i >= n_lat))
    def _():
        stage(ctx_ref, shc_ref if modulated else None, scc_ref if modulated else None)

    o_ref[...] = _dot(a_scr[...], w_ref[...]).astype(o_ref.dtype)


def _in_proj(lat, ctx, w, tm, t, mod=None):
    d = lat.shape[1]
    nj, _, tn = w.shape
    n_lat, n_ctx = lat.shape[0] // tm, ctx.shape[0] // tm
    lat_i = lambda i: jnp.minimum(i, n_lat - 1)
    in_specs = [pl.BlockSpec((tm, d), lambda i, j: (lat_i(i), 0)),
                pl.BlockSpec((tm, d), lambda i, j: (jnp.maximum(i - n_lat, 0), 0), pipeline_mode=pl.Buffered(1))]
    args = [lat, ctx]
    if mod is not None:
        mod3, layer, ctx_row = mod
        lat_row = lambda i: lat_i(i) // (t // tm)
        in_specs += [_mod_spec(d, layer, 0, lat_row), _mod_spec(d, layer, 1, lat_row),
                     _mod_spec(d, layer, 0, lambda i: ctx_row), _mod_spec(d, layer, 1, lambda i: ctx_row)]
        args += [mod3] * 4
    return pl.pallas_call(
        functools.partial(_in_proj_body, n_lat=n_lat, modulated=mod is not None),
        grid=(n_lat + n_ctx, nj),
        in_specs=in_specs + [pl.BlockSpec((None, d, tn), lambda i, j: (j, 0, 0))],
        out_specs=pl.BlockSpec((tm, tn), lambda i, j: (i, j)),
        out_shape=jax.ShapeDtypeStruct(((n_lat + n_ctx) * tm, nj * tn), _BF),
        scratch_shapes=[pltpu.VMEM((tm, d), _BF)],
        compiler_params=_cparams(("arbitrary", "arbitrary")),
    )(*args, w)


def _rope_tables(t, rot_dim, identity=False):
    pos = np.arange(t)
    row = (pos // GRID_W).astype(np.float32)
    col = (pos % GRID_W).astype(np.float32)
    nf = rot_dim // 4
    inv = (ROPE_BASE ** (-np.arange(nf, dtype=np.float32) / nf)).astype(np.float32)
    ar = row[:, None] * inv[None]
    ac = col[:, None] * inv[None]
    ang = np.concatenate([ar, ar, ac, ac], -1)
    cos, sin = np.cos(ang), np.sin(ang)
    if identity:
        cos, sin = np.ones_like(cos), np.zeros_like(sin)
    first = (np.arange(rot_dim) % (rot_dim // 2)) < nf
    sin_a = np.where(first, -sin, 0.0)
    sin_b = np.where(first, 0.0, sin)
    pad = LANE - rot_dim
    if pad:
        cos = np.pad(cos, ((0, 0), (0, pad)), constant_values=1.0)
        sin_a = np.pad(sin_a, ((0, 0), (0, pad)))
        sin_b = np.pad(sin_b, ((0, 0), (0, pad)))
    return tuple(jnp.asarray(a, _F32) for a in (cos, sin_a, sin_b))


def _rope(x, cos, sin_a, sin_b, nf):
    return x * cos + pltpu.roll(x, LANE - nf, 1) * sin_a + pltpu.roll(x, nf, 1) * sin_b


def _rms(x, g):
    return x * lax.rsqrt(jnp.mean(x * x, -1, keepdims=True) + RMS_EPS) * g


VT_ROWS = HEAD_DIM + 16


def _values_t(v):
    n = v.shape[0]
    ones = (lax.broadcasted_iota(jnp.int32, (VT_ROWS - HEAD_DIM, n), 0) == 0).astype(_BF)
    return jnp.concatenate([v.astype(_F32).T.astype(_BF), ones], 0)


def _mla_qkv_body(cq_ref, ckv_ref, kpe_ref, gq_ref, gkv_ref, wq_ref, wkv_ref, cos_ref, sa_ref, sb_ref,
                  q_ref, k_ref, vt_ref):
    cos, sa, sb = cos_ref[...], sa_ref[...], sb_ref[...]
    nf = MLA_ROPE // 4
    xq = _rms(cq_ref[...].astype(_F32), gq_ref[...]).astype(_BF)
    xkv = _rms(ckv_ref[...].astype(_F32), gkv_ref[...]).astype(_BF)
    pe = _rope(kpe_ref[...].astype(_F32), cos, sa, sb, nf).astype(_BF)
    width = MLA_NOPE + MLA_V
    for h in range(MLA_HEADS):
        q = _dot(xq, wq_ref[:, h * MLA_QK_PAD:(h + 1) * MLA_QK_PAD])
        q_ref[0, h, :, 0:LANE] = q[:, :LANE].astype(_BF)
        q_ref[0, h, :, LANE:] = _rope(q[:, LANE:], cos, sa, sb, nf).astype(_BF)
        kv = _dot(xkv, wkv_ref[:, h * width:(h + 1) * width])
        k_ref[0, h, :, 0:LANE] = kv[:, :MLA_NOPE].astype(_BF)
        k_ref[0, h, :, LANE:] = pe
        vt_ref[0, h] = _values_t(kv[:, MLA_NOPE:])


def _mla_qkv(p, gq, gkv, wq, wkv, tables, b, t, tm, row0=0):
    nb = t // tm
    blk0 = row0 // tm
    tab_spec = pl.BlockSpec((tm, LANE), lambda i: (i % nb, 0))
    head_spec = lambda rows, cols, ri, ci: pl.BlockSpec(
        (1, MLA_HEADS, rows, cols), lambda i: (i // nb, 0, (i % nb) * ri, (i % nb) * ci))
    return pl.pallas_call(
        _mla_qkv_body,
        grid=(b * nb,),
        in_specs=[pl.BlockSpec((tm, MLA_Q_RANK), lambda i: (blk0 + i, CB_CQ // 4)),
                  pl.BlockSpec((tm, MLA_KV_RANK), lambda i: (blk0 + i, CB_CKV // 4)),
                  pl.BlockSpec((tm, LANE), lambda i: (blk0 + i, CB_KPE)),
                  pl.BlockSpec((1, MLA_Q_RANK), lambda i: (0, 0)),
                  pl.BlockSpec((1, MLA_KV_RANK), lambda i: (0, 0)),
                  pl.BlockSpec(wq.shape, lambda i: (0, 0)),
                  pl.BlockSpec(wkv.shape, lambda i: (0, 0)),
                  tab_spec, tab_spec, tab_spec],
        out_specs=[head_spec(tm, MLA_QK_PAD, 1, 0), head_spec(tm, MLA_QK_PAD, 1, 0), head_spec(VT_ROWS, tm, 0, 1)],
        out_shape=[jax.ShapeDtypeStruct((b, MLA_HEADS, t, MLA_QK_PAD), _BF),
                   jax.ShapeDtypeStruct((b, MLA_HEADS, t, MLA_QK_PAD), _BF),
                   jax.ShapeDtypeStruct((b, MLA_HEADS, VT_ROWS, t), _BF)],
        compiler_params=_cparams(("arbitrary",)),
    )(p, p, p, gq, gkv, wq, wkv, *tables)


def _skewed(n, stages):
    carry = [None] * n
    for step in range(n + len(stages) - 1):
        for s, stage in enumerate(stages):
            j = step - s
            if 0 <= j < n:
                carry[j] = stage(j, carry[j])


def _attn_stages(scores_fn, values_fn, store_fn):
    def s_scores(j, _):
        return scores_fn(j)

    def s_max(j, sc):
        return sc, functools.reduce(jnp.maximum, [jnp.max(s, 0, keepdims=True) for s in sc])

    def s_exp(j, c):
        sc, m = c
        return [jnp.exp2(s - m).astype(_BF) for s in sc]

    def s_pv(j, p):
        acc = functools.reduce(jnp.add, [_dot(vt, pi) for vt, pi in zip(values_fn(j), p)])
        store_fn(j, (acc[0:HEAD_DIM, :] / acc[HEAD_DIM:HEAD_DIM + 1, :]).T)

    return [s_scores, s_max, s_exp, s_pv]


def _attn_body(*refs, nseg, nsub, v_row_major):
    q_ref, o_ref = refs[0], refs[-1]
    ks = [refs[1 + 2 * i][...] for i in range(nseg)]
    vts = [refs[2 + 2 * i][...] for i in range(nseg)]
    if v_row_major:
        vts = [_values_t(v) for v in vts]
    ts = q_ref.shape[0] // nsub
    rows = lambda j: slice(j * ts, (j + 1) * ts)

    def store(j, out):
        o_ref[rows(j), :] = out.astype(o_ref.dtype)

    _skewed(nsub, _attn_stages(lambda j: [_dot_nt(k, q_ref[rows(j), :]) for k in ks], lambda j: vts, store))


def _dense_attn(q, q_spec, kv, b, heads, tq_blocks, tq, dv, nsub, v_row_major=False):
    args, specs = [q], [q_spec]
    for k, ks, v, vs in kv:
        args += [k, v]
        specs += [ks, vs]
    return pl.pallas_call(
        functools.partial(_attn_body, nseg=len(kv), nsub=nsub, v_row_major=v_row_major),
        grid=(b, heads, tq_blocks),
        in_specs=specs,
        out_specs=pl.BlockSpec((tq, dv), lambda bi, h, i: (bi * tq_blocks + i, h)),
        out_shape=jax.ShapeDtypeStruct((b * tq_blocks * tq, heads * dv), _BF),
        compiler_params=_cparams(("arbitrary", "arbitrary", "arbitrary")),
    )(*args)


def _head4_spec(rows, cols):
    return pl.BlockSpec((None, None, rows, cols), lambda bi, h, i: (bi, h, 0, 0))


NA_GROUP = 4
NA_KEY_ROWS = 12


def _na_group_window(gi, rows):
    ngroups = rows // NA_GROUP
    if gi == 0:
        return 0, 0
    if gi == ngroups - 1:
        return 2, rows - NA_KEY_ROWS
    return 1, gi * NA_GROUP - NA_WIN_R // 2


NA_DR = 2 * NA_WIN_R - 1


def _na_col_bias(rpb):
    c = np.arange(GRID_W)
    col_start = np.clip(c - NA_WIN_C // 2, 0, GRID_W - NA_WIN_C)
    kc = np.arange(GRID_W)
    in_win = (kc[None, :] >= col_start[:, None]) & (kc[None, :] < col_start[:, None] + NA_WIN_C)
    dc = np.clip(kc[None, :] - c[:, None] + NA_WIN_C - 1, 0, 2 * NA_WIN_C - 2)
    col_sel = (dc[:, :, None] == np.arange(2 * NA_WIN_C - 1)).astype(np.float32)
    toep = jnp.einsum('hrj,cqj->hrqc', rpb.astype(_F32), col_sel, precision=lax.Precision.HIGHEST)
    toep = jnp.where(in_win.T[None, None], toep * LOG2E, NEG_INF)
    masked = jnp.full((rpb.shape[0], 1, GRID_W, GRID_W), NEG_INF, _F32)
    return jnp.concatenate([toep, masked], 1)


def _na_block_index():
    g = np.arange(NA_GROUP)[:, None]
    kr = np.arange(NA_KEY_ROWS)[None, :]
    half = NA_WIN_R // 2
    dr = np.stack([kr - g + NA_WIN_R - 1, kr - g + NA_WIN_R - 1 - half, kr - g - 1])
    w0 = np.stack([0 * g, g, half + 0 * g])
    row_ok = (kr[None] >= w0) & (kr[None] < w0 + NA_WIN_R)
    assert np.all((dr[row_ok] >= 0) & (dr[row_ok] < NA_DR))
    return np.where(row_ok, dr, NA_DR)


def _na_body(q_ref, k_ref, v_ref, kc_ref, vc_ref, cb_ref, o_ref, bias_ref, vt_ref, *, rows):
    @pl.when(pl.program_id(1) == 0)
    def _():
        idx = _na_block_index()
        for typ in range(3):
            for kr in range(NA_KEY_ROWS):
                for gp in range(NA_GROUP // 2):
                    pair = jnp.concatenate([cb_ref[int(idx[typ, 2 * gp, kr])], cb_ref[int(idx[typ, 2 * gp + 1, kr])]], 1)
                    bias_ref[typ, kr * GRID_W:(kr + 1) * GRID_W, gp * 2 * GRID_W:(gp + 1) * 2 * GRID_W] = pair

    vt_ref[...] = _values_t(v_ref[...])
    kc = kc_ref[...]
    vct = _values_t(vc_ref[...])
    gq = NA_GROUP * GRID_W

    def band(gi):
        ws = _na_group_window(gi, rows)[1]
        return slice(ws * GRID_W, (ws + NA_KEY_ROWS) * GRID_W)

    def scores(gi):
        q = q_ref[gi * gq:(gi + 1) * gq, :]
        return [_dot_nt(k_ref[band(gi), :], q) + bias_ref[_na_group_window(gi, rows)[0]], _dot_nt(kc, q)]

    def store(gi, out):
        o_ref[gi * gq:(gi + 1) * gq, :] = out.astype(o_ref.dtype)

    _skewed(rows // NA_GROUP, _attn_stages(scores, lambda gi: [vt_ref[:, band(gi)], vct], store))


def _na_attn(p, bias, b, t, c):
    rows = t // GRID_W
    cblk0 = b * t // c
    assert t % GRID_W == 0 and rows % NA_GROUP == 0 and rows >= NA_KEY_ROWS + NA_GROUP and 2 * GRID_W == LANE
    return pl.pallas_call(
        functools.partial(_na_body, rows=rows),
        grid=(NA_HEADS, b),
        in_specs=[pl.BlockSpec((t, LANE), lambda h, bi: (bi, CB_NAQ + h)),
                  pl.BlockSpec((t, LANE), lambda h, bi: (bi, CB_NAK + h)),
                  pl.BlockSpec((t, LANE), lambda h, bi: (bi, CB_NAV + h)),
                  pl.BlockSpec((c, LANE), lambda h, bi: (cblk0 + bi, CB_NAK + h)),
                  pl.BlockSpec((c, LANE), lambda h, bi: (cblk0 + bi, CB_NAV + h)),
                  pl.BlockSpec((None, NA_DR + 1, GRID_W, GRID_W), lambda h, bi: (h, 0, 0, 0))],
        out_specs=pl.BlockSpec((t, LANE), lambda h, bi: (bi, h)),
        out_shape=jax.ShapeDtypeStruct((b * t, NA_WIDTH), _BF),
        scratch_shapes=[pltpu.VMEM((3, NA_KEY_ROWS * GRID_W, NA_GROUP * GRID_W), _F32),
                        pltpu.VMEM((VT_ROWS, t), _BF)],
        compiler_params=_cparams(("arbitrary", "arbitrary")),
    )(p, p, p, p, p, bias)


def _ret_body(lg_ref, q_ref, k_ref, v_ref, g_ref, s0f_ref, s0b_ref, cos_ref, sa_ref, sb_ref,
              o_ref, sf_ref, sbo_ref, kr_scr, u_hist, st_hist, *, n):
    h = pl.program_id(0)
    lgf = lg_ref[0, h]
    lgb = lg_ref[1, h]
    L = RET_CHUNK
    pos_c = lax.broadcasted_iota(jnp.int32, (L, 1), 0).astype(_F32)
    pos_r = lax.broadcasted_iota(jnp.int32, (1, L), 1).astype(_F32)
    diff = pos_c - pos_r
    decay = jnp.where(diff > 0, jnp.exp(jnp.maximum(diff, 0.0) * lgf),
                      jnp.where(diff < 0, jnp.exp(jnp.maximum(-diff, 0.0) * lgb), 2.0))
    qdec_f = jnp.exp((pos_c + 1.0) * lgf)
    kdec_f = jnp.exp((L - 1.0 - pos_c) * lgf)
    qdec_b = jnp.exp((L - pos_c) * lgb)
    kdec_b = jnp.exp(pos_c * lgb)
    cd_f = jnp.exp(L * lgf)
    cd_b = jnp.exp(L * lgb)
    nf = HEAD_DIM // 4

    def rows_of(i):
        return slice(i * L, (i + 1) * L)

    def roped(ref, i):
        r = rows_of(i)
        return _rope(ref[r, :].astype(_F32), cos_ref[r, :], sa_ref[r, :], sb_ref[r, :], nf)

    for i in range(n):
        k = roped(k_ref, i)
        kr_scr[rows_of(i), :] = k.astype(_BF)
        kd = jnp.concatenate([k * kdec_f, k * kdec_b], 1).astype(_BF)
        u_hist[i] = lax.dot_general(kd, v_ref[rows_of(i), :], (((0,), (0,)), ((), ())),
                                    preferred_element_type=_F32)

    state = s0f_ref[...]
    for i in range(n):
        st_hist[i, 0:HEAD_DIM, :] = state.astype(_BF)
        state = cd_f * state + u_hist[i, 0:HEAD_DIM, :]
    sf_ref[...] = state
    state = s0b_ref[...]
    for i in reversed(range(n)):
        st_hist[i, HEAD_DIM:, :] = state.astype(_BF)
        state = cd_b * state + u_hist[i, HEAD_DIM:, :]
    sbo_ref[...] = state

    for i in range(n):
        r = rows_of(i)
        q = roped(q_ref, i)
        a = _dot_nt(q.astype(_BF), kr_scr[r, :]) * decay
        qd = jnp.concatenate([q * qdec_f, q * qdec_b], 1).astype(_BF)
        o = _dot(a.astype(_BF), v_ref[r, :]) + _dot(qd, st_hist[i])
        mu = jnp.mean(o, -1, keepdims=True)
        oc = o - mu
        var = jnp.mean(oc * oc, -1, keepdims=True)
        y = oc * lax.rsqrt(var + LN_EPS) * _silu(g_ref[r, :].astype(_F32))
        o_ref[r, :] = y.astype(o_ref.dtype)


def _retention(p, lg, s0f, s0b, tables, b, t, row0=0):
    n = t // RET_CHUNK
    blk0 = row0 // t
    col = lambda cb: pl.BlockSpec((t, LANE), lambda h, bi: (blk0 + bi, cb + h))
    st_spec = pl.BlockSpec((None, None, HEAD_DIM, HEAD_DIM), lambda h, bi: (bi, h, 0, 0))
    tab_spec = pl.BlockSpec((t, LANE), lambda h, bi: (0, 0))
    st_shape = jax.ShapeDtypeStruct((b, RET_HEADS, HEAD_DIM, HEAD_DIM), _F32)
    return pl.pallas_call(
        functools.partial(_ret_body, n=n),
        grid=(RET_HEADS, b),
        in_specs=[pl.BlockSpec(memory_space=pltpu.SMEM),
                  col(CB_RQ), col(CB_RK), col(CB_RV), col(CB_RG), st_spec, st_spec,
                  tab_spec, tab_spec, tab_spec],
        out_specs=[pl.BlockSpec((t, LANE), lambda h, bi: (bi, h)), st_spec, st_spec],
        out_shape=[jax.ShapeDtypeStruct((b * t, RET_WIDTH), _BF), st_shape, st_shape],
        scratch_shapes=[pltpu.VMEM((t, HEAD_DIM), _BF),
                        pltpu.VMEM((n, 2 * HEAD_DIM, HEAD_DIM), _F32),
                        pltpu.VMEM((n, 2 * HEAD_DIM, HEAD_DIM), _BF)],
        compiler_params=_cparams(("arbitrary", "arbitrary")),
    )(lg, p, p, p, p, s0f, s0b, *tables)


def _ln_epilogue(y, rows, x_ref, gate_ref, lng_ref, lnb_ref, alpha, xo_ref, ho_ref, sh_ref, sc_ref):
    z = alpha * x_ref[rows, :] + gate_ref[0] * y
    mu = jnp.mean(z, -1, keepdims=True)
    zc = z - mu
    var = jnp.mean(zc * zc, -1, keepdims=True)
    xn = zc * lax.rsqrt(var + LN_EPS) * lng_ref[...] + lnb_ref[...]
    xo_ref[rows, :] = xn
    if ho_ref is not None:
        ho_ref[rows, :] = (xn * (1.0 + sc_ref[0]) + sh_ref[0]).astype(ho_ref.dtype)


def _proj_ln_body(*refs, n_in, alpha, with_next, nsub):
    a_refs, refs = refs[:n_in], refs[n_in:]
    if with_next:
        w_ref, x_ref, gate_ref, sh_ref, sc_ref, lng_ref, lnb_ref, xo_ref, ho_ref = refs
    else:
        w_ref, x_ref, gate_ref, lng_ref, lnb_ref, xo_ref = refs
        sh_ref = sc_ref = ho_ref = None
    ts = x_ref.shape[0] // nsub
    for j in range(nsub):
        r = slice(j * ts, (j + 1) * ts)
        a = a_refs[0][r, :] if n_in == 1 else jnp.concatenate([ar[r, :] for ar in a_refs], 1)
        _ln_epilogue(_dot(a, w_ref[...]), r, x_ref, gate_ref, lng_ref, lnb_ref, alpha, xo_ref, ho_ref, sh_ref, sc_ref)


def _proj_ln(a_list, w, x2, mod3, layer, gate_chunk, row_fn, lng, lnb, alpha, tm, nsub, with_next):
    m, d = x2.shape
    row = lambda width: pl.BlockSpec((tm, width), lambda i: (i, 0))
    vec = pl.BlockSpec((1, d), lambda i: (0, 0))
    w_spec = pl.BlockSpec((None,) + w.shape[1:], lambda i: (layer, 0, 0), pipeline_mode=pl.Buffered(1))
    in_specs = [row(a.shape[1]) for a in a_list] + [w_spec, row(d), _mod_spec(d, layer, gate_chunk, row_fn)]
    args = list(a_list) + [w, x2, mod3]
    if with_next:
        nxt = (layer, gate_chunk + 1) if gate_chunk < 5 else (layer + 1, 0)
        in_specs += [_mod_spec(d, nxt[0], nxt[1], row_fn), _mod_spec(d, nxt[0], nxt[1] + 1, row_fn)]
        args += [mod3, mod3]
    in_specs += [vec, vec]
    args += [lng, lnb]
    out_specs = [row(d)] + ([row(d)] if with_next else [])
    out_shape = [jax.ShapeDtypeStruct((m, d), _F32)] + ([jax.ShapeDtypeStruct((m, d), _BF)] if with_next else [])
    out = pl.pallas_call(
        functools.partial(_proj_ln_body, n_in=len(a_list), alpha=alpha, with_next=with_next, nsub=nsub),
        grid=(m // tm,),
        in_specs=in_specs, out_specs=out_specs, out_shape=out_shape,
        compiler_params=_cparams(("arbitrary",)),
    )(*args)
    return (out[0], out[1]) if with_next else (out[0], None)


def _glu_chunks(t):
    rc = _tile(t, 512)
    return [(i * rc, rc) for i in range(t // rc)]


def _glu_body(h_ref, wa_ref, wu_ref, cw_ref, cb_ref, o_ref, *rest, t, cast):
    if cast:
        wab, wub, a_scr = rest

        @pl.when(pl.program_id(1) == 0)
        def _():
            wab[...] = wa_ref[...].astype(_BF)
            wub[...] = wu_ref[...].astype(_BF)
    else:
        (a_scr,) = rest
        wab, wub = wa_ref, wu_ref

    nseq = a_scr.shape[0]
    units = [(q, r0, rc) for q in range(nseq) for r0, rc in _glu_chunks(t)]
    halo = 8
    tn = o_ref.shape[1]
    zeros = jnp.zeros((halo, tn), _F32)
    for q in range(nseq):
        a_scr[q, 0:halo, :] = zeros
        a_scr[q, halo + t:, :] = zeros
    w0, w1, w2 = cw_ref[0:1, :], cw_ref[1:2, :], cw_ref[2:3, :]
    bias = cb_ref[...]
    u_prev = None
    for i in range(len(units) + 1):
        u_cur = None
        if i < len(units):
            q, r0, rc = units[i]
            hc = h_ref[q * t + r0:q * t + r0 + rc, :]
            a_scr[q, halo + r0:halo + r0 + rc, :] = _dot(hc, wab[...])
            u_cur = _dot(hc, wub[...])
        if i > 0:
            q, r0, rc = units[i - 1]
            base = halo + r0
            lo = a_scr[q, base - 1:base - 1 + rc, :]
            mid = a_scr[q, base:base + rc, :]
            hi = a_scr[q, base + 1:base + 1 + rc, :]
            acc = bias + lo * w0 + mid * w1 + hi * w2
            o_ref[q * t + r0:q * t + r0 + rc, :] = (_silu(acc) * u_prev).astype(o_ref.dtype)
        u_prev = u_cur


def _glu(h, w_up, layer, conv_w, conv_b, b, t, tn, nseq=1, w_bf16=None):
    d = h.shape[1]
    dff = conv_w.shape[1]
    nj = dff // tn
    cast = w_bf16 is None
    g_spec = pl.BlockSpec((nseq * t, tn), lambda j, bi: (bi, j))
    g_shape = jax.ShapeDtypeStruct((b * t, dff), _BF)
    wb_spec = pl.BlockSpec((None, d, tn), lambda j, bi: (j, 0, 0))
    if cast:
        w_args = (w_up, w_up)
        w_specs = [pl.BlockSpec((None, d, tn), lambda j, bi: (layer, 0, j)),
                   pl.BlockSpec((None, d, tn), lambda j, bi: (layer, 0, nj + j))]
        wb_shape = jax.ShapeDtypeStruct((nj, d, tn), _BF)
        out_specs, out_shape = [g_spec, wb_spec, wb_spec], [g_shape, wb_shape, wb_shape]
    else:
        w_args, w_specs = tuple(w_bf16), [wb_spec, wb_spec]
        out_specs, out_shape = g_spec, g_shape
    out = pl.pallas_call(
        functools.partial(_glu_body, t=t, cast=cast),
        grid=(nj, b // nseq),
        in_specs=[pl.BlockSpec((nseq * t, d), lambda j, bi: (bi, 0)), *w_specs,
                  pl.BlockSpec((CONV_W, tn), lambda j, bi: (0, j)),
                  pl.BlockSpec((1, tn), lambda j, bi: (0, j))],
        out_specs=out_specs, out_shape=out_shape,
        scratch_shapes=[pltpu.VMEM((nseq, t + 16, tn), _F32)],
        compiler_params=_cparams(("arbitrary", "arbitrary")),
    )(h, *w_args, conv_w, conv_b)
    return (out[0], (out[1], out[2])) if cast else out


PACK_SRC_ROWS = 64
PACK_CHUNKS = 4


def _pack_plan():
    src = dict(zip(("na_q", "na_k", "na_v", "cq", "ckv", "kpe", "r_q", "r_k", "r_v", "r_g"),
                   [int(i) for i in np.cumsum((0,) + IN_SPLITS[:-1])]))
    plan = np.zeros((PACKED_IN_WIDTH // LANE, 3), np.int32)
    segments = [(CB_CQ, "cq", MLA_Q_RANK + MLA_KV_RANK, 0), (CB_NAQ, "na_q", NA_WIDTH, 1),
                (CB_NAK, "na_k", 2 * NA_WIDTH, 0), (CB_RQ, "r_q", RET_WIDTH, 0), (CB_RK, "r_k", RET_WIDTH, 2),
                (CB_RV, "r_v", 2 * RET_WIDTH, 0), (CB_KPE, "kpe", MLA_ROPE, 0)]
    for dst, name, width, kind in segments:
        for off in range(0, width, LANE):
            row = src[name] + off
            assert row % PACK_SRC_ROWS == 0
            plan[dst + off // LANE] = (row // PACK_SRC_ROWS, kind, min(LANE, width - off))
    return plan


def _pack_w_in_body(plan_ref, *refs):
    q_scale = HEAD_DIM ** -0.5 * LOG2E
    k_scale = HEAD_DIM ** -0.5
    o_ref = refs[-1]
    row = lax.broadcasted_iota(jnp.int32, (LANE, 1), 0)
    for c in range(PACK_CHUNKS):
        chunk = pl.program_id(0) * PACK_CHUNKS + c
        kind, valid = plan_ref[chunk, 1], plan_ref[chunk, 2]
        scale = jnp.where(kind == 1, q_scale, jnp.where(kind == 2, k_scale, 1.0))
        x = jnp.concatenate([refs[2 * c][...], refs[2 * c + 1][...]], 0)
        x = jnp.where(row < valid, x * scale, 0.0)
        o_ref[:, c * LANE:(c + 1) * LANE] = x.T.astype(o_ref.dtype)


def _pack_w_in(w_in, layer, tn):
    depth, d, n = w_in.shape
    w_t = jnp.swapaxes(w_in, 1, 2)
    step_cols = PACK_CHUNKS * LANE
    per_block = tn // step_cols

    def src_spec(c, second):
        return pl.BlockSpec((None, PACK_SRC_ROWS, d),
                            lambda j, plan: (layer, plan[j * PACK_CHUNKS + c, 0] + second, 0))

    return pl.pallas_call(
        _pack_w_in_body,
        grid_spec=pltpu.PrefetchScalarGridSpec(
            num_scalar_prefetch=1,
            grid=(PACKED_IN_WIDTH // step_cols,),
            in_specs=[src_spec(c, second) for c in range(PACK_CHUNKS) for second in (0, 1)],
            out_specs=pl.BlockSpec((None, d, step_cols), lambda j, plan: (j // per_block, 0, j % per_block))),
        out_shape=jax.ShapeDtypeStruct((PACKED_IN_WIDTH // tn, d, tn), _BF),
        compiler_params=_cparams(("arbitrary",)),
    )(jnp.asarray(_pack_plan()), *([w_t] * (2 * PACK_CHUNKS)))


def _pack_w_uq(w):
    r = w.shape[0]
    w = w.reshape(r, MLA_HEADS, MLA_NOPE + MLA_ROPE) * ((MLA_NOPE + MLA_ROPE) ** -0.5 * LOG2E)
    w = jnp.pad(w, ((0, 0), (0, 0), (0, MLA_QK_PAD - MLA_NOPE - MLA_ROPE)))
    return w.reshape(r, MLA_HEADS * MLA_QK_PAD).astype(_BF)


def kernel(x, c, ctx, c_ctx, w_ada, b_ada, w_in, mla_q_norm, mla_kv_norm, w_uq, w_ukv, na_rpb, ret_decay,
           w_o, ln1_g, ln1_b, w_up, conv_w, conv_b, w_down, ln2_g, ln2_b):
    b, t, d = x.shape
    cl = ctx.shape[1]
    depth = w_ada.shape[0]
    dff = conv_w.shape[-1]
    assert b + 1 <= MOD_ROWS
    alpha = (2 * depth) ** 0.25
    ctx_row = b

    cvec = jnp.concatenate([c, c_ctx[None], jnp.zeros((MOD_ROWS - b - 1, d), c.dtype)], 0)
    mod3 = _ada(cvec, w_ada, b_ada)

    tm_in = _tile(b * cl, _tile(t, 1024))
    ctx0 = b * t
    lat_row = lambda tm: (lambda i: i // (t // tm))
    ctx_rowf = lambda i: ctx_row

    rope_pe = _rope_tables(t, MLA_ROPE)
    rope_ret = _rope_tables(t, HEAD_DIM)
    id_pe = _rope_tables(cl, MLA_ROPE, identity=True)
    id_ret = _rope_tables(cl, HEAD_DIM, identity=True)

    xf = x.reshape(b * t, d)
    cf = ctx.reshape(b * cl, d)
    h1, hc1 = xf, cf

    tq = _tile(t, 2048)
    tqc = _tile(cl, 256)
    tn_ff = _tile(dff, 512)
    zero_state = jnp.zeros((b, RET_HEADS, HEAD_DIM, HEAD_DIM), _F32)
    w_o_b = w_o.astype(_BF)
    w_down_b = w_down.astype(_BF)

    for l in range(depth):
        last = l == depth - 1
        w_in_p = _pack_w_in(w_in, l, 1024)
        w_uq_p = _pack_w_uq(w_uq[l])
        w_ukv_p = w_ukv[l].astype(_BF)
        gq = mla_q_norm[l][None]
        gkv = mla_kv_norm[l][None]
        lg = jnp.log1p(-jnp.exp2(ret_decay[l].astype(_F32)))
        bias = _na_col_bias(na_rpb[l])

        p = _in_proj(h1, hc1, w_in_p, tm_in, t, mod=(mod3, 0, ctx_row) if l == 0 else None)

        qc_m, kc_m, vc_m = _mla_qkv(p, gq, gkv, w_uq_p, w_ukv_p, id_pe, b, cl, _tile(cl, 1024), row0=ctx0)
        q_m, kl_m, vl_m = _mla_qkv(p, gq, gkv, w_uq_p, w_ukv_p, rope_pe, b, t, _tile(t, 1024))
        y_mla = _dense_attn(
            q_m, pl.BlockSpec((None, None, tq, MLA_QK_PAD), lambda bi, h, i: (bi, h, i, 0)),
            [(kc_m, _head4_spec(cl, MLA_QK_PAD), vc_m, _head4_spec(VT_ROWS, cl)),
             (kl_m, _head4_spec(t, MLA_QK_PAD), vl_m, _head4_spec(VT_ROWS, t))],
            b, MLA_HEADS, t // tq, tq, MLA_V, nsub=tq // _tile(tq, 256))

        y_na = _na_attn(p, bias, b, t, cl)

        yc_ret, s_f, s_b = _retention(p, lg, zero_state, zero_state, id_ret, b, cl, row0=ctx0)
        y_ret, _, _ = _retention(p, lg, s_f, s_b, rope_ret, b, t)

        tm_o = _tile(t, 512)
        x_new, h2 = _proj_ln([y_na, y_mla, y_ret], w_o_b, xf, mod3, l, 2, lat_row(tm_o),
                             ln1_g[l][None], ln1_b[l][None], alpha, tm_o, tm_o // _tile(tm_o, 256), with_next=True)

        tm_d = _tile(t, 256)

        if not last:
            yc_mla = _dense_attn(
                qc_m, pl.BlockSpec((None, None, tqc, MLA_QK_PAD), lambda bi, h, i: (bi, h, i, 0)),
                [(kc_m, _head4_spec(cl, MLA_QK_PAD), vc_m, _head4_spec(VT_ROWS, cl))],
                b, MLA_HEADS, cl // tqc, tqc, MLA_V, nsub=1)
            nbc = cl // tqc
            cq0, ck0 = ctx0 // tqc, ctx0 // cl
            yc_na = _dense_attn(
                p, pl.BlockSpec((tqc, LANE), lambda bi, h, i: (cq0 + bi * nbc + i, CB_NAQ + h)),
                [(p, pl.BlockSpec((cl, LANE), lambda bi, h, i: (ck0 + bi, CB_NAK + h)),
                  p, pl.BlockSpec((cl, LANE), lambda bi, h, i: (ck0 + bi, CB_NAV + h)))],
                b, NA_HEADS, nbc, tqc, HEAD_DIM, nsub=1, v_row_major=True)
            tm_oc = _tile(cl, 256)
            c_new, hc2 = _proj_ln([yc_na, yc_mla, yc_ret], w_o_b, cf, mod3, l, 2, ctx_rowf,
                                  ln1_g[l][None], ln1_b[l][None], alpha, tm_oc, 1, with_next=True)

        g_l, w_up_b = _glu(h2, w_up, l, conv_w[l], conv_b[l][None], b, t, tn_ff)
        xf, h1 = _proj_ln([g_l], w_down_b, x_new, mod3, l, 5, lat_row(tm_d), ln2_g[l][None], ln2_b[l][None],
                          alpha, tm_d, 1, with_next=not last)
        if not last:
            g_c = _glu(hc2, w_up, l, conv_w[l], conv_b[l][None], b, cl, tn_ff, nseq=b, w_bf16=w_up_b)
            tm_dc = _tile(cl, 256)
            cf, hc1 = _proj_ln([g_c], w_down_b, c_new, mod3, l, 5, ctx_rowf, ln2_g[l][None], ln2_b[l][None],
                               alpha, tm_dc, 1, with_next=True)

    return xf.reshape(b, t, d)
```

```python
import functools

import numpy as np
import jax
import jax.numpy as jnp
from jax import lax
from jax.experimental import pallas as pl
from jax.experimental.pallas import tpu as pltpu

GRID_W = 64
HEAD_DIM = 128
NA_HEADS = 6
MLA_HEADS = 5
RET_HEADS = 5
NA_WIDTH = NA_HEADS * HEAD_DIM
MLA_WIDTH = MLA_HEADS * HEAD_DIM
RET_WIDTH = RET_HEADS * HEAD_DIM
NA_WIN_R = 8
NA_WIN_C = 16
MLA_Q_RANK = 512
MLA_KV_RANK = 512
MLA_NOPE = 128
MLA_ROPE = 64
MLA_V = 128
MLA_QK_PAD = 256
RET_CHUNK = 128
CONV_W = 3
ROPE_BASE = 10000.0
LN_EPS = 1e-5
RMS_EPS = 1e-6
NEG_INF = -1e30
LOG2E = 1.4426950408889634
IN_SPLITS = (NA_WIDTH, NA_WIDTH, NA_WIDTH, MLA_Q_RANK, MLA_KV_RANK, MLA_ROPE,
             RET_WIDTH, RET_WIDTH, RET_WIDTH, RET_WIDTH)

LANE = 128
CB_CQ, CB_CKV, CB_NAQ, CB_NAK, CB_NAV = 0, 4, 8, 14, 20
CB_RQ, CB_RK, CB_RV, CB_RG, CB_KPE = 26, 31, 36, 41, 46
PACKED_IN_WIDTH = 48 * LANE
MOD_ROWS = 8
VMEM_LIMIT_MB = 56
VMEM_LIMIT_MAX_MB = 60

_BF = jnp.bfloat16
_F32 = jnp.float32


def _cparams(sem, vmem_mb=VMEM_LIMIT_MB):
    return pltpu.CompilerParams(dimension_semantics=sem, vmem_limit_bytes=vmem_mb << 20)


def _tile(n, pref):
    t = min(n, pref)
    while n % t:
        t //= 2
    return t


def _dot(a, b):
    return jnp.dot(a, b, preferred_element_type=_F32)


def _dot_nt(a, b):
    return lax.dot_general(a, b, (((1,), (1,)), ((), ())), preferred_element_type=_F32)


def _silu(x):
    return x / (1.0 + jnp.exp(-x))


def _ada_body(c_ref, w_ref, b_ref, o_ref):
    part = _dot(_silu(c_ref[...]).astype(_BF), w_ref[0].astype(_BF))

    @pl.when(pl.program_id(1) == 0)
    def _():
        o_ref[0] = part + b_ref[0]

    @pl.when(pl.program_id(1) > 0)
    def _():
        o_ref[0] += part


def _ada(cvec, w_ada, b_ada):
    depth, d, n = w_ada.shape
    tk = _tile(d, 256)
    out = pl.pallas_call(
        _ada_body,
        grid=(depth, d // tk),
        in_specs=[pl.BlockSpec((MOD_ROWS, tk), lambda l, k: (0, k)),
                  pl.BlockSpec((1, tk, n), lambda l, k: (l, k, 0)),
                  pl.BlockSpec((1, 1, n), lambda l, k: (l, 0, 0))],
        out_specs=pl.BlockSpec((1, MOD_ROWS, n), lambda l, k: (l, 0, 0)),
        out_shape=jax.ShapeDtypeStruct((depth, MOD_ROWS, n), _F32),
        compiler_params=_cparams(("arbitrary", "arbitrary")),
    )(cvec, w_ada, b_ada.reshape(depth, 1, n))
    return out.reshape(depth * MOD_ROWS, 1, n)


def _mod_spec(d, layer, chunk, row_fn):
    return pl.BlockSpec((1, 1, d), lambda i, *_: (layer * MOD_ROWS + row_fn(i), 0, chunk))


def _in_proj_body(*refs, n_lat, modulated):
    if modulated:
        lat_ref, ctx_ref, shl_ref, scl_ref, shc_ref, scc_ref, w_ref, o_ref, a_scr = refs
    else:
        lat_ref, ctx_ref, w_ref, o_ref, a_scr = refs
    i = pl.program_id(0)

    def stage(src_ref, sh_ref, sc_ref):
        x = src_ref[...]
        if modulated:
            x = x * (1.0 + sc_ref[0]) + sh_ref[0]
        a_scr[...] = x.astype(a_scr.dtype)

    @pl.when((pl.program_id(1) == 0) & (i < n_lat))
    def _():
        stage(lat_ref, shl_ref if modulated else None, scl_ref if modulated else None)

    @pl.when((pl.program_id(1) == 0) & (i >= n_lat))
    def _():
        stage(ctx_ref, shc_ref if modulated else None, scc_ref if modulated else None)

    o_ref[...] = _dot(a_scr[...], w_ref[...]).astype(o_ref.dtype)


def _in_proj(lat, ctx, w, tm, t, mod=None):
    d = lat.shape[1]
    nj, _, tn = w.shape
    n_lat, n_ctx = lat.shape[0] // tm, ctx.shape[0] // tm
    lat_i = lambda i: jnp.minimum(i, n_lat - 1)
    in_specs = [pl.BlockSpec((tm, d), lambda i, j: (lat_i(i), 0)),
                pl.BlockSpec((tm, d), lambda i, j: (jnp.maximum(i - n_lat, 0), 0), pipeline_mode=pl.Buffered(1))]
    args = [lat, ctx]
    if mod is not None:
        mod3, layer, ctx_row = mod
        lat_row = lambda i: lat_i(i) // (t // tm)
        in_specs += [_mod_spec(d, layer, 0, lat_row), _mod_spec(d, layer, 1, lat_row),
                     _mod_spec(d, layer, 0, lambda i: ctx_row), _mod_spec(d, layer, 1, lambda i: ctx_row)]
        args += [mod3] * 4
    return pl.pallas_call(
        functools.partial(_in_proj_body, n_lat=n_lat, modulated=mod is not None),
        grid=(n_lat + n_ctx, nj),
        in_specs=in_specs + [pl.BlockSpec((None, d, tn), lambda i, j: (j, 0, 0))],
        out_specs=pl.BlockSpec((tm, tn), lambda i, j: (i, j)),
        out_shape=jax.ShapeDtypeStruct(((n_lat + n_ctx) * tm, nj * tn), _BF),
        scratch_shapes=[pltpu.VMEM((tm, d), _BF)],
        compiler_params=_cparams(("arbitrary", "arbitrary")),
    )(*args, w)


def _rope_tables(t, rot_dim, identity=False):
    pos = np.arange(t)
    row = (pos // GRID_W).astype(np.float32)
    col = (pos % GRID_W).astype(np.float32)
    nf = rot_dim // 4
    inv = (ROPE_BASE ** (-np.arange(nf, dtype=np.float32) / nf)).astype(np.float32)
    ar = row[:, None] * inv[None]
    ac = col[:, None] * inv[None]
    ang = np.concatenate([ar, ar, ac, ac], -1)
    cos, sin = np.cos(ang), np.sin(ang)
    if identity:
        cos, sin = np.ones_like(cos), np.zeros_like(sin)
    first = (np.arange(rot_dim) % (rot_dim // 2)) < nf
    sin_a = np.where(first, -sin, 0.0)
    sin_b = np.where(first, 0.0, sin)
    pad = LANE - rot_dim
    if pad:
        cos = np.pad(cos, ((0, 0), (0, pad)), constant_values=1.0)
        sin_a = np.pad(sin_a, ((0, 0), (0, pad)))
        sin_b = np.pad(sin_b, ((0, 0), (0, pad)))
    return tuple(jnp.asarray(a, _F32) for a in (cos, sin_a, sin_b))


def _rope(x, cos, sin_a, sin_b, nf):
    return x * cos + pltpu.roll(x, LANE - nf, 1) * sin_a + pltpu.roll(x, nf, 1) * sin_b


def _rms(x, g):
    return x * lax.rsqrt(jnp.mean(x * x, -1, keepdims=True) + RMS_EPS) * g


VT_ROWS = HEAD_DIM + 16


def _values_t(v):
    n = v.shape[0]
    ones = (lax.broadcasted_iota(jnp.int32, (VT_ROWS - HEAD_DIM, n), 0) == 0).astype(_BF)
    return jnp.concatenate([v.astype(_F32).T.astype(_BF), ones], 0)


def _mla_qkv_body(cq_ref, ckv_ref, kpe_ref, gq_ref, gkv_ref, wq_ref, wkv_ref, cos_ref, sa_ref, sb_ref,
                  q_ref, k_ref, vt_ref):
    cos, sa, sb = cos_ref[...], sa_ref[...], sb_ref[...]
    nf = MLA_ROPE // 4
    xq = _rms(cq_ref[...].astype(_F32), gq_ref[...]).astype(_BF)
    xkv = _rms(ckv_ref[...].astype(_F32), gkv_ref[...]).astype(_BF)
    pe = _rope(kpe_ref[...].astype(_F32), cos, sa, sb, nf).astype(_BF)
    width = MLA_NOPE + MLA_V
    for h in range(MLA_HEADS):
        q = _dot(xq, wq_ref[:, h * MLA_QK_PAD:(h + 1) * MLA_QK_PAD])
        q_ref[0, h, :, 0:LANE] = q[:, :LANE].astype(_BF)
        q_ref[0, h, :, LANE:] = _rope(q[:, LANE:], cos, sa, sb, nf).astype(_BF)
        kv = _dot(xkv, wkv_ref[:, h * width:(h + 1) * width])
        k_ref[0, h, :, 0:LANE] = kv[:, :MLA_NOPE].astype(_BF)
        k_ref[0, h, :, LANE:] = pe
        vt_ref[0, h] = _values_t(kv[:, MLA_NOPE:])


def _mla_qkv(p, gq, gkv, wq, wkv, tables, b, t, tm, row0=0):
    nb = t // tm
    blk0 = row0 // tm
    tab_spec = pl.BlockSpec((tm, LANE), lambda i: (i % nb, 0))
    head_spec = lambda rows, cols, ri, ci: pl.BlockSpec(
        (1, MLA_HEADS, rows, cols), lambda i: (i // nb, 0, (i % nb) * ri, (i % nb) * ci))
    return pl.pallas_call(
        _mla_qkv_body,
        grid=(b * nb,),
        in_specs=[pl.BlockSpec((tm, MLA_Q_RANK), lambda i: (blk0 + i, CB_CQ // 4)),
                  pl.BlockSpec((tm, MLA_KV_RANK), lambda i: (blk0 + i, CB_CKV // 4)),
                  pl.BlockSpec((tm, LANE), lambda i: (blk0 + i, CB_KPE)),
                  pl.BlockSpec((1, MLA_Q_RANK), lambda i: (0, 0)),
                  pl.BlockSpec((1, MLA_KV_RANK), lambda i: (0, 0)),
                  pl.BlockSpec(wq.shape, lambda i: (0, 0)),
                  pl.BlockSpec(wkv.shape, lambda i: (0, 0)),
                  tab_spec, tab_spec, tab_spec],
        out_specs=[head_spec(tm, MLA_QK_PAD, 1, 0), head_spec(tm, MLA_QK_PAD, 1, 0), head_spec(VT_ROWS, tm, 0, 1)],
        out_shape=[jax.ShapeDtypeStruct((b, MLA_HEADS, t, MLA_QK_PAD), _BF),
                   jax.ShapeDtypeStruct((b, MLA_HEADS, t, MLA_QK_PAD), _BF),
                   jax.ShapeDtypeStruct((b, MLA_HEADS, VT_ROWS, t), _BF)],
        compiler_params=_cparams(("arbitrary",)),
    )(p, p, p, gq, gkv, wq, wkv, *tables)


def _skewed(n, stages):
    carry = [None] * n
    for step in range(n + len(stages) - 1):
        for s, stage in enumerate(stages):
            j = step - s
            if 0 <= j < n:
                carry[j] = stage(j, carry[j])


def _attn_stages(scores_fn, values_fn, store_fn):
    def s_scores(j, _):
        return scores_fn(j)

    def s_max(j, sc):
        return sc, functools.reduce(jnp.maximum, [jnp.max(s, 0, keepdims=True) for s in sc])

    def s_exp(j, c):
        sc, m = c
        return [jnp.exp2(s - m).astype(_BF) for s in sc]

    def s_pv(j, p):
        acc = functools.reduce(jnp.add, [_dot(vt, pi) for vt, pi in zip(values_fn(j), p)])
        store_fn(j, (acc[0:HEAD_DIM, :] / acc[HEAD_DIM:HEAD_DIM + 1, :]).T)

    return [s_scores, s_max, s_exp, s_pv]


def _attn_body(*refs, nseg, nsub, v_row_major):
    q_ref, o_ref = refs[0], refs[-1]
    ks = [refs[1 + 2 * i][...] for i in range(nseg)]
    vts = [refs[2 + 2 * i][...] for i in range(nseg)]
    if v_row_major:
        vts = [_values_t(v) for v in vts]
    ts = q_ref.shape[0] // nsub
    rows = lambda j: slice(j * ts, (j + 1) * ts)

    def store(j, out):
        o_ref[rows(j), :] = out.astype(o_ref.dtype)

    _skewed(nsub, _attn_stages(lambda j: [_dot_nt(k, q_ref[rows(j), :]) for k in ks], lambda j: vts, store))


def _dense_attn(q, q_spec, kv, b, heads, tq_blocks, tq, dv, nsub, v_row_major=False):
    args, specs = [q], [q_spec]
    for k, ks, v, vs in kv:
        args += [k, v]
        specs += [ks, vs]
    return pl.pallas_call(
        functools.partial(_attn_body, nseg=len(kv), nsub=nsub, v_row_major=v_row_major),
        grid=(b, heads, tq_blocks),
        in_specs=specs,
        out_specs=pl.BlockSpec((tq, dv), lambda bi, h, i: (bi * tq_blocks + i, h)),
        out_shape=jax.ShapeDtypeStruct((b * tq_blocks * tq, heads * dv), _BF),
        compiler_params=_cparams(("arbitrary", "arbitrary", "arbitrary")),
    )(*args)


def _head4_spec(rows, cols):
    return pl.BlockSpec((None, None, rows, cols), lambda bi, h, i: (bi, h, 0, 0))


NA_GROUP = 4
NA_KEY_ROWS = 12


def _na_group_window(gi, rows):
    ngroups = rows // NA_GROUP
    if gi == 0:
        return 0, 0
    if gi == ngroups - 1:
        return 2, rows - NA_KEY_ROWS
    return 1, gi * NA_GROUP - NA_WIN_R // 2


NA_DR = 2 * NA_WIN_R - 1


def _na_col_bias(rpb):
    c = np.arange(GRID_W)
    col_start = np.clip(c - NA_WIN_C // 2, 0, GRID_W - NA_WIN_C)
    kc = np.arange(GRID_W)
    in_win = (kc[None, :] >= col_start[:, None]) & (kc[None, :] < col_start[:, None] + NA_WIN_C)
    dc = np.clip(kc[None, :] - c[:, None] + NA_WIN_C - 1, 0, 2 * NA_WIN_C - 2)
    col_sel = (dc[:, :, None] == np.arange(2 * NA_WIN_C - 1)).astype(np.float32)
    toep = jnp.einsum('hrj,cqj->hrqc', rpb.astype(_F32), col_sel, precision=lax.Precision.HIGHEST)
    toep = jnp.where(in_win.T[None, None], toep * LOG2E, NEG_INF)
    masked = jnp.full((rpb.shape[0], 1, GRID_W, GRID_W), NEG_INF, _F32)
    return jnp.concatenate([toep, masked], 1)


def _na_block_index():
    g = np.arange(NA_GROUP)[:, None]
    kr = np.arange(NA_KEY_ROWS)[None, :]
    half = NA_WIN_R // 2
    dr = np.stack([kr - g + NA_WIN_R - 1, kr - g + NA_WIN_R - 1 - half, kr - g - 1])
    w0 = np.stack([0 * g, g, half + 0 * g])
    row_ok = (kr[None] >= w0) & (kr[None] < w0 + NA_WIN_R)
    assert np.all((dr[row_ok] >= 0) & (dr[row_ok] < NA_DR))
    return np.where(row_ok, dr, NA_DR)


def _na_body(q_ref, k_ref, v_ref, kc_ref, vc_ref, cb_ref, o_ref, bias_ref, vt_ref, *, rows):
    @pl.when(pl.program_id(1) == 0)
    def _():
        idx = _na_block_index()
        for typ in range(3):
            for kr in range(NA_KEY_ROWS):
                for gp in range(NA_GROUP // 2):
                    pair = jnp.concatenate([cb_ref[int(idx[typ, 2 * gp, kr])], cb_ref[int(idx[typ, 2 * gp + 1, kr])]], 1)
                    bias_ref[typ, kr * GRID_W:(kr + 1) * GRID_W, gp * 2 * GRID_W:(gp + 1) * 2 * GRID_W] = pair

    vt_ref[...] = _values_t(v_ref[...])
    kc = kc_ref[...]
    vct = _values_t(vc_ref[...])
    gq = NA_GROUP * GRID_W

    def band(gi):
        ws = _na_group_window(gi, rows)[1]
        return slice(ws * GRID_W, (ws + NA_KEY_ROWS) * GRID_W)

    def scores(gi):
        q = q_ref[gi * gq:(gi + 1) * gq, :]
        return [_dot_nt(k_ref[band(gi), :], q) + bias_ref[_na_group_window(gi, rows)[0]], _dot_nt(kc, q)]

    def store(gi, out):
        o_ref[gi * gq:(gi + 1) * gq, :] = out.astype(o_ref.dtype)

    _skewed(rows // NA_GROUP, _attn_stages(scores, lambda gi: [vt_ref[:, band(gi)], vct], store))


def _na_attn(p, bias, b, t, c):
    rows = t // GRID_W
    cblk0 = b * t // c
    assert t % GRID_W == 0 and rows % NA_GROUP == 0 and rows >= NA_KEY_ROWS + NA_GROUP and 2 * GRID_W == LANE
    return pl.pallas_call(
        functools.partial(_na_body, rows=rows),
        grid=(NA_HEADS, b),
        in_specs=[pl.BlockSpec((t, LANE), lambda h, bi: (bi, CB_NAQ + h)),
                  pl.BlockSpec((t, LANE), lambda h, bi: (bi, CB_NAK + h)),
                  pl.BlockSpec((t, LANE), lambda h, bi: (bi, CB_NAV + h)),
                  pl.BlockSpec((c, LANE), lambda h, bi: (cblk0 + bi, CB_NAK + h)),
                  pl.BlockSpec((c, LANE), lambda h, bi: (cblk0 + bi, CB_NAV + h)),
                  pl.BlockSpec((None, NA_DR + 1, GRID_W, GRID_W), lambda h, bi: (h, 0, 0, 0))],
        out_specs=pl.BlockSpec((t, LANE), lambda h, bi: (bi, h)),
        out_shape=jax.ShapeDtypeStruct((b * t, NA_WIDTH), _BF),
        scratch_shapes=[pltpu.VMEM((3, NA_KEY_ROWS * GRID_W, NA_GROUP * GRID_W), _F32),
                        pltpu.VMEM((VT_ROWS, t), _BF)],
        compiler_params=_cparams(("arbitrary", "arbitrary")),
    )(p, p, p, p, p, bias)


def _ret_body(lg_ref, q_ref, k_ref, v_ref, g_ref, s0f_ref, s0b_ref, cos_ref, sa_ref, sb_ref,
              o_ref, sf_ref, sbo_ref, kr_scr, u_hist, st_hist, *, n):
    h = pl.program_id(0)
    lgf = lg_ref[0, h]
    lgb = lg_ref[1, h]
    L = RET_CHUNK
    pos_c = lax.broadcasted_iota(jnp.int32, (L, 1), 0).astype(_F32)
    pos_r = lax.broadcasted_iota(jnp.int32, (1, L), 1).astype(_F32)
    diff = pos_c - pos_r
    decay = jnp.where(diff > 0, jnp.exp(jnp.maximum(diff, 0.0) * lgf),
                      jnp.where(diff < 0, jnp.exp(jnp.maximum(-diff, 0.0) * lgb), 2.0))
    qdec_f = jnp.exp((pos_c + 1.0) * lgf)
    kdec_f = jnp.exp((L - 1.0 - pos_c) * lgf)
    qdec_b = jnp.exp((L - pos_c) * lgb)
    kdec_b = jnp.exp(pos_c * lgb)
    cd_f = jnp.exp(L * lgf)
    cd_b = jnp.exp(L * lgb)
    nf = HEAD_DIM // 4

    def rows_of(i):
        return slice(i * L, (i + 1) * L)

    def roped(ref, i):
        r = rows_of(i)
        return _rope(ref[r, :].astype(_F32), cos_ref[r, :], sa_ref[r, :], sb_ref[r, :], nf)

    for i in range(n):
        k = roped(k_ref, i)
        kr_scr[rows_of(i), :] = k.astype(_BF)
        kd = jnp.concatenate([k * kdec_f, k * kdec_b], 1).astype(_BF)
        u_hist[i] = lax.dot_general(kd, v_ref[rows_of(i), :], (((0,), (0,)), ((), ())),
                                    preferred_element_type=_F32)

    state = s0f_ref[...]
    for i in range(n):
        st_hist[i, 0:HEAD_DIM, :] = state.astype(_BF)
        state = cd_f * state + u_hist[i, 0:HEAD_DIM, :]
    sf_ref[...] = state
    state = s0b_ref[...]
    for i in reversed(range(n)):
        st_hist[i, HEAD_DIM:, :] = state.astype(_BF)
        state = cd_b * state + u_hist[i, HEAD_DIM:, :]
    sbo_ref[...] = state

    for i in range(n):
        r = rows_of(i)
        q = roped(q_ref, i)
        a = _dot_nt(q.astype(_BF), kr_scr[r, :]) * decay
        qd = jnp.concatenate([q * qdec_f, q * qdec_b], 1).astype(_BF)
        o = _dot(a.astype(_BF), v_ref[r, :]) + _dot(qd, st_hist[i])
        mu = jnp.mean(o, -1, keepdims=True)
        oc = o - mu
        var = jnp.mean(oc * oc, -1, keepdims=True)
        y = oc * lax.rsqrt(var + LN_EPS) * _silu(g_ref[r, :].astype(_F32))
        o_ref[r, :] = y.astype(o_ref.dtype)


def _retention(p, lg, s0f, s0b, tables, b, t, row0=0):
    n = t // RET_CHUNK
    blk0 = row0 // t
    col = lambda cb: pl.BlockSpec((t, LANE), lambda h, bi: (blk0 + bi, cb + h))
    st_spec = pl.BlockSpec((None, None, HEAD_DIM, HEAD_DIM), lambda h, bi: (bi, h, 0, 0))
    tab_spec = pl.BlockSpec((t, LANE), lambda h, bi: (0, 0))
    st_shape = jax.ShapeDtypeStruct((b, RET_HEADS, HEAD_DIM, HEAD_DIM), _F32)
    return pl.pallas_call(
        functools.partial(_ret_body, n=n),
        grid=(RET_HEADS, b),
        in_specs=[pl.BlockSpec(memory_space=pltpu.SMEM),
                  col(CB_RQ), col(CB_RK), col(CB_RV), col(CB_RG), st_spec, st_spec,
                  tab_spec, tab_spec, tab_spec],
        out_specs=[pl.BlockSpec((t, LANE), lambda h, bi: (bi, h)), st_spec, st_spec],
        out_shape=[jax.ShapeDtypeStruct((b * t, RET_WIDTH), _BF), st_shape, st_shape],
        scratch_shapes=[pltpu.VMEM((t, HEAD_DIM), _BF),
                        pltpu.VMEM((n, 2 * HEAD_DIM, HEAD_DIM), _F32),
                        pltpu.VMEM((n, 2 * HEAD_DIM, HEAD_DIM), _BF)],
        compiler_params=_cparams(("arbitrary", "arbitrary")),
    )(lg, p, p, p, p, s0f, s0b, *tables)


def _ln_epilogue(y, rows, x_ref, gate_ref, lng_ref, lnb_ref, alpha, xo_ref, ho_ref, sh_ref, sc_ref):
    z = alpha * x_ref[rows, :] + gate_ref[0] * y
    mu = jnp.mean(z, -1, keepdims=True)
    zc = z - mu
    var = jnp.mean(zc * zc, -1, keepdims=True)
    xn = zc * lax.rsqrt(var + LN_EPS) * lng_ref[...] + lnb_ref[...]
    xo_ref[rows, :] = xn
    if ho_ref is not None:
        ho_ref[rows, :] = (xn * (1.0 + sc_ref[0]) + sh_ref[0]).astype(ho_ref.dtype)


def _proj_ln_body(*refs, n_in, alpha, with_next, nsub):
    a_refs, refs = refs[:n_in], refs[n_in:]
    if with_next:
        w_ref, x_ref, gate_ref, sh_ref, sc_ref, lng_ref, lnb_ref, xo_ref, ho_ref = refs
    else:
        w_ref, x_ref, gate_ref, lng_ref, lnb_ref, xo_ref = refs
        sh_ref = sc_ref = ho_ref = None
    ts = x_ref.shape[0] // nsub
    for j in range(nsub):
        r = slice(j * ts, (j + 1) * ts)
        a = a_refs[0][r, :] if n_in == 1 else jnp.concatenate([ar[r, :] for ar in a_refs], 1)
        _ln_epilogue(_dot(a, w_ref[...]), r, x_ref, gate_ref, lng_ref, lnb_ref, alpha, xo_ref, ho_ref, sh_ref, sc_ref)


def _proj_ln(a_list, w, x2, mod3, layer, gate_chunk, row_fn, lng, lnb, alpha, tm, nsub, with_next):
    m, d = x2.shape
    row = lambda width: pl.BlockSpec((tm, width), lambda i: (i, 0))
    vec = pl.BlockSpec((1, d), lambda i: (0, 0))
    w_spec = pl.BlockSpec((None,) + w.shape[1:], lambda i: (layer, 0, 0), pipeline_mode=pl.Buffered(1))
    in_specs = [row(a.shape[1]) for a in a_list] + [w_spec, row(d), _mod_spec(d, layer, gate_chunk, row_fn)]
    args = list(a_list) + [w, x2, mod3]
    if with_next:
        nxt = (layer, gate_chunk + 1) if gate_chunk < 5 else (layer + 1, 0)
        in_specs += [_mod_spec(d, nxt[0], nxt[1], row_fn), _mod_spec(d, nxt[0], nxt[1] + 1, row_fn)]
        args += [mod3, mod3]
    in_specs += [vec, vec]
    args += [lng, lnb]
    out_specs = [row(d)] + ([row(d)] if with_next else [])
    out_shape = [jax.ShapeDtypeStruct((m, d), _F32)] + ([jax.ShapeDtypeStruct((m, d), _BF)] if with_next else [])
    k_in = sum(a.shape[1] for a in a_list)
    est_mb = (2 * k_in * d + 2 * tm * (2 * k_in + 8 * d + (2 * d if with_next else 0))) / 2 ** 20
    vmem_mb = VMEM_LIMIT_MB if est_mb + 4 <= VMEM_LIMIT_MB else VMEM_LIMIT_MAX_MB
    out = pl.pallas_call(
        functools.partial(_proj_ln_body, n_in=len(a_list), alpha=alpha, with_next=with_next, nsub=nsub),
        grid=(m // tm,),
        in_specs=in_specs, out_specs=out_specs, out_shape=out_shape,
        compiler_params=_cparams(("arbitrary",), vmem_mb),
    )(*args)
    return (out[0], out[1]) if with_next else (out[0], None)


def _glu_chunks(t):
    rc = _tile(t, 512)
    return [(i * rc, rc) for i in range(t // rc)]


def _glu_body(h_ref, wa_ref, wu_ref, cw_ref, cb_ref, o_ref, *rest, t, cast):
    if cast:
        wab, wub, a_scr = rest

        @pl.when(pl.program_id(1) == 0)
        def _():
            wab[...] = wa_ref[...].astype(_BF)
            wub[...] = wu_ref[...].astype(_BF)
    else:
        (a_scr,) = rest
        wab, wub = wa_ref, wu_ref

    nseq = a_scr.shape[0]
    units = [(q, r0, rc) for q in range(nseq) for r0, rc in _glu_chunks(t)]
    halo = 8
    tn = o_ref.shape[1]
    zeros = jnp.zeros((halo, tn), _F32)
    for q in range(nseq):
        a_scr[q, 0:halo, :] = zeros
        a_scr[q, halo + t:, :] = zeros
    w0, w1, w2 = cw_ref[0:1, :], cw_ref[1:2, :], cw_ref[2:3, :]
    bias = cb_ref[...]
    u_prev = None
    for i in range(len(units) + 1):
        u_cur = None
        if i < len(units):
            q, r0, rc = units[i]
            hc = h_ref[q * t + r0:q * t + r0 + rc, :]
            a_scr[q, halo + r0:halo + r0 + rc, :] = _dot(hc, wab[...])
            u_cur = _dot(hc, wub[...])
        if i > 0:
            q, r0, rc = units[i - 1]
            base = halo + r0
            lo = a_scr[q, base - 1:base - 1 + rc, :]
            mid = a_scr[q, base:base + rc, :]
            hi = a_scr[q, base + 1:base + 1 + rc, :]
            acc = bias + lo * w0 + mid * w1 + hi * w2
            o_ref[q * t + r0:q * t + r0 + rc, :] = (_silu(acc) * u_prev).astype(o_ref.dtype)
        u_prev = u_cur


def _glu(h, w_up, layer, conv_w, conv_b, b, t, tn, nseq=1, w_bf16=None):
    d = h.shape[1]
    dff = conv_w.shape[1]
    nj = dff // tn
    cast = w_bf16 is None
    g_spec = pl.BlockSpec((nseq * t, tn), lambda j, bi: (bi, j))
    g_shape = jax.ShapeDtypeStruct((b * t, dff), _BF)
    wb_spec = pl.BlockSpec((None, d, tn), lambda j, bi: (j, 0, 0))
    if cast:
        w_args = (w_up, w_up)
        w_specs = [pl.BlockSpec((None, d, tn), lambda j, bi: (layer, 0, j)),
                   pl.BlockSpec((None, d, tn), lambda j, bi: (layer, 0, nj + j))]
        wb_shape = jax.ShapeDtypeStruct((nj, d, tn), _BF)
        out_specs, out_shape = [g_spec, wb_spec, wb_spec], [g_shape, wb_shape, wb_shape]
    else:
        w_args, w_specs = tuple(w_bf16), [wb_spec, wb_spec]
        out_specs, out_shape = g_spec, g_shape
    out = pl.pallas_call(
        functools.partial(_glu_body, t=t, cast=cast),
        grid=(nj, b // nseq),
        in_specs=[pl.BlockSpec((nseq * t, d), lambda j, bi: (bi, 0)), *w_specs,
                  pl.BlockSpec((CONV_W, tn), lambda j, bi: (0, j)),
                  pl.BlockSpec((1, tn), lambda j, bi: (0, j))],
        out_specs=out_specs, out_shape=out_shape,
        scratch_shapes=[pltpu.VMEM((nseq, t + 16, tn), _F32)],
        compiler_params=_cparams(("arbitrary", "arbitrary")),
    )(h, *w_args, conv_w, conv_b)
    return (out[0], (out[1], out[2])) if cast else out


PACK_SRC_ROWS = 64
PACK_CHUNKS = 4


def _pack_plan():
    src = dict(zip(("na_q", "na_k", "na_v", "cq", "ckv", "kpe", "r_q", "r_k", "r_v", "r_g"),
                   [int(i) for i in np.cumsum((0,) + IN_SPLITS[:-1])]))
    plan = np.zeros((PACKED_IN_WIDTH // LANE, 3), np.int32)
    segments = [(CB_CQ, "cq", MLA_Q_RANK + MLA_KV_RANK, 0), (CB_NAQ, "na_q", NA_WIDTH, 1),
                (CB_NAK, "na_k", 2 * NA_WIDTH, 0), (CB_RQ, "r_q", RET_WIDTH, 0), (CB_RK, "r_k", RET_WIDTH, 2),
                (CB_RV, "r_v", 2 * RET_WIDTH, 0), (CB_KPE, "kpe", MLA_ROPE, 0)]
    for dst, name, width, kind in segments:
        for off in range(0, width, LANE):
            row = src[name] + off
            assert row % PACK_SRC_ROWS == 0
            plan[dst + off // LANE] = (row // PACK_SRC_ROWS, kind, min(LANE, width - off))
    return plan


def _pack_w_in_body(plan_ref, *refs):
    q_scale = HEAD_DIM ** -0.5 * LOG2E
    k_scale = HEAD_DIM ** -0.5
    o_ref = refs[-1]
    row = lax.broadcasted_iota(jnp.int32, (LANE, 1), 0)
    for c in range(PACK_CHUNKS):
        chunk = pl.program_id(0) * PACK_CHUNKS + c
        kind, valid = plan_ref[chunk, 1], plan_ref[chunk, 2]
        scale = jnp.where(kind == 1, q_scale, jnp.where(kind == 2, k_scale, 1.0))
        x = jnp.concatenate([refs[2 * c][...], refs[2 * c + 1][...]], 0)
        x = jnp.where(row < valid, x * scale, 0.0)
        o_ref[:, c * LANE:(c + 1) * LANE] = x.T.astype(o_ref.dtype)


def _pack_w_in(w_in, layer, tn):
    depth, d, n = w_in.shape
    w_t = jnp.swapaxes(w_in, 1, 2)
    step_cols = PACK_CHUNKS * LANE
    per_block = tn // step_cols

    def src_spec(c, second):
        return pl.BlockSpec((None, PACK_SRC_ROWS, d),
                            lambda j, plan: (layer, plan[j * PACK_CHUNKS + c, 0] + second, 0))

    return pl.pallas_call(
        _pack_w_in_body,
        grid_spec=pltpu.PrefetchScalarGridSpec(
            num_scalar_prefetch=1,
            grid=(PACKED_IN_WIDTH // step_cols,),
            in_specs=[src_spec(c, second) for c in range(PACK_CHUNKS) for second in (0, 1)],
            out_specs=pl.BlockSpec((None, d, step_cols), lambda j, plan: (j // per_block, 0, j % per_block))),
        out_shape=jax.ShapeDtypeStruct((PACKED_IN_WIDTH // tn, d, tn), _BF),
        compiler_params=_cparams(("arbitrary",)),
    )(jnp.asarray(_pack_plan()), *([w_t] * (2 * PACK_CHUNKS)))


def _pack_w_uq(w):
    r = w.shape[0]
    w = w.reshape(r, MLA_HEADS, MLA_NOPE + MLA_ROPE) * ((MLA_NOPE + MLA_ROPE) ** -0.5 * LOG2E)
    w = jnp.pad(w, ((0, 0), (0, 0), (0, MLA_QK_PAD - MLA_NOPE - MLA_ROPE)))
    return w.reshape(r, MLA_HEADS * MLA_QK_PAD).astype(_BF)


def kernel(x, c, ctx, c_ctx, w_ada, b_ada, w_in, mla_q_norm, mla_kv_norm, w_uq, w_ukv, na_rpb, ret_decay,
           w_o, ln1_g, ln1_b, w_up, conv_w, conv_b, w_down, ln2_g, ln2_b):
    b, t, d = x.shape
    cl = ctx.shape[1]
    depth = w_ada.shape[0]
    dff = conv_w.shape[-1]
    assert b + 1 <= MOD_ROWS
    alpha = (2 * depth) ** 0.25
    ctx_row = b

    cvec = jnp.concatenate([c, c_ctx[None], jnp.zeros((MOD_ROWS - b - 1, d), c.dtype)], 0)
    mod3 = _ada(cvec, w_ada, b_ada)

    tm_in = _tile(b * cl, _tile(t, 1024))
    ctx0 = b * t
    lat_row = lambda tm: (lambda i: i // (t // tm))
    ctx_rowf = lambda i: ctx_row

    rope_pe = _rope_tables(t, MLA_ROPE)
    rope_ret = _rope_tables(t, HEAD_DIM)
    id_pe = _rope_tables(cl, MLA_ROPE, identity=True)
    id_ret = _rope_tables(cl, HEAD_DIM, identity=True)

    xf = x.reshape(b * t, d)
    cf = ctx.reshape(b * cl, d)
    h1, hc1 = xf, cf

    tq = _tile(t, 2048)
    tqc = _tile(cl, 256)
    tn_ff = _tile(dff, 512)
    zero_state = jnp.zeros((b, RET_HEADS, HEAD_DIM, HEAD_DIM), _F32)
    w_o_b = w_o.astype(_BF)
    w_down_b = w_down.astype(_BF)

    for l in range(depth):
        last = l == depth - 1
        w_in_p = _pack_w_in(w_in, l, 1024)
        w_uq_p = _pack_w_uq(w_uq[l])
        w_ukv_p = w_ukv[l].astype(_BF)
        gq = mla_q_norm[l][None]
        gkv = mla_kv_norm[l][None]
        lg = jnp.log1p(-jnp.exp2(ret_decay[l].astype(_F32)))
        bias = _na_col_bias(na_rpb[l])

        p = _in_proj(h1, hc1, w_in_p, tm_in, t, mod=(mod3, 0, ctx_row) if l == 0 else None)

        qc_m, kc_m, vc_m = _mla_qkv(p, gq, gkv, w_uq_p, w_ukv_p, id_pe, b, cl, _tile(cl, 1024), row0=ctx0)
        q_m, kl_m, vl_m = _mla_qkv(p, gq, gkv, w_uq_p, w_ukv_p, rope_pe, b, t, _tile(t, 1024))
        y_mla = _dense_attn(
            q_m, pl.BlockSpec((None, None, tq, MLA_QK_PAD), lambda bi, h, i: (bi, h, i, 0)),
            [(kc_m, _head4_spec(cl, MLA_QK_PAD), vc_m, _head4_spec(VT_ROWS, cl)),
             (kl_m, _head4_spec(t, MLA_QK_PAD), vl_m, _head4_spec(VT_ROWS, t))],
            b, MLA_HEADS, t // tq, tq, MLA_V, nsub=tq // _tile(tq, 256))

        y_na = _na_attn(p, bias, b, t, cl)

        yc_ret, s_f, s_b = _retention(p, lg, zero_state, zero_state, id_ret, b, cl, row0=ctx0)
        y_ret, _, _ = _retention(p, lg, s_f, s_b, rope_ret, b, t)

        tm_o = _tile(t, 512)
        x_new, h2 = _proj_ln([y_na, y_mla, y_ret], w_o_b, xf, mod3, l, 2, lat_row(tm_o),
                             ln1_g[l][None], ln1_b[l][None], alpha, tm_o, tm_o // _tile(tm_o, 256), with_next=True)

        tm_d = _tile(t, 512)

        if not last:
            yc_mla = _dense_attn(
                qc_m, pl.BlockSpec((None, None, tqc, MLA_QK_PAD), lambda bi, h, i: (bi, h, i, 0)),
                [(kc_m, _head4_spec(cl, MLA_QK_PAD), vc_m, _head4_spec(VT_ROWS, cl))],
                b, MLA_HEADS, cl // tqc, tqc, MLA_V, nsub=1)
            nbc = cl // tqc
            cq0, ck0 = ctx0 // tqc, ctx0 // cl
            yc_na = _dense_attn(
                p, pl.BlockSpec((tqc, LANE), lambda bi, h, i: (cq0 + bi * nbc + i, CB_NAQ + h)),
                [(p, pl.BlockSpec((cl, LANE), lambda bi, h, i: (ck0 + bi, CB_NAK + h)),
                  p, pl.BlockSpec((cl, LANE), lambda bi, h, i: (ck0 + bi, CB_NAV + h)))],
                b, NA_HEADS, nbc, tqc, HEAD_DIM, nsub=1, v_row_major=True)
            tm_oc = _tile(cl, 256)
            c_new, hc2 = _proj_ln([yc_na, yc_mla, yc_ret], w_o_b, cf, mod3, l, 2, ctx_rowf,
                                  ln1_g[l][None], ln1_b[l][None], alpha, tm_oc, 1, with_next=True)

        g_l, w_up_b = _glu(h2, w_up, l, conv_w[l], conv_b[l][None], b, t, tn_ff)
        xf, h1 = _proj_ln([g_l], w_down_b, x_new, mod3, l, 5, lat_row(tm_d), ln2_g[l][None], ln2_b[l][None],
                          alpha, tm_d, tm_d // _tile(tm_d, 256), with_next=not last)
        if not last:
            g_c = _glu(hc2, w_up, l, conv_w[l], conv_b[l][None], b, cl, tn_ff, nseq=b, w_bf16=w_up_b)
            tm_dc = _tile(cl, 256)
            cf, hc1 = _proj_ln([g_c], w_down_b, c_new, mod3, l, 5, ctx_rowf, ln2_g[l][None], ln2_b[l][None],
                               alpha, tm_dc, 1, with_next=True)

    return xf.reshape(b, t, d)
```

```python
import functools

import numpy as np
import jax
import jax.numpy as jnp
from jax import lax
from jax.experimental import pallas as pl
from jax.experimental.pallas import tpu as pltpu

GRID_W = 64
HEAD_DIM = 128
NA_HEADS = 6
MLA_HEADS = 5
RET_HEADS = 5
NA_WIDTH = NA_HEADS * HEAD_DIM
MLA_WIDTH = MLA_HEADS * HEAD_DIM
RET_WIDTH = RET_HEADS * HEAD_DIM
NA_WIN_R = 8
NA_WIN_C = 16
MLA_Q_RANK = 512
MLA_KV_RANK = 512
MLA_NOPE = 128
MLA_ROPE = 64
MLA_V = 128
MLA_QK_PAD = 256
RET_CHUNK = 128
CONV_W = 3
ROPE_BASE = 10000.0
LN_EPS = 1e-5
RMS_EPS = 1e-6
NEG_INF = -1e30
LOG2E = 1.4426950408889634
IN_SPLITS = (NA_WIDTH, NA_WIDTH, NA_WIDTH, MLA_Q_RANK, MLA_KV_RANK, MLA_ROPE,
             RET_WIDTH, RET_WIDTH, RET_WIDTH, RET_WIDTH)

LANE = 128
CB_CQ, CB_CKV, CB_NAQ, CB_NAK, CB_NAV = 0, 4, 8, 14, 20
CB_RQ, CB_RK, CB_RV, CB_RG, CB_KPE = 26, 31, 36, 41, 46
PACKED_IN_WIDTH = 48 * LANE
MOD_ROWS = 8
VMEM_LIMIT_MB = 56

_BF = jnp.bfloat16
_F32 = jnp.float32


def _cparams(sem, vmem_mb=VMEM_LIMIT_MB):
    return pltpu.CompilerParams(dimension_semantics=sem, vmem_limit_bytes=vmem_mb << 20)


def _tile(n, pref):
    t = min(n, pref)
    while n % t:
        t //= 2
    return t


def _dot(a, b):
    return jnp.dot(a, b, preferred_element_type=_F32)


def _dot_nt(a, b):
    return lax.dot_general(a, b, (((1,), (1,)), ((), ())), preferred_element_type=_F32)


def _silu(x):
    return x / (1.0 + jnp.exp(-x))


def _ada_body(c_ref, w_ref, b_ref, o_ref):
    part = _dot(_silu(c_ref[...]).astype(_BF), w_ref[0].astype(_BF))

    @pl.when(pl.program_id(1) == 0)
    def _():
        o_ref[0] = part + b_ref[0]

    @pl.when(pl.program_id(1) > 0)
    def _():
        o_ref[0] += part


def _ada(cvec, w_ada, b_ada):
    depth, d, n = w_ada.shape
    tk = _tile(d, 256)
    out = pl.pallas_call(
        _ada_body,
        grid=(depth, d // tk),
        in_specs=[pl.BlockSpec((MOD_ROWS, tk), lambda l, k: (0, k)),
                  pl.BlockSpec((1, tk, n), lambda l, k: (l, k, 0)),
                  pl.BlockSpec((1, 1, n), lambda l, k: (l, 0, 0))],
        out_specs=pl.BlockSpec((1, MOD_ROWS, n), lambda l, k: (l, 0, 0)),
        out_shape=jax.ShapeDtypeStruct((depth, MOD_ROWS, n), _F32),
        compiler_params=_cparams(("arbitrary", "arbitrary")),
    )(cvec, w_ada, b_ada.reshape(depth, 1, n))
    return out.reshape(depth * MOD_ROWS, 1, n)


def _mod_spec(d, layer, chunk, row_fn):
    return pl.BlockSpec((1, 1, d), lambda i, *_: (layer * MOD_ROWS + row_fn(i), 0, chunk))


def _in_proj_body(*refs, n_lat, modulated):
    if modulated:
        lat_ref, ctx_ref, shl_ref, scl_ref, shc_ref, scc_ref, w_ref, o_ref, a_scr = refs
    else:
        lat_ref, ctx_ref, w_ref, o_ref, a_scr = refs
    i = pl.program_id(0)

    def stage(src_ref, sh_ref, sc_ref):
        x = src_ref[...]
        if modulated:
            x = x * (1.0 + sc_ref[0]) + sh_ref[0]
        a_scr[...] = x.astype(a_scr.dtype)

    @pl.when((pl.program_id(1) == 0) & (i < n_lat))
    def _():
        stage(lat_ref, shl_ref if modulated else None, scl_ref if modulated else None)

    @pl.when((pl.program_id(1) == 0) & (i >= n_lat))
    def _():
        stage(ctx_ref, shc_ref if modulated else None, scc_ref if modulated else None)

    o_ref[...] = _dot(a_scr[...], w_ref[...]).astype(o_ref.dtype)


def _in_proj(lat, ctx, w, tm, t, mod=None):
    d = lat.shape[1]
    nj, _, tn = w.shape
    n_lat, n_ctx = lat.shape[0] // tm, ctx.shape[0] // tm
    lat_i = lambda i: jnp.minimum(i, n_lat - 1)
    in_specs = [pl.BlockSpec((tm, d), lambda i, j: (lat_i(i), 0)),
                pl.BlockSpec((tm, d), lambda i, j: (jnp.maximum(i - n_lat, 0), 0), pipeline_mode=pl.Buffered(1))]
    args = [lat, ctx]
    if mod is not None:
        mod3, layer, ctx_row = mod
        lat_row = lambda i: lat_i(i) // (t // tm)
        in_specs += [_mod_spec(d, layer, 0, lat_row), _mod_spec(d, layer, 1, lat_row),
                     _mod_spec(d, layer, 0, lambda i: ctx_row), _mod_spec(d, layer, 1, lambda i: ctx_row)]
        args += [mod3] * 4
    return pl.pallas_call(
        functools.partial(_in_proj_body, n_lat=n_lat, modulated=mod is not None),
        grid=(n_lat + n_ctx, nj),
        in_specs=in_specs + [pl.BlockSpec((None, d, tn), lambda i, j: (j, 0, 0))],
        out_specs=pl.BlockSpec((tm, tn), lambda i, j: (i, j)),
        out_shape=jax.ShapeDtypeStruct(((n_lat + n_ctx) * tm, nj * tn), _BF),
        scratch_shapes=[pltpu.VMEM((tm, d), _BF)],
        compiler_params=_cparams(("arbitrary", "arbitrary")),
    )(*args, w)


def _rope_tables(t, rot_dim, identity=False):
    pos = np.arange(t)
    row = (pos // GRID_W).astype(np.float32)
    col = (pos % GRID_W).astype(np.float32)
    nf = rot_dim // 4
    inv = (ROPE_BASE ** (-np.arange(nf, dtype=np.float32) / nf)).astype(np.float32)
    ar = row[:, None] * inv[None]
    ac = col[:, None] * inv[None]
    ang = np.concatenate([ar, ar, ac, ac], -1)
    cos, sin = np.cos(ang), np.sin(ang)
    if identity:
        cos, sin = np.ones_like(cos), np.zeros_like(sin)
    first = (np.arange(rot_dim) % (rot_dim // 2)) < nf
    sin_a = np.where(first, -sin, 0.0)
    sin_b = np.where(first, 0.0, sin)
    pad = LANE - rot_dim
    if pad:
        cos = np.pad(cos, ((0, 0), (0, pad)), constant_values=1.0)
        sin_a = np.pad(sin_a, ((0, 0), (0, pad)))
        sin_b = np.pad(sin_b, ((0, 0), (0, pad)))
    return tuple(jnp.asarray(a, _F32) for a in (cos, sin_a, sin_b))


def _rope(x, cos, sin_a, sin_b, nf):
    return x * cos + pltpu.roll(x, LANE - nf, 1) * sin_a + pltpu.roll(x, nf, 1) * sin_b


def _rms(x, g):
    return x * lax.rsqrt(jnp.mean(x * x, -1, keepdims=True) + RMS_EPS) * g


VT_ROWS = HEAD_DIM + 16


def _values_t(v):
    n = v.shape[0]
    ones = (lax.broadcasted_iota(jnp.int32, (VT_ROWS - HEAD_DIM, n), 0) == 0).astype(_BF)
    return jnp.concatenate([v.astype(_F32).T.astype(_BF), ones], 0)


def _mla_qkv_body(cq_ref, ckv_ref, kpe_ref, gq_ref, gkv_ref, wq_ref, wkv_ref, cos_ref, sa_ref, sb_ref,
                  *rest):
    if len(rest) == 5:
        wo_ref, q_ref, k_ref, vt_ref, wob = rest
        wob[...] = wo_ref[...].astype(_BF)
    else:
        q_ref, k_ref, vt_ref = rest
    cos, sa, sb = cos_ref[...], sa_ref[...], sb_ref[...]
    nf = MLA_ROPE // 4
    xq = _rms(cq_ref[...].astype(_F32), gq_ref[...]).astype(_BF)
    xkv = _rms(ckv_ref[...].astype(_F32), gkv_ref[...]).astype(_BF)
    pe = _rope(kpe_ref[...].astype(_F32), cos, sa, sb, nf).astype(_BF)
    width = MLA_NOPE + MLA_V
    for h in range(MLA_HEADS):
        q = _dot(xq, wq_ref[:, h * MLA_QK_PAD:(h + 1) * MLA_QK_PAD])
        q_ref[0, h, :, 0:LANE] = q[:, :LANE].astype(_BF)
        q_ref[0, h, :, LANE:] = _rope(q[:, LANE:], cos, sa, sb, nf).astype(_BF)
        kv = _dot(xkv, wkv_ref[:, h * width:(h + 1) * width])
        k_ref[0, h, :, 0:LANE] = kv[:, :MLA_NOPE].astype(_BF)
        k_ref[0, h, :, LANE:] = pe
        vt_ref[0, h] = _values_t(kv[:, MLA_NOPE:])


def _mla_qkv(p, gq, gkv, wq, wkv, tables, b, t, tm, row0=0, cast=None):
    nb = t // tm
    blk0 = row0 // tm
    extra_in, extra_args, extra_out, extra_shape = [], [], [], []
    if cast is not None:
        w, layer = cast
        slab = w.shape[1] // (b * nb)
        assert w.shape[1] % (b * nb) == 0 and slab % 16 == 0
        extra_in = [pl.BlockSpec((None, slab, w.shape[2]), lambda i: (layer, i, 0))]
        extra_args = [w]
        extra_out = [pl.BlockSpec((slab, w.shape[2]), lambda i: (i, 0))]
        extra_shape = [jax.ShapeDtypeStruct(w.shape[1:], _BF)]
    tab_spec = pl.BlockSpec((tm, LANE), lambda i: (i % nb, 0))
    head_spec = lambda rows, cols, ri, ci: pl.BlockSpec(
        (1, MLA_HEADS, rows, cols), lambda i: (i // nb, 0, (i % nb) * ri, (i % nb) * ci))
    return pl.pallas_call(
        _mla_qkv_body,
        grid=(b * nb,),
        in_specs=[pl.BlockSpec((tm, MLA_Q_RANK), lambda i: (blk0 + i, CB_CQ // 4)),
                  pl.BlockSpec((tm, MLA_KV_RANK), lambda i: (blk0 + i, CB_CKV // 4)),
                  pl.BlockSpec((tm, LANE), lambda i: (blk0 + i, CB_KPE)),
                  pl.BlockSpec((1, MLA_Q_RANK), lambda i: (0, 0)),
                  pl.BlockSpec((1, MLA_KV_RANK), lambda i: (0, 0)),
                  pl.BlockSpec(wq.shape, lambda i: (0, 0)),
                  pl.BlockSpec(wkv.shape, lambda i: (0, 0)),
                  tab_spec, tab_spec, tab_spec] + extra_in,
        out_specs=[head_spec(tm, MLA_QK_PAD, 1, 0), head_spec(tm, MLA_QK_PAD, 1, 0),
                   head_spec(VT_ROWS, tm, 0, 1)] + extra_out,
        out_shape=[jax.ShapeDtypeStruct((b, MLA_HEADS, t, MLA_QK_PAD), _BF),
                   jax.ShapeDtypeStruct((b, MLA_HEADS, t, MLA_QK_PAD), _BF),
                   jax.ShapeDtypeStruct((b, MLA_HEADS, VT_ROWS, t), _BF)] + extra_shape,
        compiler_params=_cparams(("arbitrary",)),
    )(p, p, p, gq, gkv, wq, wkv, *tables, *extra_args)


def _skewed(n, stages):
    carry = [None] * n
    for step in range(n + len(stages) - 1):
        for s, stage in enumerate(stages):
            j = step - s
            if 0 <= j < n:
                carry[j] = stage(j, carry[j])


def _attn_stages(scores_fn, values_fn, store_fn):
    def s_scores(j, _):
        return scores_fn(j)

    def s_max(j, sc):
        return sc, functools.reduce(jnp.maximum, [jnp.max(s, 0, keepdims=True) for s in sc])

    def s_exp(j, c):
        sc, m = c
        return [jnp.exp2(s - m).astype(_BF) for s in sc]

    def s_pv(j, p):
        acc = functools.reduce(jnp.add, [_dot(vt, pi) for vt, pi in zip(values_fn(j), p)])
        store_fn(j, (acc[0:HEAD_DIM, :] / acc[HEAD_DIM:HEAD_DIM + 1, :]).T)

    return [s_scores, s_max, s_exp, s_pv]


def _attn_body(*refs, nseg, nsub, v_row_major):
    q_ref, o_ref = refs[0], refs[-1]
    ks = [refs[1 + 2 * i][...] for i in range(nseg)]
    vts = [refs[2 + 2 * i][...] for i in range(nseg)]
    if v_row_major:
        vts = [_values_t(v) for v in vts]
    ts = q_ref.shape[0] // nsub
    rows = lambda j: slice(j * ts, (j + 1) * ts)

    def store(j, out):
        o_ref[rows(j), :] = out.astype(o_ref.dtype)

    _skewed(nsub, _attn_stages(lambda j: [_dot_nt(k, q_ref[rows(j), :]) for k in ks], lambda j: vts, store))


def _dense_attn(q, q_spec, kv, b, heads, tq_blocks, tq, dv, nsub, v_row_major=False):
    args, specs = [q], [q_spec]
    for k, ks, v, vs in kv:
        args += [k, v]
        specs += [ks, vs]
    return pl.pallas_call(
        functools.partial(_attn_body, nseg=len(kv), nsub=nsub, v_row_major=v_row_major),
        grid=(b, heads, tq_blocks),
        in_specs=specs,
        out_specs=pl.BlockSpec((tq, dv), lambda bi, h, i: (bi * tq_blocks + i, h)),
        out_shape=jax.ShapeDtypeStruct((b * tq_blocks * tq, heads * dv), _BF),
        compiler_params=_cparams(("arbitrary", "arbitrary", "arbitrary")),
    )(*args)


def _head4_spec(rows, cols):
    return pl.BlockSpec((None, None, rows, cols), lambda bi, h, i: (bi, h, 0, 0))


NA_GROUP = 4
NA_KEY_ROWS = 12


def _na_group_window(gi, rows):
    ngroups = rows // NA_GROUP
    if gi == 0:
        return 0, 0
    if gi == ngroups - 1:
        return 2, rows - NA_KEY_ROWS
    return 1, gi * NA_GROUP - NA_WIN_R // 2


NA_DR = 2 * NA_WIN_R - 1


def _na_col_bias(rpb):
    c = np.arange(GRID_W)
    col_start = np.clip(c - NA_WIN_C // 2, 0, GRID_W - NA_WIN_C)
    kc = np.arange(GRID_W)
    in_win = (kc[None, :] >= col_start[:, None]) & (kc[None, :] < col_start[:, None] + NA_WIN_C)
    dc = np.clip(kc[None, :] - c[:, None] + NA_WIN_C - 1, 0, 2 * NA_WIN_C - 2)
    col_sel = (dc[:, :, None] == np.arange(2 * NA_WIN_C - 1)).astype(np.float32)
    toep = jnp.einsum('hrj,cqj->hrqc', rpb.astype(_F32), col_sel, precision=lax.Precision.HIGHEST)
    toep = jnp.where(in_win.T[None, None], toep * LOG2E, NEG_INF)
    masked = jnp.full((rpb.shape[0], 1, GRID_W, GRID_W), NEG_INF, _F32)
    return jnp.concatenate([toep, masked], 1)


def _na_block_index():
    g = np.arange(NA_GROUP)[:, None]
    kr = np.arange(NA_KEY_ROWS)[None, :]
    half = NA_WIN_R // 2
    dr = np.stack([kr - g + NA_WIN_R - 1, kr - g + NA_WIN_R - 1 - half, kr - g - 1])
    w0 = np.stack([0 * g, g, half + 0 * g])
    row_ok = (kr[None] >= w0) & (kr[None] < w0 + NA_WIN_R)
    assert np.all((dr[row_ok] >= 0) & (dr[row_ok] < NA_DR))
    return np.where(row_ok, dr, NA_DR)


def _na_body(q_ref, k_ref, v_ref, kc_ref, vc_ref, cb_ref, o_ref, bias_ref, vt_ref, *, rows):
    @pl.when(pl.program_id(1) == 0)
    def _():
        idx = _na_block_index()
        for typ in range(3):
            for kr in range(NA_KEY_ROWS):
                for gp in range(NA_GROUP // 2):
                    pair = jnp.concatenate([cb_ref[int(idx[typ, 2 * gp, kr])], cb_ref[int(idx[typ, 2 * gp + 1, kr])]], 1)
                    bias_ref[typ, kr * GRID_W:(kr + 1) * GRID_W, gp * 2 * GRID_W:(gp + 1) * 2 * GRID_W] = pair

    vt_ref[...] = _values_t(v_ref[...])
    kc = kc_ref[...]
    vct = _values_t(vc_ref[...])
    gq = NA_GROUP * GRID_W

    def band(gi):
        ws = _na_group_window(gi, rows)[1]
        return slice(ws * GRID_W, (ws + NA_KEY_ROWS) * GRID_W)

    def scores(gi):
        q = q_ref[gi * gq:(gi + 1) * gq, :]
        return [_dot_nt(k_ref[band(gi), :], q) + bias_ref[_na_group_window(gi, rows)[0]], _dot_nt(kc, q)]

    def store(gi, out):
        o_ref[gi * gq:(gi + 1) * gq, :] = out.astype(o_ref.dtype)

    _skewed(rows // NA_GROUP, _attn_stages(scores, lambda gi: [vt_ref[:, band(gi)], vct], store))


def _na_attn(p, bias, b, t, c):
    rows = t // GRID_W
    cblk0 = b * t // c
    assert t % GRID_W == 0 and rows % NA_GROUP == 0 and rows >= NA_KEY_ROWS + NA_GROUP and 2 * GRID_W == LANE
    return pl.pallas_call(
        functools.partial(_na_body, rows=rows),
        grid=(NA_HEADS, b),
        in_specs=[pl.BlockSpec((t, LANE), lambda h, bi: (bi, CB_NAQ + h)),
                  pl.BlockSpec((t, LANE), lambda h, bi: (bi, CB_NAK + h)),
                  pl.BlockSpec((t, LANE), lambda h, bi: (bi, CB_NAV + h)),
                  pl.BlockSpec((c, LANE), lambda h, bi: (cblk0 + bi, CB_NAK + h)),
                  pl.BlockSpec((c, LANE), lambda h, bi: (cblk0 + bi, CB_NAV + h)),
                  pl.BlockSpec((None, NA_DR + 1, GRID_W, GRID_W), lambda h, bi: (h, 0, 0, 0))],
        out_specs=pl.BlockSpec((t, LANE), lambda h, bi: (bi, h)),
        out_shape=jax.ShapeDtypeStruct((b * t, NA_WIDTH), _BF),
        scratch_shapes=[pltpu.VMEM((3, NA_KEY_ROWS * GRID_W, NA_GROUP * GRID_W), _F32),
                        pltpu.VMEM((VT_ROWS, t), _BF)],
        compiler_params=_cparams(("arbitrary", "arbitrary")),
    )(p, p, p, p, p, bias)


def _ret_body(lg_ref, q_ref, k_ref, v_ref, g_ref, s0f_ref, s0b_ref, cos_ref, sa_ref, sb_ref,
              o_ref, sf_ref, sbo_ref, kr_scr, u_hist, st_hist, *, n):
    h = pl.program_id(0)
    lgf = lg_ref[0, h]
    lgb = lg_ref[1, h]
    L = RET_CHUNK
    pos_c = lax.broadcasted_iota(jnp.int32, (L, 1), 0).astype(_F32)
    pos_r = lax.broadcasted_iota(jnp.int32, (1, L), 1).astype(_F32)
    diff = pos_c - pos_r
    decay = jnp.where(diff > 0, jnp.exp(jnp.maximum(diff, 0.0) * lgf),
                      jnp.where(diff < 0, jnp.exp(jnp.maximum(-diff, 0.0) * lgb), 2.0))
    qdec_f = jnp.exp((pos_c + 1.0) * lgf)
    kdec_f = jnp.exp((L - 1.0 - pos_c) * lgf)
    qdec_b = jnp.exp((L - pos_c) * lgb)
    kdec_b = jnp.exp(pos_c * lgb)
    cd_f = jnp.exp(L * lgf)
    cd_b = jnp.exp(L * lgb)
    nf = HEAD_DIM // 4

    def rows_of(i):
        return slice(i * L, (i + 1) * L)

    def roped(ref, i):
        r = rows_of(i)
        return _rope(ref[r, :].astype(_F32), cos_ref[r, :], sa_ref[r, :], sb_ref[r, :], nf)

    for i in range(n):
        k = roped(k_ref, i)
        kr_scr[rows_of(i), :] = k.astype(_BF)
        kd = jnp.concatenate([k * kdec_f, k * kdec_b], 1).astype(_BF)
        u_hist[i] = lax.dot_general(kd, v_ref[rows_of(i), :], (((0,), (0,)), ((), ())),
                                    preferred_element_type=_F32)

    state = s0f_ref[...]
    for i in range(n):
        st_hist[i, 0:HEAD_DIM, :] = state.astype(_BF)
        state = cd_f * state + u_hist[i, 0:HEAD_DIM, :]
    sf_ref[...] = state
    state = s0b_ref[...]
    for i in reversed(range(n)):
        st_hist[i, HEAD_DIM:, :] = state.astype(_BF)
        state = cd_b * state + u_hist[i, HEAD_DIM:, :]
    sbo_ref[...] = state

    for i in range(n):
        r = rows_of(i)
        q = roped(q_ref, i)
        a = _dot_nt(q.astype(_BF), kr_scr[r, :]) * decay
        qd = jnp.concatenate([q * qdec_f, q * qdec_b], 1).astype(_BF)
        o = _dot(a.astype(_BF), v_ref[r, :]) + _dot(qd, st_hist[i])
        mu = jnp.mean(o, -1, keepdims=True)
        oc = o - mu
        var = jnp.mean(oc * oc, -1, keepdims=True)
        y = oc * lax.rsqrt(var + LN_EPS) * _silu(g_ref[r, :].astype(_F32))
        o_ref[r, :] = y.astype(o_ref.dtype)


def _retention(p, lg, s0f, s0b, tables, b, t, row0=0):
    n = t // RET_CHUNK
    blk0 = row0 // t
    col = lambda cb: pl.BlockSpec((t, LANE), lambda h, bi: (blk0 + bi, cb + h))
    st_spec = pl.BlockSpec((None, None, HEAD_DIM, HEAD_DIM), lambda h, bi: (bi, h, 0, 0))
    tab_spec = pl.BlockSpec((t, LANE), lambda h, bi: (0, 0))
    st_shape = jax.ShapeDtypeStruct((b, RET_HEADS, HEAD_DIM, HEAD_DIM), _F32)
    return pl.pallas_call(
        functools.partial(_ret_body, n=n),
        grid=(RET_HEADS, b),
        in_specs=[pl.BlockSpec(memory_space=pltpu.SMEM),
                  col(CB_RQ), col(CB_RK), col(CB_RV), col(CB_RG), st_spec, st_spec,
                  tab_spec, tab_spec, tab_spec],
        out_specs=[pl.BlockSpec((t, LANE), lambda h, bi: (bi, h)), st_spec, st_spec],
        out_shape=[jax.ShapeDtypeStruct((b * t, RET_WIDTH), _BF), st_shape, st_shape],
        scratch_shapes=[pltpu.VMEM((t, HEAD_DIM), _BF),
                        pltpu.VMEM((n, 2 * HEAD_DIM, HEAD_DIM), _F32),
                        pltpu.VMEM((n, 2 * HEAD_DIM, HEAD_DIM), _BF)],
        compiler_params=_cparams(("arbitrary", "arbitrary")),
    )(lg, p, p, p, p, s0f, s0b, *tables)


def _ln_epilogue(y, rows, x_ref, gate_ref, lng_ref, lnb_ref, alpha, xo_ref, ho_ref, sh_ref, sc_ref):
    z = alpha * x_ref[rows, :] + gate_ref[0] * y
    mu = jnp.mean(z, -1, keepdims=True)
    zc = z - mu
    var = jnp.mean(zc * zc, -1, keepdims=True)
    xn = zc * lax.rsqrt(var + LN_EPS) * lng_ref[...] + lnb_ref[...]
    xo_ref[rows, :] = xn
    if ho_ref is not None:
        ho_ref[rows, :] = (xn * (1.0 + sc_ref[0]) + sh_ref[0]).astype(ho_ref.dtype)


def _proj_ln_body(*refs, n_in, alpha, with_next, nsub):
    a_refs, refs = refs[:n_in], refs[n_in:]
    if with_next:
        w_ref, x_ref, gate_ref, sh_ref, sc_ref, lng_ref, lnb_ref, xo_ref, ho_ref = refs
    else:
        w_ref, x_ref, gate_ref, lng_ref, lnb_ref, xo_ref = refs
        sh_ref = sc_ref = ho_ref = None
    ts = x_ref.shape[0] // nsub
    for j in range(nsub):
        r = slice(j * ts, (j + 1) * ts)
        a = a_refs[0][r, :] if n_in == 1 else jnp.concatenate([ar[r, :] for ar in a_refs], 1)
        _ln_epilogue(_dot(a, w_ref[...]), r, x_ref, gate_ref, lng_ref, lnb_ref, alpha, xo_ref, ho_ref, sh_ref, sc_ref)


def _proj_ln(a_list, w, x2, mod3, layer, gate_chunk, row_fn, lng, lnb, alpha, tm, nsub, with_next):
    m, d = x2.shape
    row = lambda width: pl.BlockSpec((tm, width), lambda i: (i, 0))
    vec = pl.BlockSpec((1, d), lambda i: (0, 0))
    w_spec = pl.BlockSpec(w.shape, lambda i: (0, 0), pipeline_mode=pl.Buffered(1))
    in_specs = [row(a.shape[1]) for a in a_list] + [w_spec, row(d), _mod_spec(d, layer, gate_chunk, row_fn)]
    args = list(a_list) + [w, x2, mod3]
    if with_next:
        nxt = (layer, gate_chunk + 1) if gate_chunk < 5 else (layer + 1, 0)
        in_specs += [_mod_spec(d, nxt[0], nxt[1], row_fn), _mod_spec(d, nxt[0], nxt[1] + 1, row_fn)]
        args += [mod3, mod3]
    in_specs += [vec, vec]
    args += [lng, lnb]
    out_specs = [row(d)] + ([row(d)] if with_next else [])
    out_shape = [jax.ShapeDtypeStruct((m, d), _F32)] + ([jax.ShapeDtypeStruct((m, d), _BF)] if with_next else [])
    out = pl.pallas_call(
        functools.partial(_proj_ln_body, n_in=len(a_list), alpha=alpha, with_next=with_next, nsub=nsub),
        grid=(m // tm,),
        in_specs=in_specs, out_specs=out_specs, out_shape=out_shape,
        compiler_params=_cparams(("arbitrary",)),
    )(*args)
    return (out[0], out[1]) if with_next else (out[0], None)


def _glu_chunks(t):
    rc = _tile(t, 512)
    return [(i * rc, rc) for i in range(t // rc)]


def _glu_body(h_ref, wa_ref, wu_ref, *rest, t, cast):
    if cast:
        wd_ref, cw_ref, cb_ref, o_ref, wab, wub, wdb, a_scr = rest
        wdb[...] = wd_ref[...].astype(_BF)

        @pl.when(pl.program_id(1) == 0)
        def _():
            wab[...] = wa_ref[...].astype(_BF)
            wub[...] = wu_ref[...].astype(_BF)
    else:
        cw_ref, cb_ref, o_ref, a_scr = rest
        wab, wub = wa_ref, wu_ref

    nseq = a_scr.shape[0]
    units = [(q, r0, rc) for q in range(nseq) for r0, rc in _glu_chunks(t)]
    halo = 8
    tn = o_ref.shape[1]
    zeros = jnp.zeros((halo, tn), _F32)
    for q in range(nseq):
        a_scr[q, 0:halo, :] = zeros
        a_scr[q, halo + t:, :] = zeros
    w0, w1, w2 = cw_ref[0:1, :], cw_ref[1:2, :], cw_ref[2:3, :]
    bias = cb_ref[...]
    u_prev = None
    for i in range(len(units) + 1):
        u_cur = None
        if i < len(units):
            q, r0, rc = units[i]
            hc = h_ref[q * t + r0:q * t + r0 + rc, :]
            a_scr[q, halo + r0:halo + r0 + rc, :] = _dot(hc, wab[...])
            u_cur = _dot(hc, wub[...])
        if i > 0:
            q, r0, rc = units[i - 1]
            base = halo + r0
            lo = a_scr[q, base - 1:base - 1 + rc, :]
            mid = a_scr[q, base:base + rc, :]
            hi = a_scr[q, base + 1:base + 1 + rc, :]
            acc = bias + lo * w0 + mid * w1 + hi * w2
            o_ref[q * t + r0:q * t + r0 + rc, :] = (_silu(acc) * u_prev).astype(o_ref.dtype)
        u_prev = u_cur


def _glu(h, w_up, w_down, layer, conv_w, conv_b, b, t, tn, nseq=1, w_bf16=None):
    d = h.shape[1]
    dff = conv_w.shape[1]
    nj = dff // tn
    cast = w_bf16 is None
    steps = nj * (b // nseq)
    g_spec = pl.BlockSpec((nseq * t, tn), lambda j, bi: (bi, j))
    g_shape = jax.ShapeDtypeStruct((b * t, dff), _BF)
    wb_spec = pl.BlockSpec((None, d, tn), lambda j, bi: (j, 0, 0))
    if cast:
        w_args = (w_up, w_up)
        w_specs = [pl.BlockSpec((None, d, tn), lambda j, bi: (layer, 0, j)),
                   pl.BlockSpec((None, d, tn), lambda j, bi: (layer, 0, nj + j))]
        wb_shape = jax.ShapeDtypeStruct((nj, d, tn), _BF)
        slab = dff // steps
        assert dff % steps == 0 and slab % 16 == 0
        step = lambda j, bi: j * (b // nseq) + bi
        w_args += (w_down,)
        w_specs.append(pl.BlockSpec((None, slab, w_down.shape[2]), lambda j, bi: (layer, step(j, bi), 0)))
        wd_spec = pl.BlockSpec((slab, w_down.shape[2]), lambda j, bi: (step(j, bi), 0))
        wd_shape = jax.ShapeDtypeStruct(w_down.shape[1:], _BF)
        out_specs, out_shape = [g_spec, wb_spec, wb_spec, wd_spec], [g_shape, wb_shape, wb_shape, wd_shape]
    else:
        w_args, w_specs = tuple(w_bf16), [wb_spec, wb_spec]
        out_specs, out_shape = g_spec, g_shape
    out = pl.pallas_call(
        functools.partial(_glu_body, t=t, cast=cast),
        grid=(nj, b // nseq),
        in_specs=[pl.BlockSpec((nseq * t, d), lambda j, bi: (bi, 0)), *w_specs,
                  pl.BlockSpec((CONV_W, tn), lambda j, bi: (0, j)),
                  pl.BlockSpec((1, tn), lambda j, bi: (0, j))],
        out_specs=out_specs, out_shape=out_shape,
        scratch_shapes=[pltpu.VMEM((nseq, t + 16, tn), _F32)],
        compiler_params=_cparams(("arbitrary", "arbitrary")),
    )(h, *w_args, conv_w, conv_b)
    return (out[0], (out[1], out[2]), out[3]) if cast else out


PACK_SRC_ROWS = 64
PACK_CHUNKS = 4


def _pack_plan():
    src = dict(zip(("na_q", "na_k", "na_v", "cq", "ckv", "kpe", "r_q", "r_k", "r_v", "r_g"),
                   [int(i) for i in np.cumsum((0,) + IN_SPLITS[:-1])]))
    plan = np.zeros((PACKED_IN_WIDTH // LANE, 3), np.int32)
    segments = [(CB_CQ, "cq", MLA_Q_RANK + MLA_KV_RANK, 0), (CB_NAQ, "na_q", NA_WIDTH, 1),
                (CB_NAK, "na_k", 2 * NA_WIDTH, 0), (CB_RQ, "r_q", RET_WIDTH, 0), (CB_RK, "r_k", RET_WIDTH, 2),
                (CB_RV, "r_v", 2 * RET_WIDTH, 0), (CB_KPE, "kpe", MLA_ROPE, 0)]
    for dst, name, width, kind in segments:
        for off in range(0, width, LANE):
            row = src[name] + off
            assert row % PACK_SRC_ROWS == 0
            plan[dst + off // LANE] = (row // PACK_SRC_ROWS, kind, min(LANE, width - off))
    return plan


def _pack_w_in_body(plan_ref, *refs):
    q_scale = HEAD_DIM ** -0.5 * LOG2E
    k_scale = HEAD_DIM ** -0.5
    o_ref = refs[-1]
    row = lax.broadcasted_iota(jnp.int32, (LANE, 1), 0)
    for c in range(PACK_CHUNKS):
        chunk = pl.program_id(0) * PACK_CHUNKS + c
        kind, valid = plan_ref[chunk, 1], plan_ref[chunk, 2]
        scale = jnp.where(kind == 1, q_scale, jnp.where(kind == 2, k_scale, 1.0))
        x = jnp.concatenate([refs[2 * c][...], refs[2 * c + 1][...]], 0)
        x = jnp.where(row < valid, x * scale, 0.0)
        o_ref[:, c * LANE:(c + 1) * LANE] = x.T.astype(o_ref.dtype)


def _pack_w_in(w_in, layer, tn):
    depth, d, n = w_in.shape
    w_t = jnp.swapaxes(w_in, 1, 2)
    step_cols = PACK_CHUNKS * LANE
    per_block = tn // step_cols

    def src_spec(c, second):
        return pl.BlockSpec((None, PACK_SRC_ROWS, d),
                            lambda j, plan: (layer, plan[j * PACK_CHUNKS + c, 0] + second, 0))

    return pl.pallas_call(
        _pack_w_in_body,
        grid_spec=pltpu.PrefetchScalarGridSpec(
            num_scalar_prefetch=1,
            grid=(PACKED_IN_WIDTH // step_cols,),
            in_specs=[src_spec(c, second) for c in range(PACK_CHUNKS) for second in (0, 1)],
            out_specs=pl.BlockSpec((None, d, step_cols), lambda j, plan: (j // per_block, 0, j % per_block))),
        out_shape=jax.ShapeDtypeStruct((PACKED_IN_WIDTH // tn, d, tn), _BF),
        compiler_params=_cparams(("arbitrary",)),
    )(jnp.asarray(_pack_plan()), *([w_t] * (2 * PACK_CHUNKS)))


def _pack_w_uq(w):
    r = w.shape[0]
    w = w.reshape(r, MLA_HEADS, MLA_NOPE + MLA_ROPE) * ((MLA_NOPE + MLA_ROPE) ** -0.5 * LOG2E)
    w = jnp.pad(w, ((0, 0), (0, 0), (0, MLA_QK_PAD - MLA_NOPE - MLA_ROPE)))
    return w.reshape(r, MLA_HEADS * MLA_QK_PAD).astype(_BF)


def kernel(x, c, ctx, c_ctx, w_ada, b_ada, w_in, mla_q_norm, mla_kv_norm, w_uq, w_ukv, na_rpb, ret_decay,
           w_o, ln1_g, ln1_b, w_up, conv_w, conv_b, w_down, ln2_g, ln2_b):
    b, t, d = x.shape
    cl = ctx.shape[1]
    depth = w_ada.shape[0]
    dff = conv_w.shape[-1]
    assert b + 1 <= MOD_ROWS
    alpha = (2 * depth) ** 0.25
    ctx_row = b

    cvec = jnp.concatenate([c, c_ctx[None], jnp.zeros((MOD_ROWS - b - 1, d), c.dtype)], 0)
    mod3 = _ada(cvec, w_ada, b_ada)

    tm_in = _tile(b * cl, _tile(t, 1024))
    ctx0 = b * t
    lat_row = lambda tm: (lambda i: i // (t // tm))
    ctx_rowf = lambda i: ctx_row

    rope_pe = _rope_tables(t, MLA_ROPE)
    rope_ret = _rope_tables(t, HEAD_DIM)
    id_pe = _rope_tables(cl, MLA_ROPE, identity=True)
    id_ret = _rope_tables(cl, HEAD_DIM, identity=True)

    xf = x.reshape(b * t, d)
    cf = ctx.reshape(b * cl, d)
    h1, hc1 = xf, cf

    tq = _tile(t, 2048)
    tqc = _tile(cl, 256)
    tn_ff = _tile(dff, 512)
    zero_state = jnp.zeros((b, RET_HEADS, HEAD_DIM, HEAD_DIM), _F32)

    for l in range(depth):
        last = l == depth - 1
        w_in_p = _pack_w_in(w_in, l, 1024)
        w_uq_p = _pack_w_uq(w_uq[l])
        w_ukv_p = w_ukv[l].astype(_BF)
        gq = mla_q_norm[l][None]
        gkv = mla_kv_norm[l][None]
        lg = jnp.log1p(-jnp.exp2(ret_decay[l].astype(_F32)))
        bias = _na_col_bias(na_rpb[l])

        p = _in_proj(h1, hc1, w_in_p, tm_in, t, mod=(mod3, 0, ctx_row) if l == 0 else None)

        qc_m, kc_m, vc_m = _mla_qkv(p, gq, gkv, w_uq_p, w_ukv_p, id_pe, b, cl, _tile(cl, 1024), row0=ctx0)
        q_m, kl_m, vl_m, w_o_b = _mla_qkv(p, gq, gkv, w_uq_p, w_ukv_p, rope_pe, b, t, _tile(t, 1024), cast=(w_o, l))
        y_mla = _dense_attn(
            q_m, pl.BlockSpec((None, None, tq, MLA_QK_PAD), lambda bi, h, i: (bi, h, i, 0)),
            [(kc_m, _head4_spec(cl, MLA_QK_PAD), vc_m, _head4_spec(VT_ROWS, cl)),
             (kl_m, _head4_spec(t, MLA_QK_PAD), vl_m, _head4_spec(VT_ROWS, t))],
            b, MLA_HEADS, t // tq, tq, MLA_V, nsub=tq // _tile(tq, 256))

        y_na = _na_attn(p, bias, b, t, cl)

        yc_ret, s_f, s_b = _retention(p, lg, zero_state, zero_state, id_ret, b, cl, row0=ctx0)
        y_ret, _, _ = _retention(p, lg, s_f, s_b, rope_ret, b, t)

        tm_o = _tile(t, 512)
        x_new, h2 = _proj_ln([y_na, y_mla, y_ret], w_o_b, xf, mod3, l, 2, lat_row(tm_o),
                             ln1_g[l][None], ln1_b[l][None], alpha, tm_o, tm_o // _tile(tm_o, 256), with_next=True)

        tm_d = _tile(t, 256)

        if not last:
            yc_mla = _dense_attn(
                qc_m, pl.BlockSpec((None, None, tqc, MLA_QK_PAD), lambda bi, h, i: (bi, h, i, 0)),
                [(kc_m, _head4_spec(cl, MLA_QK_PAD), vc_m, _head4_spec(VT_ROWS, cl))],
                b, MLA_HEADS, cl // tqc, tqc, MLA_V, nsub=1)
            nbc = cl // tqc
            cq0, ck0 = ctx0 // tqc, ctx0 // cl
            yc_na = _dense_attn(
                p, pl.BlockSpec((tqc, LANE), lambda bi, h, i: (cq0 + bi * nbc + i, CB_NAQ + h)),
                [(p, pl.BlockSpec((cl, LANE), lambda bi, h, i: (ck0 + bi, CB_NAK + h)),
                  p, pl.BlockSpec((cl, LANE), lambda bi, h, i: (ck0 + bi, CB_NAV + h)))],
                b, NA_HEADS, nbc, tqc, HEAD_DIM, nsub=1, v_row_major=True)
            tm_oc = _tile(cl, 256)
            c_new, hc2 = _proj_ln([yc_na, yc_mla, yc_ret], w_o_b, cf, mod3, l, 2, ctx_rowf,
                                  ln1_g[l][None], ln1_b[l][None], alpha, tm_oc, 1, with_next=True)

        g_l, w_up_b, w_down_b = _glu(h2, w_up, w_down, l, conv_w[l], conv_b[l][None], b, t, tn_ff)
        xf, h1 = _proj_ln([g_l], w_down_b, x_new, mod3, l, 5, lat_row(tm_d), ln2_g[l][None], ln2_b[l][None],
                          alpha, tm_d, 1, with_next=not last)
        if not last:
            g_c = _glu(hc2, w_up, w_down, l, conv_w[l], conv_b[l][None], b, cl, tn_ff, nseq=b, w_bf16=w_up_b)
            tm_dc = _tile(cl, 256)
            cf, hc1 = _proj_ln([g_c], w_down_b, c_new, mod3, l, 5, ctx_rowf, ln2_g[l][None], ln2_b[l][None],
                               alpha, tm_dc, 1, with_next=True)

    return xf.reshape(b, t, d)
```

```python
import functools

import numpy as np
import jax
import jax.numpy as jnp
from jax import lax
from jax.experimental import pallas as pl
from jax.experimental.pallas import tpu as pltpu

GRID_W = 64
HEAD_DIM = 128
NA_HEADS = 6
MLA_HEADS = 5
RET_HEADS = 5
NA_WIDTH = NA_HEADS * HEAD_DIM
MLA_WIDTH = MLA_HEADS * HEAD_DIM
RET_WIDTH = RET_HEADS * HEAD_DIM
NA_WIN_R = 8
NA_WIN_C = 16
MLA_Q_RANK = 512
MLA_KV_RANK = 512
MLA_NOPE = 128
MLA_ROPE = 64
MLA_V = 128
MLA_QK_PAD = 256
RET_CHUNK = 128
CONV_W = 3
ROPE_BASE = 10000.0
LN_EPS = 1e-5
RMS_EPS = 1e-6
NEG_INF = -1e30
LOG2E = 1.4426950408889634
IN_SPLITS = (NA_WIDTH, NA_WIDTH, NA_WIDTH, MLA_Q_RANK, MLA_KV_RANK, MLA_ROPE,
             RET_WIDTH, RET_WIDTH, RET_WIDTH, RET_WIDTH)

LANE = 128
CB_CQ, CB_CKV, CB_NAQ, CB_NAK, CB_NAV = 0, 4, 8, 14, 20
CB_RQ, CB_RK, CB_RV, CB_RG, CB_KPE = 26, 31, 36, 41, 46
PACKED_IN_WIDTH = 48 * LANE
MOD_ROWS = 8
VMEM_LIMIT_MB = 56

_BF = jnp.bfloat16
_F32 = jnp.float32


def _cparams(sem, vmem_mb=VMEM_LIMIT_MB):
    return pltpu.CompilerParams(dimension_semantics=sem, vmem_limit_bytes=vmem_mb << 20)


def _tile(n, pref):
    t = min(n, pref)
    while n % t:
        t //= 2
    return t


def _dot(a, b):
    return jnp.dot(a, b, preferred_element_type=_F32)


def _dot_nt(a, b):
    return lax.dot_general(a, b, (((1,), (1,)), ((), ())), preferred_element_type=_F32)


def _silu(x):
    return x / (1.0 + jnp.exp(-x))


def _ada_body(c_ref, w_ref, b_ref, o_ref):
    part = _dot(_silu(c_ref[...]).astype(_BF), w_ref[0].astype(_BF))

    @pl.when(pl.program_id(1) == 0)
    def _():
        o_ref[0] = part + b_ref[0]

    @pl.when(pl.program_id(1) > 0)
    def _():
        o_ref[0] += part


def _ada(cvec, w_ada, b_ada):
    depth, d, n = w_ada.shape
    tk = _tile(d, 256)
    out = pl.pallas_call(
        _ada_body,
        grid=(depth, d // tk),
        in_specs=[pl.BlockSpec((MOD_ROWS, tk), lambda l, k: (0, k)),
                  pl.BlockSpec((1, tk, n), lambda l, k: (l, k, 0)),
                  pl.BlockSpec((1, 1, n), lambda l, k: (l, 0, 0))],
        out_specs=pl.BlockSpec((1, MOD_ROWS, n), lambda l, k: (l, 0, 0)),
        out_shape=jax.ShapeDtypeStruct((depth, MOD_ROWS, n), _F32),
        compiler_params=_cparams(("arbitrary", "arbitrary")),
    )(cvec, w_ada, b_ada.reshape(depth, 1, n))
    return out.reshape(depth * MOD_ROWS, 1, n)


def _mod_spec(d, layer, chunk, row_fn):
    return pl.BlockSpec((1, 1, d), lambda i, *_: (layer * MOD_ROWS + row_fn(i), 0, chunk))


def _in_proj_body(*refs, n_lat, modulated):
    if modulated:
        lat_ref, ctx_ref, shl_ref, scl_ref, shc_ref, scc_ref, w_ref, o_ref, a_scr = refs
    else:
        lat_ref, ctx_ref, w_ref, o_ref, a_scr = refs
    i = pl.program_id(0)

    def stage(src_ref, sh_ref, sc_ref):
        x = src_ref[...]
        if modulated:
            x = x * (1.0 + sc_ref[0]) + sh_ref[0]
        a_scr[...] = x.astype(a_scr.dtype)

    @pl.when((pl.program_id(1) == 0) & (i < n_lat))
    def _():
        stage(lat_ref, shl_ref if modulated else None, scl_ref if modulated else None)

    @pl.when((pl.program_id(1) == 0) & (i >= n_lat))
    def _():
        stage(ctx_ref, shc_ref if modulated else None, scc_ref if modulated else None)

    o_ref[...] = _dot(a_scr[...], w_ref[...]).astype(o_ref.dtype)


def _in_proj(lat, ctx, w, tm, t, mod=None):
    d = lat.shape[1]
    nj, _, tn = w.shape
    n_lat, n_ctx = lat.shape[0] // tm, ctx.shape[0] // tm
    lat_i = lambda i: jnp.minimum(i, n_lat - 1)
    in_specs = [pl.BlockSpec((tm, d), lambda i, j: (lat_i(i), 0)),
                pl.BlockSpec((tm, d), lambda i, j: (jnp.maximum(i - n_lat, 0), 0), pipeline_mode=pl.Buffered(1))]
    args = [lat, ctx]
    if mod is not None:
        mod3, layer, ctx_row = mod
        lat_row = lambda i: lat_i(i) // (t // tm)
        in_specs += [_mod_spec(d, layer, 0, lat_row), _mod_spec(d, layer, 1, lat_row),
                     _mod_spec(d, layer, 0, lambda i: ctx_row), _mod_spec(d, layer, 1, lambda i: ctx_row)]
        args += [mod3] * 4
    return pl.pallas_call(
        functools.partial(_in_proj_body, n_lat=n_lat, modulated=mod is not None),
        grid=(n_lat + n_ctx, nj),
        in_specs=in_specs + [pl.BlockSpec((None, d, tn), lambda i, j: (j, 0, 0))],
        out_specs=pl.BlockSpec((tm, tn), lambda i, j: (i, j)),
        out_shape=jax.ShapeDtypeStruct(((n_lat + n_ctx) * tm, nj * tn), _BF),
        scratch_shapes=[pltpu.VMEM((tm, d), _BF)],
        compiler_params=_cparams(("arbitrary", "arbitrary")),
    )(*args, w)


def _rope_tables(t, rot_dim, identity=False):
    pos = np.arange(t)
    row = (pos // GRID_W).astype(np.float32)
    col = (pos % GRID_W).astype(np.float32)
    nf = rot_dim // 4
    inv = (ROPE_BASE ** (-np.arange(nf, dtype=np.float32) / nf)).astype(np.float32)
    ar = row[:, None] * inv[None]
    ac = col[:, None] * inv[None]
    ang = np.concatenate([ar, ar, ac, ac], -1)
    cos, sin = np.cos(ang), np.sin(ang)
    if identity:
        cos, sin = np.ones_like(cos), np.zeros_like(sin)
    first = (np.arange(rot_dim) % (rot_dim // 2)) < nf
    sin_a = np.where(first, -sin, 0.0)
    sin_b = np.where(first, 0.0, sin)
    pad = LANE - rot_dim
    if pad:
        cos = np.pad(cos, ((0, 0), (0, pad)), constant_values=1.0)
        sin_a = np.pad(sin_a, ((0, 0), (0, pad)))
        sin_b = np.pad(sin_b, ((0, 0), (0, pad)))
    return tuple(jnp.asarray(a, _F32) for a in (cos, sin_a, sin_b))


def _rope(x, cos, sin_a, sin_b, nf):
    return x * cos + pltpu.roll(x, LANE - nf, 1) * sin_a + pltpu.roll(x, nf, 1) * sin_b


def _rms(x, g):
    return x * lax.rsqrt(jnp.mean(x * x, -1, keepdims=True) + RMS_EPS) * g


VT_ROWS = HEAD_DIM + 16


def _values_t(v):
    n = v.shape[0]
    ones = (lax.broadcasted_iota(jnp.int32, (VT_ROWS - HEAD_DIM, n), 0) == 0).astype(_BF)
    return jnp.concatenate([v.astype(_F32).T.astype(_BF), ones], 0)


def _mla_qkv_body(cq_ref, ckv_ref, kpe_ref, gq_ref, gkv_ref, wq_ref, wkv_ref, cos_ref, sa_ref, sb_ref,
                  *rest):
    if len(rest) == 5:
        wo_ref, q_ref, k_ref, vt_ref, wob = rest
        wob[...] = wo_ref[...].astype(_BF)
    else:
        q_ref, k_ref, vt_ref = rest
    cos, sa, sb = cos_ref[...], sa_ref[...], sb_ref[...]
    nf = MLA_ROPE // 4
    xq = _rms(cq_ref[...].astype(_F32), gq_ref[...]).astype(_BF)
    xkv = _rms(ckv_ref[...].astype(_F32), gkv_ref[...]).astype(_BF)
    pe = _rope(kpe_ref[...].astype(_F32), cos, sa, sb, nf).astype(_BF)
    width = MLA_NOPE + MLA_V
    for h in range(MLA_HEADS):
        q = _dot(xq, wq_ref[:, h * MLA_QK_PAD:(h + 1) * MLA_QK_PAD])
        q_ref[0, h, :, 0:LANE] = q[:, :LANE].astype(_BF)
        q_ref[0, h, :, LANE:] = _rope(q[:, LANE:], cos, sa, sb, nf).astype(_BF)
        kv = _dot(xkv, wkv_ref[:, h * width:(h + 1) * width])
        k_ref[0, h, :, 0:LANE] = kv[:, :MLA_NOPE].astype(_BF)
        k_ref[0, h, :, LANE:] = pe
        vt_ref[0, h] = _values_t(kv[:, MLA_NOPE:])


def _mla_qkv(p, gq, gkv, wq, wkv, tables, b, t, tm, row0=0, cast=None):
    nb = t // tm
    blk0 = row0 // tm
    extra_in, extra_args, extra_out, extra_shape = [], [], [], []
    if cast is not None:
        w, layer = cast
        slab = w.shape[1] // (b * nb)
        assert w.shape[1] % (b * nb) == 0 and slab % 16 == 0
        extra_in = [pl.BlockSpec((None, slab, w.shape[2]), lambda i: (layer, i, 0))]
        extra_args = [w]
        extra_out = [pl.BlockSpec((slab, w.shape[2]), lambda i: (i, 0))]
        extra_shape = [jax.ShapeDtypeStruct(w.shape[1:], _BF)]
    tab_spec = pl.BlockSpec((tm, LANE), lambda i: (i % nb, 0))
    head_spec = lambda rows, cols, ri, ci: pl.BlockSpec(
        (1, MLA_HEADS, rows, cols), lambda i: (i // nb, 0, (i % nb) * ri, (i % nb) * ci))
    return pl.pallas_call(
        _mla_qkv_body,
        grid=(b * nb,),
        in_specs=[pl.BlockSpec((tm, MLA_Q_RANK), lambda i: (blk0 + i, CB_CQ // 4)),
                  pl.BlockSpec((tm, MLA_KV_RANK), lambda i: (blk0 + i, CB_CKV // 4)),
                  pl.BlockSpec((tm, LANE), lambda i: (blk0 + i, CB_KPE)),
                  pl.BlockSpec((1, MLA_Q_RANK), lambda i: (0, 0)),
                  pl.BlockSpec((1, MLA_KV_RANK), lambda i: (0, 0)),
                  pl.BlockSpec(wq.shape, lambda i: (0, 0)),
                  pl.BlockSpec(wkv.shape, lambda i: (0, 0)),
                  tab_spec, tab_spec, tab_spec] + extra_in,
        out_specs=[head_spec(tm, MLA_QK_PAD, 1, 0), head_spec(tm, MLA_QK_PAD, 1, 0),
                   head_spec(VT_ROWS, tm, 0, 1)] + extra_out,
        out_shape=[jax.ShapeDtypeStruct((b, MLA_HEADS, t, MLA_QK_PAD), _BF),
                   jax.ShapeDtypeStruct((b, MLA_HEADS, t, MLA_QK_PAD), _BF),
                   jax.ShapeDtypeStruct((b, MLA_HEADS, VT_ROWS, t), _BF)] + extra_shape,
        compiler_params=_cparams(("arbitrary",)),
    )(p, p, p, gq, gkv, wq, wkv, *tables, *extra_args)


def _skewed(n, stages):
    carry = [None] * n
    for step in range(n + len(stages) - 1):
        for s, stage in enumerate(stages):
            j = step - s
            if 0 <= j < n:
                carry[j] = stage(j, carry[j])


def _attn_stages(scores_fn, values_fn, store_fn):
    def s_scores(j, _):
        return scores_fn(j)

    def s_max(j, sc):
        return sc, functools.reduce(jnp.maximum, [jnp.max(s, 0, keepdims=True) for s in sc])

    def s_exp(j, c):
        sc, m = c
        return [jnp.exp2(s - m).astype(_BF) for s in sc]

    def s_pv(j, p):
        acc = functools.reduce(jnp.add, [_dot(vt, pi) for vt, pi in zip(values_fn(j), p)])
        store_fn(j, (acc[0:HEAD_DIM, :] / acc[HEAD_DIM:HEAD_DIM + 1, :]).T)

    return [s_scores, s_max, s_exp, s_pv]


def _attn_body(*refs, nseg, nsub, v_row_major):
    q_ref, o_ref = refs[0], refs[-1]
    ks = [refs[1 + 2 * i][...] for i in range(nseg)]
    vts = [refs[2 + 2 * i][...] for i in range(nseg)]
    if v_row_major:
        vts = [_values_t(v) for v in vts]
    ts = q_ref.shape[0] // nsub
    rows = lambda j: slice(j * ts, (j + 1) * ts)

    def store(j, out):
        o_ref[rows(j), :] = out.astype(o_ref.dtype)

    _skewed(nsub, _attn_stages(lambda j: [_dot_nt(k, q_ref[rows(j), :]) for k in ks], lambda j: vts, store))


def _dense_attn(q, q_spec, kv, b, heads, tq_blocks, tq, dv, nsub, v_row_major=False):
    args, specs = [q], [q_spec]
    for k, ks, v, vs in kv:
        args += [k, v]
        specs += [ks, vs]
    return pl.pallas_call(
        functools.partial(_attn_body, nseg=len(kv), nsub=nsub, v_row_major=v_row_major),
        grid=(b, heads, tq_blocks),
        in_specs=specs,
        out_specs=pl.BlockSpec((tq, dv), lambda bi, h, i: (bi * tq_blocks + i, h)),
        out_shape=jax.ShapeDtypeStruct((b * tq_blocks * tq, heads * dv), _BF),
        compiler_params=_cparams(("arbitrary", "arbitrary", "arbitrary")),
    )(*args)


def _head4_spec(rows, cols):
    return pl.BlockSpec((None, None, rows, cols), lambda bi, h, i: (bi, h, 0, 0))


NA_GROUP = 4
NA_KEY_ROWS = 12


def _na_group_window(gi, rows):
    ngroups = rows // NA_GROUP
    if gi == 0:
        return 0, 0
    if gi == ngroups - 1:
        return 2, rows - NA_KEY_ROWS
    return 1, gi * NA_GROUP - NA_WIN_R // 2


NA_DR = 2 * NA_WIN_R - 1


def _na_col_bias(rpb):
    c = np.arange(GRID_W)
    col_start = np.clip(c - NA_WIN_C // 2, 0, GRID_W - NA_WIN_C)
    kc = np.arange(GRID_W)
    in_win = (kc[None, :] >= col_start[:, None]) & (kc[None, :] < col_start[:, None] + NA_WIN_C)
    dc = np.clip(kc[None, :] - c[:, None] + NA_WIN_C - 1, 0, 2 * NA_WIN_C - 2)
    col_sel = (dc[:, :, None] == np.arange(2 * NA_WIN_C - 1)).astype(np.float32)
    toep = jnp.einsum('hrj,cqj->hrqc', rpb.astype(_F32), col_sel, precision=lax.Precision.HIGHEST)
    toep = jnp.where(in_win.T[None, None], toep * LOG2E, NEG_INF)
    masked = jnp.full((rpb.shape[0], 1, GRID_W, GRID_W), NEG_INF, _F32)
    return jnp.concatenate([toep, masked], 1)


def _na_block_index():
    g = np.arange(NA_GROUP)[:, None]
    kr = np.arange(NA_KEY_ROWS)[None, :]
    half = NA_WIN_R // 2
    dr = np.stack([kr - g + NA_WIN_R - 1, kr - g + NA_WIN_R - 1 - half, kr - g - 1])
    w0 = np.stack([0 * g, g, half + 0 * g])
    row_ok = (kr[None] >= w0) & (kr[None] < w0 + NA_WIN_R)
    assert np.all((dr[row_ok] >= 0) & (dr[row_ok] < NA_DR))
    return np.where(row_ok, dr, NA_DR)


def _na_body(q_ref, k_ref, v_ref, kc_ref, vc_ref, cb_ref, o_ref, bias_ref, vt_ref, *, rows):
    @pl.when(pl.program_id(1) == 0)
    def _():
        idx = _na_block_index()
        for typ in range(3):
            for kr in range(NA_KEY_ROWS):
                for gp in range(NA_GROUP // 2):
                    pair = jnp.concatenate([cb_ref[int(idx[typ, 2 * gp, kr])], cb_ref[int(idx[typ, 2 * gp + 1, kr])]], 1)
                    bias_ref[typ, kr * GRID_W:(kr + 1) * GRID_W, gp * 2 * GRID_W:(gp + 1) * 2 * GRID_W] = pair

    vt_ref[...] = _values_t(v_ref[...])
    kc = kc_ref[...]
    vct = _values_t(vc_ref[...])
    gq = NA_GROUP * GRID_W

    def band(gi):
        ws = _na_group_window(gi, rows)[1]
        return slice(ws * GRID_W, (ws + NA_KEY_ROWS) * GRID_W)

    def scores(gi):
        q = q_ref[gi * gq:(gi + 1) * gq, :]
        return [_dot_nt(k_ref[band(gi), :], q) + bias_ref[_na_group_window(gi, rows)[0]], _dot_nt(kc, q)]

    def store(gi, out):
        o_ref[gi * gq:(gi + 1) * gq, :] = out.astype(o_ref.dtype)

    _skewed(rows // NA_GROUP, _attn_stages(scores, lambda gi: [vt_ref[:, band(gi)], vct], store))


def _na_attn(p, bias, b, t, c):
    rows = t // GRID_W
    cblk0 = b * t // c
    assert t % GRID_W == 0 and rows % NA_GROUP == 0 and rows >= NA_KEY_ROWS + NA_GROUP and 2 * GRID_W == LANE
    return pl.pallas_call(
        functools.partial(_na_body, rows=rows),
        grid=(NA_HEADS, b),
        in_specs=[pl.BlockSpec((t, LANE), lambda h, bi: (bi, CB_NAQ + h)),
                  pl.BlockSpec((t, LANE), lambda h, bi: (bi, CB_NAK + h)),
                  pl.BlockSpec((t, LANE), lambda h, bi: (bi, CB_NAV + h)),
                  pl.BlockSpec((c, LANE), lambda h, bi: (cblk0 + bi, CB_NAK + h)),
                  pl.BlockSpec((c, LANE), lambda h, bi: (cblk0 + bi, CB_NAV + h)),
                  pl.BlockSpec((None, NA_DR + 1, GRID_W, GRID_W), lambda h, bi: (h, 0, 0, 0))],
        out_specs=pl.BlockSpec((t, LANE), lambda h, bi: (bi, h)),
        out_shape=jax.ShapeDtypeStruct((b * t, NA_WIDTH), _BF),
        scratch_shapes=[pltpu.VMEM((3, NA_KEY_ROWS * GRID_W, NA_GROUP * GRID_W), _F32),
                        pltpu.VMEM((VT_ROWS, t), _BF)],
        compiler_params=_cparams(("arbitrary", "arbitrary")),
    )(p, p, p, p, p, bias)


def _ret_body(lg_ref, q_ref, k_ref, v_ref, g_ref, s0f_ref, s0b_ref, cos_ref, sa_ref, sb_ref,
              o_ref, sf_ref, sbo_ref, kr_scr, u_hist, st_hist, *, n):
    h = pl.program_id(0)
    lgf = lg_ref[0, h]
    lgb = lg_ref[1, h]
    L = RET_CHUNK
    pos_c = lax.broadcasted_iota(jnp.int32, (L, 1), 0).astype(_F32)
    pos_r = lax.broadcasted_iota(jnp.int32, (1, L), 1).astype(_F32)
    diff = pos_c - pos_r
    decay = jnp.where(diff > 0, jnp.exp(jnp.maximum(diff, 0.0) * lgf),
                      jnp.where(diff < 0, jnp.exp(jnp.maximum(-diff, 0.0) * lgb), 2.0))
    qdec_f = jnp.exp((pos_c + 1.0) * lgf)
    kdec_f = jnp.exp((L - 1.0 - pos_c) * lgf)
    qdec_b = jnp.exp((L - pos_c) * lgb)
    kdec_b = jnp.exp(pos_c * lgb)
    cd_f = jnp.exp(L * lgf)
    cd_b = jnp.exp(L * lgb)
    nf = HEAD_DIM // 4

    def rows_of(i):
        return slice(i * L, (i + 1) * L)

    def roped(ref, i):
        r = rows_of(i)
        return _rope(ref[r, :].astype(_F32), cos_ref[r, :], sa_ref[r, :], sb_ref[r, :], nf)

    for i in range(n):
        k = roped(k_ref, i)
        kr_scr[rows_of(i), :] = k.astype(_BF)
        kd = jnp.concatenate([k * kdec_f, k * kdec_b], 1).astype(_BF)
        u_hist[i] = lax.dot_general(kd, v_ref[rows_of(i), :], (((0,), (0,)), ((), ())),
                                    preferred_element_type=_F32)

    state = s0f_ref[...]
    for i in range(n):
        st_hist[i, 0:HEAD_DIM, :] = state.astype(_BF)
        state = cd_f * state + u_hist[i, 0:HEAD_DIM, :]
    sf_ref[...] = state
    state = s0b_ref[...]
    for i in reversed(range(n)):
        st_hist[i, HEAD_DIM:, :] = state.astype(_BF)
        state = cd_b * state + u_hist[i, HEAD_DIM:, :]
    sbo_ref[...] = state

    for i in range(n):
        r = rows_of(i)
        q = roped(q_ref, i)
        a = _dot_nt(q.astype(_BF), kr_scr[r, :]) * decay
        qd = jnp.concatenate([q * qdec_f, q * qdec_b], 1).astype(_BF)
        o = _dot(a.astype(_BF), v_ref[r, :]) + _dot(qd, st_hist[i])
        mu = jnp.mean(o, -1, keepdims=True)
        oc = o - mu
        var = jnp.mean(oc * oc, -1, keepdims=True)
        y = oc * lax.rsqrt(var + LN_EPS) * _silu(g_ref[r, :].astype(_F32))
        o_ref[r, :] = y.astype(o_ref.dtype)


def _retention(p, lg, s0f, s0b, tables, b, t, row0=0):
    n = t // RET_CHUNK
    blk0 = row0 // t
    col = lambda cb: pl.BlockSpec((t, LANE), lambda h, bi: (blk0 + bi, cb + h))
    st_spec = pl.BlockSpec((None, None, HEAD_DIM, HEAD_DIM), lambda h, bi: (bi, h, 0, 0))
    tab_spec = pl.BlockSpec((t, LANE), lambda h, bi: (0, 0))
    st_shape = jax.ShapeDtypeStruct((b, RET_HEADS, HEAD_DIM, HEAD_DIM), _F32)
    return pl.pallas_call(
        functools.partial(_ret_body, n=n),
        grid=(RET_HEADS, b),
        in_specs=[pl.BlockSpec(memory_space=pltpu.SMEM),
                  col(CB_RQ), col(CB_RK), col(CB_RV), col(CB_RG), st_spec, st_spec,
                  tab_spec, tab_spec, tab_spec],
        out_specs=[pl.BlockSpec((t, LANE), lambda h, bi: (bi, h)), st_spec, st_spec],
        out_shape=[jax.ShapeDtypeStruct((b * t, RET_WIDTH), _BF), st_shape, st_shape],
        scratch_shapes=[pltpu.VMEM((t, HEAD_DIM), _BF),
                        pltpu.VMEM((n, 2 * HEAD_DIM, HEAD_DIM), _F32),
                        pltpu.VMEM((n, 2 * HEAD_DIM, HEAD_DIM), _BF)],
        compiler_params=_cparams(("arbitrary", "arbitrary")),
    )(lg, p, p, p, p, s0f, s0b, *tables)


def _ln_epilogue(y, rows, x_ref, gate_ref, lng_ref, lnb_ref, alpha, xo_ref, ho_ref, sh_ref, sc_ref):
    z = alpha * x_ref[rows, :] + gate_ref[0] * y
    mu = jnp.mean(z, -1, keepdims=True)
    zc = z - mu
    var = jnp.mean(zc * zc, -1, keepdims=True)
    xn = zc * lax.rsqrt(var + LN_EPS) * lng_ref[...] + lnb_ref[...]
    xo_ref[rows, :] = xn
    if ho_ref is not None:
        ho_ref[rows, :] = (xn * (1.0 + sc_ref[0]) + sh_ref[0]).astype(ho_ref.dtype)


def _proj_ln_body(*refs, n_in, alpha, with_next, nsub):
    a_refs, refs = refs[:n_in], refs[n_in:]
    if with_next:
        w_ref, x_ref, gate_ref, sh_ref, sc_ref, lng_ref, lnb_ref, xo_ref, ho_ref = refs
    else:
        w_ref, x_ref, gate_ref, lng_ref, lnb_ref, xo_ref = refs
        sh_ref = sc_ref = ho_ref = None
    ts = x_ref.shape[0] // nsub
    for j in range(nsub):
        r = slice(j * ts, (j + 1) * ts)
        a = a_refs[0][r, :] if n_in == 1 else jnp.concatenate([ar[r, :] for ar in a_refs], 1)
        _ln_epilogue(_dot(a, w_ref[...]), r, x_ref, gate_ref, lng_ref, lnb_ref, alpha, xo_ref, ho_ref, sh_ref, sc_ref)


def _proj_ln(a_list, w, x2, mod3, layer, gate_chunk, row_fn, lng, lnb, alpha, tm, nsub, with_next):
    m, d = x2.shape
    row = lambda width: pl.BlockSpec((tm, width), lambda i: (i, 0))
    vec = pl.BlockSpec((1, d), lambda i: (0, 0))
    w_spec = pl.BlockSpec(w.shape, lambda i: (0, 0), pipeline_mode=pl.Buffered(1))
    in_specs = [row(a.shape[1]) for a in a_list] + [w_spec, row(d), _mod_spec(d, layer, gate_chunk, row_fn)]
    args = list(a_list) + [w, x2, mod3]
    if with_next:
        nxt = (layer, gate_chunk + 1) if gate_chunk < 5 else (layer + 1, 0)
        in_specs += [_mod_spec(d, nxt[0], nxt[1], row_fn), _mod_spec(d, nxt[0], nxt[1] + 1, row_fn)]
        args += [mod3, mod3]
    in_specs += [vec, vec]
    args += [lng, lnb]
    out_specs = [row(d)] + ([row(d)] if with_next else [])
    out_shape = [jax.ShapeDtypeStruct((m, d), _F32)] + ([jax.ShapeDtypeStruct((m, d), _BF)] if with_next else [])
    out = pl.pallas_call(
        functools.partial(_proj_ln_body, n_in=len(a_list), alpha=alpha, with_next=with_next, nsub=nsub),
        grid=(m // tm,),
        in_specs=in_specs, out_specs=out_specs, out_shape=out_shape,
        compiler_params=_cparams(("arbitrary",)),
    )(*args)
    return (out[0], out[1]) if with_next else (out[0], None)


def _glu_chunks(t):
    rc = _tile(t, 1024)
    return [(i * rc, rc) for i in range(t // rc)]


def _glu_body(h_ref, wa_ref, wu_ref, *rest, t, cast):
    if cast:
        wd_ref, cw_ref, cb_ref, o_ref, wab, wub, wdb, a_scr = rest
        wdb[...] = wd_ref[...].astype(_BF)

        @pl.when(pl.program_id(1) == 0)
        def _():
            wab[...] = wa_ref[...].astype(_BF)
            wub[...] = wu_ref[...].astype(_BF)
    else:
        cw_ref, cb_ref, o_ref, a_scr = rest
        wab, wub = wa_ref, wu_ref

    nseq = a_scr.shape[0]
    units = [(q, r0, rc) for q in range(nseq) for r0, rc in _glu_chunks(t)]
    halo = 8
    tn = o_ref.shape[1]
    zeros = jnp.zeros((halo, tn), _F32)
    for q in range(nseq):
        a_scr[q, 0:halo, :] = zeros
        a_scr[q, halo + t:, :] = zeros
    w0, w1, w2 = cw_ref[0:1, :], cw_ref[1:2, :], cw_ref[2:3, :]
    bias = cb_ref[...]
    u_prev = None
    for i in range(len(units) + 1):
        u_cur = None
        if i < len(units):
            q, r0, rc = units[i]
            hc = h_ref[q * t + r0:q * t + r0 + rc, :]
            a_scr[q, halo + r0:halo + r0 + rc, :] = _dot(hc, wab[...])
            u_cur = _dot(hc, wub[...])
        if i > 0:
            q, r0, rc = units[i - 1]
            base = halo + r0
            lo = a_scr[q, base - 1:base - 1 + rc, :]
            mid = a_scr[q, base:base + rc, :]
            hi = a_scr[q, base + 1:base + 1 + rc, :]
            acc = bias + lo * w0 + mid * w1 + hi * w2
            o_ref[q * t + r0:q * t + r0 + rc, :] = (_silu(acc) * u_prev).astype(o_ref.dtype)
        u_prev = u_cur


def _glu(h, w_up, w_down, layer, conv_w, conv_b, b, t, tn, nseq=1, w_bf16=None):
    d = h.shape[1]
    dff = conv_w.shape[1]
    nj = dff // tn
    cast = w_bf16 is None
    steps = nj * (b // nseq)
    g_spec = pl.BlockSpec((nseq * t, tn), lambda j, bi: (bi, j))
    g_shape = jax.ShapeDtypeStruct((b * t, dff), _BF)
    wb_spec = pl.BlockSpec((None, d, tn), lambda j, bi: (j, 0, 0))
    if cast:
        w_args = (w_up, w_up)
        w_specs = [pl.BlockSpec((None, d, tn), lambda j, bi: (layer, 0, j)),
                   pl.BlockSpec((None, d, tn), lambda j, bi: (layer, 0, nj + j))]
        wb_shape = jax.ShapeDtypeStruct((nj, d, tn), _BF)
        slab = dff // steps
        assert dff % steps == 0 and slab % 16 == 0
        step = lambda j, bi: j * (b // nseq) + bi
        w_args += (w_down,)
        w_specs.append(pl.BlockSpec((None, slab, w_down.shape[2]), lambda j, bi: (layer, step(j, bi), 0)))
        wd_spec = pl.BlockSpec((slab, w_down.shape[2]), lambda j, bi: (step(j, bi), 0))
        wd_shape = jax.ShapeDtypeStruct(w_down.shape[1:], _BF)
        out_specs, out_shape = [g_spec, wb_spec, wb_spec, wd_spec], [g_shape, wb_shape, wb_shape, wd_shape]
    else:
        w_args, w_specs = tuple(w_bf16), [wb_spec, wb_spec]
        out_specs, out_shape = g_spec, g_shape
    out = pl.pallas_call(
        functools.partial(_glu_body, t=t, cast=cast),
        grid=(nj, b // nseq),
        in_specs=[pl.BlockSpec((nseq * t, d), lambda j, bi: (bi, 0)), *w_specs,
                  pl.BlockSpec((CONV_W, tn), lambda j, bi: (0, j)),
                  pl.BlockSpec((1, tn), lambda j, bi: (0, j))],
        out_specs=out_specs, out_shape=out_shape,
        scratch_shapes=[pltpu.VMEM((nseq, t + 16, tn), _F32)],
        compiler_params=_cparams(("arbitrary", "arbitrary")),
    )(h, *w_args, conv_w, conv_b)
    return (out[0], (out[1], out[2]), out[3]) if cast else out


PACK_SRC_ROWS = 64
PACK_CHUNKS = 4


def _pack_plan():
    src = dict(zip(("na_q", "na_k", "na_v", "cq", "ckv", "kpe", "r_q", "r_k", "r_v", "r_g"),
                   [int(i) for i in np.cumsum((0,) + IN_SPLITS[:-1])]))
    plan = np.zeros((PACKED_IN_WIDTH // LANE, 3), np.int32)
    segments = [(CB_CQ, "cq", MLA_Q_RANK + MLA_KV_RANK, 0), (CB_NAQ, "na_q", NA_WIDTH, 1),
                (CB_NAK, "na_k", 2 * NA_WIDTH, 0), (CB_RQ, "r_q", RET_WIDTH, 0), (CB_RK, "r_k", RET_WIDTH, 2),
                (CB_RV, "r_v", 2 * RET_WIDTH, 0), (CB_KPE, "kpe", MLA_ROPE, 0)]
    for dst, name, width, kind in segments:
        for off in range(0, width, LANE):
            row = src[name] + off
            assert row % PACK_SRC_ROWS == 0
            plan[dst + off // LANE] = (row // PACK_SRC_ROWS, kind, min(LANE, width - off))
    return plan


def _pack_w_in_body(plan_ref, *refs):
    q_scale = HEAD_DIM ** -0.5 * LOG2E
    k_scale = HEAD_DIM ** -0.5
    o_ref = refs[-1]
    row = lax.broadcasted_iota(jnp.int32, (LANE, 1), 0)
    for c in range(PACK_CHUNKS):
        chunk = pl.program_id(0) * PACK_CHUNKS + c
        kind, valid = plan_ref[chunk, 1], plan_ref[chunk, 2]
        scale = jnp.where(kind == 1, q_scale, jnp.where(kind == 2, k_scale, 1.0))
        x = jnp.concatenate([refs[2 * c][...], refs[2 * c + 1][...]], 0)
        x = jnp.where(row < valid, x * scale, 0.0)
        o_ref[:, c * LANE:(c + 1) * LANE] = x.T.astype(o_ref.dtype)


def _pack_w_in(w_in, layer, tn):
    depth, d, n = w_in.shape
    w_t = jnp.swapaxes(w_in, 1, 2)
    step_cols = PACK_CHUNKS * LANE
    per_block = tn // step_cols

    def src_spec(c, second):
        return pl.BlockSpec((None, PACK_SRC_ROWS, d),
                            lambda j, plan: (layer, plan[j * PACK_CHUNKS + c, 0] + second, 0))

    return pl.pallas_call(
        _pack_w_in_body,
        grid_spec=pltpu.PrefetchScalarGridSpec(
            num_scalar_prefetch=1,
            grid=(PACKED_IN_WIDTH // step_cols,),
            in_specs=[src_spec(c, second) for c in range(PACK_CHUNKS) for second in (0, 1)],
            out_specs=pl.BlockSpec((None, d, step_cols), lambda j, plan: (j // per_block, 0, j % per_block))),
        out_shape=jax.ShapeDtypeStruct((PACKED_IN_WIDTH // tn, d, tn), _BF),
        compiler_params=_cparams(("arbitrary",)),
    )(jnp.asarray(_pack_plan()), *([w_t] * (2 * PACK_CHUNKS)))


def _pack_w_uq(w):
    r = w.shape[0]
    w = w.reshape(r, MLA_HEADS, MLA_NOPE + MLA_ROPE) * ((MLA_NOPE + MLA_ROPE) ** -0.5 * LOG2E)
    w = jnp.pad(w, ((0, 0), (0, 0), (0, MLA_QK_PAD - MLA_NOPE - MLA_ROPE)))
    return w.reshape(r, MLA_HEADS * MLA_QK_PAD).astype(_BF)


def kernel(x, c, ctx, c_ctx, w_ada, b_ada, w_in, mla_q_norm, mla_kv_norm, w_uq, w_ukv, na_rpb, ret_decay,
           w_o, ln1_g, ln1_b, w_up, conv_w, conv_b, w_down, ln2_g, ln2_b):
    b, t, d = x.shape
    cl = ctx.shape[1]
    depth = w_ada.shape[0]
    dff = conv_w.shape[-1]
    assert b + 1 <= MOD_ROWS
    alpha = (2 * depth) ** 0.25
    ctx_row = b

    cvec = jnp.concatenate([c, c_ctx[None], jnp.zeros((MOD_ROWS - b - 1, d), c.dtype)], 0)
    mod3 = _ada(cvec, w_ada, b_ada)

    tm_in = _tile(b * cl, _tile(t, 1024))
    ctx0 = b * t
    lat_row = lambda tm: (lambda i: i // (t // tm))
    ctx_rowf = lambda i: ctx_row

    rope_pe = _rope_tables(t, MLA_ROPE)
    rope_ret = _rope_tables(t, HEAD_DIM)
    id_pe = _rope_tables(cl, MLA_ROPE, identity=True)
    id_ret = _rope_tables(cl, HEAD_DIM, identity=True)

    xf = x.reshape(b * t, d)
    cf = ctx.reshape(b * cl, d)
    h1, hc1 = xf, cf

    tq = _tile(t, 2048)
    tqc = _tile(cl, 256)
    tn_ff = _tile(dff, 512)
    zero_state = jnp.zeros((b, RET_HEADS, HEAD_DIM, HEAD_DIM), _F32)

    for l in range(depth):
        last = l == depth - 1
        w_in_p = _pack_w_in(w_in, l, 1024)
        w_uq_p = _pack_w_uq(w_uq[l])
        w_ukv_p = w_ukv[l].astype(_BF)
        gq = mla_q_norm[l][None]
        gkv = mla_kv_norm[l][None]
        lg = jnp.log1p(-jnp.exp2(ret_decay[l].astype(_F32)))
        bias = _na_col_bias(na_rpb[l])

        p = _in_proj(h1, hc1, w_in_p, tm_in, t, mod=(mod3, 0, ctx_row) if l == 0 else None)

        qc_m, kc_m, vc_m = _mla_qkv(p, gq, gkv, w_uq_p, w_ukv_p, id_pe, b, cl, _tile(cl, 1024), row0=ctx0)
        q_m, kl_m, vl_m, w_o_b = _mla_qkv(p, gq, gkv, w_uq_p, w_ukv_p, rope_pe, b, t, _tile(t, 1024), cast=(w_o, l))
        y_mla = _dense_attn(
            q_m, pl.BlockSpec((None, None, tq, MLA_QK_PAD), lambda bi, h, i: (bi, h, i, 0)),
            [(kc_m, _head4_spec(cl, MLA_QK_PAD), vc_m, _head4_spec(VT_ROWS, cl)),
             (kl_m, _head4_spec(t, MLA_QK_PAD), vl_m, _head4_spec(VT_ROWS, t))],
            b, MLA_HEADS, t // tq, tq, MLA_V, nsub=tq // _tile(tq, 256))

        y_na = _na_attn(p, bias, b, t, cl)

        yc_ret, s_f, s_b = _retention(p, lg, zero_state, zero_state, id_ret, b, cl, row0=ctx0)
        y_ret, _, _ = _retention(p, lg, s_f, s_b, rope_ret, b, t)

        tm_o = _tile(t, 512)
        x_new, h2 = _proj_ln([y_na, y_mla, y_ret], w_o_b, xf, mod3, l, 2, lat_row(tm_o),
                             ln1_g[l][None], ln1_b[l][None], alpha, tm_o, tm_o // _tile(tm_o, 256), with_next=True)

        tm_d = _tile(t, 256)

        if not last:
            yc_mla = _dense_attn(
                qc_m, pl.BlockSpec((None, None, tqc, MLA_QK_PAD), lambda bi, h, i: (bi, h, i, 0)),
                [(kc_m, _head4_spec(cl, MLA_QK_PAD), vc_m, _head4_spec(VT_ROWS, cl))],
                b, MLA_HEADS, cl // tqc, tqc, MLA_V, nsub=1)
            nbc = cl // tqc
            cq0, ck0 = ctx0 // tqc, ctx0 // cl
            yc_na = _dense_attn(
                p, pl.BlockSpec((tqc, LANE), lambda bi, h, i: (cq0 + bi * nbc + i, CB_NAQ + h)),
                [(p, pl.BlockSpec((cl, LANE), lambda bi, h, i: (ck0 + bi, CB_NAK + h)),
                  p, pl.BlockSpec((cl, LANE), lambda bi, h, i: (ck0 + bi, CB_NAV + h)))],
                b, NA_HEADS, nbc, tqc, HEAD_DIM, nsub=1, v_row_major=True)
            tm_oc = _tile(cl, 256)
            c_new, hc2 = _proj_ln([yc_na, yc_mla, yc_ret], w_o_b, cf, mod3, l, 2, ctx_rowf,
                                  ln1_g[l][None], ln1_b[l][None], alpha, tm_oc, 1, with_next=True)

        g_l, w_up_b, w_down_b = _glu(h2, w_up, w_down, l, conv_w[l], conv_b[l][None], b, t, tn_ff)
        xf, h1 = _proj_ln([g_l], w_down_b, x_new, mod3, l, 5, lat_row(tm_d), ln2_g[l][None], ln2_b[l][None],
                          alpha, tm_d, 1, with_next=not last)
        if not last:
            g_c = _glu(hc2, w_up, w_down, l, conv_w[l], conv_b[l][None], b, cl, tn_ff, nseq=b, w_bf16=w_up_b)
            tm_dc = _tile(cl, 256)
            cf, hc1 = _proj_ln([g_c], w_down_b, c_new, mod3, l, 5, ctx_rowf, ln2_g[l][None], ln2_b[l][None],
                               alpha, tm_dc, 1, with_next=True)

    return xf.reshape(b, t, d)
```

```python
import functools

import numpy as np
import jax
import jax.numpy as jnp
from jax import lax
from jax.experimental import pallas as pl
from jax.experimental.pallas import tpu as pltpu

GRID_W = 64
HEAD_DIM = 128
NA_HEADS = 6
MLA_HEADS = 5
RET_HEADS = 5
NA_WIDTH = NA_HEADS * HEAD_DIM
MLA_WIDTH = MLA_HEADS * HEAD_DIM
RET_WIDTH = RET_HEADS * HEAD_DIM
NA_WIN_R = 8
NA_WIN_C = 16
MLA_Q_RANK = 512
MLA_KV_RANK = 512
MLA_NOPE = 128
MLA_ROPE = 64
MLA_V = 128
MLA_QK_PAD = 256
RET_CHUNK = 128
CONV_W = 3
ROPE_BASE = 10000.0
LN_EPS = 1e-5
RMS_EPS = 1e-6
NEG_INF = -1e30
LOG2E = 1.4426950408889634
IN_SPLITS = (NA_WIDTH, NA_WIDTH, NA_WIDTH, MLA_Q_RANK, MLA_KV_RANK, MLA_ROPE,
             RET_WIDTH, RET_WIDTH, RET_WIDTH, RET_WIDTH)

LANE = 128
CB_CQ, CB_CKV, CB_NAQ, CB_NAK, CB_NAV = 0, 4, 8, 14, 20
CB_RQ, CB_RK, CB_RV, CB_RG, CB_KPE = 26, 31, 36, 41, 46
PACKED_IN_WIDTH = 48 * LANE
MOD_ROWS = 8
VMEM_LIMIT_MB = 56

_BF = jnp.bfloat16
_F32 = jnp.float32


def _cparams(sem, vmem_mb=VMEM_LIMIT_MB):
    return pltpu.CompilerParams(dimension_semantics=sem, vmem_limit_bytes=vmem_mb << 20)


def _tile(n, pref):
    t = min(n, pref)
    while n % t:
        t //= 2
    return t


def _dot(a, b):
    return jnp.dot(a, b, preferred_element_type=_F32)


def _dot_nt(a, b):
    return lax.dot_general(a, b, (((1,), (1,)), ((), ())), preferred_element_type=_F32)


def _silu(x):
    return x / (1.0 + jnp.exp(-x))


def _ada_body(c_ref, w_ref, b_ref, o_ref):
    part = _dot(_silu(c_ref[...]).astype(_BF), w_ref[0].astype(_BF))

    @pl.when(pl.program_id(1) == 0)
    def _():
        o_ref[0] = part + b_ref[0]

    @pl.when(pl.program_id(1) > 0)
    def _():
        o_ref[0] += part


def _ada(cvec, w_ada, b_ada):
    depth, d, n = w_ada.shape
    tk = _tile(d, 256)
    out = pl.pallas_call(
        _ada_body,
        grid=(depth, d // tk),
        in_specs=[pl.BlockSpec((MOD_ROWS, tk), lambda l, k: (0, k)),
                  pl.BlockSpec((1, tk, n), lambda l, k: (l, k, 0)),
                  pl.BlockSpec((1, 1, n), lambda l, k: (l, 0, 0))],
        out_specs=pl.BlockSpec((1, MOD_ROWS, n), lambda l, k: (l, 0, 0)),
        out_shape=jax.ShapeDtypeStruct((depth, MOD_ROWS, n), _F32),
        compiler_params=_cparams(("arbitrary", "arbitrary")),
    )(cvec, w_ada, b_ada.reshape(depth, 1, n))
    return out.reshape(depth * MOD_ROWS, 1, n)


def _mod_spec(d, layer, chunk, row_fn):
    return pl.BlockSpec((1, 1, d), lambda i, *_: (layer * MOD_ROWS + row_fn(i), 0, chunk))


def _in_proj_body(*refs, n_lat, modulated):
    if modulated:
        lat_ref, ctx_ref, shl_ref, scl_ref, shc_ref, scc_ref, w_ref, o_ref, a_scr = refs
    else:
        lat_ref, ctx_ref, w_ref, o_ref, a_scr = refs
    i = pl.program_id(0)

    def stage(src_ref, sh_ref, sc_ref):
        x = src_ref[...]
        if modulated:
            x = x * (1.0 + sc_ref[0]) + sh_ref[0]
        a_scr[...] = x.astype(a_scr.dtype)

    @pl.when((pl.program_id(1) == 0) & (i < n_lat))
    def _():
        stage(lat_ref, shl_ref if modulated else None, scl_ref if modulated else None)

    @pl.when((pl.program_id(1) == 0) & (i >= n_lat))
    def _():
        stage(ctx_ref, shc_ref if modulated else None, scc_ref if modulated else None)

    acc = _dot(a_scr[...], w_ref[...])
    for c in range(o_ref.shape[0]):
        o_ref[c] = acc[:, c * LANE:(c + 1) * LANE].astype(o_ref.dtype)


def _in_proj(lat, ctx, w, tm, t, mod=None):
    d = lat.shape[1]
    nj, _, tn = w.shape
    n_lat, n_ctx = lat.shape[0] // tm, ctx.shape[0] // tm
    lat_i = lambda i: jnp.minimum(i, n_lat - 1)
    in_specs = [pl.BlockSpec((tm, d), lambda i, j: (lat_i(i), 0)),
                pl.BlockSpec((tm, d), lambda i, j: (jnp.maximum(i - n_lat, 0), 0), pipeline_mode=pl.Buffered(1))]
    args = [lat, ctx]
    if mod is not None:
        mod3, layer, ctx_row = mod
        lat_row = lambda i: lat_i(i) // (t // tm)
        in_specs += [_mod_spec(d, layer, 0, lat_row), _mod_spec(d, layer, 1, lat_row),
                     _mod_spec(d, layer, 0, lambda i: ctx_row), _mod_spec(d, layer, 1, lambda i: ctx_row)]
        args += [mod3] * 4
    return pl.pallas_call(
        functools.partial(_in_proj_body, n_lat=n_lat, modulated=mod is not None),
        grid=(n_lat + n_ctx, nj),
        in_specs=in_specs + [pl.BlockSpec((None, d, tn), lambda i, j: (j, 0, 0))],
        out_specs=pl.BlockSpec((tn // LANE, tm, LANE), lambda i, j: (j, i, 0)),
        out_shape=jax.ShapeDtypeStruct((nj * tn // LANE, (n_lat + n_ctx) * tm, LANE), _BF),
        scratch_shapes=[pltpu.VMEM((tm, d), _BF)],
        compiler_params=_cparams(("arbitrary", "arbitrary")),
    )(*args, w)


def _rope_tables(t, rot_dim, identity=False):
    pos = np.arange(t)
    row = (pos // GRID_W).astype(np.float32)
    col = (pos % GRID_W).astype(np.float32)
    nf = rot_dim // 4
    inv = (ROPE_BASE ** (-np.arange(nf, dtype=np.float32) / nf)).astype(np.float32)
    ar = row[:, None] * inv[None]
    ac = col[:, None] * inv[None]
    ang = np.concatenate([ar, ar, ac, ac], -1)
    cos, sin = np.cos(ang), np.sin(ang)
    if identity:
        cos, sin = np.ones_like(cos), np.zeros_like(sin)
    first = (np.arange(rot_dim) % (rot_dim // 2)) < nf
    sin_a = np.where(first, -sin, 0.0)
    sin_b = np.where(first, 0.0, sin)
    pad = LANE - rot_dim
    if pad:
        cos = np.pad(cos, ((0, 0), (0, pad)), constant_values=1.0)
        sin_a = np.pad(sin_a, ((0, 0), (0, pad)))
        sin_b = np.pad(sin_b, ((0, 0), (0, pad)))
    return tuple(jnp.asarray(a, _F32) for a in (cos, sin_a, sin_b))


def _rope(x, cos, sin_a, sin_b, nf):
    return x * cos + pltpu.roll(x, LANE - nf, 1) * sin_a + pltpu.roll(x, nf, 1) * sin_b


def _rms(x, g):
    return x * lax.rsqrt(jnp.mean(x * x, -1, keepdims=True) + RMS_EPS) * g


VT_ROWS = HEAD_DIM + 16


def _values_t(v):
    n = v.shape[0]
    ones = (lax.broadcasted_iota(jnp.int32, (VT_ROWS - HEAD_DIM, n), 0) == 0).astype(_BF)
    return jnp.concatenate([v.astype(_F32).T.astype(_BF), ones], 0)


def _mla_qkv_body(cq_ref, ckv_ref, kpe_ref, gq_ref, gkv_ref, wq_ref, wkv_ref, cos_ref, sa_ref, sb_ref,
                  *rest):
    if len(rest) == 5:
        wo_ref, q_ref, k_ref, vt_ref, wob = rest
        wob[...] = wo_ref[...].astype(_BF)
    else:
        q_ref, k_ref, vt_ref = rest
    cos, sa, sb = cos_ref[...], sa_ref[...], sb_ref[...]
    nf = MLA_ROPE // 4
    wide = lambda ref: jnp.concatenate([ref[c] for c in range(ref.shape[0])], 1).astype(_F32)
    xq = _rms(wide(cq_ref), gq_ref[...]).astype(_BF)
    xkv = _rms(wide(ckv_ref), gkv_ref[...]).astype(_BF)
    pe = _rope(kpe_ref[...].astype(_F32), cos, sa, sb, nf).astype(_BF)
    width = MLA_NOPE + MLA_V
    for h in range(MLA_HEADS):
        q = _dot(xq, wq_ref[:, h * MLA_QK_PAD:(h + 1) * MLA_QK_PAD])
        q_ref[0, h, :, 0:LANE] = q[:, :LANE].astype(_BF)
        q_ref[0, h, :, LANE:] = _rope(q[:, LANE:], cos, sa, sb, nf).astype(_BF)
        kv = _dot(xkv, wkv_ref[:, h * width:(h + 1) * width])
        k_ref[0, h, :, 0:LANE] = kv[:, :MLA_NOPE].astype(_BF)
        k_ref[0, h, :, LANE:] = pe
        vt_ref[0, h] = _values_t(kv[:, MLA_NOPE:])


def _mla_qkv(p, gq, gkv, wq, wkv, tables, b, t, tm, row0=0, cast=None):
    nb = t // tm
    blk0 = row0 // tm
    extra_in, extra_args, extra_out, extra_shape = [], [], [], []
    if cast is not None:
        w, layer = cast
        slab = w.shape[1] // (b * nb)
        assert w.shape[1] % (b * nb) == 0 and slab % 16 == 0
        extra_in = [pl.BlockSpec((None, slab, w.shape[2]), lambda i: (layer, i, 0))]
        extra_args = [w]
        extra_out = [pl.BlockSpec((slab, w.shape[2]), lambda i: (i, 0))]
        extra_shape = [jax.ShapeDtypeStruct(w.shape[1:], _BF)]
    tab_spec = pl.BlockSpec((tm, LANE), lambda i: (i % nb, 0))
    head_spec = lambda rows, cols, ri, ci: pl.BlockSpec(
        (1, MLA_HEADS, rows, cols), lambda i: (i // nb, 0, (i % nb) * ri, (i % nb) * ci))
    return pl.pallas_call(
        _mla_qkv_body,
        grid=(b * nb,),
        in_specs=[pl.BlockSpec((MLA_Q_RANK // LANE, tm, LANE), lambda i: (CB_CQ * LANE // MLA_Q_RANK, blk0 + i, 0)),
                  pl.BlockSpec((MLA_KV_RANK // LANE, tm, LANE), lambda i: (CB_CKV * LANE // MLA_KV_RANK, blk0 + i, 0)),
                  pl.BlockSpec((None, tm, LANE), lambda i: (CB_KPE, blk0 + i, 0)),
                  pl.BlockSpec((1, MLA_Q_RANK), lambda i: (0, 0)),
                  pl.BlockSpec((1, MLA_KV_RANK), lambda i: (0, 0)),
                  pl.BlockSpec(wq.shape, lambda i: (0, 0)),
                  pl.BlockSpec(wkv.shape, lambda i: (0, 0)),
                  tab_spec, tab_spec, tab_spec] + extra_in,
        out_specs=[head_spec(tm, MLA_QK_PAD, 1, 0), head_spec(tm, MLA_QK_PAD, 1, 0),
                   head_spec(VT_ROWS, tm, 0, 1)] + extra_out,
        out_shape=[jax.ShapeDtypeStruct((b, MLA_HEADS, t, MLA_QK_PAD), _BF),
                   jax.ShapeDtypeStruct((b, MLA_HEADS, t, MLA_QK_PAD), _BF),
                   jax.ShapeDtypeStruct((b, MLA_HEADS, VT_ROWS, t), _BF)] + extra_shape,
        compiler_params=_cparams(("arbitrary",)),
    )(p, p, p, gq, gkv, wq, wkv, *tables, *extra_args)


def _skewed(n, stages):
    carry = [None] * n
    for step in range(n + len(stages) - 1):
        for s, stage in enumerate(stages):
            j = step - s
            if 0 <= j < n:
                carry[j] = stage(j, carry[j])


def _attn_stages(scores_fn, values_fn, store_fn):
    def s_scores(j, _):
        return scores_fn(j)

    def s_max(j, sc):
        return sc, functools.reduce(jnp.maximum, [jnp.max(s, 0, keepdims=True) for s in sc])

    def s_exp(j, c):
        sc, m = c
        return [jnp.exp2(s - m).astype(_BF) for s in sc]

    def s_pv(j, p):
        acc = functools.reduce(jnp.add, [_dot(vt, pi) for vt, pi in zip(values_fn(j), p)])
        store_fn(j, (acc[0:HEAD_DIM, :] / acc[HEAD_DIM:HEAD_DIM + 1, :]).T)

    return [s_scores, s_max, s_exp, s_pv]


def _attn_body(*refs, nseg, nsub, v_row_major):
    q_ref, o_ref = refs[0], refs[-1]
    ks = [refs[1 + 2 * i][...] for i in range(nseg)]
    vts = [refs[2 + 2 * i][...] for i in range(nseg)]
    if v_row_major:
        vts = [_values_t(v) for v in vts]
    ts = q_ref.shape[0] // nsub
    rows = lambda j: slice(j * ts, (j + 1) * ts)

    def store(j, out):
        o_ref[rows(j), :] = out.astype(o_ref.dtype)

    _skewed(nsub, _attn_stages(lambda j: [_dot_nt(k, q_ref[rows(j), :]) for k in ks], lambda j: vts, store))


def _dense_attn(q, q_spec, kv, b, heads, tq_blocks, tq, dv, nsub, v_row_major=False):
    args, specs = [q], [q_spec]
    for k, ks, v, vs in kv:
        args += [k, v]
        specs += [ks, vs]
    return pl.pallas_call(
        functools.partial(_attn_body, nseg=len(kv), nsub=nsub, v_row_major=v_row_major),
        grid=(b, heads, tq_blocks),
        in_specs=specs,
        out_specs=pl.BlockSpec((tq, dv), lambda bi, h, i: (bi * tq_blocks + i, h)),
        out_shape=jax.ShapeDtypeStruct((b * tq_blocks * tq, heads * dv), _BF),
        compiler_params=_cparams(("arbitrary", "arbitrary", "arbitrary")),
    )(*args)


def _head4_spec(rows, cols):
    return pl.BlockSpec((None, None, rows, cols), lambda bi, h, i: (bi, h, 0, 0))


NA_GROUP = 4
NA_KEY_ROWS = 12


def _na_group_window(gi, rows):
    ngroups = rows // NA_GROUP
    if gi == 0:
        return 0, 0
    if gi == ngroups - 1:
        return 2, rows - NA_KEY_ROWS
    return 1, gi * NA_GROUP - NA_WIN_R // 2


NA_DR = 2 * NA_WIN_R - 1


def _na_col_bias(rpb):
    c = np.arange(GRID_W)
    col_start = np.clip(c - NA_WIN_C // 2, 0, GRID_W - NA_WIN_C)
    kc = np.arange(GRID_W)
    in_win = (kc[None, :] >= col_start[:, None]) & (kc[None, :] < col_start[:, None] + NA_WIN_C)
    dc = np.clip(kc[None, :] - c[:, None] + NA_WIN_C - 1, 0, 2 * NA_WIN_C - 2)
    col_sel = (dc[:, :, None] == np.arange(2 * NA_WIN_C - 1)).astype(np.float32)
    toep = jnp.einsum('hrj,cqj->hrqc', rpb.astype(_F32), col_sel, precision=lax.Precision.HIGHEST)
    toep = jnp.where(in_win.T[None, None], toep * LOG2E, NEG_INF)
    masked = jnp.full((rpb.shape[0], 1, GRID_W, GRID_W), NEG_INF, _F32)
    return jnp.concatenate([toep, masked], 1)


def _na_block_index():
    g = np.arange(NA_GROUP)[:, None]
    kr = np.arange(NA_KEY_ROWS)[None, :]
    half = NA_WIN_R // 2
    dr = np.stack([kr - g + NA_WIN_R - 1, kr - g + NA_WIN_R - 1 - half, kr - g - 1])
    w0 = np.stack([0 * g, g, half + 0 * g])
    row_ok = (kr[None] >= w0) & (kr[None] < w0 + NA_WIN_R)
    assert np.all((dr[row_ok] >= 0) & (dr[row_ok] < NA_DR))
    return np.where(row_ok, dr, NA_DR)


def _na_body(q_ref, k_ref, v_ref, kc_ref, vc_ref, cb_ref, o_ref, bias_ref, vt_ref, *, rows):
    @pl.when(pl.program_id(1) == 0)
    def _():
        idx = _na_block_index()
        for typ in range(3):
            for kr in range(NA_KEY_ROWS):
                for gp in range(NA_GROUP // 2):
                    pair = jnp.concatenate([cb_ref[int(idx[typ, 2 * gp, kr])], cb_ref[int(idx[typ, 2 * gp + 1, kr])]], 1)
                    bias_ref[typ, kr * GRID_W:(kr + 1) * GRID_W, gp * 2 * GRID_W:(gp + 1) * 2 * GRID_W] = pair

    vt_ref[...] = _values_t(v_ref[...])
    kc = kc_ref[...]
    vct = _values_t(vc_ref[...])
    gq = NA_GROUP * GRID_W

    def band(gi):
        ws = _na_group_window(gi, rows)[1]
        return slice(ws * GRID_W, (ws + NA_KEY_ROWS) * GRID_W)

    def scores(gi):
        q = q_ref[gi * gq:(gi + 1) * gq, :]
        return [_dot_nt(k_ref[band(gi), :], q) + bias_ref[_na_group_window(gi, rows)[0]], _dot_nt(kc, q)]

    def store(gi, out):
        o_ref[gi * gq:(gi + 1) * gq, :] = out.astype(o_ref.dtype)

    _skewed(rows // NA_GROUP, _attn_stages(scores, lambda gi: [vt_ref[:, band(gi)], vct], store))


def _na_attn(p, bias, b, t, c):
    rows = t // GRID_W
    cblk0 = b * t // c
    assert t % GRID_W == 0 and rows % NA_GROUP == 0 and rows >= NA_KEY_ROWS + NA_GROUP and 2 * GRID_W == LANE
    return pl.pallas_call(
        functools.partial(_na_body, rows=rows),
        grid=(NA_HEADS, b),
        in_specs=[pl.BlockSpec((None, t, LANE), lambda h, bi: (CB_NAQ + h, bi, 0)),
                  pl.BlockSpec((None, t, LANE), lambda h, bi: (CB_NAK + h, bi, 0)),
                  pl.BlockSpec((None, t, LANE), lambda h, bi: (CB_NAV + h, bi, 0)),
                  pl.BlockSpec((None, c, LANE), lambda h, bi: (CB_NAK + h, cblk0 + bi, 0)),
                  pl.BlockSpec((None, c, LANE), lambda h, bi: (CB_NAV + h, cblk0 + bi, 0)),
                  pl.BlockSpec((None, NA_DR + 1, GRID_W, GRID_W), lambda h, bi: (h, 0, 0, 0))],
        out_specs=pl.BlockSpec((t, LANE), lambda h, bi: (bi, h)),
        out_shape=jax.ShapeDtypeStruct((b * t, NA_WIDTH), _BF),
        scratch_shapes=[pltpu.VMEM((3, NA_KEY_ROWS * GRID_W, NA_GROUP * GRID_W), _F32),
                        pltpu.VMEM((VT_ROWS, t), _BF)],
        compiler_params=_cparams(("arbitrary", "arbitrary")),
    )(p, p, p, p, p, bias)


def _ret_body(lg_ref, q_ref, k_ref, v_ref, g_ref, s0f_ref, s0b_ref, cos_ref, sa_ref, sb_ref,
              o_ref, sf_ref, sbo_ref, kr_scr, u_hist, st_hist, *, n):
    h = pl.program_id(0)
    lgf = lg_ref[0, h]
    lgb = lg_ref[1, h]
    L = RET_CHUNK
    pos_c = lax.broadcasted_iota(jnp.int32, (L, 1), 0).astype(_F32)
    pos_r = lax.broadcasted_iota(jnp.int32, (1, L), 1).astype(_F32)
    diff = pos_c - pos_r
    decay = jnp.where(diff > 0, jnp.exp(jnp.maximum(diff, 0.0) * lgf),
                      jnp.where(diff < 0, jnp.exp(jnp.maximum(-diff, 0.0) * lgb), 2.0))
    qdec_f = jnp.exp((pos_c + 1.0) * lgf)
    kdec_f = jnp.exp((L - 1.0 - pos_c) * lgf)
    qdec_b = jnp.exp((L - pos_c) * lgb)
    kdec_b = jnp.exp(pos_c * lgb)
    cd_f = jnp.exp(L * lgf)
    cd_b = jnp.exp(L * lgb)
    nf = HEAD_DIM // 4

    def rows_of(i):
        return slice(i * L, (i + 1) * L)

    def roped(ref, i):
        r = rows_of(i)
        return _rope(ref[r, :].astype(_F32), cos_ref[r, :], sa_ref[r, :], sb_ref[r, :], nf)

    for i in range(n):
        k = roped(k_ref, i)
        kr_scr[rows_of(i), :] = k.astype(_BF)
        kd = jnp.concatenate([k * kdec_f, k * kdec_b], 1).astype(_BF)
        u_hist[i] = lax.dot_general(kd, v_ref[rows_of(i), :], (((0,), (0,)), ((), ())),
                                    preferred_element_type=_F32)

    state = s0f_ref[...]
    for i in range(n):
        st_hist[i, 0:HEAD_DIM, :] = state.astype(_BF)
        state = cd_f * state + u_hist[i, 0:HEAD_DIM, :]
    sf_ref[...] = state
    state = s0b_ref[...]
    for i in reversed(range(n)):
        st_hist[i, HEAD_DIM:, :] = state.astype(_BF)
        state = cd_b * state + u_hist[i, HEAD_DIM:, :]
    sbo_ref[...] = state

    for i in range(n):
        r = rows_of(i)
        q = roped(q_ref, i)
        a = _dot_nt(q.astype(_BF), kr_scr[r, :]) * decay
        qd = jnp.concatenate([q * qdec_f, q * qdec_b], 1).astype(_BF)
        o = _dot(a.astype(_BF), v_ref[r, :]) + _dot(qd, st_hist[i])
        mu = jnp.mean(o, -1, keepdims=True)
        oc = o - mu
        var = jnp.mean(oc * oc, -1, keepdims=True)
        y = oc * lax.rsqrt(var + LN_EPS) * _silu(g_ref[r, :].astype(_F32))
        o_ref[r, :] = y.astype(o_ref.dtype)


def _retention(p, lg, s0f, s0b, tables, b, t, row0=0):
    n = t // RET_CHUNK
    blk0 = row0 // t
    col = lambda cb: pl.BlockSpec((None, t, LANE), lambda h, bi: (cb + h, blk0 + bi, 0))
    st_spec = pl.BlockSpec((None, None, HEAD_DIM, HEAD_DIM), lambda h, bi: (bi, h, 0, 0))
    tab_spec = pl.BlockSpec((t, LANE), lambda h, bi: (0, 0))
    st_shape = jax.ShapeDtypeStruct((b, RET_HEADS, HEAD_DIM, HEAD_DIM), _F32)
    return pl.pallas_call(
        functools.partial(_ret_body, n=n),
        grid=(RET_HEADS, b),
        in_specs=[pl.BlockSpec(memory_space=pltpu.SMEM),
                  col(CB_RQ), col(CB_RK), col(CB_RV), col(CB_RG), st_spec, st_spec,
                  tab_spec, tab_spec, tab_spec],
        out_specs=[pl.BlockSpec((t, LANE), lambda h, bi: (bi, h)), st_spec, st_spec],
        out_shape=[jax.ShapeDtypeStruct((b * t, RET_WIDTH), _BF), st_shape, st_shape],
        scratch_shapes=[pltpu.VMEM((t, HEAD_DIM), _BF),
                        pltpu.VMEM((n, 2 * HEAD_DIM, HEAD_DIM), _F32),
                        pltpu.VMEM((n, 2 * HEAD_DIM, HEAD_DIM), _BF)],
        compiler_params=_cparams(("arbitrary", "arbitrary")),
    )(lg, p, p, p, p, s0f, s0b, *tables)


def _ln_epilogue(y, rows, x_ref, gate_ref, lng_ref, lnb_ref, alpha, xo_ref, ho_ref, sh_ref, sc_ref):
    z = alpha * x_ref[rows, :] + gate_ref[0] * y
    mu = jnp.mean(z, -1, keepdims=True)
    zc = z - mu
    var = jnp.mean(zc * zc, -1, keepdims=True)
    xn = zc * lax.rsqrt(var + LN_EPS) * lng_ref[...] + lnb_ref[...]
    xo_ref[rows, :] = xn
    if ho_ref is not None:
        ho_ref[rows, :] = (xn * (1.0 + sc_ref[0]) + sh_ref[0]).astype(ho_ref.dtype)


def _proj_ln_body(*refs, n_in, alpha, with_next, nsub):
    a_refs, refs = refs[:n_in], refs[n_in:]
    if with_next:
        w_ref, x_ref, gate_ref, sh_ref, sc_ref, lng_ref, lnb_ref, xo_ref, ho_ref = refs
    else:
        w_ref, x_ref, gate_ref, lng_ref, lnb_ref, xo_ref = refs
        sh_ref = sc_ref = ho_ref = None
    ts = x_ref.shape[0] // nsub
    for j in range(nsub):
        r = slice(j * ts, (j + 1) * ts)
        a = a_refs[0][r, :] if n_in == 1 else jnp.concatenate([ar[r, :] for ar in a_refs], 1)
        _ln_epilogue(_dot(a, w_ref[...]), r, x_ref, gate_ref, lng_ref, lnb_ref, alpha, xo_ref, ho_ref, sh_ref, sc_ref)


def _proj_ln(a_list, w, x2, mod3, layer, gate_chunk, row_fn, lng, lnb, alpha, tm, nsub, with_next):
    m, d = x2.shape
    row = lambda width: pl.BlockSpec((tm, width), lambda i: (i, 0))
    vec = pl.BlockSpec((1, d), lambda i: (0, 0))
    w_spec = pl.BlockSpec(w.shape, lambda i: (0, 0), pipeline_mode=pl.Buffered(1))
    in_specs = [row(a.shape[1]) for a in a_list] + [w_spec, row(d), _mod_spec(d, layer, gate_chunk, row_fn)]
    args = list(a_list) + [w, x2, mod3]
    if with_next:
        nxt = (layer, gate_chunk + 1) if gate_chunk < 5 else (layer + 1, 0)
        in_specs += [_mod_spec(d, nxt[0], nxt[1], row_fn), _mod_spec(d, nxt[0], nxt[1] + 1, row_fn)]
        args += [mod3, mod3]
    in_specs += [vec, vec]
    args += [lng, lnb]
    out_specs = [row(d)] + ([row(d)] if with_next else [])
    out_shape = [jax.ShapeDtypeStruct((m, d), _F32)] + ([jax.ShapeDtypeStruct((m, d), _BF)] if with_next else [])
    out = pl.pallas_call(
        functools.partial(_proj_ln_body, n_in=len(a_list), alpha=alpha, with_next=with_next, nsub=nsub),
        grid=(m // tm,),
        in_specs=in_specs, out_specs=out_specs, out_shape=out_shape,
        compiler_params=_cparams(("arbitrary",)),
    )(*args)
    return (out[0], out[1]) if with_next else (out[0], None)


def _glu_chunks(t):
    rc = _tile(t, 1024)
    return [(i * rc, rc) for i in range(t // rc)]


def _glu_body(h_ref, wa_ref, wu_ref, *rest, t, cast):
    if cast:
        wd_ref, cw_ref, cb_ref, o_ref, wab, wub, wdb, a_scr = rest
        wdb[...] = wd_ref[...].astype(_BF)

        @pl.when(pl.program_id(1) == 0)
        def _():
            wab[...] = wa_ref[...].astype(_BF)
            wub[...] = wu_ref[...].astype(_BF)
    else:
        cw_ref, cb_ref, o_ref, a_scr = rest
        wab, wub = wa_ref, wu_ref

    nseq = a_scr.shape[0]
    units = [(q, r0, rc) for q in range(nseq) for r0, rc in _glu_chunks(t)]
    halo = 8
    tn = o_ref.shape[1]
    zeros = jnp.zeros((halo, tn), _F32)
    for q in range(nseq):
        a_scr[q, 0:halo, :] = zeros
        a_scr[q, halo + t:, :] = zeros
    w0, w1, w2 = cw_ref[0:1, :], cw_ref[1:2, :], cw_ref[2:3, :]
    bias = cb_ref[...]
    u_prev = None
    for i in range(len(units) + 1):
        u_cur = None
        if i < len(units):
            q, r0, rc = units[i]
            hc = h_ref[q * t + r0:q * t + r0 + rc, :]
            a_scr[q, halo + r0:halo + r0 + rc, :] = _dot(hc, wab[...])
            u_cur = _dot(hc, wub[...])
        if i > 0:
            q, r0, rc = units[i - 1]
            base = halo + r0
            lo = a_scr[q, base - 1:base - 1 + rc, :]
            mid = a_scr[q, base:base + rc, :]
            hi = a_scr[q, base + 1:base + 1 + rc, :]
            acc = bias + lo * w0 + mid * w1 + hi * w2
            o_ref[q * t + r0:q * t + r0 + rc, :] = (_silu(acc) * u_prev).astype(o_ref.dtype)
        u_prev = u_cur


def _glu(h, w_up, w_down, layer, conv_w, conv_b, b, t, tn, nseq=1, w_bf16=None):
    d = h.shape[1]
    dff = conv_w.shape[1]
    nj = dff // tn
    cast = w_bf16 is None
    steps = nj * (b // nseq)
    g_spec = pl.BlockSpec((nseq * t, tn), lambda j, bi: (bi, j))
    g_shape = jax.ShapeDtypeStruct((b * t, dff), _BF)
    wb_spec = pl.BlockSpec((None, d, tn), lambda j, bi: (j, 0, 0))
    if cast:
        w_args = (w_up, w_up)
        w_specs = [pl.BlockSpec((None, d, tn), lambda j, bi: (layer, 0, j)),
                   pl.BlockSpec((None, d, tn), lambda j, bi: (layer, 0, nj + j))]
        wb_shape = jax.ShapeDtypeStruct((nj, d, tn), _BF)
        slab = dff // steps
        assert dff % steps == 0 and slab % 16 == 0
        step = lambda j, bi: j * (b // nseq) + bi
        w_args += (w_down,)
        w_specs.append(pl.BlockSpec((None, slab, w_down.shape[2]), lambda j, bi: (layer, step(j, bi), 0)))
        wd_spec = pl.BlockSpec((slab, w_down.shape[2]), lambda j, bi: (step(j, bi), 0))
        wd_shape = jax.ShapeDtypeStruct(w_down.shape[1:], _BF)
        out_specs, out_shape = [g_spec, wb_spec, wb_spec, wd_spec], [g_shape, wb_shape, wb_shape, wd_shape]
    else:
        w_args, w_specs = tuple(w_bf16), [wb_spec, wb_spec]
        out_specs, out_shape = g_spec, g_shape
    out = pl.pallas_call(
        functools.partial(_glu_body, t=t, cast=cast),
        grid=(nj, b // nseq),
        in_specs=[pl.BlockSpec((nseq * t, d), lambda j, bi: (bi, 0)), *w_specs,
                  pl.BlockSpec((CONV_W, tn), lambda j, bi: (0, j)),
                  pl.BlockSpec((1, tn), lambda j, bi: (0, j))],
        out_specs=out_specs, out_shape=out_shape,
        scratch_shapes=[pltpu.VMEM((nseq, t + 16, tn), _F32)],
        compiler_params=_cparams(("arbitrary", "arbitrary")),
    )(h, *w_args, conv_w, conv_b)
    return (out[0], (out[1], out[2]), out[3]) if cast else out


PACK_SRC_ROWS = 64
PACK_CHUNKS = 4


def _pack_plan():
    src = dict(zip(("na_q", "na_k", "na_v", "cq", "ckv", "kpe", "r_q", "r_k", "r_v", "r_g"),
                   [int(i) for i in np.cumsum((0,) + IN_SPLITS[:-1])]))
    plan = np.zeros((PACKED_IN_WIDTH // LANE, 3), np.int32)
    segments = [(CB_CQ, "cq", MLA_Q_RANK + MLA_KV_RANK, 0), (CB_NAQ, "na_q", NA_WIDTH, 1),
                (CB_NAK, "na_k", 2 * NA_WIDTH, 0), (CB_RQ, "r_q", RET_WIDTH, 0), (CB_RK, "r_k", RET_WIDTH, 2),
                (CB_RV, "r_v", 2 * RET_WIDTH, 0), (CB_KPE, "kpe", MLA_ROPE, 0)]
    for dst, name, width, kind in segments:
        for off in range(0, width, LANE):
            row = src[name] + off
            assert row % PACK_SRC_ROWS == 0
            plan[dst + off // LANE] = (row // PACK_SRC_ROWS, kind, min(LANE, width - off))
    return plan


def _pack_w_in_body(plan_ref, *refs):
    q_scale = HEAD_DIM ** -0.5 * LOG2E
    k_scale = HEAD_DIM ** -0.5
    o_ref = refs[-1]
    row = lax.broadcasted_iota(jnp.int32, (LANE, 1), 0)
    for c in range(PACK_CHUNKS):
        chunk = pl.program_id(0) * PACK_CHUNKS + c
        kind, valid = plan_ref[chunk, 1], plan_ref[chunk, 2]
        scale = jnp.where(kind == 1, q_scale, jnp.where(kind == 2, k_scale, 1.0))
        x = jnp.concatenate([refs[2 * c][...], refs[2 * c + 1][...]], 0)
        x = jnp.where(row < valid, x * scale, 0.0)
        o_ref[:, c * LANE:(c + 1) * LANE] = x.T.astype(o_ref.dtype)


def _pack_w_in(w_in, layer, tn):
    depth, d, n = w_in.shape
    w_t = jnp.swapaxes(w_in, 1, 2)
    step_cols = PACK_CHUNKS * LANE
    per_block = tn // step_cols

    def src_spec(c, second):
        return pl.BlockSpec((None, PACK_SRC_ROWS, d),
                            lambda j, plan: (layer, plan[j * PACK_CHUNKS + c, 0] + second, 0))

    return pl.pallas_call(
        _pack_w_in_body,
        grid_spec=pltpu.PrefetchScalarGridSpec(
            num_scalar_prefetch=1,
            grid=(PACKED_IN_WIDTH // step_cols,),
            in_specs=[src_spec(c, second) for c in range(PACK_CHUNKS) for second in (0, 1)],
            out_specs=pl.BlockSpec((None, d, step_cols), lambda j, plan: (j // per_block, 0, j % per_block))),
        out_shape=jax.ShapeDtypeStruct((PACKED_IN_WIDTH // tn, d, tn), _BF),
        compiler_params=_cparams(("arbitrary",)),
    )(jnp.asarray(_pack_plan()), *([w_t] * (2 * PACK_CHUNKS)))


def _pack_w_uq(w):
    r = w.shape[0]
    w = w.reshape(r, MLA_HEADS, MLA_NOPE + MLA_ROPE) * ((MLA_NOPE + MLA_ROPE) ** -0.5 * LOG2E)
    w = jnp.pad(w, ((0, 0), (0, 0), (0, MLA_QK_PAD - MLA_NOPE - MLA_ROPE)))
    return w.reshape(r, MLA_HEADS * MLA_QK_PAD).astype(_BF)


def kernel(x, c, ctx, c_ctx, w_ada, b_ada, w_in, mla_q_norm, mla_kv_norm, w_uq, w_ukv, na_rpb, ret_decay,
           w_o, ln1_g, ln1_b, w_up, conv_w, conv_b, w_down, ln2_g, ln2_b):
    b, t, d = x.shape
    cl = ctx.shape[1]
    depth = w_ada.shape[0]
    dff = conv_w.shape[-1]
    assert b + 1 <= MOD_ROWS
    alpha = (2 * depth) ** 0.25
    ctx_row = b

    cvec = jnp.concatenate([c, c_ctx[None], jnp.zeros((MOD_ROWS - b - 1, d), c.dtype)], 0)
    mod3 = _ada(cvec, w_ada, b_ada)

    tm_in = _tile(b * cl, _tile(t, 1024))
    ctx0 = b * t
    lat_row = lambda tm: (lambda i: i // (t // tm))
    ctx_rowf = lambda i: ctx_row

    rope_pe = _rope_tables(t, MLA_ROPE)
    rope_ret = _rope_tables(t, HEAD_DIM)
    id_pe = _rope_tables(cl, MLA_ROPE, identity=True)
    id_ret = _rope_tables(cl, HEAD_DIM, identity=True)

    xf = x.reshape(b * t, d)
    cf = ctx.reshape(b * cl, d)
    h1, hc1 = xf, cf

    tq = _tile(t, 2048)
    tqc = _tile(cl, 256)
    tn_ff = _tile(dff, 512)
    zero_state = jnp.zeros((b, RET_HEADS, HEAD_DIM, HEAD_DIM), _F32)

    for l in range(depth):
        last = l == depth - 1
        w_in_p = _pack_w_in(w_in, l, 1024)
        w_uq_p = _pack_w_uq(w_uq[l])
        w_ukv_p = w_ukv[l].astype(_BF)
        gq = mla_q_norm[l][None]
        gkv = mla_kv_norm[l][None]
        lg = jnp.log1p(-jnp.exp2(ret_decay[l].astype(_F32)))
        bias = _na_col_bias(na_rpb[l])

        p = _in_proj(h1, hc1, w_in_p, tm_in, t, mod=(mod3, 0, ctx_row) if l == 0 else None)

        qc_m, kc_m, vc_m = _mla_qkv(p, gq, gkv, w_uq_p, w_ukv_p, id_pe, b, cl, _tile(cl, 1024), row0=ctx0)
        q_m, kl_m, vl_m, w_o_b = _mla_qkv(p, gq, gkv, w_uq_p, w_ukv_p, rope_pe, b, t, _tile(t, 1024), cast=(w_o, l))
        y_mla = _dense_attn(
            q_m, pl.BlockSpec((None, None, tq, MLA_QK_PAD), lambda bi, h, i: (bi, h, i, 0)),
            [(kc_m, _head4_spec(cl, MLA_QK_PAD), vc_m, _head4_spec(VT_ROWS, cl)),
             (kl_m, _head4_spec(t, MLA_QK_PAD), vl_m, _head4_spec(VT_ROWS, t))],
            b, MLA_HEADS, t // tq, tq, MLA_V, nsub=tq // _tile(tq, 256))

        y_na = _na_attn(p, bias, b, t, cl)

        yc_ret, s_f, s_b = _retention(p, lg, zero_state, zero_state, id_ret, b, cl, row0=ctx0)
        y_ret, _, _ = _retention(p, lg, s_f, s_b, rope_ret, b, t)

        tm_o = _tile(t, 512)
        x_new, h2 = _proj_ln([y_na, y_mla, y_ret], w_o_b, xf, mod3, l, 2, lat_row(tm_o),
                             ln1_g[l][None], ln1_b[l][None], alpha, tm_o, tm_o // _tile(tm_o, 256), with_next=True)

        tm_d = _tile(t, 256)

        if not last:
            yc_mla = _dense_attn(
                qc_m, pl.BlockSpec((None, None, tqc, MLA_QK_PAD), lambda bi, h, i: (bi, h, i, 0)),
                [(kc_m, _head4_spec(cl, MLA_QK_PAD), vc_m, _head4_spec(VT_ROWS, cl))],
                b, MLA_HEADS, cl // tqc, tqc, MLA_V, nsub=1)
            nbc = cl // tqc
            cq0, ck0 = ctx0 // tqc, ctx0 // cl
            yc_na = _dense_attn(
                p, pl.BlockSpec((None, tqc, LANE), lambda bi, h, i: (CB_NAQ + h, cq0 + bi * nbc + i, 0)),
                [(p, pl.BlockSpec((None, cl, LANE), lambda bi, h, i: (CB_NAK + h, ck0 + bi, 0)),
                  p, pl.BlockSpec((None, cl, LANE), lambda bi, h, i: (CB_NAV + h, ck0 + bi, 0)))],
                b, NA_HEADS, nbc, tqc, HEAD_DIM, nsub=1, v_row_major=True)
            tm_oc = _tile(cl, 256)
            c_new, hc2 = _proj_ln([yc_na, yc_mla, yc_ret], w_o_b, cf, mod3, l, 2, ctx_rowf,
                                  ln1_g[l][None], ln1_b[l][None], alpha, tm_oc, 1, with_next=True)

        g_l, w_up_b, w_down_b = _glu(h2, w_up, w_down, l, conv_w[l], conv_b[l][None], b, t, tn_ff)
        xf, h1 = _proj_ln([g_l], w_down_b, x_new, mod3, l, 5, lat_row(tm_d), ln2_g[l][None], ln2_b[l][None],
                          alpha, tm_d, 1, with_next=not last)
        if not last:
            g_c = _glu(hc2, w_up, w_down, l, conv_w[l], conv_b[l][None], b, cl, tn_ff, nseq=b, w_bf16=w_up_b)
            tm_dc = _tile(cl, 256)
            cf, hc1 = _proj_ln([g_c], w_down_b, c_new, mod3, l, 5, ctx_rowf, ln2_g[l][None], ln2_b[l][None],
                               alpha, tm_dc, 1, with_next=True)

    return xf.reshape(b, t, d)
```

```python
import functools

import numpy as np
import jax
import jax.numpy as jnp
from jax import lax
from jax.experimental import pallas as pl
from jax.experimental.pallas import tpu as pltpu

GRID_W = 64
HEAD_DIM = 128
NA_HEADS = 6
MLA_HEADS = 5
RET_HEADS = 5
NA_WIDTH = NA_HEADS * HEAD_DIM
MLA_WIDTH = MLA_HEADS * HEAD_DIM
RET_WIDTH = RET_HEADS * HEAD_DIM
NA_WIN_R = 8
NA_WIN_C = 16
MLA_Q_RANK = 512
MLA_KV_RANK = 512
MLA_NOPE = 128
MLA_ROPE = 64
MLA_V = 128
MLA_QK_PAD = 256
RET_CHUNK = 128
CONV_W = 3
ROPE_BASE = 10000.0
LN_EPS = 1e-5
RMS_EPS = 1e-6
NEG_INF = -1e30
LOG2E = 1.4426950408889634
IN_SPLITS = (NA_WIDTH, NA_WIDTH, NA_WIDTH, MLA_Q_RANK, MLA_KV_RANK, MLA_ROPE,
             RET_WIDTH, RET_WIDTH, RET_WIDTH, RET_WIDTH)

LANE = 128
CB_RET, RET_BLOCKS = 0, 4
CB_CQ = CB_RET + RET_HEADS * RET_BLOCKS
CB_NA, NA_BLOCKS = CB_CQ + 4, 3
CB_KPE = CB_NA + NA_HEADS * NA_BLOCKS
CB_CKV = CB_KPE + 2
PACKED_IN_WIDTH = 48 * LANE
MOD_ROWS = 8
VMEM_LIMIT_MB = 56

_BF = jnp.bfloat16
_F32 = jnp.float32


def _cparams(sem, vmem_mb=VMEM_LIMIT_MB):
    return pltpu.CompilerParams(dimension_semantics=sem, vmem_limit_bytes=vmem_mb << 20)


def _tile(n, pref):
    t = min(n, pref)
    while n % t:
        t //= 2
    return t


def _dot(a, b):
    return jnp.dot(a, b, preferred_element_type=_F32)


def _dot_nt(a, b):
    return lax.dot_general(a, b, (((1,), (1,)), ((), ())), preferred_element_type=_F32)


def _silu(x):
    return x / (1.0 + jnp.exp(-x))


def _ada_body(c_ref, w_ref, b_ref, o_ref):
    part = _dot(_silu(c_ref[...]).astype(_BF), w_ref[0].astype(_BF))

    @pl.when(pl.program_id(1) == 0)
    def _():
        o_ref[0] = part + b_ref[0]

    @pl.when(pl.program_id(1) > 0)
    def _():
        o_ref[0] += part


def _ada(cvec, w_ada, b_ada):
    depth, d, n = w_ada.shape
    tk = _tile(d, 256)
    out = pl.pallas_call(
        _ada_body,
        grid=(depth, d // tk),
        in_specs=[pl.BlockSpec((MOD_ROWS, tk), lambda l, k: (0, k)),
                  pl.BlockSpec((1, tk, n), lambda l, k: (l, k, 0)),
                  pl.BlockSpec((1, 1, n), lambda l, k: (l, 0, 0))],
        out_specs=pl.BlockSpec((1, MOD_ROWS, n), lambda l, k: (l, 0, 0)),
        out_shape=jax.ShapeDtypeStruct((depth, MOD_ROWS, n), _F32),
        compiler_params=_cparams(("arbitrary", "arbitrary")),
    )(cvec, w_ada, b_ada.reshape(depth, 1, n))
    return out.reshape(depth * MOD_ROWS, 1, n)


def _mod_spec(d, layer, chunk, row_fn):
    return pl.BlockSpec((1, 1, d), lambda i, *_: (layer * MOD_ROWS + row_fn(i), 0, chunk))


def _in_proj_body(*refs, n_lat, modulated):
    if modulated:
        lat_ref, ctx_ref, shl_ref, scl_ref, shc_ref, scc_ref, w_ref, o_ref, a_scr = refs
    else:
        lat_ref, ctx_ref, w_ref, o_ref, a_scr = refs
    i = pl.program_id(0)

    def stage(src_ref, sh_ref, sc_ref):
        x = src_ref[...]
        if modulated:
            x = x * (1.0 + sc_ref[0]) + sh_ref[0]
        a_scr[...] = x.astype(a_scr.dtype)

    @pl.when((pl.program_id(1) == 0) & (i < n_lat))
    def _():
        stage(lat_ref, shl_ref if modulated else None, scl_ref if modulated else None)

    @pl.when((pl.program_id(1) == 0) & (i >= n_lat))
    def _():
        stage(ctx_ref, shc_ref if modulated else None, scc_ref if modulated else None)

    o_ref[...] = _dot(a_scr[...], w_ref[...]).astype(o_ref.dtype)


def _in_proj(lat, ctx, w, tm, t, mod=None):
    d = lat.shape[1]
    nj, _, tn = w.shape
    n_lat, n_ctx = lat.shape[0] // tm, ctx.shape[0] // tm
    lat_i = lambda i: jnp.minimum(i, n_lat - 1)
    in_specs = [pl.BlockSpec((tm, d), lambda i, j: (lat_i(i), 0)),
                pl.BlockSpec((tm, d), lambda i, j: (jnp.maximum(i - n_lat, 0), 0), pipeline_mode=pl.Buffered(1))]
    args = [lat, ctx]
    if mod is not None:
        mod3, layer, ctx_row = mod
        lat_row = lambda i: lat_i(i) // (t // tm)
        in_specs += [_mod_spec(d, layer, 0, lat_row), _mod_spec(d, layer, 1, lat_row),
                     _mod_spec(d, layer, 0, lambda i: ctx_row), _mod_spec(d, layer, 1, lambda i: ctx_row)]
        args += [mod3] * 4
    return pl.pallas_call(
        functools.partial(_in_proj_body, n_lat=n_lat, modulated=mod is not None),
        grid=(n_lat + n_ctx, nj),
        in_specs=in_specs + [pl.BlockSpec((None, d, tn), lambda i, j: (j, 0, 0))],
        out_specs=pl.BlockSpec((tm, tn), lambda i, j: (i, j)),
        out_shape=jax.ShapeDtypeStruct(((n_lat + n_ctx) * tm, nj * tn), _BF),
        scratch_shapes=[pltpu.VMEM((tm, d), _BF)],
        compiler_params=_cparams(("arbitrary", "arbitrary")),
    )(*args, w)


def _rope_tables(t, rot_dim, identity=False):
    pos = np.arange(t)
    row = (pos // GRID_W).astype(np.float32)
    col = (pos % GRID_W).astype(np.float32)
    nf = rot_dim // 4
    inv = (ROPE_BASE ** (-np.arange(nf, dtype=np.float32) / nf)).astype(np.float32)
    ar = row[:, None] * inv[None]
    ac = col[:, None] * inv[None]
    ang = np.concatenate([ar, ar, ac, ac], -1)
    cos, sin = np.cos(ang), np.sin(ang)
    if identity:
        cos, sin = np.ones_like(cos), np.zeros_like(sin)
    first = (np.arange(rot_dim) % (rot_dim // 2)) < nf
    sin_a = np.where(first, -sin, 0.0)
    sin_b = np.where(first, 0.0, sin)
    pad = LANE - rot_dim
    if pad:
        cos = np.pad(cos, ((0, 0), (0, pad)), constant_values=1.0)
        sin_a = np.pad(sin_a, ((0, 0), (0, pad)))
        sin_b = np.pad(sin_b, ((0, 0), (0, pad)))
    return tuple(jnp.asarray(a, _F32) for a in (cos, sin_a, sin_b))


def _rope(x, cos, sin_a, sin_b, nf):
    return x * cos + pltpu.roll(x, LANE - nf, 1) * sin_a + pltpu.roll(x, nf, 1) * sin_b


def _rms(x, g):
    return x * lax.rsqrt(jnp.mean(x * x, -1, keepdims=True) + RMS_EPS) * g


VT_ROWS = HEAD_DIM + 16


def _values_t(v):
    n = v.shape[0]
    ones = (lax.broadcasted_iota(jnp.int32, (VT_ROWS - HEAD_DIM, n), 0) == 0).astype(_BF)
    return jnp.concatenate([v.astype(_F32).T.astype(_BF), ones], 0)


def _mla_qkv_body(cq_ref, ckv_ref, kpe_ref, gq_ref, gkv_ref, wq_ref, wkv_ref, cos_ref, sa_ref, sb_ref,
                  *rest):
    if len(rest) == 5:
        wo_ref, q_ref, k_ref, vt_ref, wob = rest
        wob[...] = wo_ref[...].astype(_BF)
    else:
        q_ref, k_ref, vt_ref = rest
    cos, sa, sb = cos_ref[...], sa_ref[...], sb_ref[...]
    nf = MLA_ROPE // 4
    xq = _rms(cq_ref[...].astype(_F32), gq_ref[...]).astype(_BF)
    xkv = _rms(ckv_ref[...].astype(_F32), gkv_ref[...]).astype(_BF)
    pe = _rope(kpe_ref[...].astype(_F32), cos, sa, sb, nf).astype(_BF)
    width = MLA_NOPE + MLA_V
    for h in range(MLA_HEADS):
        q = _dot(xq, wq_ref[:, h * MLA_QK_PAD:(h + 1) * MLA_QK_PAD])
        q_ref[0, h, :, 0:LANE] = q[:, :LANE].astype(_BF)
        q_ref[0, h, :, LANE:] = _rope(q[:, LANE:], cos, sa, sb, nf).astype(_BF)
        kv = _dot(xkv, wkv_ref[:, h * width:(h + 1) * width])
        k_ref[0, h, :, 0:LANE] = kv[:, :MLA_NOPE].astype(_BF)
        k_ref[0, h, :, LANE:] = pe
        vt_ref[0, h] = _values_t(kv[:, MLA_NOPE:])


def _mla_qkv(p, gq, gkv, wq, wkv, tables, b, t, tm, row0=0, cast=None):
    nb = t // tm
    blk0 = row0 // tm
    extra_in, extra_args, extra_out, extra_shape = [], [], [], []
    if cast is not None:
        w, layer = cast
        slab = w.shape[1] // (b * nb)
        assert w.shape[1] % (b * nb) == 0 and slab % 16 == 0
        extra_in = [pl.BlockSpec((None, slab, w.shape[2]), lambda i: (layer, i, 0))]
        extra_args = [w]
        extra_out = [pl.BlockSpec((slab, w.shape[2]), lambda i: (i, 0))]
        extra_shape = [jax.ShapeDtypeStruct(w.shape[1:], _BF)]
    tab_spec = pl.BlockSpec((tm, LANE), lambda i: (i % nb, 0))
    head_spec = lambda rows, cols, ri, ci: pl.BlockSpec(
        (1, MLA_HEADS, rows, cols), lambda i: (i // nb, 0, (i % nb) * ri, (i % nb) * ci))
    return pl.pallas_call(
        _mla_qkv_body,
        grid=(b * nb,),
        in_specs=[pl.BlockSpec((tm, MLA_Q_RANK), lambda i: (blk0 + i, CB_CQ * LANE // MLA_Q_RANK)),
                  pl.BlockSpec((tm, MLA_KV_RANK), lambda i: (blk0 + i, CB_CKV * LANE // MLA_KV_RANK)),
                  pl.BlockSpec((tm, LANE), lambda i: (blk0 + i, CB_KPE)),
                  pl.BlockSpec((1, MLA_Q_RANK), lambda i: (0, 0)),
                  pl.BlockSpec((1, MLA_KV_RANK), lambda i: (0, 0)),
                  pl.BlockSpec(wq.shape, lambda i: (0, 0)),
                  pl.BlockSpec(wkv.shape, lambda i: (0, 0)),
                  tab_spec, tab_spec, tab_spec] + extra_in,
        out_specs=[head_spec(tm, MLA_QK_PAD, 1, 0), head_spec(tm, MLA_QK_PAD, 1, 0),
                   head_spec(VT_ROWS, tm, 0, 1)] + extra_out,
        out_shape=[jax.ShapeDtypeStruct((b, MLA_HEADS, t, MLA_QK_PAD), _BF),
                   jax.ShapeDtypeStruct((b, MLA_HEADS, t, MLA_QK_PAD), _BF),
                   jax.ShapeDtypeStruct((b, MLA_HEADS, VT_ROWS, t), _BF)] + extra_shape,
        compiler_params=_cparams(("arbitrary",)),
    )(p, p, p, gq, gkv, wq, wkv, *tables, *extra_args)


def _skewed(n, stages):
    carry = [None] * n
    for step in range(n + len(stages) - 1):
        for s, stage in enumerate(stages):
            j = step - s
            if 0 <= j < n:
                carry[j] = stage(j, carry[j])


def _attn_stages(scores_fn, values_fn, store_fn):
    def s_scores(j, _):
        return scores_fn(j)

    def s_max(j, sc):
        return sc, functools.reduce(jnp.maximum, [jnp.max(s, 0, keepdims=True) for s in sc])

    def s_exp(j, c):
        sc, m = c
        return [jnp.exp2(s - m).astype(_BF) for s in sc]

    def s_pv(j, p):
        acc = functools.reduce(jnp.add, [_dot(vt, pi) for vt, pi in zip(values_fn(j), p)])
        store_fn(j, (acc[0:HEAD_DIM, :] / acc[HEAD_DIM:HEAD_DIM + 1, :]).T)

    return [s_scores, s_max, s_exp, s_pv]


def _attn_body(*refs, nseg, nsub, v_row_major):
    q_ref, o_ref = refs[0], refs[-1]
    ks = [refs[1 + 2 * i][...] for i in range(nseg)]
    vts = [refs[2 + 2 * i][...] for i in range(nseg)]
    if v_row_major:
        vts = [_values_t(v) for v in vts]
    ts = q_ref.shape[0] // nsub
    rows = lambda j: slice(j * ts, (j + 1) * ts)

    def store(j, out):
        o_ref[rows(j), :] = out.astype(o_ref.dtype)

    _skewed(nsub, _attn_stages(lambda j: [_dot_nt(k, q_ref[rows(j), :]) for k in ks], lambda j: vts, store))


def _dense_attn(q, q_spec, kv, b, heads, tq_blocks, tq, dv, nsub, v_row_major=False):
    args, specs = [q], [q_spec]
    for k, ks, v, vs in kv:
        args += [k, v]
        specs += [ks, vs]
    return pl.pallas_call(
        functools.partial(_attn_body, nseg=len(kv), nsub=nsub, v_row_major=v_row_major),
        grid=(b, heads, tq_blocks),
        in_specs=specs,
        out_specs=pl.BlockSpec((tq, dv), lambda bi, h, i: (bi * tq_blocks + i, h)),
        out_shape=jax.ShapeDtypeStruct((b * tq_blocks * tq, heads * dv), _BF),
        compiler_params=_cparams(("arbitrary", "arbitrary", "arbitrary")),
    )(*args)


def _head4_spec(rows, cols):
    return pl.BlockSpec((None, None, rows, cols), lambda bi, h, i: (bi, h, 0, 0))


NA_GROUP = 4
NA_KEY_ROWS = 12


def _na_group_window(gi, rows):
    ngroups = rows // NA_GROUP
    if gi == 0:
        return 0, 0
    if gi == ngroups - 1:
        return 2, rows - NA_KEY_ROWS
    return 1, gi * NA_GROUP - NA_WIN_R // 2


NA_DR = 2 * NA_WIN_R - 1


def _na_col_bias(rpb):
    c = np.arange(GRID_W)
    col_start = np.clip(c - NA_WIN_C // 2, 0, GRID_W - NA_WIN_C)
    kc = np.arange(GRID_W)
    in_win = (kc[None, :] >= col_start[:, None]) & (kc[None, :] < col_start[:, None] + NA_WIN_C)
    dc = np.clip(kc[None, :] - c[:, None] + NA_WIN_C - 1, 0, 2 * NA_WIN_C - 2)
    col_sel = (dc[:, :, None] == np.arange(2 * NA_WIN_C - 1)).astype(np.float32)
    toep = jnp.einsum('hrj,cqj->hrqc', rpb.astype(_F32), col_sel, precision=lax.Precision.HIGHEST)
    toep = jnp.where(in_win.T[None, None], toep * LOG2E, NEG_INF)
    masked = jnp.full((rpb.shape[0], 1, GRID_W, GRID_W), NEG_INF, _F32)
    return jnp.concatenate([toep, masked], 1)


def _na_block_index():
    g = np.arange(NA_GROUP)[:, None]
    kr = np.arange(NA_KEY_ROWS)[None, :]
    half = NA_WIN_R // 2
    dr = np.stack([kr - g + NA_WIN_R - 1, kr - g + NA_WIN_R - 1 - half, kr - g - 1])
    w0 = np.stack([0 * g, g, half + 0 * g])
    row_ok = (kr[None] >= w0) & (kr[None] < w0 + NA_WIN_R)
    assert np.all((dr[row_ok] >= 0) & (dr[row_ok] < NA_DR))
    return np.where(row_ok, dr, NA_DR)


def _na_body(qkv_ref, ctx_ref, cb_ref, o_ref, bias_ref, vt_ref, *, rows):
    q_cols, k_cols, v_cols = (slice(i * LANE, (i + 1) * LANE) for i in range(NA_BLOCKS))
    @pl.when(pl.program_id(1) == 0)
    def _():
        idx = _na_block_index()
        for typ in range(3):
            for kr in range(NA_KEY_ROWS):
                for gp in range(NA_GROUP // 2):
                    pair = jnp.concatenate([cb_ref[int(idx[typ, 2 * gp, kr])], cb_ref[int(idx[typ, 2 * gp + 1, kr])]], 1)
                    bias_ref[typ, kr * GRID_W:(kr + 1) * GRID_W, gp * 2 * GRID_W:(gp + 1) * 2 * GRID_W] = pair

    vt_ref[...] = _values_t(qkv_ref[:, v_cols])
    kc = ctx_ref[:, k_cols]
    vct = _values_t(ctx_ref[:, v_cols])
    gq = NA_GROUP * GRID_W

    def band(gi):
        ws = _na_group_window(gi, rows)[1]
        return slice(ws * GRID_W, (ws + NA_KEY_ROWS) * GRID_W)

    def scores(gi):
        q = qkv_ref[gi * gq:(gi + 1) * gq, q_cols]
        return [_dot_nt(qkv_ref[band(gi), k_cols], q) + bias_ref[_na_group_window(gi, rows)[0]], _dot_nt(kc, q)]

    def store(gi, out):
        o_ref[gi * gq:(gi + 1) * gq, :] = out.astype(o_ref.dtype)

    _skewed(rows // NA_GROUP, _attn_stages(scores, lambda gi: [vt_ref[:, band(gi)], vct], store))


def _na_attn(p, bias, b, t, c):
    rows = t // GRID_W
    cblk0 = b * t // c
    assert t % GRID_W == 0 and rows % NA_GROUP == 0 and rows >= NA_KEY_ROWS + NA_GROUP and 2 * GRID_W == LANE
    assert CB_NA % NA_BLOCKS == 0
    return pl.pallas_call(
        functools.partial(_na_body, rows=rows),
        grid=(NA_HEADS, b),
        in_specs=[pl.BlockSpec((t, NA_BLOCKS * LANE), lambda h, bi: (bi, CB_NA // NA_BLOCKS + h)),
                  pl.BlockSpec((c, NA_BLOCKS * LANE), lambda h, bi: (cblk0 + bi, CB_NA // NA_BLOCKS + h)),
                  pl.BlockSpec((None, NA_DR + 1, GRID_W, GRID_W), lambda h, bi: (h, 0, 0, 0))],
        out_specs=pl.BlockSpec((t, LANE), lambda h, bi: (bi, h)),
        out_shape=jax.ShapeDtypeStruct((b * t, NA_WIDTH), _BF),
        scratch_shapes=[pltpu.VMEM((3, NA_KEY_ROWS * GRID_W, NA_GROUP * GRID_W), _F32),
                        pltpu.VMEM((VT_ROWS, t), _BF)],
        compiler_params=_cparams(("arbitrary", "arbitrary")),
    )(p, p, bias)


def _ret_body(lg_ref, x_ref, s0f_ref, s0b_ref, tab_ref, o_ref, sf_ref, sbo_ref, kr_scr, u_hist, st_hist, *, n):
    q_cols, k_cols, v_cols, g_cols = (slice(i * LANE, (i + 1) * LANE) for i in range(RET_BLOCKS))
    h = pl.program_id(0)
    lgf = lg_ref[0, h]
    lgb = lg_ref[1, h]
    L = RET_CHUNK
    pos_c = lax.broadcasted_iota(jnp.int32, (L, 1), 0).astype(_F32)
    pos_r = lax.broadcasted_iota(jnp.int32, (1, L), 1).astype(_F32)
    diff = pos_c - pos_r
    decay = jnp.where(diff > 0, jnp.exp(jnp.maximum(diff, 0.0) * lgf),
                      jnp.where(diff < 0, jnp.exp(jnp.maximum(-diff, 0.0) * lgb), 2.0))
    qdec_f = jnp.exp((pos_c + 1.0) * lgf)
    kdec_f = jnp.exp((L - 1.0 - pos_c) * lgf)
    qdec_b = jnp.exp((L - pos_c) * lgb)
    kdec_b = jnp.exp(pos_c * lgb)
    cd_f = jnp.exp(L * lgf)
    cd_b = jnp.exp(L * lgb)
    nf = HEAD_DIM // 4

    def rows_of(i):
        return slice(i * L, (i + 1) * L)

    def roped(cols, i):
        r = rows_of(i)
        return _rope(x_ref[r, cols].astype(_F32), tab_ref[0, r, :], tab_ref[1, r, :], tab_ref[2, r, :], nf)

    for i in range(n):
        k = roped(k_cols, i)
        kr_scr[rows_of(i), :] = k.astype(_BF)
        kd = jnp.concatenate([k * kdec_f, k * kdec_b], 1).astype(_BF)
        u_hist[i] = lax.dot_general(kd, x_ref[rows_of(i), v_cols], (((0,), (0,)), ((), ())),
                                    preferred_element_type=_F32)

    state = s0f_ref[...]
    for i in range(n):
        st_hist[i, 0:HEAD_DIM, :] = state.astype(_BF)
        state = cd_f * state + u_hist[i, 0:HEAD_DIM, :]
    sf_ref[...] = state
    state = s0b_ref[...]
    for i in reversed(range(n)):
        st_hist[i, HEAD_DIM:, :] = state.astype(_BF)
        state = cd_b * state + u_hist[i, HEAD_DIM:, :]
    sbo_ref[...] = state

    for i in range(n):
        r = rows_of(i)
        q = roped(q_cols, i)
        a = _dot_nt(q.astype(_BF), kr_scr[r, :]) * decay
        qd = jnp.concatenate([q * qdec_f, q * qdec_b], 1).astype(_BF)
        o = _dot(a.astype(_BF), x_ref[r, v_cols]) + _dot(qd, st_hist[i])
        mu = jnp.mean(o, -1, keepdims=True)
        oc = o - mu
        var = jnp.mean(oc * oc, -1, keepdims=True)
        y = oc * lax.rsqrt(var + LN_EPS) * _silu(x_ref[r, g_cols].astype(_F32))
        o_ref[r, :] = y.astype(o_ref.dtype)


def _retention(p, lg, s0f, s0b, tables, b, t, row0=0):
    n = t // RET_CHUNK
    blk0 = row0 // t
    st_spec = pl.BlockSpec((None, None, HEAD_DIM, HEAD_DIM), lambda h, bi: (bi, h, 0, 0))
    assert CB_RET % RET_BLOCKS == 0
    st_shape = jax.ShapeDtypeStruct((b, RET_HEADS, HEAD_DIM, HEAD_DIM), _F32)
    return pl.pallas_call(
        functools.partial(_ret_body, n=n),
        grid=(RET_HEADS, b),
        in_specs=[pl.BlockSpec(memory_space=pltpu.SMEM),
                  pl.BlockSpec((t, RET_BLOCKS * LANE), lambda h, bi: (blk0 + bi, CB_RET // RET_BLOCKS + h)),
                  st_spec, st_spec,
                  pl.BlockSpec((3, t, LANE), lambda h, bi: (0, 0, 0))],
        out_specs=[pl.BlockSpec((t, LANE), lambda h, bi: (bi, h)), st_spec, st_spec],
        out_shape=[jax.ShapeDtypeStruct((b * t, RET_WIDTH), _BF), st_shape, st_shape],
        scratch_shapes=[pltpu.VMEM((t, HEAD_DIM), _BF),
                        pltpu.VMEM((n, 2 * HEAD_DIM, HEAD_DIM), _F32),
                        pltpu.VMEM((n, 2 * HEAD_DIM, HEAD_DIM), _BF)],
        compiler_params=_cparams(("arbitrary", "arbitrary")),
    )(lg, p, s0f, s0b, jnp.stack(tables))


def _ln_epilogue(y, rows, x_ref, gate_ref, lng_ref, lnb_ref, alpha, xo_ref, ho_ref, sh_ref, sc_ref):
    z = alpha * x_ref[rows, :] + gate_ref[0] * y
    mu = jnp.mean(z, -1, keepdims=True)
    zc = z - mu
    var = jnp.mean(zc * zc, -1, keepdims=True)
    xn = zc * lax.rsqrt(var + LN_EPS) * lng_ref[...] + lnb_ref[...]
    xo_ref[rows, :] = xn
    if ho_ref is not None:
        ho_ref[rows, :] = (xn * (1.0 + sc_ref[0]) + sh_ref[0]).astype(ho_ref.dtype)


def _proj_ln_body(*refs, n_in, alpha, with_next, nsub):
    a_refs, refs = refs[:n_in], refs[n_in:]
    if with_next:
        w_ref, x_ref, gate_ref, sh_ref, sc_ref, lng_ref, lnb_ref, xo_ref, ho_ref = refs
    else:
        w_ref, x_ref, gate_ref, lng_ref, lnb_ref, xo_ref = refs
        sh_ref = sc_ref = ho_ref = None
    ts = x_ref.shape[0] // nsub
    for j in range(nsub):
        r = slice(j * ts, (j + 1) * ts)
        a = a_refs[0][r, :] if n_in == 1 else jnp.concatenate([ar[r, :] for ar in a_refs], 1)
        _ln_epilogue(_dot(a, w_ref[...]), r, x_ref, gate_ref, lng_ref, lnb_ref, alpha, xo_ref, ho_ref, sh_ref, sc_ref)


def _proj_ln(a_list, w, x2, mod3, layer, gate_chunk, row_fn, lng, lnb, alpha, tm, nsub, with_next):
    m, d = x2.shape
    row = lambda width: pl.BlockSpec((tm, width), lambda i: (i, 0))
    vec = pl.BlockSpec((1, d), lambda i: (0, 0))
    w_spec = pl.BlockSpec(w.shape, lambda i: (0, 0), pipeline_mode=pl.Buffered(1))
    in_specs = [row(a.shape[1]) for a in a_list] + [w_spec, row(d), _mod_spec(d, layer, gate_chunk, row_fn)]
    args = list(a_list) + [w, x2, mod3]
    if with_next:
        nxt = (layer, gate_chunk + 1) if gate_chunk < 5 else (layer + 1, 0)
        in_specs += [_mod_spec(d, nxt[0], nxt[1], row_fn), _mod_spec(d, nxt[0], nxt[1] + 1, row_fn)]
        args += [mod3, mod3]
    in_specs += [vec, vec]
    args += [lng, lnb]
    out_specs = [row(d)] + ([row(d)] if with_next else [])
    out_shape = [jax.ShapeDtypeStruct((m, d), _F32)] + ([jax.ShapeDtypeStruct((m, d), _BF)] if with_next else [])
    out = pl.pallas_call(
        functools.partial(_proj_ln_body, n_in=len(a_list), alpha=alpha, with_next=with_next, nsub=nsub),
        grid=(m // tm,),
        in_specs=in_specs, out_specs=out_specs, out_shape=out_shape,
        compiler_params=_cparams(("arbitrary",)),
    )(*args)
    return (out[0], out[1]) if with_next else (out[0], None)


def _glu_chunks(t):
    rc = _tile(t, 1024)
    return [(i * rc, rc) for i in range(t // rc)]


def _glu_body(h_ref, wa_ref, wu_ref, *rest, t, cast):
    if cast:
        wd_ref, cw_ref, cb_ref, o_ref, wab, wub, wdb, a_scr = rest
        wdb[...] = wd_ref[...].astype(_BF)

        @pl.when(pl.program_id(1) == 0)
        def _():
            wab[...] = wa_ref[...].astype(_BF)
            wub[...] = wu_ref[...].astype(_BF)
    else:
        cw_ref, cb_ref, o_ref, a_scr = rest
        wab, wub = wa_ref, wu_ref

    nseq = a_scr.shape[0]
    units = [(q, r0, rc) for q in range(nseq) for r0, rc in _glu_chunks(t)]
    halo = 8
    tn = o_ref.shape[1]
    zeros = jnp.zeros((halo, tn), _F32)
    for q in range(nseq):
        a_scr[q, 0:halo, :] = zeros
        a_scr[q, halo + t:, :] = zeros
    w0, w1, w2 = cw_ref[0:1, :], cw_ref[1:2, :], cw_ref[2:3, :]
    bias = cb_ref[...]
    u_prev = None
    for i in range(len(units) + 1):
        u_cur = None
        if i < len(units):
            q, r0, rc = units[i]
            hc = h_ref[q * t + r0:q * t + r0 + rc, :]
            a_scr[q, halo + r0:halo + r0 + rc, :] = _dot(hc, wab[...])
            u_cur = _dot(hc, wub[...])
        if i > 0:
            q, r0, rc = units[i - 1]
            base = halo + r0
            lo = a_scr[q, base - 1:base - 1 + rc, :]
            mid = a_scr[q, base:base + rc, :]
            hi = a_scr[q, base + 1:base + 1 + rc, :]
            acc = bias + lo * w0 + mid * w1 + hi * w2
            o_ref[q * t + r0:q * t + r0 + rc, :] = (_silu(acc) * u_prev).astype(o_ref.dtype)
        u_prev = u_cur


def _glu(h, w_up, w_down, layer, conv_w, conv_b, b, t, tn, nseq=1, w_bf16=None):
    d = h.shape[1]
    dff = conv_w.shape[1]
    nj = dff // tn
    cast = w_bf16 is None
    steps = nj * (b // nseq)
    g_spec = pl.BlockSpec((nseq * t, tn), lambda j, bi: (bi, j))
    g_shape = jax.ShapeDtypeStruct((b * t, dff), _BF)
    wb_spec = pl.BlockSpec((None, d, tn), lambda j, bi: (j, 0, 0))
    if cast:
        w_args = (w_up, w_up)
        w_specs = [pl.BlockSpec((None, d, tn), lambda j, bi: (layer, 0, j)),
                   pl.BlockSpec((None, d, tn), lambda j, bi: (layer, 0, nj + j))]
        wb_shape = jax.ShapeDtypeStruct((nj, d, tn), _BF)
        slab = dff // steps
        assert dff % steps == 0 and slab % 16 == 0
        step = lambda j, bi: j * (b // nseq) + bi
        w_args += (w_down,)
        w_specs.append(pl.BlockSpec((None, slab, w_down.shape[2]), lambda j, bi: (layer, step(j, bi), 0)))
        wd_spec = pl.BlockSpec((slab, w_down.shape[2]), lambda j, bi: (step(j, bi), 0))
        wd_shape = jax.ShapeDtypeStruct(w_down.shape[1:], _BF)
        out_specs, out_shape = [g_spec, wb_spec, wb_spec, wd_spec], [g_shape, wb_shape, wb_shape, wd_shape]
    else:
        w_args, w_specs = tuple(w_bf16), [wb_spec, wb_spec]
        out_specs, out_shape = g_spec, g_shape
    out = pl.pallas_call(
        functools.partial(_glu_body, t=t, cast=cast),
        grid=(nj, b // nseq),
        in_specs=[pl.BlockSpec((nseq * t, d), lambda j, bi: (bi, 0)), *w_specs,
                  pl.BlockSpec((CONV_W, tn), lambda j, bi: (0, j)),
                  pl.BlockSpec((1, tn), lambda j, bi: (0, j))],
        out_specs=out_specs, out_shape=out_shape,
        scratch_shapes=[pltpu.VMEM((nseq, t + 16, tn), _F32)],
        compiler_params=_cparams(("arbitrary", "arbitrary")),
    )(h, *w_args, conv_w, conv_b)
    return (out[0], (out[1], out[2]), out[3]) if cast else out


PACK_SRC_ROWS = 64
PACK_CHUNKS = 4


def _pack_plan():
    src = dict(zip(("na_q", "na_k", "na_v", "cq", "ckv", "kpe", "r_q", "r_k", "r_v", "r_g"),
                   [int(i) for i in np.cumsum((0,) + IN_SPLITS[:-1])]))
    plan = np.zeros((PACKED_IN_WIDTH // LANE, 3), np.int32)
    segments = [(CB_CQ, 1, "cq", MLA_Q_RANK, 0), (CB_CKV, 1, "ckv", MLA_KV_RANK, 0), (CB_KPE, 1, "kpe", MLA_ROPE, 0)]
    segments += [(CB_NA + i, NA_BLOCKS, name, NA_WIDTH, kind)
                 for i, (name, kind) in enumerate((("na_q", 1), ("na_k", 0), ("na_v", 0)))]
    segments += [(CB_RET + i, RET_BLOCKS, name, RET_WIDTH, kind)
                 for i, (name, kind) in enumerate((("r_q", 0), ("r_k", 2), ("r_v", 0), ("r_g", 0)))]
    for dst, stride, name, width, kind in segments:
        for off in range(0, width, LANE):
            row = src[name] + off
            assert row % PACK_SRC_ROWS == 0
            plan[dst + stride * (off // LANE)] = (row // PACK_SRC_ROWS, kind, min(LANE, width - off))
    return plan


def _pack_w_in_body(plan_ref, *refs):
    q_scale = HEAD_DIM ** -0.5 * LOG2E
    k_scale = HEAD_DIM ** -0.5
    o_ref = refs[-1]
    row = lax.broadcasted_iota(jnp.int32, (LANE, 1), 0)
    for c in range(PACK_CHUNKS):
        chunk = pl.program_id(0) * PACK_CHUNKS + c
        kind, valid = plan_ref[chunk, 1], plan_ref[chunk, 2]
        scale = jnp.where(kind == 1, q_scale, jnp.where(kind == 2, k_scale, 1.0))
        x = jnp.concatenate([refs[2 * c][...], refs[2 * c + 1][...]], 0)
        x = jnp.where(row < valid, x * scale, 0.0)
        o_ref[:, c * LANE:(c + 1) * LANE] = x.T.astype(o_ref.dtype)


def _pack_w_in(w_in, layer, tn):
    depth, d, n = w_in.shape
    w_t = jnp.swapaxes(w_in, 1, 2)
    step_cols = PACK_CHUNKS * LANE
    per_block = tn // step_cols

    def src_spec(c, second):
        return pl.BlockSpec((None, PACK_SRC_ROWS, d),
                            lambda j, plan: (layer, plan[j * PACK_CHUNKS + c, 0] + second, 0))

    return pl.pallas_call(
        _pack_w_in_body,
        grid_spec=pltpu.PrefetchScalarGridSpec(
            num_scalar_prefetch=1,
            grid=(PACKED_IN_WIDTH // step_cols,),
            in_specs=[src_spec(c, second) for c in range(PACK_CHUNKS) for second in (0, 1)],
            out_specs=pl.BlockSpec((None, d, step_cols), lambda j, plan: (j // per_block, 0, j % per_block))),
        out_shape=jax.ShapeDtypeStruct((PACKED_IN_WIDTH // tn, d, tn), _BF),
        compiler_params=_cparams(("arbitrary",)),
    )(jnp.asarray(_pack_plan()), *([w_t] * (2 * PACK_CHUNKS)))


def _pack_w_uq(w):
    r = w.shape[0]
    w = w.reshape(r, MLA_HEADS, MLA_NOPE + MLA_ROPE) * ((MLA_NOPE + MLA_ROPE) ** -0.5 * LOG2E)
    w = jnp.pad(w, ((0, 0), (0, 0), (0, MLA_QK_PAD - MLA_NOPE - MLA_ROPE)))
    return w.reshape(r, MLA_HEADS * MLA_QK_PAD).astype(_BF)


def kernel(x, c, ctx, c_ctx, w_ada, b_ada, w_in, mla_q_norm, mla_kv_norm, w_uq, w_ukv, na_rpb, ret_decay,
           w_o, ln1_g, ln1_b, w_up, conv_w, conv_b, w_down, ln2_g, ln2_b):
    b, t, d = x.shape
    cl = ctx.shape[1]
    depth = w_ada.shape[0]
    dff = conv_w.shape[-1]
    assert b + 1 <= MOD_ROWS
    alpha = (2 * depth) ** 0.25
    ctx_row = b

    cvec = jnp.concatenate([c, c_ctx[None], jnp.zeros((MOD_ROWS - b - 1, d), c.dtype)], 0)
    mod3 = _ada(cvec, w_ada, b_ada)

    tm_in = _tile(b * cl, _tile(t, 1024))
    ctx0 = b * t
    lat_row = lambda tm: (lambda i: i // (t // tm))
    ctx_rowf = lambda i: ctx_row

    rope_pe = _rope_tables(t, MLA_ROPE)
    rope_ret = _rope_tables(t, HEAD_DIM)
    id_pe = _rope_tables(cl, MLA_ROPE, identity=True)
    id_ret = _rope_tables(cl, HEAD_DIM, identity=True)

    xf = x.reshape(b * t, d)
    cf = ctx.reshape(b * cl, d)
    h1, hc1 = xf, cf

    tq = _tile(t, 2048)
    tqc = _tile(cl, 256)
    tn_ff = _tile(dff, 512)
    zero_state = jnp.zeros((b, RET_HEADS, HEAD_DIM, HEAD_DIM), _F32)

    for l in range(depth):
        last = l == depth - 1
        w_in_p = _pack_w_in(w_in, l, 1024)
        w_uq_p = _pack_w_uq(w_uq[l])
        w_ukv_p = w_ukv[l].astype(_BF)
        gq = mla_q_norm[l][None]
        gkv = mla_kv_norm[l][None]
        lg = jnp.log1p(-jnp.exp2(ret_decay[l].astype(_F32)))
        bias = _na_col_bias(na_rpb[l])

        p = _in_proj(h1, hc1, w_in_p, tm_in, t, mod=(mod3, 0, ctx_row) if l == 0 else None)

        qc_m, kc_m, vc_m = _mla_qkv(p, gq, gkv, w_uq_p, w_ukv_p, id_pe, b, cl, _tile(cl, 1024), row0=ctx0)
        q_m, kl_m, vl_m, w_o_b = _mla_qkv(p, gq, gkv, w_uq_p, w_ukv_p, rope_pe, b, t, _tile(t, 1024), cast=(w_o, l))
        y_mla = _dense_attn(
            q_m, pl.BlockSpec((None, None, tq, MLA_QK_PAD), lambda bi, h, i: (bi, h, i, 0)),
            [(kc_m, _head4_spec(cl, MLA_QK_PAD), vc_m, _head4_spec(VT_ROWS, cl)),
             (kl_m, _head4_spec(t, MLA_QK_PAD), vl_m, _head4_spec(VT_ROWS, t))],
            b, MLA_HEADS, t // tq, tq, MLA_V, nsub=tq // _tile(tq, 256))

        y_na = _na_attn(p, bias, b, t, cl)

        yc_ret, s_f, s_b = _retention(p, lg, zero_state, zero_state, id_ret, b, cl, row0=ctx0)
        y_ret, _, _ = _retention(p, lg, s_f, s_b, rope_ret, b, t)

        tm_o = _tile(t, 512)
        x_new, h2 = _proj_ln([y_na, y_mla, y_ret], w_o_b, xf, mod3, l, 2, lat_row(tm_o),
                             ln1_g[l][None], ln1_b[l][None], alpha, tm_o, tm_o // _tile(tm_o, 256), with_next=True)

        tm_d = _tile(t, 256)

        if not last:
            yc_mla = _dense_attn(
                qc_m, pl.BlockSpec((None, None, tqc, MLA_QK_PAD), lambda bi, h, i: (bi, h, i, 0)),
                [(kc_m, _head4_spec(cl, MLA_QK_PAD), vc_m, _head4_spec(VT_ROWS, cl))],
                b, MLA_HEADS, cl // tqc, tqc, MLA_V, nsub=1)
            nbc = cl // tqc
            cq0, ck0 = ctx0 // tqc, ctx0 // cl
            yc_na = _dense_attn(
                p, pl.BlockSpec((tqc, LANE), lambda bi, h, i: (cq0 + bi * nbc + i, CB_NA + NA_BLOCKS * h)),
                [(p, pl.BlockSpec((cl, LANE), lambda bi, h, i: (ck0 + bi, CB_NA + NA_BLOCKS * h + 1)),
                  p, pl.BlockSpec((cl, LANE), lambda bi, h, i: (ck0 + bi, CB_NA + NA_BLOCKS * h + 2)))],
                b, NA_HEADS, nbc, tqc, HEAD_DIM, nsub=1, v_row_major=True)
            tm_oc = _tile(cl, 256)
            c_new, hc2 = _proj_ln([yc_na, yc_mla, yc_ret], w_o_b, cf, mod3, l, 2, ctx_rowf,
                                  ln1_g[l][None], ln1_b[l][None], alpha, tm_oc, 1, with_next=True)

        g_l, w_up_b, w_down_b = _glu(h2, w_up, w_down, l, conv_w[l], conv_b[l][None], b, t, tn_ff)
        xf, h1 = _proj_ln([g_l], w_down_b, x_new, mod3, l, 5, lat_row(tm_d), ln2_g[l][None], ln2_b[l][None],
                          alpha, tm_d, 1, with_next=not last)
        if not last:
            g_c = _glu(hc2, w_up, w_down, l, conv_w[l], conv_b[l][None], b, cl, tn_ff, nseq=b, w_bf16=w_up_b)
            tm_dc = _tile(cl, 256)
            cf, hc1 = _proj_ln([g_c], w_down_b, c_new, mod3, l, 5, ctx_rowf, ln2_g[l][None], ln2_b[l][None],
                               alpha, tm_dc, 1, with_next=True)

    return xf.reshape(b, t, d)
```
